```python
import jax, jax.numpy as jnp
from jax import lax
import numpy as np

D_MODEL = 1024
BATCH = 4
SEQ = 4096
DEPTH = 1
DEC_BATCH = 128
DEC_SEQ = 1
PAST_LEN = 16384
PAGE_SIZE = 128

HEAD_DIM = 64
ATTN_Q_HEADS = 8
ATTN_KV_HEADS = 2
ATTN_GROUP = ATTN_Q_HEADS // ATTN_KV_HEADS
WINDOW = 128
HG_HEADS = 4
HG_DK = 128
HG_DV = 128
HG_CHUNK = 64
FFN_DIM = 2816
EPS = 1e-6

ATTN_Q_W = ATTN_Q_HEADS * HEAD_DIM
ATTN_KV_W = ATTN_KV_HEADS * HEAD_DIM
HG_K_W = HG_HEADS * HG_DK
HG_V_W = HG_HEADS * HG_DV
SPLIT_SIZES = (ATTN_Q_W, ATTN_KV_W, ATTN_KV_W, HG_K_W, HG_K_W, HG_V_W, HG_V_W, D_MODEL, D_MODEL)
IN_WIDTH = sum(SPLIT_SIZES)

kernel_name = "hybrid_swa_hgrn2_macaron_step"


def rmsnorm(x, g):
    xf = x.astype(jnp.float32)
    y = xf * lax.rsqrt(jnp.mean(xf * xf, axis=-1, keepdims=True) + EPS)
    return (y * g.astype(jnp.float32)).astype(x.dtype)


def split_cols(z):
    out, start = [], 0
    for w in SPLIT_SIZES:
        out.append(z[..., start:start + w])
        start += w
    return out


def ffn_half(x, pre, post, wg, wu, wd):
    h = rmsnorm(x, pre)
    u = jax.nn.silu(h @ wg) * (h @ wu)
    return x + 0.5 * rmsnorm(u @ wd, post)


def sink_attend(s, valid, v, sinks, eq):
    s = jnp.where(valid, s, -jnp.inf)
    sk = sinks.astype(jnp.float32).reshape(ATTN_KV_HEADS, ATTN_GROUP, 1, 1)
    m = jnp.maximum(jnp.max(s, axis=-1, keepdims=True), sk)
    p = jnp.exp(s - m)
    p = p / (jnp.sum(p, axis=-1, keepdims=True) + jnp.exp(sk - m))
    return jnp.einsum(eq, p.astype(v.dtype), v)


def swa_prompt(q, k, v, sinks):
    B, T = q.shape[:2]
    nb = T // WINDOW
    qb = q.reshape(B, nb, WINDOW, ATTN_KV_HEADS, ATTN_GROUP, HEAD_DIM)

    def band(t):
        tp = jnp.concatenate([jnp.zeros_like(t[:, :WINDOW]), t], axis=1)
        tp = tp.reshape(B, nb + 1, WINDOW, ATTN_KV_HEADS, HEAD_DIM)
        return jnp.concatenate([tp[:, :-1], tp[:, 1:]], axis=2)

    kb, vb = band(k), band(v)
    s = jnp.einsum('bnqhgd,bnkhd->bnhgqk', qb, kb,
                   preferred_element_type=jnp.float32) * (HEAD_DIM ** -0.5)
    i = jnp.arange(WINDOW)[:, None]
    j = jnp.arange(2 * WINDOW)[None, :]
    n = jnp.arange(nb)[:, None, None]
    rel = i + WINDOW - j
    valid = (rel >= 0) & (rel < WINDOW) & (n * WINDOW - WINDOW + j >= 0)
    o = sink_attend(s, valid[None, :, None, None], vb, sinks, 'bnhgqk,bnkhd->bnqhgd')
    return o.reshape(B, T, ATTN_Q_W), k[:, -WINDOW:], v[:, -WINDOW:]


def swa_sample(q, k, v, ck, cv, sinks):
    B, T = q.shape[:2]
    kk = jnp.concatenate([ck, k], axis=1)
    vv = jnp.concatenate([cv, v], axis=1)
    qg = q.reshape(B, T, ATTN_KV_HEADS, ATTN_GROUP, HEAD_DIM)
    s = jnp.einsum('bqhgd,bkhd->bhgqk', qg, kk,
                   preferred_element_type=jnp.float32) * (HEAD_DIM ** -0.5)
    qpos = jnp.arange(T)[:, None]
    kpos = jnp.arange(WINDOW + T)[None, :] - WINDOW
    valid = (kpos <= qpos) & (kpos > qpos - WINDOW)
    o = sink_attend(s, valid, vv, sinks, 'bhgqk,bkhd->bqhgd')
    return o.reshape(B, T, ATTN_Q_W), kk[:, -WINDOW:], vv[:, -WINDOW:]


def hgrn_chunked(q, k, v, logf, s0):
    B, T, H, DK = q.shape
    C = min(HG_CHUNK, T)
    n = -(-T // C)
    pad = n * C - T

    def prep(t):
        if pad:
            t = jnp.pad(t, ((0, 0), (0, pad), (0, 0), (0, 0)))
        return t.reshape(B, n, C, H, t.shape[-1]).transpose(1, 0, 3, 2, 4)

    tri = jnp.tril(jnp.ones((C, C), dtype=bool))[:, :, None]

    def step(S, inp):
        qc, kc, vc, gc = inp
        b = jnp.cumsum(gc, axis=2)
        o_inter = jnp.einsum('bhtk,bhkv->bhtv', qc * jnp.exp(b), S)
        diff = b[:, :, :, None, :] - b[:, :, None, :, :]
        decay = jnp.exp(jnp.where(tri, diff, -jnp.inf))
        A = jnp.einsum('bhtk,bhsk,bhtsk->bhts', qc, kc, decay)
        o = o_inter + jnp.einsum('bhts,bhsv->bhtv', A, vc)
        b_last = b[:, :, -1:, :]
        S_new = jnp.exp(b_last[:, :, 0, :])[..., None] * S + jnp.einsum(
            'bhsk,bhsv->bhkv', kc * jnp.exp(b_last - b), vc)
        return S_new, o

    S_fin, o = lax.scan(step, s0, (prep(q), prep(k), prep(v), prep(logf)))
    o = o.transpose(1, 0, 3, 2, 4).reshape(B, n * C, H, v.shape[-1])[:, :T]
    return o, S_fin


def token_mix(h, w_in, sinks, lb, hg_norm, w_attn_out, w_hgrn_out, w_out, attn_fn, s0):
    B, T, _ = h.shape
    q_a, k_a, v_a, q_h, f_h, i_h, g_h, gate_a, gate_h = split_cols(h @ w_in)
    o_a, nk, nv = attn_fn(q_a.reshape(B, T, ATTN_Q_HEADS, HEAD_DIM),
                          k_a.reshape(B, T, ATTN_KV_HEADS, HEAD_DIM),
                          v_a.reshape(B, T, ATTN_KV_HEADS, HEAD_DIM), sinks)
    fpre = f_h.astype(jnp.float32).reshape(B, T, HG_HEADS, HG_DK)
    lbh = lb.astype(jnp.float32).reshape(HG_HEADS, HG_DK)
    logf = jnp.log(lbh + (1.0 - lbh) * jax.nn.sigmoid(fpre))
    kh = (1.0 - lbh) * jax.nn.sigmoid(-fpre)
    qh = jax.nn.silu(q_h.astype(jnp.float32)).reshape(B, T, HG_HEADS, HG_DK)
    vh = i_h.astype(jnp.float32).reshape(B, T, HG_HEADS, HG_DV)
    o_h, s_new = hgrn_chunked(qh, kh, vh, logf, s0.astype(jnp.float32))
    o_h = rmsnorm(o_h, hg_norm) * jax.nn.silu(
        g_h.astype(jnp.float32).reshape(B, T, HG_HEADS, HG_DV))
    o_h = o_h.reshape(B, T, HG_V_W).astype(h.dtype)
    m = jax.nn.sigmoid(gate_a) * (o_a @ w_attn_out) + jax.nn.sigmoid(gate_h) * (o_h @ w_hgrn_out)
    return m @ w_out, nk, nv, s_new.astype(s0.dtype)


def setup_inputs(seed: int = 0) -> dict:
    key = jax.random.key(seed)
    ks = jax.random.split(key, 32)
    f32 = jnp.float32

    def nrm(k, shape, scale):
        return jax.random.normal(k, shape, f32) * scale

    def gain(k, shape):
        return 1.0 + 0.05 * jax.random.normal(k, shape, f32)

    D, F = D_MODEL, FFN_DIM
    return {
        "x_prompt": nrm(ks[0], (BATCH, SEQ, D), 1.0),
        "x_sample": nrm(ks[1], (DEC_BATCH, DEC_SEQ, D), 1.0),
        "cache_k": nrm(ks[2], (DEPTH, DEC_BATCH, WINDOW, ATTN_KV_HEADS, HEAD_DIM), 1.0),
        "cache_v": nrm(ks[3], (DEPTH, DEC_BATCH, WINDOW, ATTN_KV_HEADS, HEAD_DIM), 1.0),
        "state_hgrn": nrm(ks[4], (DEPTH, DEC_BATCH, HG_HEADS, HG_DK, HG_DV), 0.5),
        "norm_ffn1_pre": gain(ks[5], (DEPTH, D)),
        "norm_ffn1_post": gain(ks[6], (DEPTH, D)),
        "w_ffn1_gate": nrm(ks[7], (DEPTH, D, F), D ** -0.5),
        "w_ffn1_up": nrm(ks[8], (DEPTH, D, F), D ** -0.5),
        "w_ffn1_down": nrm(ks[9], (DEPTH, F, D), F ** -0.5),
        "norm_mix_pre": gain(ks[10], (DEPTH, D)),
        "norm_mix_post": gain(ks[11], (DEPTH, D)),
        "w_in": nrm(ks[12], (DEPTH, D, IN_WIDTH), D ** -0.5),
        "attn_sinks": nrm(ks[13], (DEPTH, ATTN_Q_HEADS), 0.5),
        "hgrn_lb_logits": nrm(ks[14], (DEPTH + 1, HG_K_W), 0.1),
        "hgrn_norm": gain(ks[15], (DEPTH, HG_DV)),
        "w_attn_out": nrm(ks[16], (DEPTH, ATTN_Q_W, D), ATTN_Q_W ** -0.5),
        "w_hgrn_out": nrm(ks[17], (DEPTH, HG_V_W, D), HG_V_W ** -0.5),
        "w_out": nrm(ks[18], (DEPTH, D, D), D ** -0.5),
        "norm_ffn2_pre": gain(ks[19], (DEPTH, D)),
        "norm_ffn2_post": gain(ks[20], (DEPTH, D)),
        "w_ffn2_gate": nrm(ks[21], (DEPTH, D, F), D ** -0.5),
        "w_ffn2_up": nrm(ks[22], (DEPTH, D, F), D ** -0.5),
        "w_ffn2_down": nrm(ks[23], (DEPTH, F, D), F ** -0.5),
    }


def reference(x_prompt, x_sample, cache_k, cache_v, state_hgrn,
              norm_ffn1_pre, norm_ffn1_post, w_ffn1_gate, w_ffn1_up, w_ffn1_down,
              norm_mix_pre, norm_mix_post, w_in, attn_sinks, hgrn_lb_logits, hgrn_norm,
              w_attn_out, w_hgrn_out, w_out,
              norm_ffn2_pre, norm_ffn2_post, w_ffn2_gate, w_ffn2_up, w_ffn2_down):
    lb_all = jnp.cumsum(jax.nn.softmax(hgrn_lb_logits.astype(jnp.float32), axis=0), axis=0)
    xp, xs = x_prompt, x_sample
    kp_l, vp_l, sp_l, ks_l, vs_l, ss_l = [], [], [], [], [], []
    for l in range(DEPTH):
        xp = ffn_half(xp, norm_ffn1_pre[l], norm_ffn1_post[l], w_ffn1_gate[l], w_ffn1_up[l], w_ffn1_down[l])
        xs = ffn_half(xs, norm_ffn1_pre[l], norm_ffn1_post[l], w_ffn1_gate[l], w_ffn1_up[l], w_ffn1_down[l])

        s0p = jnp.zeros((xp.shape[0], HG_HEADS, HG_DK, HG_DV), xp.dtype)
        mp, kp, vp, sp = token_mix(rmsnorm(xp, norm_mix_pre[l]), w_in[l], attn_sinks[l], lb_all[l],
                                   hgrn_norm[l], w_attn_out[l], w_hgrn_out[l], w_out[l],
                                   swa_prompt, s0p)
        ck, cv = cache_k[l], cache_v[l]
        samp_attn = lambda q, k, v, sk: swa_sample(q, k, v, ck, cv, sk)
        ms, ksn, vsn, ssn = token_mix(rmsnorm(xs, norm_mix_pre[l]), w_in[l], attn_sinks[l], lb_all[l],
                                      hgrn_norm[l], w_attn_out[l], w_hgrn_out[l], w_out[l],
                                      samp_attn, state_hgrn[l])
        xp = xp + rmsnorm(mp, norm_mix_post[l])
        xs = xs + rmsnorm(ms, norm_mix_post[l])

        xp = ffn_half(xp, norm_ffn2_pre[l], norm_ffn2_post[l], w_ffn2_gate[l], w_ffn2_up[l], w_ffn2_down[l])
        xs = ffn_half(xs, norm_ffn2_pre[l], norm_ffn2_post[l], w_ffn2_gate[l], w_ffn2_up[l], w_ffn2_down[l])
        kp_l.append(kp); vp_l.append(vp); sp_l.append(sp)
        ks_l.append(ksn); vs_l.append(vsn); ss_l.append(ssn)

    new_k_prompt = jnp.stack(kp_l)
    new_v_prompt = jnp.stack(vp_l)
    new_hgrn_prompt = jnp.stack(sp_l)
    new_k_sample = jnp.stack(ks_l)
    new_v_sample = jnp.stack(vs_l)
    new_hgrn_sample = jnp.stack(ss_l)
    return (xp, xs, new_k_prompt, new_v_prompt, new_hgrn_prompt, new_k_sample, new_v_sample, new_hgrn_sample)
```

```python
import functools

import jax
import jax.numpy as jnp
from jax import lax
from jax.experimental import pallas as pl
from jax.experimental.pallas import tpu as pltpu

F32 = jnp.float32
BF16 = jnp.bfloat16

D_MODEL = 1024
FFN_DIM = 2816
HEAD_DIM = 64
ATTN_Q_HEADS = 8
ATTN_KV_HEADS = 2
WINDOW = 128
HG_HEADS = 4
HG_D = 128
EPS = 1e-6

ATTN_Q_W = ATTN_Q_HEADS * HEAD_DIM
ATTN_KV_W = ATTN_KV_HEADS * HEAD_DIM
HG_W = HG_HEADS * HG_D
OFF_QA = 0
OFF_KA = OFF_QA + ATTN_Q_W
OFF_VA = OFF_KA + ATTN_KV_W
OFF_QH = OFF_VA + ATTN_KV_W
OFF_FH = OFF_QH + HG_W
OFF_IH = OFF_FH + HG_W
OFF_GH = OFF_IH + HG_W
OFF_GA = OFF_GH + HG_W
OFF_GB = OFF_GA + D_MODEL
IN_WIDTH = OFF_GB + D_MODEL

LANES = 128
FFN_CHUNK = 256
N_FFN_CHUNKS = FFN_DIM // FFN_CHUNK
FFN_ROWS = 512
MIX_ROWS = 512
HG_CHUNK = 128
SAMPLE_TILE = 8
VMEM_LIMIT = 56 * 1024 * 1024

NT_DIMS = (((1,), (1,)), ((), ()))
TN_DIMS = (((0,), (0,)), ((), ()))


def _rms(x, g):
    return x * lax.rsqrt(jnp.mean(x * x, axis=-1, keepdims=True) + EPS) * g


def _dot(a, b):
    return jnp.dot(a, b, preferred_element_type=F32)


def _const_spec(shape):
    zeros = (0,) * len(shape)
    return pl.BlockSpec(shape, lambda *_: zeros, pipeline_mode=pl.Buffered(1))


def _ffn_tile(x, pre, post, wg_ref, wu_ref, wd_ref):
    h = _rms(x, pre).astype(BF16)
    acc = None
    for c in range(N_FFN_CHUNKS):
        g = _dot(h, wg_ref[c])
        u = _dot(h, wu_ref[c])
        a = (g * jax.nn.sigmoid(g) * u).astype(BF16)
        d = _dot(a, wd_ref[c])
        acc = d if acc is None else acc + d
    return x + 0.5 * _rms(acc, post)


def _ffn_kernel(n_prompt_steps, xp_ref, xs_ref, pre_ref, post_ref, wg_ref, wu_ref, wd_ref,
                yp_ref, ys_ref):
    i = pl.program_id(0)

    @pl.when(i < n_prompt_steps)
    def _():
        yp_ref[...] = _ffn_tile(xp_ref[...], pre_ref[...], post_ref[...], wg_ref, wu_ref, wd_ref)

    @pl.when(i == n_prompt_steps)
    def _():
        ys_ref[...] = _ffn_tile(xs_ref[...], pre_ref[...], post_ref[...], wg_ref, wu_ref, wd_ref)


def _ffn_call(xp, xs, pre, post, wg3, wu3, wd3):
    n_p, n_s = xp.shape[0], xs.shape[0]
    rows = min(FFN_ROWS, n_p)
    steps = n_p // rows
    prompt_spec = pl.BlockSpec((rows, D_MODEL), lambda i: (jnp.minimum(i, steps - 1), 0))
    sample_spec = pl.BlockSpec((n_s, D_MODEL), lambda i: (0, 0))
    return pl.pallas_call(
        functools.partial(_ffn_kernel, steps),
        grid=(steps + 1,),
        in_specs=[prompt_spec, sample_spec,
                  _const_spec((1, D_MODEL)), _const_spec((1, D_MODEL)),
                  _const_spec(wg3.shape), _const_spec(wu3.shape), _const_spec(wd3.shape)],
        out_specs=[prompt_spec, sample_spec],
        out_shape=[jax.ShapeDtypeStruct(xp.shape, F32), jax.ShapeDtypeStruct(xs.shape, F32)],
        compiler_params=pltpu.CompilerParams(
            dimension_semantics=("arbitrary",), vmem_limit_bytes=VMEM_LIMIT),
        name="ffn_half",
    )(xp, xs, pre, post, wg3, wu3, wd3)


def _forget_lower_bound(lbl):
    l0, l1 = lbl[0:1, :], lbl[1:2, :]
    m = jnp.maximum(l0, l1)
    e0, e1 = jnp.exp(l0 - m), jnp.exp(l1 - m)
    return e0 / (e0 + e1)


def _lane_is_low(shape):
    return lax.broadcasted_iota(jnp.int32, shape, len(shape) - 1) < HEAD_DIM


def _dup_kv(x):
    swapped = pltpu.roll(x, HEAD_DIM, axis=1)
    low = _lane_is_low(x.shape)
    return jnp.where(low, x, swapped), jnp.where(low, swapped, x)


def _merge_out(h, w_ref, off, gates_from):
    return jax.nn.sigmoid(gates_from(off, D_MODEL)) * _dot(h.astype(BF16), w_ref[...])


def _hgrn_chunk(q, k, v, g, st):
    c = q.shape[0]
    row = lax.broadcasted_iota(jnp.int32, (c, c), 0)
    col = lax.broadcasted_iota(jnp.int32, (c, c), 1)
    tril = (col <= row).astype(F32)
    b = jnp.dot(tril, g, precision=lax.Precision.HIGHEST, preferred_element_type=F32)
    t = lax.broadcasted_iota(jnp.int32, (c, 1), 0)

    o = lax.dot_general((q * jnp.exp(b)).astype(BF16), st.astype(BF16), NT_DIMS,
                        preferred_element_type=F32)

    a = jnp.zeros((c, c), F32)
    last = b
    n = 1
    while n < c:
        second = (t & n) != 0
        edge = jnp.where(second, pltpu.roll(last, n, axis=0), last)
        w = jnp.exp(-jnp.abs(b - edge))
        qt = jnp.where(second, q * w, 0.0).astype(BF16)
        kt = jnp.where(second, 0.0, k * w).astype(BF16)
        al = lax.dot_general(qt, kt, NT_DIMS, preferred_element_type=F32)
        if 2 * n < c:
            al = jnp.where((row & -(2 * n)) == (col & -(2 * n)), al, 0.0)
        a = a + al
        last = jnp.where(second, last, pltpu.roll(last, c - n, axis=0))
        n *= 2
    diag = jnp.sum(q * k, axis=1, keepdims=True)
    o = o + _dot(a.astype(BF16), v.astype(BF16)) + diag * v

    b_end = b[c - 1:c, :]
    k_end = (k * jnp.exp(b_end - b)).astype(BF16)
    st_new = st * jnp.exp(b_end) + lax.dot_general(v.astype(BF16), k_end, TN_DIMS,
                                                   preferred_element_type=F32)
    return o, st_new


def _mix_prompt_kernel(sinks_ref, x_ref, pre_ref, post_ref, lbl_ref, hgn_ref,
                       win_ref, wao_ref, who_ref, wo_ref,
                       y_ref, nk_ref, nv_ref, ns_ref,
                       kbuf, vbuf, st_ref, qh_ref, kh_ref, vh_ref, lf_ref, oa_ref, oh_ref):
    j = pl.program_id(1)
    rows = x_ref.shape[0]
    n_blocks = rows // WINDOW

    @pl.when(j == 0)
    def _():
        kbuf[0:WINDOW, :] = jnp.zeros((WINDOW, LANES), F32)
        vbuf[0:WINDOW, :] = jnp.zeros((WINDOW, LANES), F32)
        st_ref[...] = jnp.zeros(st_ref.shape, F32)

    x = x_ref[...]
    h = _rms(x, pre_ref[...]).astype(BF16)

    def proj(off, width):
        return _dot(h, win_ref[:, off:off + width])

    qa = proj(OFF_QA, ATTN_Q_W) * (HEAD_DIM ** -0.5)
    ka = proj(OFF_KA, ATTN_KV_W)
    va = proj(OFF_VA, ATTN_KV_W)
    kbuf[WINDOW:WINDOW + rows, :] = ka
    vbuf[WINDOW:WINDOW + rows, :] = va

    qi = lax.broadcasted_iota(jnp.int32, (WINDOW, 2 * WINDOW), 0)
    kj = lax.broadcasted_iota(jnp.int32, (WINDOW, 2 * WINDOW), 1)
    band = (kj > qi) & (kj <= qi + WINDOW)
    low = _lane_is_low((WINDOW, LANES))
    top = lax.broadcasted_iota(jnp.int32, (2 * WINDOW, 1), 0) < WINDOW
    for n in range(n_blocks):
        kd = _dup_kv(kbuf[n * WINDOW:(n + 2) * WINDOW, :])
        vd = _dup_kv(vbuf[n * WINDOW:(n + 2) * WINDOW, :])
        valid = band & ((j > 0) | (kj >= WINDOW)) if n == 0 else band
        valid2 = jnp.concatenate([valid, valid], axis=0)
        for slab in range(ATTN_Q_HEADS // 2):
            kv = (2 * slab) // (ATTN_Q_HEADS // ATTN_KV_HEADS)
            qs = qa[n * WINDOW:(n + 1) * WINDOW, slab * LANES:(slab + 1) * LANES]
            q2 = jnp.concatenate([jnp.where(low, qs, 0.0), jnp.where(low, 0.0, qs)], axis=0)
            s = lax.dot_general(q2.astype(BF16), kd[kv].astype(BF16), NT_DIMS,
                                preferred_element_type=F32)
            s = jnp.where(valid2, s, -jnp.inf)
            sk = jnp.where(top, sinks_ref[0, 2 * slab], sinks_ref[0, 2 * slab + 1])
            m = jnp.maximum(jnp.max(s, axis=1, keepdims=True), sk)
            p = jnp.exp(s - m)
            p = p / (jnp.sum(p, axis=1, keepdims=True) + jnp.exp(sk - m))
            o2 = _dot(p.astype(BF16), vd[kv].astype(BF16))
            oa_ref[n * WINDOW:(n + 1) * WINDOW, slab * LANES:(slab + 1) * LANES] = (
                jnp.where(low, o2[0:WINDOW], o2[WINDOW:2 * WINDOW]))
    kbuf[0:WINDOW, :] = kbuf[rows:rows + WINDOW, :]
    vbuf[0:WINDOW, :] = vbuf[rows:rows + WINDOW, :]

    @pl.when(j == pl.num_programs(1) - 1)
    def _():
        nk_ref[0] = ka[rows - WINDOW:rows, :]
        nv_ref[0] = va[rows - WINDOW:rows, :]

    lb = _forget_lower_bound(lbl_ref[...])
    qh = proj(OFF_QH, HG_W)
    qh_ref[...] = qh * jax.nn.sigmoid(qh)
    fp = proj(OFF_FH, HG_W)
    lf_ref[...] = jnp.log(lb + (1.0 - lb) * jax.nn.sigmoid(fp))
    kh_ref[...] = (1.0 - lb) * jax.nn.sigmoid(-fp)
    vh_ref[...] = proj(OFF_IH, HG_W)
    hgn = hgn_ref[...]

    def chunk_body(ci, carry):
        r = pl.ds(pl.multiple_of(ci * HG_CHUNK, HG_CHUNK), HG_CHUNK)
        for hd in range(HG_HEADS):
            cs = slice(hd * HG_D, (hd + 1) * HG_D)
            o, st_new = _hgrn_chunk(qh_ref[r, cs], kh_ref[r, cs], vh_ref[r, cs], lf_ref[r, cs],
                                    st_ref[cs, :])
            st_ref[cs, :] = st_new
            oh_ref[r, cs] = _rms(o, hgn)
        return carry

    lax.fori_loop(0, rows // HG_CHUNK, chunk_body, 0)

    @pl.when(j == pl.num_programs(1) - 1)
    def _():
        for hd in range(HG_HEADS):
            cs = slice(hd * HG_D, (hd + 1) * HG_D)
            ns_ref[0, cs, :] = st_ref[cs, :].T

    gh = proj(OFF_GH, HG_W)
    oh = oh_ref[...] * (gh * jax.nn.sigmoid(gh))

    m = (_merge_out(oa_ref[...], wao_ref, OFF_GA, proj)
         + _merge_out(oh, who_ref, OFF_GB, proj))
    y_ref[...] = x + _rms(_dot(m.astype(BF16), wo_ref[...]), post_ref[...])


def _mix_prompt_call(x, batch, sinks, pre, post, lbl, hgn, win, wao, who, wo):
    seq = x.shape[0] // batch
    rows = min(MIX_ROWS, seq)
    steps = seq // rows
    x_spec = pl.BlockSpec((rows, D_MODEL), lambda b, j: (b * steps + j, 0))
    kv_spec = pl.BlockSpec((1, WINDOW, ATTN_KV_W), lambda b, j: (b, 0, 0))
    st_spec = pl.BlockSpec((1, HG_W, HG_D), lambda b, j: (b, 0, 0))
    act = pltpu.VMEM((rows, HG_W), F32)
    return pl.pallas_call(
        _mix_prompt_kernel,
        grid=(batch, steps),
        in_specs=[pl.BlockSpec(memory_space=pltpu.SMEM), x_spec,
                  _const_spec((1, D_MODEL)), _const_spec((1, D_MODEL)),
                  _const_spec(lbl.shape), _const_spec((1, HG_D)),
                  _const_spec(win.shape), _const_spec(wao.shape), _const_spec(who.shape),
                  _const_spec(wo.shape)],
        out_specs=[x_spec, kv_spec, kv_spec, st_spec],
        out_shape=[jax.ShapeDtypeStruct(x.shape, F32),
                   jax.ShapeDtypeStruct((batch, WINDOW, ATTN_KV_W), F32),
                   jax.ShapeDtypeStruct((batch, WINDOW, ATTN_KV_W), F32),
                   jax.ShapeDtypeStruct((batch, HG_W, HG_D), F32)],
        scratch_shapes=[pltpu.VMEM((rows + WINDOW, ATTN_KV_W), F32),
                        pltpu.VMEM((rows + WINDOW, ATTN_KV_W), F32),
                        pltpu.VMEM((HG_W, HG_D), F32),
                        act, act, act, act,
                        pltpu.VMEM((rows, ATTN_Q_W), F32), act],
        compiler_params=pltpu.CompilerParams(
            dimension_semantics=("arbitrary", "arbitrary"), vmem_limit_bytes=VMEM_LIMIT),
        name="mix_prompt",
    )(sinks, x, pre, post, lbl, hgn, win, wao, who, wo)


def _row_select(rows_list):
    n = -(-len(rows_list) // 8) * 8
    lanes = rows_list[0].shape[1]
    ridx = lax.broadcasted_iota(jnp.int32, (n, lanes), 0)
    out = jnp.zeros((n, lanes), F32)
    for i, r in enumerate(rows_list):
        out = jnp.where(ridx == i, jnp.broadcast_to(r, (n, lanes)), out)
    return out


def _mix_sample_kernel(sinks_ref, x_ref, ck_ref, cv_ref, s0_ref, pre_ref, post_ref, lbl_ref, hgn_ref,
                       win_ref, wao_ref, who_ref, wo_ref,
                       y_ref, nk_ref, nv_ref, ns_ref,
                       h_ref, qa_ref, ka_ref, va_ref, qh_ref, kh_ref, fh_ref, vh_ref, oa_ref, oh_ref):
    i = pl.program_id(0)
    tile = ck_ref.shape[0]

    def proj(off, width):
        return _dot(h_ref[...], win_ref[:, off:off + width])

    @pl.when(i == 0)
    def _():
        h_ref[...] = _rms(x_ref[...], pre_ref[...]).astype(BF16)
        qa_ref[...] = proj(OFF_QA, ATTN_Q_W) * (HEAD_DIM ** -0.5)
        ka_ref[...] = proj(OFF_KA, ATTN_KV_W)
        va_ref[...] = proj(OFF_VA, ATTN_KV_W)
        lb = _forget_lower_bound(lbl_ref[...])
        qh = proj(OFF_QH, HG_W)
        qh_ref[...] = qh * jax.nn.sigmoid(qh)
        fp = proj(OFF_FH, HG_W)
        fh_ref[...] = jnp.exp(jnp.log(lb + (1.0 - lb) * jax.nn.sigmoid(fp)))
        kh_ref[...] = (1.0 - lb) * jax.nn.sigmoid(-fp)
        vh_ref[...] = proj(OFF_IH, HG_W)

    low = _lane_is_low((1, LANES))
    pos = lax.broadcasted_iota(jnp.int32, (WINDOW, LANES), 0)
    hrow = lax.broadcasted_iota(jnp.int32, (ATTN_Q_HEADS, 1), 0)
    sk = jnp.zeros((ATTN_Q_HEADS, 1), F32)
    for hd in range(ATTN_Q_HEADS):
        sk = jnp.where(hrow == hd, sinks_ref[0, hd], sk)
    group = ATTN_Q_HEADS // ATTN_KV_HEADS
    hgn = hgn_ref[...]

    r8 = pl.ds(pl.multiple_of(i * tile, tile), tile)
    qa8, ka8, va8 = qa_ref[r8, :], ka_ref[r8, :], va_ref[r8, :]
    fh8, kh8, qh8, vh8 = fh_ref[r8, :], kh_ref[r8, :], qh_ref[r8, :], vh_ref[r8, :]
    oa_rows = [[] for _ in range(ATTN_Q_HEADS // 2)]
    oh_rows = [[] for _ in range(HG_HEADS)]
    for bi in range(tile):
        r1 = slice(bi, bi + 1)
        kw = jnp.where(pos == WINDOW - 1, ka8[r1, :], pltpu.roll(ck_ref[bi], WINDOW - 1, axis=0))
        vw = jnp.where(pos == WINDOW - 1, va8[r1, :], pltpu.roll(cv_ref[bi], WINDOW - 1, axis=0))
        nk_ref[bi] = kw
        nv_ref[bi] = vw
        qrows = []
        for hd in range(ATTN_Q_HEADS):
            slab = qa8[r1, (hd // 2) * LANES:(hd // 2 + 1) * LANES]
            in_place = (hd % 2) == (hd // group)
            src = slab if in_place else pltpu.roll(slab, HEAD_DIM, axis=1)
            on_kv_lanes = low if hd // group == 0 else jnp.logical_not(low)
            qrows.append(jnp.where(on_kv_lanes, src, 0.0))
        q8 = _row_select(qrows)
        s = lax.dot_general(q8.astype(BF16), kw.astype(BF16), NT_DIMS, preferred_element_type=F32)
        m = jnp.maximum(jnp.max(s, axis=1, keepdims=True), sk)
        p = jnp.exp(s - m)
        p = p / (jnp.sum(p, axis=1, keepdims=True) + jnp.exp(sk - m))
        o8 = _dot(p.astype(BF16), vw.astype(BF16))
        for slab in range(ATTN_Q_HEADS // 2):
            kv = (2 * slab) // group
            even, odd = o8[2 * slab:2 * slab + 1, :], o8[2 * slab + 1:2 * slab + 2, :]
            if kv == 0:
                out = jnp.where(low, even, pltpu.roll(odd, HEAD_DIM, axis=1))
            else:
                out = jnp.where(low, pltpu.roll(even, HEAD_DIM, axis=1), odd)
            oa_rows[slab].append(out)

        vecs = [a8[r1, hd * HG_D:(hd + 1) * HG_D] for a8 in (fh8, kh8, qh8) for hd in range(HG_HEADS)]
        packed = _row_select(vecs)
        square = jnp.concatenate([packed, jnp.zeros((HG_D - packed.shape[0], HG_D), F32)], axis=0)
        cols = square.T
        for hd in range(HG_HEADS):
            def col(c):
                return jnp.broadcast_to(cols[:, c:c + 1], (HG_D, HG_D))
            rs = slice(hd * HG_D, (hd + 1) * HG_D)
            s_new = col(hd) * s0_ref[bi, rs, :] + col(HG_HEADS + hd) * vh8[r1, rs]
            ns_ref[bi, rs, :] = s_new
            o = jnp.sum(col(2 * HG_HEADS + hd) * s_new, axis=0, keepdims=True)
            oh_rows[hd].append(_rms(o, hgn))
    for slab in range(ATTN_Q_HEADS // 2):
        oa_ref[r8, slab * LANES:(slab + 1) * LANES] = _row_select(oa_rows[slab])
    for hd in range(HG_HEADS):
        oh_ref[r8, hd * HG_D:(hd + 1) * HG_D] = _row_select(oh_rows[hd])

    @pl.when(i == pl.num_programs(0) - 1)
    def _():
        gh = proj(OFF_GH, HG_W)
        oh = oh_ref[...] * (gh * jax.nn.sigmoid(gh))
        m = (_merge_out(oa_ref[...], wao_ref, OFF_GA, proj)
             + _merge_out(oh, who_ref, OFF_GB, proj))
        y_ref[...] = x_ref[...] + _rms(_dot(m.astype(BF16), wo_ref[...]), post_ref[...])


def _mix_sample_call(x, ck, cv, s0, sinks, pre, post, lbl, hgn, win, wao, who, wo):
    n = x.shape[0]
    tile = min(SAMPLE_TILE, n)
    full = pl.BlockSpec((n, D_MODEL), lambda i: (0, 0))
    kv_spec = pl.BlockSpec((tile, WINDOW, ATTN_KV_W), lambda i: (i, 0, 0))
    st_spec = pl.BlockSpec((tile, HG_W, HG_D), lambda i: (i, 0, 0))
    act = pltpu.VMEM((n, HG_W), F32)
    kv_act = pltpu.VMEM((n, ATTN_KV_W), F32)
    return pl.pallas_call(
        _mix_sample_kernel,
        grid=(n // tile,),
        in_specs=[pl.BlockSpec(memory_space=pltpu.SMEM), full, kv_spec, kv_spec, st_spec,
                  _const_spec((1, D_MODEL)), _const_spec((1, D_MODEL)),
                  _const_spec(lbl.shape), _const_spec((1, HG_D)),
                  _const_spec(win.shape), _const_spec(wao.shape), _const_spec(who.shape),
                  _const_spec(wo.shape)],
        out_specs=[full, kv_spec, kv_spec, st_spec],
        out_shape=[jax.ShapeDtypeStruct(x.shape, F32),
                   jax.ShapeDtypeStruct(ck.shape, F32),
                   jax.ShapeDtypeStruct(cv.shape, F32),
                   jax.ShapeDtypeStruct(s0.shape, F32)],
        scratch_shapes=[pltpu.VMEM((n, D_MODEL), BF16),
                        pltpu.VMEM((n, ATTN_Q_W), F32), kv_act, kv_act,
                        act, act, act, act,
                        pltpu.VMEM((n, ATTN_Q_W), F32), act],
        compiler_params=pltpu.CompilerParams(
            dimension_semantics=("arbitrary",), vmem_limit_bytes=VMEM_LIMIT),
        name="mix_sample",
    )(sinks, x, ck, cv, s0, pre, post, lbl, hgn, win, wao, who, wo)


def _ffn_weights(wg, wu, wd):
    def cols(w):
        return w.astype(BF16).reshape(D_MODEL, N_FFN_CHUNKS, FFN_CHUNK).transpose(1, 0, 2)
    return cols(wg), cols(wu), wd.astype(BF16).reshape(N_FFN_CHUNKS, FFN_CHUNK, D_MODEL)


def kernel(x_prompt, x_sample, cache_k, cache_v, state_hgrn, norm_ffn1_pre, norm_ffn1_post, w_ffn1_gate, w_ffn1_up, w_ffn1_down, norm_mix_pre, norm_mix_post, w_in, attn_sinks, hgrn_lb_logits, hgrn_norm, w_attn_out, w_hgrn_out, w_out, norm_ffn2_pre, norm_ffn2_post, w_ffn2_gate, w_ffn2_up, w_ffn2_down):
    depth = w_in.shape[0]
    assert depth == 1 and hgrn_lb_logits.shape[0] == 2, "single-layer stack only"
    batch, seq, _ = x_prompt.shape
    n_s = x_sample.shape[0]
    assert x_sample.shape[1] == 1 and seq % WINDOW == 0

    xp = x_prompt.reshape(batch * seq, D_MODEL)
    xs = x_sample.reshape(n_s, D_MODEL)
    ck = cache_k[0].reshape(n_s, WINDOW, ATTN_KV_W)
    cv = cache_v[0].reshape(n_s, WINDOW, ATTN_KV_W)
    s0 = state_hgrn[0].reshape(n_s, HG_W, HG_D)

    xp, xs = _ffn_call(xp, xs, norm_ffn1_pre, norm_ffn1_post,
                       *_ffn_weights(w_ffn1_gate[0], w_ffn1_up[0], w_ffn1_down[0]))

    mix_w = (attn_sinks, norm_mix_pre, norm_mix_post, hgrn_lb_logits, hgrn_norm,
             w_in[0].astype(BF16), w_attn_out[0].astype(BF16), w_hgrn_out[0].astype(BF16),
             w_out[0].astype(BF16))
    xp, nkp, nvp, nsp = _mix_prompt_call(xp, batch, *mix_w)
    xs, nks, nvs, nss = _mix_sample_call(xs, ck, cv, s0, *mix_w)

    xp, xs = _ffn_call(xp, xs, norm_ffn2_pre, norm_ffn2_post,
                       *_ffn_weights(w_ffn2_gate[0], w_ffn2_up[0], w_ffn2_down[0]))

    kv_shape = (1, -1, WINDOW, ATTN_KV_HEADS, HEAD_DIM)
    st_shape = (1, -1, HG_HEADS, HG_D, HG_D)
    return (xp.reshape(batch, seq, D_MODEL), xs.reshape(n_s, 1, D_MODEL),
            nkp.reshape(kv_shape), nvp.reshape(kv_shape), nsp.reshape(st_shape),
            nks.reshape(kv_shape), nvs.reshape(kv_shape), nss.reshape(st_shape))
```

```python
import functools

import jax
import jax.numpy as jnp
from jax import lax
from jax.experimental import pallas as pl
from jax.experimental.pallas import tpu as pltpu

F32 = jnp.float32
BF16 = jnp.bfloat16

D_MODEL = 1024
FFN_DIM = 2816
HEAD_DIM = 64
ATTN_Q_HEADS = 8
ATTN_KV_HEADS = 2
WINDOW = 128
HG_HEADS = 4
HG_D = 128
EPS = 1e-6

ATTN_Q_W = ATTN_Q_HEADS * HEAD_DIM
ATTN_KV_W = ATTN_KV_HEADS * HEAD_DIM
HG_W = HG_HEADS * HG_D
OFF_QA = 0
OFF_KA = OFF_QA + ATTN_Q_W
OFF_VA = OFF_KA + ATTN_KV_W
OFF_QH = OFF_VA + ATTN_KV_W
OFF_FH = OFF_QH + HG_W
OFF_IH = OFF_FH + HG_W
OFF_GH = OFF_IH + HG_W
OFF_GA = OFF_GH + HG_W
OFF_GB = OFF_GA + D_MODEL
IN_WIDTH = OFF_GB + D_MODEL

LANES = 128
FFN_CHUNK = 256
N_FFN_CHUNKS = FFN_DIM // FFN_CHUNK
FFN_ROWS = 1024
MIX_ROWS = 512
HG_CHUNK = 128
SAMPLE_TILE = 8
VMEM_LIMIT = 56 * 1024 * 1024

NT_DIMS = (((1,), (1,)), ((), ()))
TN_DIMS = (((0,), (0,)), ((), ()))


def _rms(x, g):
    return x * lax.rsqrt(jnp.mean(x * x, axis=-1, keepdims=True) + EPS) * g


def _dot(a, b):
    return jnp.dot(a, b, preferred_element_type=F32)


def _const_spec(shape):
    zeros = (0,) * len(shape)
    return pl.BlockSpec(shape, lambda *_: zeros, pipeline_mode=pl.Buffered(1))


def _ffn_tile(x, pre, post, wg_ref, wu_ref, wd_ref):
    h = _rms(x, pre).astype(BF16)
    acc = None
    for c in range(N_FFN_CHUNKS):
        g = _dot(h, wg_ref[c])
        u = _dot(h, wu_ref[c])
        a = (g * jax.nn.sigmoid(g) * u).astype(BF16)
        d = _dot(a, wd_ref[c])
        acc = d if acc is None else acc + d
    return x + 0.5 * _rms(acc, post)


def _ffn_kernel(n_prompt_steps, xp_ref, xs_ref, pre_ref, post_ref, wg_ref, wu_ref, wd_ref,
                yp_ref, ys_ref):
    i = pl.program_id(0)

    @pl.when(i < n_prompt_steps)
    def _():
        yp_ref[...] = _ffn_tile(xp_ref[...], pre_ref[...], post_ref[...], wg_ref, wu_ref, wd_ref)

    @pl.when(i == n_prompt_steps)
    def _():
        ys_ref[...] = _ffn_tile(xs_ref[...], pre_ref[...], post_ref[...], wg_ref, wu_ref, wd_ref)


def _ffn_call(xp, xs, pre, post, wg3, wu3, wd3):
    n_p, n_s = xp.shape[0], xs.shape[0]
    rows = min(FFN_ROWS, n_p)
    steps = n_p // rows
    prompt_spec = pl.BlockSpec((rows, D_MODEL), lambda i: (jnp.minimum(i, steps - 1), 0))
    sample_spec = pl.BlockSpec((n_s, D_MODEL), lambda i: (0, 0))
    return pl.pallas_call(
        functools.partial(_ffn_kernel, steps),
        grid=(steps + 1,),
        in_specs=[prompt_spec, sample_spec,
                  _const_spec((1, D_MODEL)), _const_spec((1, D_MODEL)),
                  _const_spec(wg3.shape), _const_spec(wu3.shape), _const_spec(wd3.shape)],
        out_specs=[prompt_spec, sample_spec],
        out_shape=[jax.ShapeDtypeStruct(xp.shape, F32), jax.ShapeDtypeStruct(xs.shape, F32)],
        compiler_params=pltpu.CompilerParams(
            dimension_semantics=("arbitrary",), vmem_limit_bytes=VMEM_LIMIT),
        name="ffn_half",
    )(xp, xs, pre, post, wg3, wu3, wd3)


def _forget_lower_bound(lbl):
    l0, l1 = lbl[0:1, :], lbl[1:2, :]
    m = jnp.maximum(l0, l1)
    e0, e1 = jnp.exp(l0 - m), jnp.exp(l1 - m)
    return e0 / (e0 + e1)


def _lane_is_low(shape):
    return lax.broadcasted_iota(jnp.int32, shape, len(shape) - 1) < HEAD_DIM


def _dup_kv(x):
    swapped = pltpu.roll(x, HEAD_DIM, axis=1)
    low = _lane_is_low(x.shape)
    return jnp.where(low, x, swapped), jnp.where(low, swapped, x)


def _merge_out(h, w_ref, off, gates_from):
    return jax.nn.sigmoid(gates_from(off, D_MODEL)) * _dot(h.astype(BF16), w_ref[...])


def _pair_levels(c):
    t = jnp.arange(c, dtype=jnp.int32)[:, None]
    s = jnp.arange(c, dtype=jnp.int32)[None, :]
    x = jnp.maximum(t ^ s, 1)
    lvl = (31 - lax.clz(x)).astype(jnp.int32)
    return jnp.where(t > s, lvl, -1)


LOG2E = 1.4426950408889634


def _head(a, hd):
    return a[:, hd * HG_D:(hd + 1) * HG_D]


def _hgrn_chunk(q, k, v, g, st, levels):
    c = q.shape[0]
    heads = range(HG_HEADS)
    row = lax.broadcasted_iota(jnp.int32, (c, c), 0)
    col = lax.broadcasted_iota(jnp.int32, (c, c), 1)
    tril = (col <= row).astype(F32)
    b = jnp.dot(tril, g, precision=lax.Precision.HIGHEST, preferred_element_type=F32) * LOG2E
    t = lax.broadcasted_iota(jnp.int32, (c, 1), 0)

    qe = (q * jnp.exp2(b)).astype(BF16)
    o = [lax.dot_general(_head(qe, hd), st[hd * HG_D:(hd + 1) * HG_D, :].astype(BF16), NT_DIMS,
                         preferred_element_type=F32) for hd in heads]

    a = [jnp.zeros((c, c), F32) for _ in heads]
    last = b
    n, lvl = 1, 0
    while n < c:
        second = (t & n) != 0
        if n < 8:
            edge = jnp.where(second, pltpu.roll(last, n, axis=0), last)
            if 2 * n < 8:
                last = jnp.where(second, last, pltpu.roll(last, c - n, axis=0))
        else:
            edge = jnp.concatenate(
                [jnp.broadcast_to(b[p + n - 1:p + n, :], (2 * n, b.shape[1]))
                 for p in range(0, c, 2 * n)], axis=0)
        w = jnp.exp2(jnp.where(second, b - edge, edge - b))
        x = (jnp.where(second, q, k) * w).astype(BF16)
        for hd in heads:
            al = lax.dot_general(_head(x, hd), _head(x, hd), NT_DIMS, preferred_element_type=F32)
            a[hd] = jnp.where(levels == lvl, al, a[hd])
        n, lvl = 2 * n, lvl + 1

    qk = q * k
    b_end = b[c - 1:c, :]
    k_end = (k * jnp.exp2(b_end - b)).astype(BF16)
    decay = jnp.exp2(b_end)
    vb = v.astype(BF16)
    outs, states = [], []
    for hd in heads:
        diag = jnp.sum(_head(qk, hd), axis=1, keepdims=True)
        outs.append(o[hd] + _dot(a[hd].astype(BF16), _head(vb, hd)) + diag * _head(v, hd))
        states.append(st[hd * HG_D:(hd + 1) * HG_D, :] * _head(decay, hd)
                      + lax.dot_general(_head(vb, hd), _head(k_end, hd), TN_DIMS,
                                        preferred_element_type=F32))
    return outs, states


def _mix_prompt_kernel(sinks_ref, x_ref, pre_ref, post_ref, lbl_ref, hgn_ref, lvl_ref,
                       win_ref, wao_ref, who_ref, wo_ref,
                       y_ref, nk_ref, nv_ref, ns_ref,
                       kbuf, vbuf, st_ref, qh_ref, kh_ref, vh_ref, lf_ref, oa_ref, oh_ref):
    j = pl.program_id(1)
    rows = x_ref.shape[0]
    n_blocks = rows // WINDOW

    @pl.when(j == 0)
    def _():
        kbuf[0:WINDOW, :] = jnp.zeros((WINDOW, LANES), F32)
        vbuf[0:WINDOW, :] = jnp.zeros((WINDOW, LANES), F32)
        st_ref[...] = jnp.zeros(st_ref.shape, F32)

    x = x_ref[...]
    h = _rms(x, pre_ref[...]).astype(BF16)

    def proj(off, width):
        return _dot(h, win_ref[:, off:off + width])

    qa = proj(OFF_QA, ATTN_Q_W) * (HEAD_DIM ** -0.5)
    ka = proj(OFF_KA, ATTN_KV_W)
    va = proj(OFF_VA, ATTN_KV_W)
    kbuf[WINDOW:WINDOW + rows, :] = ka
    vbuf[WINDOW:WINDOW + rows, :] = va

    qi = lax.broadcasted_iota(jnp.int32, (WINDOW, 2 * WINDOW), 0)
    kj = lax.broadcasted_iota(jnp.int32, (WINDOW, 2 * WINDOW), 1)
    band = (kj > qi) & (kj <= qi + WINDOW)
    low = _lane_is_low((WINDOW, LANES))
    top = lax.broadcasted_iota(jnp.int32, (2 * WINDOW, 1), 0) < WINDOW
    for n in range(n_blocks):
        kd = _dup_kv(kbuf[n * WINDOW:(n + 2) * WINDOW, :])
        vd = _dup_kv(vbuf[n * WINDOW:(n + 2) * WINDOW, :])
        valid = band & ((j > 0) | (kj >= WINDOW)) if n == 0 else band
        valid2 = jnp.concatenate([valid, valid], axis=0)
        for slab in range(ATTN_Q_HEADS // 2):
            kv = (2 * slab) // (ATTN_Q_HEADS // ATTN_KV_HEADS)
            qs = qa[n * WINDOW:(n + 1) * WINDOW, slab * LANES:(slab + 1) * LANES]
            q2 = jnp.concatenate([jnp.where(low, qs, 0.0), jnp.where(low, 0.0, qs)], axis=0)
            s = lax.dot_general(q2.astype(BF16), kd[kv].astype(BF16), NT_DIMS,
                                preferred_element_type=F32)
            s = jnp.where(valid2, s, -jnp.inf)
            sk = jnp.where(top, sinks_ref[0, 2 * slab], sinks_ref[0, 2 * slab + 1])
            m = jnp.maximum(jnp.max(s, axis=1, keepdims=True), sk)
            p = jnp.exp(s - m)
            p = p / (jnp.sum(p, axis=1, keepdims=True) + jnp.exp(sk - m))
            o2 = _dot(p.astype(BF16), vd[kv].astype(BF16))
            oa_ref[n * WINDOW:(n + 1) * WINDOW, slab * LANES:(slab + 1) * LANES] = (
                jnp.where(low, o2[0:WINDOW], o2[WINDOW:2 * WINDOW]))
    kbuf[0:WINDOW, :] = kbuf[rows:rows + WINDOW, :]
    vbuf[0:WINDOW, :] = vbuf[rows:rows + WINDOW, :]

    @pl.when(j == pl.num_programs(1) - 1)
    def _():
        nk_ref[0] = ka[rows - WINDOW:rows, :]
        nv_ref[0] = va[rows - WINDOW:rows, :]

    lb = _forget_lower_bound(lbl_ref[...])
    qh = proj(OFF_QH, HG_W)
    qh_ref[...] = qh * jax.nn.sigmoid(qh)
    fp = proj(OFF_FH, HG_W)
    lf_ref[...] = jnp.log(lb + (1.0 - lb) * jax.nn.sigmoid(fp))
    kh_ref[...] = (1.0 - lb) * jax.nn.sigmoid(-fp)
    vh_ref[...] = proj(OFF_IH, HG_W)
    hgn = hgn_ref[...]

    def chunk_body(ci, carry):
        r = pl.ds(pl.multiple_of(ci * HG_CHUNK, HG_CHUNK), HG_CHUNK)
        outs, states = _hgrn_chunk(qh_ref[r, :], kh_ref[r, :], vh_ref[r, :], lf_ref[r, :],
                                   st_ref[...], lvl_ref[...])
        for hd in range(HG_HEADS):
            cs = slice(hd * HG_D, (hd + 1) * HG_D)
            st_ref[cs, :] = states[hd]
            oh_ref[r, cs] = _rms(outs[hd], hgn)
        return carry

    lax.fori_loop(0, rows // HG_CHUNK, chunk_body, 0, unroll=True)

    @pl.when(j == pl.num_programs(1) - 1)
    def _():
        for hd in range(HG_HEADS):
            cs = slice(hd * HG_D, (hd + 1) * HG_D)
            ns_ref[0, cs, :] = st_ref[cs, :].T

    gh = proj(OFF_GH, HG_W)
    oh = oh_ref[...] * (gh * jax.nn.sigmoid(gh))

    m = (_merge_out(oa_ref[...], wao_ref, OFF_GA, proj)
         + _merge_out(oh, who_ref, OFF_GB, proj))
    y_ref[...] = x + _rms(_dot(m.astype(BF16), wo_ref[...]), post_ref[...])


def _mix_prompt_call(x, batch, sinks, pre, post, lbl, hgn, win, wao, who, wo):
    seq = x.shape[0] // batch
    rows = min(MIX_ROWS, seq)
    steps = seq // rows
    x_spec = pl.BlockSpec((rows, D_MODEL), lambda b, j: (b * steps + j, 0))
    kv_spec = pl.BlockSpec((1, WINDOW, ATTN_KV_W), lambda b, j: (b, 0, 0))
    st_spec = pl.BlockSpec((1, HG_W, HG_D), lambda b, j: (b, 0, 0))
    act = pltpu.VMEM((rows, HG_W), F32)
    return pl.pallas_call(
        _mix_prompt_kernel,
        grid=(batch, steps),
        in_specs=[pl.BlockSpec(memory_space=pltpu.SMEM), x_spec,
                  _const_spec((1, D_MODEL)), _const_spec((1, D_MODEL)),
                  _const_spec(lbl.shape), _const_spec((1, HG_D)),
                  _const_spec((HG_CHUNK, HG_CHUNK)),
                  _const_spec(win.shape), _const_spec(wao.shape), _const_spec(who.shape),
                  _const_spec(wo.shape)],
        out_specs=[x_spec, kv_spec, kv_spec, st_spec],
        out_shape=[jax.ShapeDtypeStruct(x.shape, F32),
                   jax.ShapeDtypeStruct((batch, WINDOW, ATTN_KV_W), F32),
                   jax.ShapeDtypeStruct((batch, WINDOW, ATTN_KV_W), F32),
                   jax.ShapeDtypeStruct((batch, HG_W, HG_D), F32)],
        scratch_shapes=[pltpu.VMEM((rows + WINDOW, ATTN_KV_W), F32),
                        pltpu.VMEM((rows + WINDOW, ATTN_KV_W), F32),
                        pltpu.VMEM((HG_W, HG_D), F32),
                        act, act, act, act,
                        pltpu.VMEM((rows, ATTN_Q_W), F32), act],
        compiler_params=pltpu.CompilerParams(
            dimension_semantics=("arbitrary", "arbitrary"), vmem_limit_bytes=VMEM_LIMIT),
        name="mix_prompt",
    )(sinks, x, pre, post, lbl, hgn, _pair_levels(HG_CHUNK), win, wao, who, wo)


def _row_select(rows_list):
    n = -(-len(rows_list) // 8) * 8
    lanes = rows_list[0].shape[1]
    ridx = lax.broadcasted_iota(jnp.int32, (n, lanes), 0)
    out = jnp.zeros((n, lanes), F32)
    for i, r in enumerate(rows_list):
        out = jnp.where(ridx == i, jnp.broadcast_to(r, (n, lanes)), out)
    return out


def _mix_sample_kernel(sinks_ref, x_ref, ck_ref, cv_ref, s0_ref, pre_ref, post_ref, lbl_ref, hgn_ref,
                       win_ref, wao_ref, who_ref, wo_ref,
                       y_ref, nk_ref, nv_ref, ns_ref,
                       h_ref, qa_ref, ka_ref, va_ref, qh_ref, kh_ref, fh_ref, vh_ref, oa_ref, oh_ref):
    i = pl.program_id(0)
    tile = ck_ref.shape[0]

    def proj(off, width):
        return _dot(h_ref[...], win_ref[:, off:off + width])

    @pl.when(i == 0)
    def _():
        h_ref[...] = _rms(x_ref[...], pre_ref[...]).astype(BF16)
        qa_ref[...] = proj(OFF_QA, ATTN_Q_W) * (HEAD_DIM ** -0.5)
        ka_ref[...] = proj(OFF_KA, ATTN_KV_W)
        va_ref[...] = proj(OFF_VA, ATTN_KV_W)
        lb = _forget_lower_bound(lbl_ref[...])
        qh = proj(OFF_QH, HG_W)
        qh_ref[...] = qh * jax.nn.sigmoid(qh)
        fp = proj(OFF_FH, HG_W)
        fh_ref[...] = jnp.exp(jnp.log(lb + (1.0 - lb) * jax.nn.sigmoid(fp)))
        kh_ref[...] = (1.0 - lb) * jax.nn.sigmoid(-fp)
        vh_ref[...] = proj(OFF_IH, HG_W)

    low = _lane_is_low((1, LANES))
    pos = lax.broadcasted_iota(jnp.int32, (WINDOW, LANES), 0)
    hrow = lax.broadcasted_iota(jnp.int32, (ATTN_Q_HEADS, 1), 0)
    sk = jnp.zeros((ATTN_Q_HEADS, 1), F32)
    for hd in range(ATTN_Q_HEADS):
        sk = jnp.where(hrow == hd, sinks_ref[0, hd], sk)
    group = ATTN_Q_HEADS // ATTN_KV_HEADS
    hgn = hgn_ref[...]

    r8 = pl.ds(pl.multiple_of(i * tile, tile), tile)
    qa8, ka8, va8 = qa_ref[r8, :], ka_ref[r8, :], va_ref[r8, :]
    fh8, kh8, qh8, vh8 = fh_ref[r8, :], kh_ref[r8, :], qh_ref[r8, :], vh_ref[r8, :]
    oa_rows = [[] for _ in range(ATTN_Q_HEADS // 2)]
    oh_rows = [[] for _ in range(HG_HEADS)]
    for bi in range(tile):
        r1 = slice(bi, bi + 1)
        kw = jnp.where(pos == WINDOW - 1, ka8[r1, :], pltpu.roll(ck_ref[bi], WINDOW - 1, axis=0))
        vw = jnp.where(pos == WINDOW - 1, va8[r1, :], pltpu.roll(cv_ref[bi], WINDOW - 1, axis=0))
        nk_ref[bi] = kw
        nv_ref[bi] = vw
        qrows = []
        for hd in range(ATTN_Q_HEADS):
            slab = qa8[r1, (hd // 2) * LANES:(hd // 2 + 1) * LANES]
            in_place = (hd % 2) == (hd // group)
            src = slab if in_place else pltpu.roll(slab, HEAD_DIM, axis=1)
            on_kv_lanes = low if hd // group == 0 else jnp.logical_not(low)
            qrows.append(jnp.where(on_kv_lanes, src, 0.0))
        q8 = _row_select(qrows)
        s = lax.dot_general(q8.astype(BF16), kw.astype(BF16), NT_DIMS, preferred_element_type=F32)
        m = jnp.maximum(jnp.max(s, axis=1, keepdims=True), sk)
        p = jnp.exp(s - m)
        p = p / (jnp.sum(p, axis=1, keepdims=True) + jnp.exp(sk - m))
        o8 = _dot(p.astype(BF16), vw.astype(BF16))
        for slab in range(ATTN_Q_HEADS // 2):
            kv = (2 * slab) // group
            even, odd = o8[2 * slab:2 * slab + 1, :], o8[2 * slab + 1:2 * slab + 2, :]
            if kv == 0:
                out = jnp.where(low, even, pltpu.roll(odd, HEAD_DIM, axis=1))
            else:
                out = jnp.where(low, pltpu.roll(even, HEAD_DIM, axis=1), odd)
            oa_rows[slab].append(out)

        vecs = [a8[r1, hd * HG_D:(hd + 1) * HG_D] for a8 in (fh8, kh8, qh8) for hd in range(HG_HEADS)]
        packed = _row_select(vecs)
        square = jnp.concatenate([packed, jnp.zeros((HG_D - packed.shape[0], HG_D), F32)], axis=0)
        cols = square.T
        for hd in range(HG_HEADS):
            def col(c):
                return jnp.broadcast_to(cols[:, c:c + 1], (HG_D, HG_D))
            rs = slice(hd * HG_D, (hd + 1) * HG_D)
            s_new = col(hd) * s0_ref[bi, rs, :] + col(HG_HEADS + hd) * vh8[r1, rs]
            ns_ref[bi, rs, :] = s_new
            o = jnp.sum(col(2 * HG_HEADS + hd) * s_new, axis=0, keepdims=True)
            oh_rows[hd].append(_rms(o, hgn))
    for slab in range(ATTN_Q_HEADS // 2):
        oa_ref[r8, slab * LANES:(slab + 1) * LANES] = _row_select(oa_rows[slab])
    for hd in range(HG_HEADS):
        oh_ref[r8, hd * HG_D:(hd + 1) * HG_D] = _row_select(oh_rows[hd])

    @pl.when(i == pl.num_programs(0) - 1)
    def _():
        gh = proj(OFF_GH, HG_W)
        oh = oh_ref[...] * (gh * jax.nn.sigmoid(gh))
        m = (_merge_out(oa_ref[...], wao_ref, OFF_GA, proj)
             + _merge_out(oh, who_ref, OFF_GB, proj))
        y_ref[...] = x_ref[...] + _rms(_dot(m.astype(BF16), wo_ref[...]), post_ref[...])


def _mix_sample_call(x, ck, cv, s0, sinks, pre, post, lbl, hgn, win, wao, who, wo):
    n = x.shape[0]
    tile = min(SAMPLE_TILE, n)
    full = pl.BlockSpec((n, D_MODEL), lambda i: (0, 0))
    kv_spec = pl.BlockSpec((tile, WINDOW, ATTN_KV_W), lambda i: (i, 0, 0))
    st_spec = pl.BlockSpec((tile, HG_W, HG_D), lambda i: (i, 0, 0))
    act = pltpu.VMEM((n, HG_W), F32)
    kv_act = pltpu.VMEM((n, ATTN_KV_W), F32)
    return pl.pallas_call(
        _mix_sample_kernel,
        grid=(n // tile,),
        in_specs=[pl.BlockSpec(memory_space=pltpu.SMEM), full, kv_spec, kv_spec, st_spec,
                  _const_spec((1, D_MODEL)), _const_spec((1, D_MODEL)),
                  _const_spec(lbl.shape), _const_spec((1, HG_D)),
                  _const_spec(win.shape), _const_spec(wao.shape), _const_spec(who.shape),
                  _const_spec(wo.shape)],
        out_specs=[full, kv_spec, kv_spec, st_spec],
        out_shape=[jax.ShapeDtypeStruct(x.shape, F32),
                   jax.ShapeDtypeStruct(ck.shape, F32),
                   jax.ShapeDtypeStruct(cv.shape, F32),
                   jax.ShapeDtypeStruct(s0.shape, F32)],
        scratch_shapes=[pltpu.VMEM((n, D_MODEL), BF16),
                        pltpu.VMEM((n, ATTN_Q_W), F32), kv_act, kv_act,
                        act, act, act, act,
                        pltpu.VMEM((n, ATTN_Q_W), F32), act],
        compiler_params=pltpu.CompilerParams(
            dimension_semantics=("arbitrary",), vmem_limit_bytes=VMEM_LIMIT),
        name="mix_sample",
    )(sinks, x, ck, cv, s0, pre, post, lbl, hgn, win, wao, who, wo)


def _ffn_weights(wg, wu, wd):
    def cols(w):
        return w.astype(BF16).reshape(D_MODEL, N_FFN_CHUNKS, FFN_CHUNK).transpose(1, 0, 2)
    return cols(wg), cols(wu), wd.astype(BF16).reshape(N_FFN_CHUNKS, FFN_CHUNK, D_MODEL)


def kernel(x_prompt, x_sample, cache_k, cache_v, state_hgrn, norm_ffn1_pre, norm_ffn1_post, w_ffn1_gate, w_ffn1_up, w_ffn1_down, norm_mix_pre, norm_mix_post, w_in, attn_sinks, hgrn_lb_logits, hgrn_norm, w_attn_out, w_hgrn_out, w_out, norm_ffn2_pre, norm_ffn2_post, w_ffn2_gate, w_ffn2_up, w_ffn2_down):
    depth = w_in.shape[0]
    assert depth == 1 and hgrn_lb_logits.shape[0] == 2, "single-layer stack only"
    batch, seq, _ = x_prompt.shape
    n_s = x_sample.shape[0]
    assert x_sample.shape[1] == 1 and seq % WINDOW == 0

    xp = x_prompt.reshape(batch * seq, D_MODEL)
    xs = x_sample.reshape(n_s, D_MODEL)
    ck = cache_k[0].reshape(n_s, WINDOW, ATTN_KV_W)
    cv = cache_v[0].reshape(n_s, WINDOW, ATTN_KV_W)
    s0 = state_hgrn[0].reshape(n_s, HG_W, HG_D)

    xp, xs = _ffn_call(xp, xs, norm_ffn1_pre, norm_ffn1_post,
                       *_ffn_weights(w_ffn1_gate[0], w_ffn1_up[0], w_ffn1_down[0]))

    mix_w = (attn_sinks, norm_mix_pre, norm_mix_post, hgrn_lb_logits, hgrn_norm,
             w_in[0].astype(BF16), w_attn_out[0].astype(BF16), w_hgrn_out[0].astype(BF16),
             w_out[0].astype(BF16))
    xp, nkp, nvp, nsp = _mix_prompt_call(xp, batch, *mix_w)
    xs, nks, nvs, nss = _mix_sample_call(xs, ck, cv, s0, *mix_w)

    xp, xs = _ffn_call(xp, xs, norm_ffn2_pre, norm_ffn2_post,
                       *_ffn_weights(w_ffn2_gate[0], w_ffn2_up[0], w_ffn2_down[0]))

    kv_shape = (1, -1, WINDOW, ATTN_KV_HEADS, HEAD_DIM)
    st_shape = (1, -1, HG_HEADS, HG_D, HG_D)
    return (xp.reshape(batch, seq, D_MODEL), xs.reshape(n_s, 1, D_MODEL),
            nkp.reshape(kv_shape), nvp.reshape(kv_shape), nsp.reshape(st_shape),
            nks.reshape(kv_shape), nvs.reshape(kv_shape), nss.reshape(st_shape))
```

```python
import functools

import jax
import jax.numpy as jnp
from jax import lax
from jax.experimental import pallas as pl
from jax.experimental.pallas import tpu as pltpu

F32 = jnp.float32
BF16 = jnp.bfloat16

D_MODEL = 1024
FFN_DIM = 2816
HEAD_DIM = 64
ATTN_Q_HEADS = 8
ATTN_KV_HEADS = 2
WINDOW = 128
HG_HEADS = 4
HG_D = 128
EPS = 1e-6

ATTN_Q_W = ATTN_Q_HEADS * HEAD_DIM
ATTN_KV_W = ATTN_KV_HEADS * HEAD_DIM
HG_W = HG_HEADS * HG_D
OFF_QA = 0
OFF_KA = OFF_QA + ATTN_Q_W
OFF_VA = OFF_KA + ATTN_KV_W
OFF_QH = OFF_VA + ATTN_KV_W
OFF_FH = OFF_QH + HG_W
OFF_IH = OFF_FH + HG_W
OFF_GH = OFF_IH + HG_W
OFF_GA = OFF_GH + HG_W
OFF_GB = OFF_GA + D_MODEL
IN_WIDTH = OFF_GB + D_MODEL

LANES = 128
FFN_CHUNK = 256
N_FFN_CHUNKS = FFN_DIM // FFN_CHUNK
FFN_ROWS = 1024
MIX_ROWS = 512
HG_CHUNK = 128
SAMPLE_TILE = 8
VMEM_LIMIT = 56 * 1024 * 1024

NT_DIMS = (((1,), (1,)), ((), ()))
TN_DIMS = (((0,), (0,)), ((), ()))


def _rms(x, g):
    return x * lax.rsqrt(jnp.mean(x * x, axis=-1, keepdims=True) + EPS) * g


def _dot(a, b):
    return jnp.dot(a, b, preferred_element_type=F32)


STREAM_SLOTS = 3
STREAM_CHUNK_BYTES = 1 << 20


def _stream_cast(src_hbm, dst_ref):
    n_rows, n_cols = src_hbm.shape
    rc = max(8, min(n_rows, STREAM_CHUNK_BYTES // (4 * n_cols) // 8 * 8))
    while n_rows % rc:
        rc -= 8
    n = n_rows // rc
    slots = min(STREAM_SLOTS, n)

    def body(stage, sem):
        def copy(c):
            return pltpu.make_async_copy(src_hbm.at[pl.ds(c * rc, rc), :], stage.at[c % slots],
                                         sem.at[c % slots])
        for c in range(slots):
            copy(c).start()
        for c in range(n):
            copy(c).wait()
            dst_ref[c * rc:(c + 1) * rc, :] = stage[c % slots].astype(BF16)
            if c + slots < n:
                copy(c + slots).start()

    pl.run_scoped(body, pltpu.VMEM((slots, rc, n_cols), F32), pltpu.SemaphoreType.DMA((slots,)))


HBM_SPEC = pl.BlockSpec(memory_space=pl.ANY)


def _const_spec(shape):
    zeros = (0,) * len(shape)
    return pl.BlockSpec(shape, lambda *_: zeros, pipeline_mode=pl.Buffered(1))


def _ffn_tile(x, pre, post, wg_ref, wu_ref, wd_ref):
    h = _rms(x, pre).astype(BF16)
    acc = None
    for c in range(N_FFN_CHUNKS):
        cols = slice(c * FFN_CHUNK, (c + 1) * FFN_CHUNK)
        g = _dot(h, wg_ref[:, cols])
        u = _dot(h, wu_ref[:, cols])
        a = (g * jax.nn.sigmoid(g) * u).astype(BF16)
        d = _dot(a, wd_ref[cols, :])
        acc = d if acc is None else acc + d
    return x + 0.5 * _rms(acc, post)


def _ffn_kernel(n_prompt_steps, xp_ref, xs_ref, pre_ref, post_ref, wg_hbm, wu_hbm, wd_hbm,
                yp_ref, ys_ref, wg_ref, wu_ref, wd_ref):
    i = pl.program_id(0)

    @pl.when(i == 0)
    def _():
        _stream_cast(wg_hbm, wg_ref)
        _stream_cast(wu_hbm, wu_ref)
        _stream_cast(wd_hbm, wd_ref)

    @pl.when(i < n_prompt_steps)
    def _():
        yp_ref[...] = _ffn_tile(xp_ref[...], pre_ref[...], post_ref[...], wg_ref, wu_ref, wd_ref)

    @pl.when(i == n_prompt_steps)
    def _():
        ys_ref[...] = _ffn_tile(xs_ref[...], pre_ref[...], post_ref[...], wg_ref, wu_ref, wd_ref)


def _ffn_call(xp, xs, pre, post, wg, wu, wd):
    n_p, n_s = xp.shape[0], xs.shape[0]
    rows = min(FFN_ROWS, n_p)
    steps = n_p // rows
    prompt_spec = pl.BlockSpec((rows, D_MODEL), lambda i: (jnp.minimum(i, steps - 1), 0))
    sample_spec = pl.BlockSpec((n_s, D_MODEL), lambda i: (0, 0))
    return pl.pallas_call(
        functools.partial(_ffn_kernel, steps),
        grid=(steps + 1,),
        in_specs=[prompt_spec, sample_spec,
                  _const_spec((1, D_MODEL)), _const_spec((1, D_MODEL)),
                  HBM_SPEC, HBM_SPEC, HBM_SPEC],
        out_specs=[prompt_spec, sample_spec],
        out_shape=[jax.ShapeDtypeStruct(xp.shape, F32), jax.ShapeDtypeStruct(xs.shape, F32)],
        scratch_shapes=[pltpu.VMEM(wg.shape, BF16), pltpu.VMEM(wu.shape, BF16),
                        pltpu.VMEM(wd.shape, BF16)],
        compiler_params=pltpu.CompilerParams(
            dimension_semantics=("arbitrary",), vmem_limit_bytes=VMEM_LIMIT),
        name="ffn_half",
    )(xp, xs, pre, post, wg, wu, wd)


def _forget_lower_bound(lbl):
    l0, l1 = lbl[0:1, :], lbl[1:2, :]
    m = jnp.maximum(l0, l1)
    e0, e1 = jnp.exp(l0 - m), jnp.exp(l1 - m)
    return e0 / (e0 + e1)


def _lane_is_low(shape):
    return lax.broadcasted_iota(jnp.int32, shape, len(shape) - 1) < HEAD_DIM


def _dup_kv(x):
    swapped = pltpu.roll(x, HEAD_DIM, axis=1)
    low = _lane_is_low(x.shape)
    return jnp.where(low, x, swapped), jnp.where(low, swapped, x)


def _merge_out(h, w_ref, off, gates_from):
    return jax.nn.sigmoid(gates_from(off, D_MODEL)) * _dot(h.astype(BF16), w_ref[...])


def _pair_levels(c):
    t = jnp.arange(c, dtype=jnp.int32)[:, None]
    s = jnp.arange(c, dtype=jnp.int32)[None, :]
    x = jnp.maximum(t ^ s, 1)
    lvl = (31 - lax.clz(x)).astype(jnp.int32)
    return jnp.where(t > s, lvl, -1)


LOG2E = 1.4426950408889634


def _head(a, hd):
    return a[:, hd * HG_D:(hd + 1) * HG_D]


def _hgrn_chunk(q, k, v, g, st, levels):
    c = q.shape[0]
    heads = range(HG_HEADS)
    row = lax.broadcasted_iota(jnp.int32, (c, c), 0)
    col = lax.broadcasted_iota(jnp.int32, (c, c), 1)
    tril = (col <= row).astype(F32)
    b = jnp.dot(tril, g, precision=lax.Precision.HIGHEST, preferred_element_type=F32) * LOG2E
    t = lax.broadcasted_iota(jnp.int32, (c, 1), 0)

    qe = (q * jnp.exp2(b)).astype(BF16)
    o = [lax.dot_general(_head(qe, hd), st[hd * HG_D:(hd + 1) * HG_D, :].astype(BF16), NT_DIMS,
                         preferred_element_type=F32) for hd in heads]

    a = [jnp.zeros((c, c), F32) for _ in heads]
    last = b
    n, lvl = 1, 0
    while n < c:
        second = (t & n) != 0
        if n < 8:
            edge = jnp.where(second, pltpu.roll(last, n, axis=0), last)
            if 2 * n < 8:
                last = jnp.where(second, last, pltpu.roll(last, c - n, axis=0))
        else:
            edge = jnp.concatenate(
                [jnp.broadcast_to(b[p + n - 1:p + n, :], (2 * n, b.shape[1]))
                 for p in range(0, c, 2 * n)], axis=0)
        w = jnp.exp2(jnp.where(second, b - edge, edge - b))
        x = (jnp.where(second, q, k) * w).astype(BF16)
        for hd in heads:
            al = lax.dot_general(_head(x, hd), _head(x, hd), NT_DIMS, preferred_element_type=F32)
            a[hd] = jnp.where(levels == lvl, al, a[hd])
        n, lvl = 2 * n, lvl + 1

    qk = q * k
    b_end = b[c - 1:c, :]
    k_end = (k * jnp.exp2(b_end - b)).astype(BF16)
    decay = jnp.exp2(b_end)
    vb = v.astype(BF16)
    outs, states = [], []
    for hd in heads:
        diag = jnp.sum(_head(qk, hd), axis=1, keepdims=True)
        outs.append(o[hd] + _dot(a[hd].astype(BF16), _head(vb, hd)) + diag * _head(v, hd))
        states.append(st[hd * HG_D:(hd + 1) * HG_D, :] * _head(decay, hd)
                      + lax.dot_general(_head(vb, hd), _head(k_end, hd), TN_DIMS,
                                        preferred_element_type=F32))
    return outs, states


def _mix_prompt_kernel(sinks_ref, x_ref, pre_ref, post_ref, lbl_ref, hgn_ref, lvl_ref,
                       win_hbm, wao_hbm, who_hbm, wo_hbm,
                       y_ref, nk_ref, nv_ref, ns_ref,
                       win_ref, wao_ref, who_ref, wo_ref,
                       kbuf, vbuf, st_ref, qh_ref, kh_ref, vh_ref, lf_ref, oa_ref, oh_ref):
    j = pl.program_id(1)
    rows = x_ref.shape[0]
    n_blocks = rows // WINDOW

    @pl.when((pl.program_id(0) == 0) & (j == 0))
    def _():
        for src, dst in ((win_hbm, win_ref), (wao_hbm, wao_ref), (who_hbm, who_ref),
                         (wo_hbm, wo_ref)):
            _stream_cast(src, dst)

    @pl.when(j == 0)
    def _():
        kbuf[0:WINDOW, :] = jnp.zeros((WINDOW, LANES), F32)
        vbuf[0:WINDOW, :] = jnp.zeros((WINDOW, LANES), F32)
        st_ref[...] = jnp.zeros(st_ref.shape, F32)

    x = x_ref[...]
    h = _rms(x, pre_ref[...]).astype(BF16)

    def proj(off, width):
        return _dot(h, win_ref[:, off:off + width])

    qa = proj(OFF_QA, ATTN_Q_W) * (HEAD_DIM ** -0.5)
    ka = proj(OFF_KA, ATTN_KV_W)
    va = proj(OFF_VA, ATTN_KV_W)
    kbuf[WINDOW:WINDOW + rows, :] = ka
    vbuf[WINDOW:WINDOW + rows, :] = va

    qi = lax.broadcasted_iota(jnp.int32, (WINDOW, 2 * WINDOW), 0)
    kj = lax.broadcasted_iota(jnp.int32, (WINDOW, 2 * WINDOW), 1)
    band = (kj > qi) & (kj <= qi + WINDOW)
    low = _lane_is_low((WINDOW, LANES))
    top = lax.broadcasted_iota(jnp.int32, (2 * WINDOW, 1), 0) < WINDOW
    for n in range(n_blocks):
        kd = _dup_kv(kbuf[n * WINDOW:(n + 2) * WINDOW, :])
        vd = _dup_kv(vbuf[n * WINDOW:(n + 2) * WINDOW, :])
        valid = band & ((j > 0) | (kj >= WINDOW)) if n == 0 else band
        valid2 = jnp.concatenate([valid, valid], axis=0)
        for slab in range(ATTN_Q_HEADS // 2):
            kv = (2 * slab) // (ATTN_Q_HEADS // ATTN_KV_HEADS)
            qs = qa[n * WINDOW:(n + 1) * WINDOW, slab * LANES:(slab + 1) * LANES]
            q2 = jnp.concatenate([jnp.where(low, qs, 0.0), jnp.where(low, 0.0, qs)], axis=0)
            s = lax.dot_general(q2.astype(BF16), kd[kv].astype(BF16), NT_DIMS,
                                preferred_element_type=F32)
            s = jnp.where(valid2, s, -jnp.inf)
            sk = jnp.where(top, sinks_ref[0, 2 * slab], sinks_ref[0, 2 * slab + 1])
            m = jnp.maximum(jnp.max(s, axis=1, keepdims=True), sk)
            p = jnp.exp(s - m)
            p = p / (jnp.sum(p, axis=1, keepdims=True) + jnp.exp(sk - m))
            o2 = _dot(p.astype(BF16), vd[kv].astype(BF16))
            oa_ref[n * WINDOW:(n + 1) * WINDOW, slab * LANES:(slab + 1) * LANES] = (
                jnp.where(low, o2[0:WINDOW], o2[WINDOW:2 * WINDOW]))
    kbuf[0:WINDOW, :] = kbuf[rows:rows + WINDOW, :]
    vbuf[0:WINDOW, :] = vbuf[rows:rows + WINDOW, :]

    @pl.when(j == pl.num_programs(1) - 1)
    def _():
        nk_ref[0] = ka[rows - WINDOW:rows, :]
        nv_ref[0] = va[rows - WINDOW:rows, :]

    lb = _forget_lower_bound(lbl_ref[...])
    qh = proj(OFF_QH, HG_W)
    qh_ref[...] = qh * jax.nn.sigmoid(qh)
    fp = proj(OFF_FH, HG_W)
    lf_ref[...] = jnp.log(lb + (1.0 - lb) * jax.nn.sigmoid(fp))
    kh_ref[...] = (1.0 - lb) * jax.nn.sigmoid(-fp)
    vh_ref[...] = proj(OFF_IH, HG_W)
    hgn = hgn_ref[...]

    def chunk_body(ci, carry):
        r = pl.ds(pl.multiple_of(ci * HG_CHUNK, HG_CHUNK), HG_CHUNK)
        outs, states = _hgrn_chunk(qh_ref[r, :], kh_ref[r, :], vh_ref[r, :], lf_ref[r, :],
                                   st_ref[...], lvl_ref[...])
        for hd in range(HG_HEADS):
            cs = slice(hd * HG_D, (hd + 1) * HG_D)
            st_ref[cs, :] = states[hd]
            oh_ref[r, cs] = _rms(outs[hd], hgn)
        return carry

    lax.fori_loop(0, rows // HG_CHUNK, chunk_body, 0, unroll=True)

    @pl.when(j == pl.num_programs(1) - 1)
    def _():
        for hd in range(HG_HEADS):
            cs = slice(hd * HG_D, (hd + 1) * HG_D)
            ns_ref[0, cs, :] = st_ref[cs, :].T

    gh = proj(OFF_GH, HG_W)
    oh = oh_ref[...] * (gh * jax.nn.sigmoid(gh))

    m = (_merge_out(oa_ref[...], wao_ref, OFF_GA, proj)
         + _merge_out(oh, who_ref, OFF_GB, proj))
    y_ref[...] = x + _rms(_dot(m.astype(BF16), wo_ref[...]), post_ref[...])


def _mix_prompt_call(x, batch, sinks, pre, post, lbl, hgn, win, wao, who, wo):
    seq = x.shape[0] // batch
    rows = min(MIX_ROWS, seq)
    steps = seq // rows
    x_spec = pl.BlockSpec((rows, D_MODEL), lambda b, j: (b * steps + j, 0))
    kv_spec = pl.BlockSpec((1, WINDOW, ATTN_KV_W), lambda b, j: (b, 0, 0))
    st_spec = pl.BlockSpec((1, HG_W, HG_D), lambda b, j: (b, 0, 0))
    act = pltpu.VMEM((rows, HG_W), F32)
    return pl.pallas_call(
        _mix_prompt_kernel,
        grid=(batch, steps),
        in_specs=[pl.BlockSpec(memory_space=pltpu.SMEM), x_spec,
                  _const_spec((1, D_MODEL)), _const_spec((1, D_MODEL)),
                  _const_spec(lbl.shape), _const_spec((1, HG_D)),
                  _const_spec((HG_CHUNK, HG_CHUNK)),
                  HBM_SPEC, HBM_SPEC, HBM_SPEC, HBM_SPEC],
        out_specs=[x_spec, kv_spec, kv_spec, st_spec],
        out_shape=[jax.ShapeDtypeStruct(x.shape, F32),
                   jax.ShapeDtypeStruct((batch, WINDOW, ATTN_KV_W), F32),
                   jax.ShapeDtypeStruct((batch, WINDOW, ATTN_KV_W), F32),
                   jax.ShapeDtypeStruct((batch, HG_W, HG_D), F32)],
        scratch_shapes=[pltpu.VMEM(win.shape, BF16), pltpu.VMEM(wao.shape, BF16),
                        pltpu.VMEM(who.shape, BF16), pltpu.VMEM(wo.shape, BF16),
                        pltpu.VMEM((rows + WINDOW, ATTN_KV_W), F32),
                        pltpu.VMEM((rows + WINDOW, ATTN_KV_W), F32),
                        pltpu.VMEM((HG_W, HG_D), F32),
                        act, act, act, act,
                        pltpu.VMEM((rows, ATTN_Q_W), F32), act],
        compiler_params=pltpu.CompilerParams(
            dimension_semantics=("arbitrary", "arbitrary"), vmem_limit_bytes=VMEM_LIMIT),
        name="mix_prompt",
    )(sinks, x, pre, post, lbl, hgn, _pair_levels(HG_CHUNK), win, wao, who, wo)


def _row_select(rows_list):
    n = -(-len(rows_list) // 8) * 8
    lanes = rows_list[0].shape[1]
    ridx = lax.broadcasted_iota(jnp.int32, (n, lanes), 0)
    out = jnp.zeros((n, lanes), F32)
    for i, r in enumerate(rows_list):
        out = jnp.where(ridx == i, jnp.broadcast_to(r, (n, lanes)), out)
    return out


def _mix_sample_kernel(sinks_ref, x_ref, ck_ref, cv_ref, s0_ref, pre_ref, post_ref, lbl_ref, hgn_ref,
                       win_hbm, wao_hbm, who_hbm, wo_hbm,
                       y_ref, nk_ref, nv_ref, ns_ref,
                       win_ref, wao_ref, who_ref, wo_ref,
                       h_ref, qa_ref, ka_ref, va_ref, qh_ref, kh_ref, fh_ref, vh_ref, oa_ref, oh_ref):
    i = pl.program_id(0)
    tile = ck_ref.shape[0]

    def proj(off, width):
        return _dot(h_ref[...], win_ref[:, off:off + width])

    @pl.when(i == 0)
    def _():
        for src, dst in ((win_hbm, win_ref), (wao_hbm, wao_ref), (who_hbm, who_ref),
                         (wo_hbm, wo_ref)):
            _stream_cast(src, dst)
        h_ref[...] = _rms(x_ref[...], pre_ref[...]).astype(BF16)
        qa_ref[...] = proj(OFF_QA, ATTN_Q_W) * (HEAD_DIM ** -0.5)
        ka_ref[...] = proj(OFF_KA, ATTN_KV_W)
        va_ref[...] = proj(OFF_VA, ATTN_KV_W)
        lb = _forget_lower_bound(lbl_ref[...])
        qh = proj(OFF_QH, HG_W)
        qh_ref[...] = qh * jax.nn.sigmoid(qh)
        fp = proj(OFF_FH, HG_W)
        fh_ref[...] = jnp.exp(jnp.log(lb + (1.0 - lb) * jax.nn.sigmoid(fp)))
        kh_ref[...] = (1.0 - lb) * jax.nn.sigmoid(-fp)
        vh_ref[...] = proj(OFF_IH, HG_W)

    low = _lane_is_low((1, LANES))
    pos = lax.broadcasted_iota(jnp.int32, (WINDOW, LANES), 0)
    hrow = lax.broadcasted_iota(jnp.int32, (ATTN_Q_HEADS, 1), 0)
    sk = jnp.zeros((ATTN_Q_HEADS, 1), F32)
    for hd in range(ATTN_Q_HEADS):
        sk = jnp.where(hrow == hd, sinks_ref[0, hd], sk)
    group = ATTN_Q_HEADS // ATTN_KV_HEADS
    hgn = hgn_ref[...]

    r8 = pl.ds(pl.multiple_of(i * tile, tile), tile)
    qa8, ka8, va8 = qa_ref[r8, :], ka_ref[r8, :], va_ref[r8, :]
    fh8, kh8, qh8, vh8 = fh_ref[r8, :], kh_ref[r8, :], qh_ref[r8, :], vh_ref[r8, :]
    oa_rows = [[] for _ in range(ATTN_Q_HEADS // 2)]
    oh_rows = [[] for _ in range(HG_HEADS)]
    for bi in range(tile):
        r1 = slice(bi, bi + 1)
        kw = jnp.where(pos == WINDOW - 1, ka8[r1, :], pltpu.roll(ck_ref[bi], WINDOW - 1, axis=0))
        vw = jnp.where(pos == WINDOW - 1, va8[r1, :], pltpu.roll(cv_ref[bi], WINDOW - 1, axis=0))
        nk_ref[bi] = kw
        nv_ref[bi] = vw
        qrows = []
        for hd in range(ATTN_Q_HEADS):
            slab = qa8[r1, (hd // 2) * LANES:(hd // 2 + 1) * LANES]
            in_place = (hd % 2) == (hd // group)
            src = slab if in_place else pltpu.roll(slab, HEAD_DIM, axis=1)
            on_kv_lanes = low if hd // group == 0 else jnp.logical_not(low)
            qrows.append(jnp.where(on_kv_lanes, src, 0.0))
        q8 = _row_select(qrows)
        s = lax.dot_general(q8.astype(BF16), kw.astype(BF16), NT_DIMS, preferred_element_type=F32)
        m = jnp.maximum(jnp.max(s, axis=1, keepdims=True), sk)
        p = jnp.exp(s - m)
        p = p / (jnp.sum(p, axis=1, keepdims=True) + jnp.exp(sk - m))
        o8 = _dot(p.astype(BF16), vw.astype(BF16))
        for slab in range(ATTN_Q_HEADS // 2):
            kv = (2 * slab) // group
            even, odd = o8[2 * slab:2 * slab + 1, :], o8[2 * slab + 1:2 * slab + 2, :]
            if kv == 0:
                out = jnp.where(low, even, pltpu.roll(odd, HEAD_DIM, axis=1))
            else:
                out = jnp.where(low, pltpu.roll(even, HEAD_DIM, axis=1), odd)
            oa_rows[slab].append(out)

        vecs = [a8[r1, hd * HG_D:(hd + 1) * HG_D] for a8 in (fh8, kh8, qh8) for hd in range(HG_HEADS)]
        packed = _row_select(vecs)
        square = jnp.concatenate([packed, jnp.zeros((HG_D - packed.shape[0], HG_D), F32)], axis=0)
        cols = square.T
        for hd in range(HG_HEADS):
            def col(c):
                return jnp.broadcast_to(cols[:, c:c + 1], (HG_D, HG_D))
            rs = slice(hd * HG_D, (hd + 1) * HG_D)
            s_new = col(hd) * s0_ref[bi, rs, :] + col(HG_HEADS + hd) * vh8[r1, rs]
            ns_ref[bi, rs, :] = s_new
            o = jnp.sum(col(2 * HG_HEADS + hd) * s_new, axis=0, keepdims=True)
            oh_rows[hd].append(_rms(o, hgn))
    for slab in range(ATTN_Q_HEADS // 2):
        oa_ref[r8, slab * LANES:(slab + 1) * LANES] = _row_select(oa_rows[slab])
    for hd in range(HG_HEADS):
        oh_ref[r8, hd * HG_D:(hd + 1) * HG_D] = _row_select(oh_rows[hd])

    @pl.when(i == pl.num_programs(0) - 1)
    def _():
        gh = proj(OFF_GH, HG_W)
        oh = oh_ref[...] * (gh * jax.nn.sigmoid(gh))
        m = (_merge_out(oa_ref[...], wao_ref, OFF_GA, proj)
             + _merge_out(oh, who_ref, OFF_GB, proj))
        y_ref[...] = x_ref[...] + _rms(_dot(m.astype(BF16), wo_ref[...]), post_ref[...])


def _mix_sample_call(x, ck, cv, s0, sinks, pre, post, lbl, hgn, win, wao, who, wo):
    n = x.shape[0]
    tile = min(SAMPLE_TILE, n)
    full = pl.BlockSpec((n, D_MODEL), lambda i: (0, 0))
    kv_spec = pl.BlockSpec((tile, WINDOW, ATTN_KV_W), lambda i: (i, 0, 0))
    st_spec = pl.BlockSpec((tile, HG_W, HG_D), lambda i: (i, 0, 0))
    act = pltpu.VMEM((n, HG_W), F32)
    kv_act = pltpu.VMEM((n, ATTN_KV_W), F32)
    return pl.pallas_call(
        _mix_sample_kernel,
        grid=(n // tile,),
        in_specs=[pl.BlockSpec(memory_space=pltpu.SMEM), full, kv_spec, kv_spec, st_spec,
                  _const_spec((1, D_MODEL)), _const_spec((1, D_MODEL)),
                  _const_spec(lbl.shape), _const_spec((1, HG_D)),
                  HBM_SPEC, HBM_SPEC, HBM_SPEC, HBM_SPEC],
        out_specs=[full, kv_spec, kv_spec, st_spec],
        out_shape=[jax.ShapeDtypeStruct(x.shape, F32),
                   jax.ShapeDtypeStruct(ck.shape, F32),
                   jax.ShapeDtypeStruct(cv.shape, F32),
                   jax.ShapeDtypeStruct(s0.shape, F32)],
        scratch_shapes=[pltpu.VMEM(win.shape, BF16), pltpu.VMEM(wao.shape, BF16),
                        pltpu.VMEM(who.shape, BF16), pltpu.VMEM(wo.shape, BF16),
                        pltpu.VMEM((n, D_MODEL), BF16),
                        pltpu.VMEM((n, ATTN_Q_W), F32), kv_act, kv_act,
                        act, act, act, act,
                        pltpu.VMEM((n, ATTN_Q_W), F32), act],
        compiler_params=pltpu.CompilerParams(
            dimension_semantics=("arbitrary",), vmem_limit_bytes=VMEM_LIMIT),
        name="mix_sample",
    )(sinks, x, ck, cv, s0, pre, post, lbl, hgn, win, wao, who, wo)


def kernel(x_prompt, x_sample, cache_k, cache_v, state_hgrn, norm_ffn1_pre, norm_ffn1_post, w_ffn1_gate, w_ffn1_up, w_ffn1_down, norm_mix_pre, norm_mix_post, w_in, attn_sinks, hgrn_lb_logits, hgrn_norm, w_attn_out, w_hgrn_out, w_out, norm_ffn2_pre, norm_ffn2_post, w_ffn2_gate, w_ffn2_up, w_ffn2_down):
    depth = w_in.shape[0]
    assert depth == 1 and hgrn_lb_logits.shape[0] == 2, "single-layer stack only"
    batch, seq, _ = x_prompt.shape
    n_s = x_sample.shape[0]
    assert x_sample.shape[1] == 1 and seq % WINDOW == 0

    xp = x_prompt.reshape(batch * seq, D_MODEL)
    xs = x_sample.reshape(n_s, D_MODEL)
    ck = cache_k[0].reshape(n_s, WINDOW, ATTN_KV_W)
    cv = cache_v[0].reshape(n_s, WINDOW, ATTN_KV_W)
    s0 = state_hgrn[0].reshape(n_s, HG_W, HG_D)

    xp, xs = _ffn_call(xp, xs, norm_ffn1_pre, norm_ffn1_post,
                       w_ffn1_gate[0], w_ffn1_up[0], w_ffn1_down[0])

    mix_w = (attn_sinks, norm_mix_pre, norm_mix_post, hgrn_lb_logits, hgrn_norm,
             w_in[0], w_attn_out[0], w_hgrn_out[0], w_out[0])
    xp, nkp, nvp, nsp = _mix_prompt_call(xp, batch, *mix_w)
    xs, nks, nvs, nss = _mix_sample_call(xs, ck, cv, s0, *mix_w)

    xp, xs = _ffn_call(xp, xs, norm_ffn2_pre, norm_ffn2_post,
                       w_ffn2_gate[0], w_ffn2_up[0], w_ffn2_down[0])

    kv_shape = (1, -1, WINDOW, ATTN_KV_HEADS, HEAD_DIM)
    st_shape = (1, -1, HG_HEADS, HG_D, HG_D)
    return (xp.reshape(batch, seq, D_MODEL), xs.reshape(n_s, 1, D_MODEL),
            nkp.reshape(kv_shape), nvp.reshape(kv_shape), nsp.reshape(st_shape),
            nks.reshape(kv_shape), nvs.reshape(kv_shape), nss.reshape(st_shape))
```

```python
import functools

import jax
import jax.numpy as jnp
from jax import lax
from jax.experimental import pallas as pl
from jax.experimental.pallas import tpu as pltpu

F32 = jnp.float32
BF16 = jnp.bfloat16

D_MODEL = 1024
FFN_DIM = 2816
HEAD_DIM = 64
ATTN_Q_HEADS = 8
ATTN_KV_HEADS = 2
WINDOW = 128
HG_HEADS = 4
HG_D = 128
EPS = 1e-6

ATTN_Q_W = ATTN_Q_HEADS * HEAD_DIM
ATTN_KV_W = ATTN_KV_HEADS * HEAD_DIM
HG_W = HG_HEADS * HG_D
OFF_QA = 0
OFF_KA = OFF_QA + ATTN_Q_W
OFF_VA = OFF_KA + ATTN_KV_W
OFF_QH = OFF_VA + ATTN_KV_W
OFF_FH = OFF_QH + HG_W
OFF_IH = OFF_FH + HG_W
OFF_GH = OFF_IH + HG_W
OFF_GA = OFF_GH + HG_W
OFF_GB = OFF_GA + D_MODEL
IN_WIDTH = OFF_GB + D_MODEL

LANES = 128
FFN_CHUNK = 256
N_FFN_CHUNKS = FFN_DIM // FFN_CHUNK
FFN_ROWS = 1024
MIX_ROWS = 512
HG_CHUNK = 128
SAMPLE_TILE = 8
VMEM_LIMIT = 56 * 1024 * 1024

NT_DIMS = (((1,), (1,)), ((), ()))
TN_DIMS = (((0,), (0,)), ((), ()))


def _rms(x, g):
    return x * lax.rsqrt(jnp.mean(x * x, axis=-1, keepdims=True) + EPS) * g


def _dot(a, b):
    return jnp.dot(a, b, preferred_element_type=F32)


STREAM_SLOTS = 3
STREAM_CHUNK_BYTES = 1 << 20


def _stream_cast(src_hbm, dst_ref):
    n_rows, n_cols = src_hbm.shape
    rc = max(8, min(n_rows, STREAM_CHUNK_BYTES // (4 * n_cols) // 8 * 8))
    while n_rows % rc:
        rc -= 8
    n = n_rows // rc
    slots = min(STREAM_SLOTS, n)

    def body(stage, sem):
        def copy(c):
            return pltpu.make_async_copy(src_hbm.at[pl.ds(c * rc, rc), :], stage.at[c % slots],
                                         sem.at[c % slots])
        for c in range(slots):
            copy(c).start()
        for c in range(n):
            copy(c).wait()
            dst_ref[c * rc:(c + 1) * rc, :] = stage[c % slots].astype(BF16)
            if c + slots < n:
                copy(c + slots).start()

    pl.run_scoped(body, pltpu.VMEM((slots, rc, n_cols), F32), pltpu.SemaphoreType.DMA((slots,)))


HBM_SPEC = pl.BlockSpec(memory_space=pl.ANY)


def _const_spec(shape):
    zeros = (0,) * len(shape)
    return pl.BlockSpec(shape, lambda *_: zeros, pipeline_mode=pl.Buffered(1))


def _ffn_tile(x, pre, post, wg_ref, wu_ref, wd_ref):
    h = _rms(x, pre).astype(BF16)
    acc = None
    for c in range(N_FFN_CHUNKS):
        cols = slice(c * FFN_CHUNK, (c + 1) * FFN_CHUNK)
        g = _dot(h, wg_ref[:, cols])
        u = _dot(h, wu_ref[:, cols])
        a = (g * jax.nn.sigmoid(g) * u).astype(BF16)
        d = _dot(a, wd_ref[cols, :])
        acc = d if acc is None else acc + d
    return x + 0.5 * _rms(acc, post)


def _ffn_kernel(n_prompt_steps, xp_ref, xs_ref, pre_ref, post_ref, wg_hbm, wu_hbm, wd_hbm,
                yp_ref, ys_ref, wg_ref, wu_ref, wd_ref):
    i = pl.program_id(0)

    @pl.when(i == 0)
    def _():
        _stream_cast(wg_hbm, wg_ref)
        _stream_cast(wu_hbm, wu_ref)
        _stream_cast(wd_hbm, wd_ref)

    @pl.when(i < n_prompt_steps)
    def _():
        yp_ref[...] = _ffn_tile(xp_ref[...], pre_ref[...], post_ref[...], wg_ref, wu_ref, wd_ref)

    @pl.when(i == n_prompt_steps)
    def _():
        ys_ref[...] = _ffn_tile(xs_ref[...], pre_ref[...], post_ref[...], wg_ref, wu_ref, wd_ref)


def _ffn_call(xp, xs, pre, post, wg, wu, wd):
    n_p, n_s = xp.shape[0], xs.shape[0]
    rows = min(FFN_ROWS, n_p)
    steps = n_p // rows
    prompt_spec = pl.BlockSpec((rows, D_MODEL), lambda i: (jnp.minimum(i, steps - 1), 0))
    sample_spec = pl.BlockSpec((n_s, D_MODEL), lambda i: (0, 0))
    return pl.pallas_call(
        functools.partial(_ffn_kernel, steps),
        grid=(steps + 1,),
        in_specs=[prompt_spec, sample_spec,
                  _const_spec((1, D_MODEL)), _const_spec((1, D_MODEL)),
                  HBM_SPEC, HBM_SPEC, HBM_SPEC],
        out_specs=[prompt_spec, sample_spec],
        out_shape=[jax.ShapeDtypeStruct(xp.shape, F32), jax.ShapeDtypeStruct(xs.shape, F32)],
        scratch_shapes=[pltpu.VMEM(wg.shape, BF16), pltpu.VMEM(wu.shape, BF16),
                        pltpu.VMEM(wd.shape, BF16)],
        compiler_params=pltpu.CompilerParams(
            dimension_semantics=("arbitrary",), vmem_limit_bytes=VMEM_LIMIT),
        name="ffn_half",
    )(xp, xs, pre, post, wg, wu, wd)


def _forget_lower_bound(lbl):
    l0, l1 = lbl[0:1, :], lbl[1:2, :]
    m = jnp.maximum(l0, l1)
    e0, e1 = jnp.exp(l0 - m), jnp.exp(l1 - m)
    return e0 / (e0 + e1)


def _lane_is_low(shape):
    return lax.broadcasted_iota(jnp.int32, shape, len(shape) - 1) < HEAD_DIM


def _dup_kv(x):
    swapped = pltpu.roll(x, HEAD_DIM, axis=1)
    low = _lane_is_low(x.shape)
    return jnp.where(low, x, swapped), jnp.where(low, swapped, x)


def _merge_out(h, w_ref, off, gates_from):
    return jax.nn.sigmoid(gates_from(off, D_MODEL)) * _dot(h.astype(BF16), w_ref[...])


def _pair_levels(c):
    t = jnp.arange(c, dtype=jnp.int32)[:, None]
    s = jnp.arange(c, dtype=jnp.int32)[None, :]
    x = jnp.maximum(t ^ s, 1)
    lvl = (31 - lax.clz(x)).astype(jnp.int32)
    return jnp.where(t > s, lvl, -1)


LOG2E = 1.4426950408889634


def _head(a, hd):
    return a[:, hd * HG_D:(hd + 1) * HG_D]


def _hgrn_chunk(q, k, v, g, st, levels):
    c = q.shape[0]
    heads = range(HG_HEADS)
    row = lax.broadcasted_iota(jnp.int32, (c, c), 0)
    col = lax.broadcasted_iota(jnp.int32, (c, c), 1)
    tril = (col <= row).astype(F32)
    b = jnp.dot(tril, g, precision=lax.Precision.HIGHEST, preferred_element_type=F32) * LOG2E
    t = lax.broadcasted_iota(jnp.int32, (c, 1), 0)

    qe = (q * jnp.exp2(b)).astype(BF16)
    o = [lax.dot_general(_head(qe, hd), st[hd * HG_D:(hd + 1) * HG_D, :].astype(BF16), NT_DIMS,
                         preferred_element_type=F32) for hd in heads]

    a = [jnp.zeros((c, c), F32) for _ in heads]
    last = b
    n, lvl = 1, 0
    while n < c:
        second = (t & n) != 0
        if n < 8:
            edge = jnp.where(second, pltpu.roll(last, n, axis=0), last)
            if 2 * n < 8:
                last = jnp.where(second, last, pltpu.roll(last, c - n, axis=0))
        else:
            edge = jnp.concatenate(
                [jnp.broadcast_to(b[p + n - 1:p + n, :], (2 * n, b.shape[1]))
                 for p in range(0, c, 2 * n)], axis=0)
        w = jnp.exp2(jnp.where(second, b - edge, edge - b))
        x = (jnp.where(second, q, k) * w).astype(BF16)
        for hd in heads:
            al = lax.dot_general(_head(x, hd), _head(x, hd), NT_DIMS, preferred_element_type=F32)
            a[hd] = jnp.where(levels == lvl, al, a[hd])
        n, lvl = 2 * n, lvl + 1

    qk = q * k
    b_end = b[c - 1:c, :]
    k_end = (k * jnp.exp2(b_end - b)).astype(BF16)
    decay = jnp.exp2(b_end)
    vb = v.astype(BF16)
    outs, states = [], []
    for hd in heads:
        diag = jnp.sum(_head(qk, hd), axis=1, keepdims=True)
        outs.append(o[hd] + _dot(a[hd].astype(BF16), _head(vb, hd)) + diag * _head(v, hd))
        states.append(st[hd * HG_D:(hd + 1) * HG_D, :] * _head(decay, hd)
                      + lax.dot_general(_head(vb, hd), _head(k_end, hd), TN_DIMS,
                                        preferred_element_type=F32))
    return outs, states


def _mix_prompt_kernel(sinks_ref, x_ref, pre_ref, post_ref, lbl_ref, hgn_ref, lvl_ref,
                       win_hbm, wao_hbm, who_hbm, wo_hbm,
                       y_ref, nk_ref, nv_ref, ns_ref,
                       win_ref, wao_ref, who_ref, wo_ref,
                       kbuf, vbuf, st_ref, qh_ref, kh_ref, vh_ref, lf_ref, oa_ref, oh_ref):
    j = pl.program_id(1)
    rows = x_ref.shape[0]
    n_blocks = rows // WINDOW

    @pl.when((pl.program_id(0) == 0) & (j == 0))
    def _():
        for src, dst in ((win_hbm, win_ref), (wao_hbm, wao_ref), (who_hbm, who_ref),
                         (wo_hbm, wo_ref)):
            _stream_cast(src, dst)

    @pl.when(j == 0)
    def _():
        kbuf[0:WINDOW, :] = jnp.zeros((WINDOW, LANES), F32)
        vbuf[0:WINDOW, :] = jnp.zeros((WINDOW, LANES), F32)
        st_ref[...] = jnp.zeros(st_ref.shape, F32)

    x = x_ref[...]
    h = _rms(x, pre_ref[...]).astype(BF16)

    def proj(off, width):
        return _dot(h, win_ref[:, off:off + width])

    qa = proj(OFF_QA, ATTN_Q_W) * (HEAD_DIM ** -0.5)
    ka = proj(OFF_KA, ATTN_KV_W)
    va = proj(OFF_VA, ATTN_KV_W)
    kbuf[WINDOW:WINDOW + rows, :] = ka
    vbuf[WINDOW:WINDOW + rows, :] = va

    qi = lax.broadcasted_iota(jnp.int32, (WINDOW, 2 * WINDOW), 0)
    kj = lax.broadcasted_iota(jnp.int32, (WINDOW, 2 * WINDOW), 1)
    band = (kj > qi) & (kj <= qi + WINDOW)
    low = _lane_is_low((WINDOW, LANES))
    top = lax.broadcasted_iota(jnp.int32, (2 * WINDOW, 1), 0) < WINDOW
    for n in range(n_blocks):
        kd = _dup_kv(kbuf[n * WINDOW:(n + 2) * WINDOW, :])
        vd = _dup_kv(vbuf[n * WINDOW:(n + 2) * WINDOW, :])
        valid = band & ((j > 0) | (kj >= WINDOW)) if n == 0 else band
        valid2 = jnp.concatenate([valid, valid], axis=0)
        for slab in range(ATTN_Q_HEADS // 2):
            kv = (2 * slab) // (ATTN_Q_HEADS // ATTN_KV_HEADS)
            qs = qa[n * WINDOW:(n + 1) * WINDOW, slab * LANES:(slab + 1) * LANES]
            q2 = jnp.concatenate([jnp.where(low, qs, 0.0), jnp.where(low, 0.0, qs)], axis=0)
            s = lax.dot_general(q2.astype(BF16), kd[kv].astype(BF16), NT_DIMS,
                                preferred_element_type=F32)
            s = jnp.where(valid2, s, -jnp.inf)
            sk = jnp.where(top, sinks_ref[0, 2 * slab], sinks_ref[0, 2 * slab + 1])
            m = jnp.maximum(jnp.max(s, axis=1, keepdims=True), sk)
            p = jnp.exp(s - m)
            p = p / (jnp.sum(p, axis=1, keepdims=True) + jnp.exp(sk - m))
            o2 = _dot(p.astype(BF16), vd[kv].astype(BF16))
            oa_ref[n * WINDOW:(n + 1) * WINDOW, slab * LANES:(slab + 1) * LANES] = (
                jnp.where(low, o2[0:WINDOW], o2[WINDOW:2 * WINDOW]))
    kbuf[0:WINDOW, :] = kbuf[rows:rows + WINDOW, :]
    vbuf[0:WINDOW, :] = vbuf[rows:rows + WINDOW, :]

    @pl.when(j == pl.num_programs(1) - 1)
    def _():
        nk_ref[0] = ka[rows - WINDOW:rows, :]
        nv_ref[0] = va[rows - WINDOW:rows, :]

    lb = _forget_lower_bound(lbl_ref[...])
    qh = proj(OFF_QH, HG_W)
    qh_ref[...] = qh * jax.nn.sigmoid(qh)
    fp = proj(OFF_FH, HG_W)
    lf_ref[...] = jnp.log(lb + (1.0 - lb) * jax.nn.sigmoid(fp))
    kh_ref[...] = (1.0 - lb) * jax.nn.sigmoid(-fp)
    vh_ref[...] = proj(OFF_IH, HG_W)
    hgn = hgn_ref[...]

    def chunk_body(ci, carry):
        r = pl.ds(pl.multiple_of(ci * HG_CHUNK, HG_CHUNK), HG_CHUNK)
        outs, states = _hgrn_chunk(qh_ref[r, :], kh_ref[r, :], vh_ref[r, :], lf_ref[r, :],
                                   st_ref[...], lvl_ref[...])
        for hd in range(HG_HEADS):
            cs = slice(hd * HG_D, (hd + 1) * HG_D)
            st_ref[cs, :] = states[hd]
            oh_ref[r, cs] = _rms(outs[hd], hgn)
        return carry

    lax.fori_loop(0, rows // HG_CHUNK, chunk_body, 0, unroll=True)

    @pl.when(j == pl.num_programs(1) - 1)
    def _():
        for hd in range(HG_HEADS):
            cs = slice(hd * HG_D, (hd + 1) * HG_D)
            ns_ref[0, cs, :] = st_ref[cs, :].T

    gh = proj(OFF_GH, HG_W)
    oh = oh_ref[...] * (gh * jax.nn.sigmoid(gh))

    m = (_merge_out(oa_ref[...], wao_ref, OFF_GA, proj)
         + _merge_out(oh, who_ref, OFF_GB, proj))
    y_ref[...] = x + _rms(_dot(m.astype(BF16), wo_ref[...]), post_ref[...])


def _mix_prompt_call(x, batch, sinks, pre, post, lbl, hgn, win, wao, who, wo):
    seq = x.shape[0] // batch
    rows = min(MIX_ROWS, seq)
    steps = seq // rows
    x_spec = pl.BlockSpec((rows, D_MODEL), lambda b, j: (b * steps + j, 0))
    kv_spec = pl.BlockSpec((1, WINDOW, ATTN_KV_W), lambda b, j: (b, 0, 0))
    st_spec = pl.BlockSpec((1, HG_W, HG_D), lambda b, j: (b, 0, 0))
    act = pltpu.VMEM((rows, HG_W), F32)
    return pl.pallas_call(
        _mix_prompt_kernel,
        grid=(batch, steps),
        in_specs=[pl.BlockSpec(memory_space=pltpu.SMEM), x_spec,
                  _const_spec((1, D_MODEL)), _const_spec((1, D_MODEL)),
                  _const_spec(lbl.shape), _const_spec((1, HG_D)),
                  _const_spec((HG_CHUNK, HG_CHUNK)),
                  HBM_SPEC, HBM_SPEC, HBM_SPEC, HBM_SPEC],
        out_specs=[x_spec, kv_spec, kv_spec, st_spec],
        out_shape=[jax.ShapeDtypeStruct(x.shape, F32),
                   jax.ShapeDtypeStruct((batch, WINDOW, ATTN_KV_W), F32),
                   jax.ShapeDtypeStruct((batch, WINDOW, ATTN_KV_W), F32),
                   jax.ShapeDtypeStruct((batch, HG_W, HG_D), F32)],
        scratch_shapes=[pltpu.VMEM(win.shape, BF16), pltpu.VMEM(wao.shape, BF16),
                        pltpu.VMEM(who.shape, BF16), pltpu.VMEM(wo.shape, BF16),
                        pltpu.VMEM((rows + WINDOW, ATTN_KV_W), F32),
                        pltpu.VMEM((rows + WINDOW, ATTN_KV_W), F32),
                        pltpu.VMEM((HG_W, HG_D), F32),
                        act, act, act, act,
                        pltpu.VMEM((rows, ATTN_Q_W), F32), act],
        compiler_params=pltpu.CompilerParams(
            dimension_semantics=("arbitrary", "arbitrary"), vmem_limit_bytes=VMEM_LIMIT),
        name="mix_prompt",
    )(sinks, x, pre, post, lbl, hgn, _pair_levels(HG_CHUNK), win, wao, who, wo)


def _row_select(rows_list):
    n = -(-len(rows_list) // 8) * 8
    lanes = rows_list[0].shape[1]
    ridx = lax.broadcasted_iota(jnp.int32, (n, lanes), 0)
    out = jnp.zeros((n, lanes), F32)
    for i, r in enumerate(rows_list):
        out = jnp.where(ridx == i, jnp.broadcast_to(r, (n, lanes)), out)
    return out


def _mix_sample_kernel(sinks_ref, x_ref, ck_ref, cv_ref, s0_ref, pre_ref, post_ref, lbl_ref, hgn_ref,
                       win_hbm, wao_hbm, who_hbm, wo_hbm,
                       y_ref, nk_ref, nv_ref, ns_ref,
                       win_ref, wao_ref, who_ref, wo_ref,
                       h_ref, qa_ref, ka_ref, va_ref, qh_ref, kh_ref, fh_ref, vh_ref, oa_ref, oh_ref):
    i = pl.program_id(0)
    tile = ck_ref.shape[0]

    def proj(off, width):
        return _dot(h_ref[...], win_ref[:, off:off + width])

    @pl.when(i == 0)
    def _():
        for src, dst in ((win_hbm, win_ref), (wao_hbm, wao_ref), (who_hbm, who_ref),
                         (wo_hbm, wo_ref)):
            _stream_cast(src, dst)
        h_ref[...] = _rms(x_ref[...], pre_ref[...]).astype(BF16)
        qa_ref[...] = proj(OFF_QA, ATTN_Q_W) * (HEAD_DIM ** -0.5)
        ka_ref[...] = proj(OFF_KA, ATTN_KV_W)
        va_ref[...] = proj(OFF_VA, ATTN_KV_W)
        lb = _forget_lower_bound(lbl_ref[...])
        qh = proj(OFF_QH, HG_W)
        qh_ref[...] = qh * jax.nn.sigmoid(qh)
        fp = proj(OFF_FH, HG_W)
        fh_ref[...] = jnp.exp(jnp.log(lb + (1.0 - lb) * jax.nn.sigmoid(fp)))
        kh_ref[...] = (1.0 - lb) * jax.nn.sigmoid(-fp)
        vh_ref[...] = proj(OFF_IH, HG_W)

    low = _lane_is_low((1, LANES))
    pos = lax.broadcasted_iota(jnp.int32, (WINDOW, LANES), 0)
    hrow = lax.broadcasted_iota(jnp.int32, (ATTN_Q_HEADS, 1), 0)
    sk = jnp.zeros((ATTN_Q_HEADS, 1), F32)
    for hd in range(ATTN_Q_HEADS):
        sk = jnp.where(hrow == hd, sinks_ref[0, hd], sk)
    group = ATTN_Q_HEADS // ATTN_KV_HEADS
    hgn = hgn_ref[...]

    r8 = pl.ds(pl.multiple_of(i * tile, tile), tile)
    qa8, ka8, va8 = qa_ref[r8, :], ka_ref[r8, :], va_ref[r8, :]
    fh8, kh8, qh8, vh8 = fh_ref[r8, :], kh_ref[r8, :], qh_ref[r8, :], vh_ref[r8, :]
    oa_rows = [[] for _ in range(ATTN_Q_HEADS // 2)]
    oh_rows = [[] for _ in range(HG_HEADS)]
    seqs = range(tile)
    kws, vws, q8s = [], [], []
    for bi in seqs:
        r1 = slice(bi, bi + 1)
        kw = jnp.where(pos == WINDOW - 1, ka8[r1, :], pltpu.roll(ck_ref[bi], WINDOW - 1, axis=0))
        vw = jnp.where(pos == WINDOW - 1, va8[r1, :], pltpu.roll(cv_ref[bi], WINDOW - 1, axis=0))
        nk_ref[bi] = kw
        nv_ref[bi] = vw
        kws.append(kw.astype(BF16))
        vws.append(vw.astype(BF16))
        qrows = []
        for hd in range(ATTN_Q_HEADS):
            slab = qa8[r1, (hd // 2) * LANES:(hd // 2 + 1) * LANES]
            in_place = (hd % 2) == (hd // group)
            src = slab if in_place else pltpu.roll(slab, HEAD_DIM, axis=1)
            on_kv_lanes = low if hd // group == 0 else jnp.logical_not(low)
            qrows.append(jnp.where(on_kv_lanes, src, 0.0))
        q8s.append(_row_select(qrows).astype(BF16))
    scores = [lax.dot_general(q8s[bi], kws[bi], NT_DIMS, preferred_element_type=F32) for bi in seqs]
    probs = []
    for s in scores:
        m = jnp.maximum(jnp.max(s, axis=1, keepdims=True), sk)
        p = jnp.exp(s - m)
        probs.append((p / (jnp.sum(p, axis=1, keepdims=True) + jnp.exp(sk - m))).astype(BF16))
    for bi in seqs:
        o8 = _dot(probs[bi], vws[bi])
        for slab in range(ATTN_Q_HEADS // 2):
            kv = (2 * slab) // group
            even, odd = o8[2 * slab:2 * slab + 1, :], o8[2 * slab + 1:2 * slab + 2, :]
            if kv == 0:
                out = jnp.where(low, even, pltpu.roll(odd, HEAD_DIM, axis=1))
            else:
                out = jnp.where(low, pltpu.roll(even, HEAD_DIM, axis=1), odd)
            oa_rows[slab].append(out)

    def columns(a8, hd):
        pad = jnp.zeros((HG_D - tile, HG_D), F32)
        return jnp.concatenate([_head(a8, hd), pad], axis=0).T
    seq_row = lax.broadcasted_iota(jnp.int32, (HG_D, HG_D), 0)
    for hd in range(HG_HEADS):
        rs = slice(hd * HG_D, (hd + 1) * HG_D)
        f_cols = columns(fh8, hd)
        k_cols = columns(kh8, hd).astype(BF16)
        v_rows = jnp.concatenate([_head(vh8, hd), jnp.zeros((HG_D - tile, HG_D), F32)], axis=0)
        q_rows = _head(qh8, hd).astype(BF16)
        outers = [_dot(k_cols, jnp.where(seq_row == bi, v_rows, 0.0).astype(BF16))
                  for bi in seqs]
        for bi in seqs:
            f_col = jnp.broadcast_to(f_cols[:, bi:bi + 1], (HG_D, HG_D))
            ns_ref[bi, rs, :] = f_col * s0_ref[bi, rs, :] + outers[bi]
        outs = [_dot(q_rows, ns_ref[bi, rs, :].astype(BF16)) for bi in seqs]
        for bi in seqs:
            oh_rows[hd].append(_rms(outs[bi][bi:bi + 1, :], hgn))
    for slab in range(ATTN_Q_HEADS // 2):
        oa_ref[r8, slab * LANES:(slab + 1) * LANES] = _row_select(oa_rows[slab])
    for hd in range(HG_HEADS):
        oh_ref[r8, hd * HG_D:(hd + 1) * HG_D] = _row_select(oh_rows[hd])

    @pl.when(i == pl.num_programs(0) - 1)
    def _():
        gh = proj(OFF_GH, HG_W)
        oh = oh_ref[...] * (gh * jax.nn.sigmoid(gh))
        m = (_merge_out(oa_ref[...], wao_ref, OFF_GA, proj)
             + _merge_out(oh, who_ref, OFF_GB, proj))
        y_ref[...] = x_ref[...] + _rms(_dot(m.astype(BF16), wo_ref[...]), post_ref[...])


def _mix_sample_call(x, ck, cv, s0, sinks, pre, post, lbl, hgn, win, wao, who, wo):
    n = x.shape[0]
    tile = min(SAMPLE_TILE, n)
    full = pl.BlockSpec((n, D_MODEL), lambda i: (0, 0))
    kv_spec = pl.BlockSpec((tile, WINDOW, ATTN_KV_W), lambda i: (i, 0, 0))
    st_spec = pl.BlockSpec((tile, HG_W, HG_D), lambda i: (i, 0, 0))
    act = pltpu.VMEM((n, HG_W), F32)
    kv_act = pltpu.VMEM((n, ATTN_KV_W), F32)
    return pl.pallas_call(
        _mix_sample_kernel,
        grid=(n // tile,),
        in_specs=[pl.BlockSpec(memory_space=pltpu.SMEM), full, kv_spec, kv_spec, st_spec,
                  _const_spec((1, D_MODEL)), _const_spec((1, D_MODEL)),
                  _const_spec(lbl.shape), _const_spec((1, HG_D)),
                  HBM_SPEC, HBM_SPEC, HBM_SPEC, HBM_SPEC],
        out_specs=[full, kv_spec, kv_spec, st_spec],
        out_shape=[jax.ShapeDtypeStruct(x.shape, F32),
                   jax.ShapeDtypeStruct(ck.shape, F32),
                   jax.ShapeDtypeStruct(cv.shape, F32),
                   jax.ShapeDtypeStruct(s0.shape, F32)],
        scratch_shapes=[pltpu.VMEM(win.shape, BF16), pltpu.VMEM(wao.shape, BF16),
                        pltpu.VMEM(who.shape, BF16), pltpu.VMEM(wo.shape, BF16),
                        pltpu.VMEM((n, D_MODEL), BF16),
                        pltpu.VMEM((n, ATTN_Q_W), F32), kv_act, kv_act,
                        act, act, act, act,
                        pltpu.VMEM((n, ATTN_Q_W), F32), act],
        compiler_params=pltpu.CompilerParams(
            dimension_semantics=("arbitrary",), vmem_limit_bytes=VMEM_LIMIT),
        name="mix_sample",
    )(sinks, x, ck, cv, s0, pre, post, lbl, hgn, win, wao, who, wo)


def kernel(x_prompt, x_sample, cache_k, cache_v, state_hgrn, norm_ffn1_pre, norm_ffn1_post, w_ffn1_gate, w_ffn1_up, w_ffn1_down, norm_mix_pre, norm_mix_post, w_in, attn_sinks, hgrn_lb_logits, hgrn_norm, w_attn_out, w_hgrn_out, w_out, norm_ffn2_pre, norm_ffn2_post, w_ffn2_gate, w_ffn2_up, w_ffn2_down):
    depth = w_in.shape[0]
    assert depth == 1 and hgrn_lb_logits.shape[0] == 2, "single-layer stack only"
    batch, seq, _ = x_prompt.shape
    n_s = x_sample.shape[0]
    assert x_sample.shape[1] == 1 and seq % WINDOW == 0

    xp = x_prompt.reshape(batch * seq, D_MODEL)
    xs = x_sample.reshape(n_s, D_MODEL)
    ck = cache_k[0].reshape(n_s, WINDOW, ATTN_KV_W)
    cv = cache_v[0].reshape(n_s, WINDOW, ATTN_KV_W)
    s0 = state_hgrn[0].reshape(n_s, HG_W, HG_D)

    xp, xs = _ffn_call(xp, xs, norm_ffn1_pre, norm_ffn1_post,
                       w_ffn1_gate[0], w_ffn1_up[0], w_ffn1_down[0])

    mix_w = (attn_sinks, norm_mix_pre, norm_mix_post, hgrn_lb_logits, hgrn_norm,
             w_in[0], w_attn_out[0], w_hgrn_out[0], w_out[0])
    xp, nkp, nvp, nsp = _mix_prompt_call(xp, batch, *mix_w)
    xs, nks, nvs, nss = _mix_sample_call(xs, ck, cv, s0, *mix_w)

    xp, xs = _ffn_call(xp, xs, norm_ffn2_pre, norm_ffn2_post,
                       w_ffn2_gate[0], w_ffn2_up[0], w_ffn2_down[0])

    kv_shape = (1, -1, WINDOW, ATTN_KV_HEADS, HEAD_DIM)
    st_shape = (1, -1, HG_HEADS, HG_D, HG_D)
    return (xp.reshape(batch, seq, D_MODEL), xs.reshape(n_s, 1, D_MODEL),
            nkp.reshape(kv_shape), nvp.reshape(kv_shape), nsp.reshape(st_shape),
            nks.reshape(kv_shape), nvs.reshape(kv_shape), nss.reshape(st_shape))
```

```python
import functools

import jax
import jax.numpy as jnp
from jax import lax
from jax.experimental import pallas as pl
from jax.experimental.pallas import tpu as pltpu

F32 = jnp.float32
BF16 = jnp.bfloat16

D_MODEL = 1024
FFN_DIM = 2816
HEAD_DIM = 64
ATTN_Q_HEADS = 8
ATTN_KV_HEADS = 2
WINDOW = 128
HG_HEADS = 4
HG_D = 128
EPS = 1e-6

ATTN_Q_W = ATTN_Q_HEADS * HEAD_DIM
ATTN_KV_W = ATTN_KV_HEADS * HEAD_DIM
HG_W = HG_HEADS * HG_D
OFF_QA = 0
OFF_KA = OFF_QA + ATTN_Q_W
OFF_VA = OFF_KA + ATTN_KV_W
OFF_QH = OFF_VA + ATTN_KV_W
OFF_FH = OFF_QH + HG_W
OFF_IH = OFF_FH + HG_W
OFF_GH = OFF_IH + HG_W
OFF_GA = OFF_GH + HG_W
OFF_GB = OFF_GA + D_MODEL
IN_WIDTH = OFF_GB + D_MODEL

LANES = 128
FFN_CHUNK = 256
N_FFN_CHUNKS = FFN_DIM // FFN_CHUNK
FFN_ROWS = 1024
MIX_ROWS = 512
HG_CHUNK = 128
SAMPLE_TILE = 8
VMEM_LIMIT = 56 * 1024 * 1024

NT_DIMS = (((1,), (1,)), ((), ()))
TN_DIMS = (((0,), (0,)), ((), ()))


def _rms(x, g):
    return x * lax.rsqrt(jnp.mean(x * x, axis=-1, keepdims=True) + EPS) * g


def _dot(a, b):
    return jnp.dot(a, b, preferred_element_type=F32)


STREAM_SLOTS = 3
STREAM_CHUNK_BYTES = 1 << 20


def _stream_cast(src_hbm, dst_ref):
    n_rows, n_cols = src_hbm.shape
    rc = max(8, min(n_rows, STREAM_CHUNK_BYTES // (4 * n_cols) // 8 * 8))
    while n_rows % rc:
        rc -= 8
    n = n_rows // rc
    slots = min(STREAM_SLOTS, n)

    def body(stage, sem):
        def copy(c):
            return pltpu.make_async_copy(src_hbm.at[pl.ds(c * rc, rc), :], stage.at[c % slots],
                                         sem.at[c % slots])
        for c in range(slots):
            copy(c).start()
        for c in range(n):
            copy(c).wait()
            dst_ref[c * rc:(c + 1) * rc, :] = stage[c % slots].astype(BF16)
            if c + slots < n:
                copy(c + slots).start()

    pl.run_scoped(body, pltpu.VMEM((slots, rc, n_cols), F32), pltpu.SemaphoreType.DMA((slots,)))


HBM_SPEC = pl.BlockSpec(memory_space=pl.ANY)


def _const_spec(shape):
    zeros = (0,) * len(shape)
    return pl.BlockSpec(shape, lambda *_: zeros, pipeline_mode=pl.Buffered(1))


def _ffn_tile(x, pre, post, wg_ref, wu_ref, wd_ref):
    h = _rms(x, pre).astype(BF16)
    acc = None
    for c in range(N_FFN_CHUNKS):
        cols = slice(c * FFN_CHUNK, (c + 1) * FFN_CHUNK)
        g = _dot(h, wg_ref[:, cols])
        u = _dot(h, wu_ref[:, cols])
        a = (g * jax.nn.sigmoid(g) * u).astype(BF16)
        d = _dot(a, wd_ref[cols, :])
        acc = d if acc is None else acc + d
    return x + 0.5 * _rms(acc, post)


def _ffn_kernel(n_prompt_steps, xp_ref, xs_ref, pre_ref, post_ref, wg_hbm, wu_hbm, wd_hbm,
                yp_ref, ys_ref, wg_ref, wu_ref, wd_ref):
    i = pl.program_id(0)

    @pl.when(i == 0)
    def _():
        _stream_cast(wg_hbm, wg_ref)
        _stream_cast(wu_hbm, wu_ref)
        _stream_cast(wd_hbm, wd_ref)

    @pl.when(i < n_prompt_steps)
    def _():
        yp_ref[...] = _ffn_tile(xp_ref[...], pre_ref[...], post_ref[...], wg_ref, wu_ref, wd_ref)

    @pl.when(i == n_prompt_steps)
    def _():
        ys_ref[...] = _ffn_tile(xs_ref[...], pre_ref[...], post_ref[...], wg_ref, wu_ref, wd_ref)


def _ffn_call(xp, xs, pre, post, wg, wu, wd):
    n_p, n_s = xp.shape[0], xs.shape[0]
    rows = min(FFN_ROWS, n_p)
    steps = n_p // rows
    prompt_spec = pl.BlockSpec((rows, D_MODEL), lambda i: (jnp.minimum(i, steps - 1), 0))
    sample_spec = pl.BlockSpec((n_s, D_MODEL), lambda i: (0, 0))
    return pl.pallas_call(
        functools.partial(_ffn_kernel, steps),
        grid=(steps + 1,),
        in_specs=[prompt_spec, sample_spec,
                  _const_spec((1, D_MODEL)), _const_spec((1, D_MODEL)),
                  HBM_SPEC, HBM_SPEC, HBM_SPEC],
        out_specs=[prompt_spec, sample_spec],
        out_shape=[jax.ShapeDtypeStruct(xp.shape, F32), jax.ShapeDtypeStruct(xs.shape, F32)],
        scratch_shapes=[pltpu.VMEM(wg.shape, BF16), pltpu.VMEM(wu.shape, BF16),
                        pltpu.VMEM(wd.shape, BF16)],
        compiler_params=pltpu.CompilerParams(
            dimension_semantics=("arbitrary",), vmem_limit_bytes=VMEM_LIMIT),
        name="ffn_half",
    )(xp, xs, pre, post, wg, wu, wd)


def _forget_lower_bound(lbl):
    l0, l1 = lbl[0:1, :], lbl[1:2, :]
    m = jnp.maximum(l0, l1)
    e0, e1 = jnp.exp(l0 - m), jnp.exp(l1 - m)
    return e0 / (e0 + e1)


def _lane_is_low(shape):
    return lax.broadcasted_iota(jnp.int32, shape, len(shape) - 1) < HEAD_DIM


def _dup_kv(x):
    swapped = pltpu.roll(x, HEAD_DIM, axis=1)
    low = _lane_is_low(x.shape)
    return jnp.where(low, x, swapped), jnp.where(low, swapped, x)


def _merge_out(h, w_ref, off, gates_from):
    return jax.nn.sigmoid(gates_from(off, D_MODEL)) * _dot(h.astype(BF16), w_ref[...])


def _pair_levels(c):
    t = jnp.arange(c, dtype=jnp.int32)[:, None]
    s = jnp.arange(c, dtype=jnp.int32)[None, :]
    x = jnp.maximum(t ^ s, 1)
    lvl = (31 - lax.clz(x)).astype(jnp.int32)
    return jnp.where(t > s, lvl, -1)


LOG2E = 1.4426950408889634


def _head(a, hd):
    return a[:, hd * HG_D:(hd + 1) * HG_D]


def _hgrn_tile(q_ref, k_ref, v_ref, g_ref, st, levels, n_chunks):
    c = HG_CHUNK
    heads, chunks = range(HG_HEADS), range(n_chunks)
    rows = [slice(ci * c, (ci + 1) * c) for ci in chunks]
    row = lax.broadcasted_iota(jnp.int32, (c, c), 0)
    col = lax.broadcasted_iota(jnp.int32, (c, c), 1)
    tril = (col <= row).astype(F32)
    t = lax.broadcasted_iota(jnp.int32, (c, 1), 0)
    b = [jnp.dot(tril, g_ref[r, :], precision=lax.Precision.HIGHEST,
                 preferred_element_type=F32) * LOG2E for r in rows]
    yield

    a = [[jnp.zeros((c, c), F32) for _ in heads] for _ in chunks]
    last = list(b)
    n, lvl = 1, 0
    while n < c:
        second = (t & n) != 0
        for ci in chunks:
            bc = b[ci]
            if n < 8:
                edge = jnp.where(second, pltpu.roll(last[ci], n, axis=0), last[ci])
                if 2 * n < 8:
                    last[ci] = jnp.where(second, last[ci], pltpu.roll(last[ci], c - n, axis=0))
            else:
                edge = jnp.concatenate(
                    [jnp.broadcast_to(bc[p + n - 1:p + n, :], (2 * n, bc.shape[1]))
                     for p in range(0, c, 2 * n)], axis=0)
            w = jnp.exp2(jnp.where(second, bc - edge, edge - bc))
            x = (jnp.where(second, q_ref[rows[ci], :], k_ref[rows[ci], :]) * w).astype(BF16)
            for hd in heads:
                al = lax.dot_general(_head(x, hd), _head(x, hd), NT_DIMS,
                                     preferred_element_type=F32)
                a[ci][hd] = jnp.where(levels == lvl, al, a[ci][hd])
        n, lvl = 2 * n, lvl + 1
        yield

    local = []
    for ci in chunks:
        q, k, v = q_ref[rows[ci], :], k_ref[rows[ci], :], v_ref[rows[ci], :]
        qk, vb = q * k, v.astype(BF16)
        outs = []
        for hd in heads:
            diag = jnp.sum(_head(qk, hd), axis=1, keepdims=True)
            outs.append(_dot(a[ci][hd].astype(BF16), _head(vb, hd)) + diag * _head(v, hd))
        local.append(outs)
    yield

    result = []
    for ci in chunks:
        q, k, v = q_ref[rows[ci], :], k_ref[rows[ci], :], v_ref[rows[ci], :]
        bc = b[ci]
        qe = (q * jnp.exp2(bc)).astype(BF16)
        b_end = bc[c - 1:c, :]
        k_end = (k * jnp.exp2(b_end - bc)).astype(BF16)
        decay = jnp.exp2(b_end)
        vb = v.astype(BF16)
        outs, new = [], []
        for hd in heads:
            outs.append(local[ci][hd] + lax.dot_general(_head(qe, hd), st[hd].astype(BF16), NT_DIMS,
                                                        preferred_element_type=F32))
            new.append(st[hd] * _head(decay, hd)
                       + lax.dot_general(_head(vb, hd), _head(k_end, hd), TN_DIMS,
                                         preferred_element_type=F32))
        st = new
        result.append(outs)
        yield
    return result, st


def _mix_prompt_kernel(sinks_ref, x_ref, pre_ref, post_ref, lbl_ref, hgn_ref, lvl_ref,
                       win_hbm, wao_hbm, who_hbm, wo_hbm,
                       y_ref, nk_ref, nv_ref, ns_ref,
                       win_ref, wao_ref, who_ref, wo_ref,
                       kbuf, vbuf, st_ref, qh_ref, kh_ref, vh_ref, lf_ref, oa_ref, oh_ref, g_ref):
    j = pl.program_id(1)
    rows = x_ref.shape[0]
    n_blocks = rows // WINDOW

    @pl.when((pl.program_id(0) == 0) & (j == 0))
    def _():
        for src, dst in ((win_hbm, win_ref), (wao_hbm, wao_ref), (who_hbm, who_ref),
                         (wo_hbm, wo_ref)):
            _stream_cast(src, dst)

    @pl.when(j == 0)
    def _():
        kbuf[0:WINDOW, :] = jnp.zeros((WINDOW, LANES), F32)
        vbuf[0:WINDOW, :] = jnp.zeros((WINDOW, LANES), F32)
        st_ref[...] = jnp.zeros(st_ref.shape, F32)

    x = x_ref[...]
    h = _rms(x, pre_ref[...]).astype(BF16)

    def proj(off, width):
        return _dot(h, win_ref[:, off:off + width])

    lb = _forget_lower_bound(lbl_ref[...])

    def proj_slab(off):
        def run():
            z = proj(off, FFN_CHUNK)
            if off < OFF_FH:
                cols = slice(off - OFF_QH, off - OFF_QH + FFN_CHUNK)
                qh_ref[:, cols] = z * jax.nn.sigmoid(z)
            elif off < OFF_IH:
                cols = slice(off - OFF_FH, off - OFF_FH + FFN_CHUNK)
                lbs = lb[:, cols]
                lf_ref[:, cols] = jnp.log(lbs + (1.0 - lbs) * jax.nn.sigmoid(z))
                kh_ref[:, cols] = (1.0 - lbs) * jax.nn.sigmoid(-z)
            elif off < OFF_GH:
                cols = slice(off - OFF_IH, off - OFF_IH + FFN_CHUNK)
                vh_ref[:, cols] = z
            else:
                cols = slice(off - OFF_GH, off - OFF_GH + FFN_CHUNK)
                g_ref[:, cols] = z
        return run
    slabs = [proj_slab(off) for off in range(OFF_QH, IN_WIDTH, FFN_CHUNK)]

    qa = proj(OFF_QA, ATTN_Q_W) * (HEAD_DIM ** -0.5)
    ka = proj(OFF_KA, ATTN_KV_W)
    va = proj(OFF_VA, ATTN_KV_W)
    kbuf[WINDOW:WINDOW + rows, :] = ka
    vbuf[WINDOW:WINDOW + rows, :] = va

    qi = lax.broadcasted_iota(jnp.int32, (WINDOW, 2 * WINDOW), 0)
    kj = lax.broadcasted_iota(jnp.int32, (WINDOW, 2 * WINDOW), 1)
    band = (kj > qi) & (kj <= qi + WINDOW)
    low = _lane_is_low((WINDOW, LANES))
    top = lax.broadcasted_iota(jnp.int32, (2 * WINDOW, 1), 0) < WINDOW
    n_slabs = ATTN_Q_HEADS // 2
    group = ATTN_Q_HEADS // ATTN_KV_HEADS
    kds = [_dup_kv(kbuf[n * WINDOW:(n + 2) * WINDOW, :]) for n in range(n_blocks)]
    vds = [_dup_kv(vbuf[n * WINDOW:(n + 2) * WINDOW, :]) for n in range(n_blocks)]

    def scores(n, slab):
        qs = qa[n * WINDOW:(n + 1) * WINDOW, slab * LANES:(slab + 1) * LANES]
        q2 = jnp.concatenate([jnp.where(low, qs, 0.0), jnp.where(low, 0.0, qs)], axis=0)
        return lax.dot_general(q2.astype(BF16), kds[n][(2 * slab) // group].astype(BF16), NT_DIMS,
                               preferred_element_type=F32)

    def attend(n, slab, s):
        valid = band & ((j > 0) | (kj >= WINDOW)) if n == 0 else band
        s = jnp.where(jnp.concatenate([valid, valid], axis=0), s, -jnp.inf)
        sk = jnp.where(top, sinks_ref[0, 2 * slab], sinks_ref[0, 2 * slab + 1])
        m = jnp.maximum(jnp.max(s, axis=1, keepdims=True), sk)
        p = jnp.exp(s - m)
        den = jnp.sum(p, axis=1, keepdims=True) + jnp.exp(sk - m)
        o2 = _dot(p.astype(BF16), vds[n][(2 * slab) // group].astype(BF16)) / den
        oa_ref[n * WINDOW:(n + 1) * WINDOW, slab * LANES:(slab + 1) * LANES] = (
            jnp.where(low, o2[0:WINDOW], o2[WINDOW:2 * WINDOW]))

    order = [(n, slab) for n in range(n_blocks) for slab in range(n_slabs)]
    pending = scores(*order[0])
    for idx, (n, slab) in enumerate(order):
        nxt = scores(*order[idx + 1]) if idx + 1 < len(order) else None
        if slabs:
            slabs.pop(0)()
        attend(n, slab, pending)
        pending = nxt
    kbuf[0:WINDOW, :] = kbuf[rows:rows + WINDOW, :]
    vbuf[0:WINDOW, :] = vbuf[rows:rows + WINDOW, :]
    for run in slabs:
        run()

    @pl.when(j == pl.num_programs(1) - 1)
    def _():
        nk_ref[0] = ka[rows - WINDOW:rows, :]
        nv_ref[0] = va[rows - WINDOW:rows, :]

    hgn = hgn_ref[...]
    n_chunks = rows // HG_CHUNK
    st0 = [st_ref[hd * HG_D:(hd + 1) * HG_D, :] for hd in range(HG_HEADS)]
    stages = _hgrn_tile(qh_ref, kh_ref, vh_ref, lf_ref, st0, lvl_ref[...], n_chunks)
    while True:
        try:
            next(stages)
        except StopIteration as done:
            outs, states = done.value
            break
    for hd in range(HG_HEADS):
        cs = slice(hd * HG_D, (hd + 1) * HG_D)
        st_ref[cs, :] = states[hd]
        for ci in range(n_chunks):
            oh_ref[ci * HG_CHUNK:(ci + 1) * HG_CHUNK, cs] = _rms(outs[ci][hd], hgn)

    @pl.when(j == pl.num_programs(1) - 1)
    def _():
        for hd in range(HG_HEADS):
            cs = slice(hd * HG_D, (hd + 1) * HG_D)
            ns_ref[0, cs, :] = st_ref[cs, :].T

    gh = g_ref[:, 0:HG_W]
    oh = oh_ref[...] * (gh * jax.nn.sigmoid(gh))

    def gate(off, width):
        return g_ref[:, off - OFF_GH:off - OFF_GH + width]
    m = (_merge_out(oa_ref[...], wao_ref, OFF_GA, gate)
         + _merge_out(oh, who_ref, OFF_GB, gate))
    y_ref[...] = x + _rms(_dot(m.astype(BF16), wo_ref[...]), post_ref[...])


def _mix_prompt_call(x, batch, sinks, pre, post, lbl, hgn, win, wao, who, wo):
    seq = x.shape[0] // batch
    rows = min(MIX_ROWS, seq)
    steps = seq // rows
    x_spec = pl.BlockSpec((rows, D_MODEL), lambda b, j: (b * steps + j, 0))
    kv_spec = pl.BlockSpec((1, WINDOW, ATTN_KV_W), lambda b, j: (b, 0, 0))
    st_spec = pl.BlockSpec((1, HG_W, HG_D), lambda b, j: (b, 0, 0))
    act = pltpu.VMEM((rows, HG_W), F32)
    return pl.pallas_call(
        _mix_prompt_kernel,
        grid=(batch, steps),
        in_specs=[pl.BlockSpec(memory_space=pltpu.SMEM), x_spec,
                  _const_spec((1, D_MODEL)), _const_spec((1, D_MODEL)),
                  _const_spec(lbl.shape), _const_spec((1, HG_D)),
                  _const_spec((HG_CHUNK, HG_CHUNK)),
                  HBM_SPEC, HBM_SPEC, HBM_SPEC, HBM_SPEC],
        out_specs=[x_spec, kv_spec, kv_spec, st_spec],
        out_shape=[jax.ShapeDtypeStruct(x.shape, F32),
                   jax.ShapeDtypeStruct((batch, WINDOW, ATTN_KV_W), F32),
                   jax.ShapeDtypeStruct((batch, WINDOW, ATTN_KV_W), F32),
                   jax.ShapeDtypeStruct((batch, HG_W, HG_D), F32)],
        scratch_shapes=[pltpu.VMEM(win.shape, BF16), pltpu.VMEM(wao.shape, BF16),
                        pltpu.VMEM(who.shape, BF16), pltpu.VMEM(wo.shape, BF16),
                        pltpu.VMEM((rows + WINDOW, ATTN_KV_W), F32),
                        pltpu.VMEM((rows + WINDOW, ATTN_KV_W), F32),
                        pltpu.VMEM((HG_W, HG_D), F32),
                        act, act, act, act,
                        pltpu.VMEM((rows, ATTN_Q_W), F32), act,
                        pltpu.VMEM((rows, IN_WIDTH - OFF_GH), F32)],
        compiler_params=pltpu.CompilerParams(
            dimension_semantics=("arbitrary", "arbitrary"), vmem_limit_bytes=VMEM_LIMIT),
        name="mix_prompt",
    )(sinks, x, pre, post, lbl, hgn, _pair_levels(HG_CHUNK), win, wao, who, wo)


def _row_select(rows_list):
    n = -(-len(rows_list) // 8) * 8
    lanes = rows_list[0].shape[1]
    ridx = lax.broadcasted_iota(jnp.int32, (n, lanes), 0)
    out = jnp.zeros((n, lanes), F32)
    for i, r in enumerate(rows_list):
        out = jnp.where(ridx == i, jnp.broadcast_to(r, (n, lanes)), out)
    return out


def _mix_sample_kernel(sinks_ref, x_ref, ck_ref, cv_ref, s0_ref, pre_ref, post_ref, lbl_ref, hgn_ref,
                       win_hbm, wao_hbm, who_hbm, wo_hbm,
                       y_ref, nk_ref, nv_ref, ns_ref,
                       win_ref, wao_ref, who_ref, wo_ref,
                       h_ref, qa_ref, ka_ref, va_ref, qh_ref, kh_ref, fh_ref, vh_ref, oa_ref, oh_ref):
    i = pl.program_id(0)
    tile = ck_ref.shape[0]

    def proj(off, width):
        return _dot(h_ref[...], win_ref[:, off:off + width])

    @pl.when(i == 0)
    def _():
        for src, dst in ((win_hbm, win_ref), (wao_hbm, wao_ref), (who_hbm, who_ref),
                         (wo_hbm, wo_ref)):
            _stream_cast(src, dst)
        h_ref[...] = _rms(x_ref[...], pre_ref[...]).astype(BF16)
        qa_ref[...] = proj(OFF_QA, ATTN_Q_W) * (HEAD_DIM ** -0.5)
        ka_ref[...] = proj(OFF_KA, ATTN_KV_W)
        va_ref[...] = proj(OFF_VA, ATTN_KV_W)
        lb = _forget_lower_bound(lbl_ref[...])
        qh = proj(OFF_QH, HG_W)
        qh_ref[...] = qh * jax.nn.sigmoid(qh)
        fp = proj(OFF_FH, HG_W)
        fh_ref[...] = jnp.exp(jnp.log(lb + (1.0 - lb) * jax.nn.sigmoid(fp)))
        kh_ref[...] = (1.0 - lb) * jax.nn.sigmoid(-fp)
        vh_ref[...] = proj(OFF_IH, HG_W)

    low = _lane_is_low((1, LANES))
    pos = lax.broadcasted_iota(jnp.int32, (WINDOW, LANES), 0)
    hrow = lax.broadcasted_iota(jnp.int32, (ATTN_Q_HEADS, 1), 0)
    sk = jnp.zeros((ATTN_Q_HEADS, 1), F32)
    for hd in range(ATTN_Q_HEADS):
        sk = jnp.where(hrow == hd, sinks_ref[0, hd], sk)
    group = ATTN_Q_HEADS // ATTN_KV_HEADS
    hgn = hgn_ref[...]

    r8 = pl.ds(pl.multiple_of(i * tile, tile), tile)
    qa8, ka8, va8 = qa_ref[r8, :], ka_ref[r8, :], va_ref[r8, :]
    fh8, kh8, qh8, vh8 = fh_ref[r8, :], kh_ref[r8, :], qh_ref[r8, :], vh_ref[r8, :]
    oa_rows = [[] for _ in range(ATTN_Q_HEADS // 2)]
    oh_rows = [[] for _ in range(HG_HEADS)]
    seqs = range(tile)
    kws, vws, q8s = [], [], []
    for bi in seqs:
        r1 = slice(bi, bi + 1)
        kw = jnp.where(pos == WINDOW - 1, ka8[r1, :], pltpu.roll(ck_ref[bi], WINDOW - 1, axis=0))
        vw = jnp.where(pos == WINDOW - 1, va8[r1, :], pltpu.roll(cv_ref[bi], WINDOW - 1, axis=0))
        nk_ref[bi] = kw
        nv_ref[bi] = vw
        kws.append(kw.astype(BF16))
        vws.append(vw.astype(BF16))
        qrows = []
        for hd in range(ATTN_Q_HEADS):
            slab = qa8[r1, (hd // 2) * LANES:(hd // 2 + 1) * LANES]
            in_place = (hd % 2) == (hd // group)
            src = slab if in_place else pltpu.roll(slab, HEAD_DIM, axis=1)
            on_kv_lanes = low if hd // group == 0 else jnp.logical_not(low)
            qrows.append(jnp.where(on_kv_lanes, src, 0.0))
        q8s.append(_row_select(qrows).astype(BF16))
    scores = [lax.dot_general(q8s[bi], kws[bi], NT_DIMS, preferred_element_type=F32) for bi in seqs]
    probs = []
    for s in scores:
        m = jnp.maximum(jnp.max(s, axis=1, keepdims=True), sk)
        p = jnp.exp(s - m)
        probs.append((p / (jnp.sum(p, axis=1, keepdims=True) + jnp.exp(sk - m))).astype(BF16))
    for bi in seqs:
        o8 = _dot(probs[bi], vws[bi])
        for slab in range(ATTN_Q_HEADS // 2):
            kv = (2 * slab) // group
            even, odd = o8[2 * slab:2 * slab + 1, :], o8[2 * slab + 1:2 * slab + 2, :]
            if kv == 0:
                out = jnp.where(low, even, pltpu.roll(odd, HEAD_DIM, axis=1))
            else:
                out = jnp.where(low, pltpu.roll(even, HEAD_DIM, axis=1), odd)
            oa_rows[slab].append(out)

    def columns(a8, hd):
        pad = jnp.zeros((HG_D - tile, HG_D), F32)
        return jnp.concatenate([_head(a8, hd), pad], axis=0).T
    seq_row = lax.broadcasted_iota(jnp.int32, (HG_D, HG_D), 0)
    for hd in range(HG_HEADS):
        rs = slice(hd * HG_D, (hd + 1) * HG_D)
        f_cols = columns(fh8, hd)
        k_cols = columns(kh8, hd).astype(BF16)
        v_rows = jnp.concatenate([_head(vh8, hd), jnp.zeros((HG_D - tile, HG_D), F32)], axis=0)
        q_rows = _head(qh8, hd).astype(BF16)
        outers = [_dot(k_cols, jnp.where(seq_row == bi, v_rows, 0.0).astype(BF16))
                  for bi in seqs]
        for bi in seqs:
            f_col = jnp.broadcast_to(f_cols[:, bi:bi + 1], (HG_D, HG_D))
            ns_ref[bi, rs, :] = f_col * s0_ref[bi, rs, :] + outers[bi]
        outs = [_dot(q_rows, ns_ref[bi, rs, :].astype(BF16)) for bi in seqs]
        for bi in seqs:
            oh_rows[hd].append(_rms(outs[bi][bi:bi + 1, :], hgn))
    for slab in range(ATTN_Q_HEADS // 2):
        oa_ref[r8, slab * LANES:(slab + 1) * LANES] = _row_select(oa_rows[slab])
    for hd in range(HG_HEADS):
        oh_ref[r8, hd * HG_D:(hd + 1) * HG_D] = _row_select(oh_rows[hd])

    @pl.when(i == pl.num_programs(0) - 1)
    def _():
        gh = proj(OFF_GH, HG_W)
        oh = oh_ref[...] * (gh * jax.nn.sigmoid(gh))
        m = (_merge_out(oa_ref[...], wao_ref, OFF_GA, proj)
             + _merge_out(oh, who_ref, OFF_GB, proj))
        y_ref[...] = x_ref[...] + _rms(_dot(m.astype(BF16), wo_ref[...]), post_ref[...])


def _mix_sample_call(x, ck, cv, s0, sinks, pre, post, lbl, hgn, win, wao, who, wo):
    n = x.shape[0]
    tile = min(SAMPLE_TILE, n)
    full = pl.BlockSpec((n, D_MODEL), lambda i: (0, 0))
    kv_spec = pl.BlockSpec((tile, WINDOW, ATTN_KV_W), lambda i: (i, 0, 0))
    st_spec = pl.BlockSpec((tile, HG_W, HG_D), lambda i: (i, 0, 0))
    act = pltpu.VMEM((n, HG_W), F32)
    kv_act = pltpu.VMEM((n, ATTN_KV_W), F32)
    return pl.pallas_call(
        _mix_sample_kernel,
        grid=(n // tile,),
        in_specs=[pl.BlockSpec(memory_space=pltpu.SMEM), full, kv_spec, kv_spec, st_spec,
                  _const_spec((1, D_MODEL)), _const_spec((1, D_MODEL)),
                  _const_spec(lbl.shape), _const_spec((1, HG_D)),
                  HBM_SPEC, HBM_SPEC, HBM_SPEC, HBM_SPEC],
        out_specs=[full, kv_spec, kv_spec, st_spec],
        out_shape=[jax.ShapeDtypeStruct(x.shape, F32),
                   jax.ShapeDtypeStruct(ck.shape, F32),
                   jax.ShapeDtypeStruct(cv.shape, F32),
                   jax.ShapeDtypeStruct(s0.shape, F32)],
        scratch_shapes=[pltpu.VMEM(win.shape, BF16), pltpu.VMEM(wao.shape, BF16),
                        pltpu.VMEM(who.shape, BF16), pltpu.VMEM(wo.shape, BF16),
                        pltpu.VMEM((n, D_MODEL), BF16),
                        pltpu.VMEM((n, ATTN_Q_W), F32), kv_act, kv_act,
                        act, act, act, act,
                        pltpu.VMEM((n, ATTN_Q_W), F32), act],
        compiler_params=pltpu.CompilerParams(
            dimension_semantics=("arbitrary",), vmem_limit_bytes=VMEM_LIMIT),
        name="mix_sample",
    )(sinks, x, ck, cv, s0, pre, post, lbl, hgn, win, wao, who, wo)


def kernel(x_prompt, x_sample, cache_k, cache_v, state_hgrn, norm_ffn1_pre, norm_ffn1_post, w_ffn1_gate, w_ffn1_up, w_ffn1_down, norm_mix_pre, norm_mix_post, w_in, attn_sinks, hgrn_lb_logits, hgrn_norm, w_attn_out, w_hgrn_out, w_out, norm_ffn2_pre, norm_ffn2_post, w_ffn2_gate, w_ffn2_up, w_ffn2_down):
    depth = w_in.shape[0]
    assert depth == 1 and hgrn_lb_logits.shape[0] == 2, "single-layer stack only"
    batch, seq, _ = x_prompt.shape
    n_s = x_sample.shape[0]
    assert x_sample.shape[1] == 1 and seq % WINDOW == 0

    xp = x_prompt.reshape(batch * seq, D_MODEL)
    xs = x_sample.reshape(n_s, D_MODEL)
    ck = cache_k[0].reshape(n_s, WINDOW, ATTN_KV_W)
    cv = cache_v[0].reshape(n_s, WINDOW, ATTN_KV_W)
    s0 = state_hgrn[0].reshape(n_s, HG_W, HG_D)

    xp, xs = _ffn_call(xp, xs, norm_ffn1_pre, norm_ffn1_post,
                       w_ffn1_gate[0], w_ffn1_up[0], w_ffn1_down[0])

    mix_w = (attn_sinks, norm_mix_pre, norm_mix_post, hgrn_lb_logits, hgrn_norm,
             w_in[0], w_attn_out[0], w_hgrn_out[0], w_out[0])
    xp, nkp, nvp, nsp = _mix_prompt_call(xp, batch, *mix_w)
    xs, nks, nvs, nss = _mix_sample_call(xs, ck, cv, s0, *mix_w)

    xp, xs = _ffn_call(xp, xs, norm_ffn2_pre, norm_ffn2_post,
                       w_ffn2_gate[0], w_ffn2_up[0], w_ffn2_down[0])

    kv_shape = (1, -1, WINDOW, ATTN_KV_HEADS, HEAD_DIM)
    st_shape = (1, -1, HG_HEADS, HG_D, HG_D)
    return (xp.reshape(batch, seq, D_MODEL), xs.reshape(n_s, 1, D_MODEL),
            nkp.reshape(kv_shape), nvp.reshape(kv_shape), nsp.reshape(st_shape),
            nks.reshape(kv_shape), nvs.reshape(kv_shape), nss.reshape(st_shape))
```

```python
import functools

import jax
import jax.numpy as jnp
from jax import lax
from jax.experimental import pallas as pl
from jax.experimental.pallas import tpu as pltpu

F32 = jnp.float32
BF16 = jnp.bfloat16

D_MODEL = 1024
FFN_DIM = 2816
HEAD_DIM = 64
ATTN_Q_HEADS = 8
ATTN_KV_HEADS = 2
WINDOW = 128
HG_HEADS = 4
HG_D = 128
EPS = 1e-6

ATTN_Q_W = ATTN_Q_HEADS * HEAD_DIM
ATTN_KV_W = ATTN_KV_HEADS * HEAD_DIM
HG_W = HG_HEADS * HG_D
OFF_QA = 0
OFF_KA = OFF_QA + ATTN_Q_W
OFF_VA = OFF_KA + ATTN_KV_W
OFF_QH = OFF_VA + ATTN_KV_W
OFF_FH = OFF_QH + HG_W
OFF_IH = OFF_FH + HG_W
OFF_GH = OFF_IH + HG_W
OFF_GA = OFF_GH + HG_W
OFF_GB = OFF_GA + D_MODEL
IN_WIDTH = OFF_GB + D_MODEL

LANES = 128
FFN_CHUNK = 256
N_FFN_CHUNKS = FFN_DIM // FFN_CHUNK
FFN_ROWS = 1024
MIX_ROWS = 512
HG_CHUNK = 128
SAMPLE_TILE = 8
VMEM_LIMIT = 56 * 1024 * 1024

NT_DIMS = (((1,), (1,)), ((), ()))
TN_DIMS = (((0,), (0,)), ((), ()))


def _rms(x, g):
    return x * lax.rsqrt(jnp.mean(x * x, axis=-1, keepdims=True) + EPS) * g


def _dot(a, b):
    return jnp.dot(a, b, preferred_element_type=F32)


STREAM_SLOTS = 3
STREAM_CHUNK_BYTES = 1 << 20


def _stream_cast(src_hbm, dst_ref):
    n_rows, n_cols = src_hbm.shape
    rc = max(8, min(n_rows, STREAM_CHUNK_BYTES // (4 * n_cols) // 8 * 8))
    while n_rows % rc:
        rc -= 8
    n = n_rows // rc
    slots = min(STREAM_SLOTS, n)

    def body(stage, sem):
        def copy(c):
            return pltpu.make_async_copy(src_hbm.at[pl.ds(c * rc, rc), :], stage.at[c % slots],
                                         sem.at[c % slots])
        for c in range(slots):
            copy(c).start()
        for c in range(n):
            copy(c).wait()
            dst_ref[c * rc:(c + 1) * rc, :] = stage[c % slots].astype(BF16)
            if c + slots < n:
                copy(c + slots).start()

    pl.run_scoped(body, pltpu.VMEM((slots, rc, n_cols), F32), pltpu.SemaphoreType.DMA((slots,)))


HBM_SPEC = pl.BlockSpec(memory_space=pl.ANY)


def _const_spec(shape):
    zeros = (0,) * len(shape)
    return pl.BlockSpec(shape, lambda *_: zeros, pipeline_mode=pl.Buffered(1))


def _ffn_tile(x, pre, post, wg_ref, wu_ref, wd_ref):
    h = _rms(x, pre).astype(BF16)
    acc = None
    for c in range(N_FFN_CHUNKS):
        cols = slice(c * FFN_CHUNK, (c + 1) * FFN_CHUNK)
        g = _dot(h, wg_ref[:, cols])
        u = _dot(h, wu_ref[:, cols])
        a = (g * jax.nn.sigmoid(g) * u).astype(BF16)
        d = _dot(a, wd_ref[cols, :])
        acc = d if acc is None else acc + d
    return x + 0.5 * _rms(acc, post)


def _ffn_kernel(n_prompt_steps, xp_ref, xs_ref, pre_ref, post_ref, wg_hbm, wu_hbm, wd_hbm,
                yp_ref, ys_ref, wg_ref, wu_ref, wd_ref):
    i = pl.program_id(0)

    @pl.when(i == 0)
    def _():
        _stream_cast(wg_hbm, wg_ref)
        _stream_cast(wu_hbm, wu_ref)
        _stream_cast(wd_hbm, wd_ref)

    @pl.when(i < n_prompt_steps)
    def _():
        yp_ref[...] = _ffn_tile(xp_ref[...], pre_ref[...], post_ref[...], wg_ref, wu_ref, wd_ref)

    @pl.when(i == n_prompt_steps)
    def _():
        ys_ref[...] = _ffn_tile(xs_ref[...], pre_ref[...], post_ref[...], wg_ref, wu_ref, wd_ref)


def _ffn_call(xp, xs, pre, post, wg, wu, wd):
    n_p, n_s = xp.shape[0], xs.shape[0]
    rows = min(FFN_ROWS, n_p)
    steps = n_p // rows
    prompt_spec = pl.BlockSpec((rows, D_MODEL), lambda i: (jnp.minimum(i, steps - 1), 0))
    sample_spec = pl.BlockSpec((n_s, D_MODEL), lambda i: (0, 0))
    return pl.pallas_call(
        functools.partial(_ffn_kernel, steps),
        grid=(steps + 1,),
        in_specs=[prompt_spec, sample_spec,
                  _const_spec((1, D_MODEL)), _const_spec((1, D_MODEL)),
                  HBM_SPEC, HBM_SPEC, HBM_SPEC],
        out_specs=[prompt_spec, sample_spec],
        out_shape=[jax.ShapeDtypeStruct(xp.shape, F32), jax.ShapeDtypeStruct(xs.shape, F32)],
        scratch_shapes=[pltpu.VMEM(wg.shape, BF16), pltpu.VMEM(wu.shape, BF16),
                        pltpu.VMEM(wd.shape, BF16)],
        compiler_params=pltpu.CompilerParams(
            dimension_semantics=("arbitrary",), vmem_limit_bytes=VMEM_LIMIT),
        name="ffn_half",
    )(xp, xs, pre, post, wg, wu, wd)


def _forget_lower_bound(lbl):
    l0, l1 = lbl[0:1, :], lbl[1:2, :]
    m = jnp.maximum(l0, l1)
    e0, e1 = jnp.exp(l0 - m), jnp.exp(l1 - m)
    return e0 / (e0 + e1)


def _lane_is_low(shape):
    return lax.broadcasted_iota(jnp.int32, shape, len(shape) - 1) < HEAD_DIM


def _dup_kv(x):
    swapped = pltpu.roll(x, HEAD_DIM, axis=1)
    low = _lane_is_low(x.shape)
    return jnp.where(low, x, swapped), jnp.where(low, swapped, x)


def _merge_out(h, w_ref, off, gates_from):
    return jax.nn.sigmoid(gates_from(off, D_MODEL)) * _dot(h.astype(BF16), w_ref[...])


def _pair_levels(c):
    t = jnp.arange(c, dtype=jnp.int32)[:, None]
    s = jnp.arange(c, dtype=jnp.int32)[None, :]
    x = jnp.maximum(t ^ s, 1)
    lvl = (31 - lax.clz(x)).astype(jnp.int32)
    return jnp.where(t > s, lvl, -1)


LOG2E = 1.4426950408889634


def _head(a, hd):
    return a[:, hd * HG_D:(hd + 1) * HG_D]


HG_LEVELS = HG_CHUNK.bit_length() - 1


def _chunk_rows(ci):
    return slice(ci * HG_CHUNK, (ci + 1) * HG_CHUNK)


def _hgrn_log_decay(g_ref, b_ref, n_chunks):
    c = HG_CHUNK
    row = lax.broadcasted_iota(jnp.int32, (c, c), 0)
    col = lax.broadcasted_iota(jnp.int32, (c, c), 1)
    tril = (col <= row).astype(F32)
    for ci in range(n_chunks):
        r = _chunk_rows(ci)
        b_ref[r, :] = jnp.dot(tril, g_ref[r, :], precision=lax.Precision.HIGHEST,
                              preferred_element_type=F32) * LOG2E


def _hgrn_factor_jobs(q_ref, k_ref, b_ref, x_ref, qe_ref, ke_ref, n_chunks):
    c = HG_CHUNK
    t = lax.broadcasted_iota(jnp.int32, (c, 1), 0)
    last = {}

    def level_job(ci, lvl):
        def run():
            n = 1 << lvl
            r = _chunk_rows(ci)
            bc = b_ref[r, :]
            second = (t & n) != 0
            if n < 8:
                prev = last.get(ci, bc)
                edge = jnp.where(second, pltpu.roll(prev, n, axis=0), prev)
                if 2 * n < 8:
                    last[ci] = jnp.where(second, prev, pltpu.roll(prev, c - n, axis=0))
            else:
                edge = jnp.concatenate(
                    [jnp.broadcast_to(bc[p + n - 1:p + n, :], (2 * n, bc.shape[1]))
                     for p in range(0, c, 2 * n)], axis=0)
            w = jnp.exp2(jnp.where(second, bc - edge, edge - bc))
            x_ref[lvl, r, :] = (jnp.where(second, q_ref[r, :], k_ref[r, :]) * w).astype(BF16)
        return run

    def state_job(ci):
        def run():
            r = _chunk_rows(ci)
            bc = b_ref[r, :]
            qe_ref[r, :] = (q_ref[r, :] * jnp.exp2(bc)).astype(BF16)
            ke_ref[r, :] = (k_ref[r, :] * jnp.exp2(bc[c - 1:c, :] - bc)).astype(BF16)
        return run

    jobs = [level_job(ci, lvl) for lvl in range(HG_LEVELS) for ci in range(n_chunks)]
    return jobs + [state_job(ci) for ci in range(n_chunks)]


def _hgrn_pair_jobs(x_ref, levels, n_chunks):
    c = HG_CHUNK
    heads, chunks = range(HG_HEADS), range(n_chunks)
    a = [[jnp.zeros((c, c), F32) for _ in heads] for _ in chunks]

    def job(lvl, ci):
        def run():
            x = x_ref[lvl, _chunk_rows(ci), :]
            for hd in heads:
                al = lax.dot_general(_head(x, hd), _head(x, hd), NT_DIMS,
                                     preferred_element_type=F32)
                a[ci][hd] = jnp.where(levels == lvl, al, a[ci][hd])
        return run
    return [job(lvl, ci) for lvl in range(HG_LEVELS) for ci in chunks], a


def _hgrn_finish(q_ref, k_ref, v_ref, b_ref, a, qe_ref, ke_ref, st, n_chunks):
    c = HG_CHUNK
    heads, chunks = range(HG_HEADS), range(n_chunks)
    local = []
    for ci in chunks:
        r = _chunk_rows(ci)
        q, k, v = q_ref[r, :], k_ref[r, :], v_ref[r, :]
        qk, vb = q * k, v.astype(BF16)
        outs = []
        for hd in heads:
            diag = jnp.sum(_head(qk, hd), axis=1, keepdims=True)
            outs.append(_dot(a[ci][hd].astype(BF16), _head(vb, hd)) + diag * _head(v, hd))
        local.append(outs)
        yield

    result = []
    for ci in chunks:
        r = _chunk_rows(ci)
        qe, ke, vb = qe_ref[r, :], ke_ref[r, :], v_ref[r, :].astype(BF16)
        decay = jnp.exp2(b_ref[ci * c + c - 1:ci * c + c, :])
        outs, new = [], []
        for hd in heads:
            outs.append(local[ci][hd] + lax.dot_general(_head(qe, hd), st[hd].astype(BF16), NT_DIMS,
                                                        preferred_element_type=F32))
            new.append(st[hd] * _head(decay, hd)
                       + lax.dot_general(_head(vb, hd), _head(ke, hd), TN_DIMS,
                                         preferred_element_type=F32))
        st = new
        result.append(outs)
        yield
    return result, st


def _mix_prompt_kernel(sinks_ref, x_ref, pre_ref, post_ref, lbl_ref, hgn_ref, lvl_ref,
                       win_hbm, wao_hbm, who_hbm, wo_hbm,
                       y_ref, nk_ref, nv_ref, ns_ref,
                       win_ref, wao_ref, who_ref, wo_ref,
                       kbuf, vbuf, st_ref, qh_ref, kh_ref, vh_ref, lf_ref, oa_ref, oh_ref, g_ref,
                       b_ref, xl_ref, qe_ref, ke_ref):
    j = pl.program_id(1)
    rows = x_ref.shape[0]
    n_blocks = rows // WINDOW

    @pl.when((pl.program_id(0) == 0) & (j == 0))
    def _():
        for src, dst in ((win_hbm, win_ref), (wao_hbm, wao_ref), (who_hbm, who_ref),
                         (wo_hbm, wo_ref)):
            _stream_cast(src, dst)

    @pl.when(j == 0)
    def _():
        kbuf[0:WINDOW, :] = jnp.zeros((WINDOW, LANES), F32)
        vbuf[0:WINDOW, :] = jnp.zeros((WINDOW, LANES), F32)
        st_ref[...] = jnp.zeros(st_ref.shape, F32)

    x = x_ref[...]
    h = _rms(x, pre_ref[...]).astype(BF16)

    def proj(off, width):
        return _dot(h, win_ref[:, off:off + width])

    lb = _forget_lower_bound(lbl_ref[...])
    n_chunks = rows // HG_CHUNK

    def proj_slab(off):
        def run():
            z = proj(off, FFN_CHUNK)
            if off < OFF_FH:
                cols = slice(off - OFF_QH, off - OFF_QH + FFN_CHUNK)
                qh_ref[:, cols] = z * jax.nn.sigmoid(z)
            elif off < OFF_IH:
                cols = slice(off - OFF_FH, off - OFF_FH + FFN_CHUNK)
                lbs = lb[:, cols]
                lf_ref[:, cols] = jnp.log(lbs + (1.0 - lbs) * jax.nn.sigmoid(z))
                kh_ref[:, cols] = (1.0 - lbs) * jax.nn.sigmoid(-z)
            elif off < OFF_GH:
                cols = slice(off - OFF_IH, off - OFF_IH + FFN_CHUNK)
                vh_ref[:, cols] = z
            else:
                cols = slice(off - OFF_GH, off - OFF_GH + FFN_CHUNK)
                g_ref[:, cols] = z
        return run
    hgrn_slabs = [proj_slab(off) for off in range(OFF_QH, OFF_GH, FFN_CHUNK)]
    gate_slabs = [proj_slab(off) for off in range(OFF_GH, IN_WIDTH, FFN_CHUNK)]
    jobs = _hgrn_factor_jobs(qh_ref, kh_ref, b_ref, xl_ref, qe_ref, ke_ref, n_chunks)
    pair_jobs, in_chunk = _hgrn_pair_jobs(xl_ref, lvl_ref[...], n_chunks)

    qa = proj(OFF_QA, ATTN_Q_W) * (HEAD_DIM ** -0.5)
    ka = proj(OFF_KA, ATTN_KV_W)
    va = proj(OFF_VA, ATTN_KV_W)
    kbuf[WINDOW:WINDOW + rows, :] = ka
    vbuf[WINDOW:WINDOW + rows, :] = va

    qi = lax.broadcasted_iota(jnp.int32, (WINDOW, 2 * WINDOW), 0)
    kj = lax.broadcasted_iota(jnp.int32, (WINDOW, 2 * WINDOW), 1)
    band = (kj > qi) & (kj <= qi + WINDOW)
    low = _lane_is_low((WINDOW, LANES))
    top = lax.broadcasted_iota(jnp.int32, (2 * WINDOW, 1), 0) < WINDOW
    n_slabs = ATTN_Q_HEADS // 2
    group = ATTN_Q_HEADS // ATTN_KV_HEADS
    kds = [_dup_kv(kbuf[n * WINDOW:(n + 2) * WINDOW, :]) for n in range(n_blocks)]
    vds = [_dup_kv(vbuf[n * WINDOW:(n + 2) * WINDOW, :]) for n in range(n_blocks)]

    def scores(n, slab):
        qs = qa[n * WINDOW:(n + 1) * WINDOW, slab * LANES:(slab + 1) * LANES]
        q2 = jnp.concatenate([jnp.where(low, qs, 0.0), jnp.where(low, 0.0, qs)], axis=0)
        return lax.dot_general(q2.astype(BF16), kds[n][(2 * slab) // group].astype(BF16), NT_DIMS,
                               preferred_element_type=F32)

    def attend(n, slab, s):
        valid = band & ((j > 0) | (kj >= WINDOW)) if n == 0 else band
        s = jnp.where(jnp.concatenate([valid, valid], axis=0), s, -jnp.inf)
        sk = jnp.where(top, sinks_ref[0, 2 * slab], sinks_ref[0, 2 * slab + 1])
        m = jnp.maximum(jnp.max(s, axis=1, keepdims=True), sk)
        p = jnp.exp(s - m)
        den = jnp.sum(p, axis=1, keepdims=True) + jnp.exp(sk - m)
        o2 = _dot(p.astype(BF16), vds[n][(2 * slab) // group].astype(BF16)) / den
        oa_ref[n * WINDOW:(n + 1) * WINDOW, slab * LANES:(slab + 1) * LANES] = (
            jnp.where(low, o2[0:WINDOW], o2[WINDOW:2 * WINDOW]))

    order = [(n, slab) for n in range(n_blocks) for slab in range(n_slabs)]
    early = len(hgrn_slabs)
    late_gates = gate_slabs[len(gate_slabs) // 2:]
    gate_slabs = gate_slabs[:len(gate_slabs) // 2]
    pending = scores(*order[0])
    ready_pairs = 0
    for idx, (n, slab) in enumerate(order):
        nxt = scores(*order[idx + 1]) if idx + 1 < len(order) else None
        if idx < early:
            hgrn_slabs[idx]()
        else:
            if idx == early:
                _hgrn_log_decay(lf_ref, b_ref, n_chunks)
            if gate_slabs and (idx - early) % 2 == 0:
                gate_slabs.pop(0)()
            for _ in range(min(len(pair_jobs), ready_pairs)):
                pair_jobs.pop(0)()
            ready_pairs = 0
            for _ in range(-(-len(jobs) // (len(order) - 1 - idx)) if idx + 1 < len(order) else 0):
                if jobs:
                    jobs.pop(0)()
                    ready_pairs += 1
        attend(n, slab, pending)
        pending = nxt
    kbuf[0:WINDOW, :] = kbuf[rows:rows + WINDOW, :]
    vbuf[0:WINDOW, :] = vbuf[rows:rows + WINDOW, :]
    assert not jobs and not hgrn_slabs[early:]
    late = gate_slabs + late_gates
    for run in pair_jobs:
        run()

    nk_ref[0] = ka[rows - WINDOW:rows, :]
    nv_ref[0] = va[rows - WINDOW:rows, :]

    hgn = hgn_ref[...]
    st0 = [st_ref[hd * HG_D:(hd + 1) * HG_D, :] for hd in range(HG_HEADS)]
    finish = _hgrn_finish(qh_ref, kh_ref, vh_ref, b_ref, in_chunk, qe_ref, ke_ref, st0, n_chunks)
    while True:
        try:
            next(finish)
        except StopIteration as done:
            outs, states = done.value
            break
        if late:
            late.pop(0)()
    for run in late:
        run()
    for hd in range(HG_HEADS):
        cs = slice(hd * HG_D, (hd + 1) * HG_D)
        st_ref[cs, :] = states[hd]
        ns_ref[0, cs, :] = states[hd].T
        for ci in range(n_chunks):
            oh_ref[ci * HG_CHUNK:(ci + 1) * HG_CHUNK, cs] = _rms(outs[ci][hd], hgn)

    gh = g_ref[:, 0:HG_W]
    oh = oh_ref[...] * (gh * jax.nn.sigmoid(gh))

    def gate(off, width):
        return g_ref[:, off - OFF_GH:off - OFF_GH + width]
    m = (_merge_out(oa_ref[...], wao_ref, OFF_GA, gate)
         + _merge_out(oh, who_ref, OFF_GB, gate))
    y_ref[...] = x + _rms(_dot(m.astype(BF16), wo_ref[...]), post_ref[...])


def _mix_prompt_call(x, batch, sinks, pre, post, lbl, hgn, win, wao, who, wo):
    seq = x.shape[0] // batch
    rows = min(MIX_ROWS, seq)
    steps = seq // rows
    x_spec = pl.BlockSpec((rows, D_MODEL), lambda b, j: (b * steps + j, 0))
    kv_spec = pl.BlockSpec((1, WINDOW, ATTN_KV_W), lambda b, j: (b, 0, 0))
    st_spec = pl.BlockSpec((1, HG_W, HG_D), lambda b, j: (b, 0, 0))
    act = pltpu.VMEM((rows, HG_W), F32)
    return pl.pallas_call(
        _mix_prompt_kernel,
        grid=(batch, steps),
        in_specs=[pl.BlockSpec(memory_space=pltpu.SMEM), x_spec,
                  _const_spec((1, D_MODEL)), _const_spec((1, D_MODEL)),
                  _const_spec(lbl.shape), _const_spec((1, HG_D)),
                  _const_spec((HG_CHUNK, HG_CHUNK)),
                  HBM_SPEC, HBM_SPEC, HBM_SPEC, HBM_SPEC],
        out_specs=[x_spec, kv_spec, kv_spec, st_spec],
        out_shape=[jax.ShapeDtypeStruct(x.shape, F32),
                   jax.ShapeDtypeStruct((batch, WINDOW, ATTN_KV_W), F32),
                   jax.ShapeDtypeStruct((batch, WINDOW, ATTN_KV_W), F32),
                   jax.ShapeDtypeStruct((batch, HG_W, HG_D), F32)],
        scratch_shapes=[pltpu.VMEM(win.shape, BF16), pltpu.VMEM(wao.shape, BF16),
                        pltpu.VMEM(who.shape, BF16), pltpu.VMEM(wo.shape, BF16),
                        pltpu.VMEM((rows + WINDOW, ATTN_KV_W), F32),
                        pltpu.VMEM((rows + WINDOW, ATTN_KV_W), F32),
                        pltpu.VMEM((HG_W, HG_D), F32),
                        act, act, act, act,
                        pltpu.VMEM((rows, ATTN_Q_W), F32), act,
                        pltpu.VMEM((rows, IN_WIDTH - OFF_GH), F32),
                        act, pltpu.VMEM((HG_LEVELS, rows, HG_W), BF16),
                        pltpu.VMEM((rows, HG_W), BF16), pltpu.VMEM((rows, HG_W), BF16)],
        compiler_params=pltpu.CompilerParams(
            dimension_semantics=("arbitrary", "arbitrary"), vmem_limit_bytes=VMEM_LIMIT),
        name="mix_prompt",
    )(sinks, x, pre, post, lbl, hgn, _pair_levels(HG_CHUNK), win, wao, who, wo)


def _row_select(rows_list):
    n = -(-len(rows_list) // 8) * 8
    lanes = rows_list[0].shape[1]
    ridx = lax.broadcasted_iota(jnp.int32, (n, lanes), 0)
    out = jnp.zeros((n, lanes), F32)
    for i, r in enumerate(rows_list):
        out = jnp.where(ridx == i, jnp.broadcast_to(r, (n, lanes)), out)
    return out


def _mix_sample_kernel(sinks_ref, x_ref, ck_ref, cv_ref, s0_ref, pre_ref, post_ref, lbl_ref, hgn_ref,
                       win_hbm, wao_hbm, who_hbm, wo_hbm,
                       y_ref, nk_ref, nv_ref, ns_ref,
                       win_ref, wao_ref, who_ref, wo_ref,
                       h_ref, qa_ref, ka_ref, va_ref, qh_ref, kh_ref, fh_ref, vh_ref, oa_ref, oh_ref):
    i = pl.program_id(0)
    tile = ck_ref.shape[0]

    def proj(off, width):
        return _dot(h_ref[...], win_ref[:, off:off + width])

    @pl.when(i == 0)
    def _():
        for src, dst in ((win_hbm, win_ref), (wao_hbm, wao_ref), (who_hbm, who_ref),
                         (wo_hbm, wo_ref)):
            _stream_cast(src, dst)
        h_ref[...] = _rms(x_ref[...], pre_ref[...]).astype(BF16)
        qa_ref[...] = proj(OFF_QA, ATTN_Q_W) * (HEAD_DIM ** -0.5)
        ka_ref[...] = proj(OFF_KA, ATTN_KV_W)
        va_ref[...] = proj(OFF_VA, ATTN_KV_W)
        lb = _forget_lower_bound(lbl_ref[...])
        qh = proj(OFF_QH, HG_W)
        qh_ref[...] = qh * jax.nn.sigmoid(qh)
        fp = proj(OFF_FH, HG_W)
        fh_ref[...] = jnp.exp(jnp.log(lb + (1.0 - lb) * jax.nn.sigmoid(fp)))
        kh_ref[...] = (1.0 - lb) * jax.nn.sigmoid(-fp)
        vh_ref[...] = proj(OFF_IH, HG_W)

    low = _lane_is_low((1, LANES))
    pos = lax.broadcasted_iota(jnp.int32, (WINDOW, LANES), 0)
    hrow = lax.broadcasted_iota(jnp.int32, (ATTN_Q_HEADS, 1), 0)
    sk = jnp.zeros((ATTN_Q_HEADS, 1), F32)
    for hd in range(ATTN_Q_HEADS):
        sk = jnp.where(hrow == hd, sinks_ref[0, hd], sk)
    group = ATTN_Q_HEADS // ATTN_KV_HEADS
    hgn = hgn_ref[...]

    r8 = pl.ds(pl.multiple_of(i * tile, tile), tile)
    qa8, ka8, va8 = qa_ref[r8, :], ka_ref[r8, :], va_ref[r8, :]
    fh8, kh8, qh8, vh8 = fh_ref[r8, :], kh_ref[r8, :], qh_ref[r8, :], vh_ref[r8, :]
    oa_rows = [[] for _ in range(ATTN_Q_HEADS // 2)]
    oh_rows = [[] for _ in range(HG_HEADS)]
    seqs = range(tile)
    kws, vws, q8s = [], [], []
    for bi in seqs:
        r1 = slice(bi, bi + 1)
        kw = jnp.where(pos == WINDOW - 1, ka8[r1, :], pltpu.roll(ck_ref[bi], WINDOW - 1, axis=0))
        vw = jnp.where(pos == WINDOW - 1, va8[r1, :], pltpu.roll(cv_ref[bi], WINDOW - 1, axis=0))
        nk_ref[bi] = kw
        nv_ref[bi] = vw
        kws.append(kw.astype(BF16))
        vws.append(vw.astype(BF16))
        qrows = []
        for hd in range(ATTN_Q_HEADS):
            slab = qa8[r1, (hd // 2) * LANES:(hd // 2 + 1) * LANES]
            in_place = (hd % 2) == (hd // group)
            src = slab if in_place else pltpu.roll(slab, HEAD_DIM, axis=1)
            on_kv_lanes = low if hd // group == 0 else jnp.logical_not(low)
            qrows.append(jnp.where(on_kv_lanes, src, 0.0))
        q8s.append(_row_select(qrows).astype(BF16))
    scores = [lax.dot_general(q8s[bi], kws[bi], NT_DIMS, preferred_element_type=F32) for bi in seqs]
    probs = []
    for s in scores:
        m = jnp.maximum(jnp.max(s, axis=1, keepdims=True), sk)
        p = jnp.exp(s - m)
        probs.append((p / (jnp.sum(p, axis=1, keepdims=True) + jnp.exp(sk - m))).astype(BF16))
    for bi in seqs:
        o8 = _dot(probs[bi], vws[bi])
        for slab in range(ATTN_Q_HEADS // 2):
            kv = (2 * slab) // group
            even, odd = o8[2 * slab:2 * slab + 1, :], o8[2 * slab + 1:2 * slab + 2, :]
            if kv == 0:
                out = jnp.where(low, even, pltpu.roll(odd, HEAD_DIM, axis=1))
            else:
                out = jnp.where(low, pltpu.roll(even, HEAD_DIM, axis=1), odd)
            oa_rows[slab].append(out)

    def columns(a8, hd):
        pad = jnp.zeros((HG_D - tile, HG_D), F32)
        return jnp.concatenate([_head(a8, hd), pad], axis=0).T
    seq_row = lax.broadcasted_iota(jnp.int32, (HG_D, HG_D), 0)
    for hd in range(HG_HEADS):
        rs = slice(hd * HG_D, (hd + 1) * HG_D)
        f_cols = columns(fh8, hd)
        k_cols = columns(kh8, hd).astype(BF16)
        v_rows = jnp.concatenate([_head(vh8, hd), jnp.zeros((HG_D - tile, HG_D), F32)], axis=0)
        q_rows = _head(qh8, hd).astype(BF16)
        outers = [_dot(k_cols, jnp.where(seq_row == bi, v_rows, 0.0).astype(BF16))
                  for bi in seqs]
        for bi in seqs:
            f_col = jnp.broadcast_to(f_cols[:, bi:bi + 1], (HG_D, HG_D))
            ns_ref[bi, rs, :] = f_col * s0_ref[bi, rs, :] + outers[bi]
        outs = [_dot(q_rows, ns_ref[bi, rs, :].astype(BF16)) for bi in seqs]
        for bi in seqs:
            oh_rows[hd].append(_rms(outs[bi][bi:bi + 1, :], hgn))
    for slab in range(ATTN_Q_HEADS // 2):
        oa_ref[r8, slab * LANES:(slab + 1) * LANES] = _row_select(oa_rows[slab])
    for hd in range(HG_HEADS):
        oh_ref[r8, hd * HG_D:(hd + 1) * HG_D] = _row_select(oh_rows[hd])

    @pl.when(i == pl.num_programs(0) - 1)
    def _():
        gh = proj(OFF_GH, HG_W)
        oh = oh_ref[...] * (gh * jax.nn.sigmoid(gh))
        m = (_merge_out(oa_ref[...], wao_ref, OFF_GA, proj)
             + _merge_out(oh, who_ref, OFF_GB, proj))
        y_ref[...] = x_ref[...] + _rms(_dot(m.astype(BF16), wo_ref[...]), post_ref[...])


def _mix_sample_call(x, ck, cv, s0, sinks, pre, post, lbl, hgn, win, wao, who, wo):
    n = x.shape[0]
    tile = min(SAMPLE_TILE, n)
    full = pl.BlockSpec((n, D_MODEL), lambda i: (0, 0))
    kv_spec = pl.BlockSpec((tile, WINDOW, ATTN_KV_W), lambda i: (i, 0, 0))
    st_spec = pl.BlockSpec((tile, HG_W, HG_D), lambda i: (i, 0, 0))
    act = pltpu.VMEM((n, HG_W), F32)
    kv_act = pltpu.VMEM((n, ATTN_KV_W), F32)
    return pl.pallas_call(
        _mix_sample_kernel,
        grid=(n // tile,),
        in_specs=[pl.BlockSpec(memory_space=pltpu.SMEM), full, kv_spec, kv_spec, st_spec,
                  _const_spec((1, D_MODEL)), _const_spec((1, D_MODEL)),
                  _const_spec(lbl.shape), _const_spec((1, HG_D)),
                  HBM_SPEC, HBM_SPEC, HBM_SPEC, HBM_SPEC],
        out_specs=[full, kv_spec, kv_spec, st_spec],
        out_shape=[jax.ShapeDtypeStruct(x.shape, F32),
                   jax.ShapeDtypeStruct(ck.shape, F32),
                   jax.ShapeDtypeStruct(cv.shape, F32),
                   jax.ShapeDtypeStruct(s0.shape, F32)],
        scratch_shapes=[pltpu.VMEM(win.shape, BF16), pltpu.VMEM(wao.shape, BF16),
                        pltpu.VMEM(who.shape, BF16), pltpu.VMEM(wo.shape, BF16),
                        pltpu.VMEM((n, D_MODEL), BF16),
                        pltpu.VMEM((n, ATTN_Q_W), F32), kv_act, kv_act,
                        act, act, act, act,
                        pltpu.VMEM((n, ATTN_Q_W), F32), act],
        compiler_params=pltpu.CompilerParams(
            dimension_semantics=("arbitrary",), vmem_limit_bytes=VMEM_LIMIT),
        name="mix_sample",
    )(sinks, x, ck, cv, s0, pre, post, lbl, hgn, win, wao, who, wo)


def kernel(x_prompt, x_sample, cache_k, cache_v, state_hgrn, norm_ffn1_pre, norm_ffn1_post, w_ffn1_gate, w_ffn1_up, w_ffn1_down, norm_mix_pre, norm_mix_post, w_in, attn_sinks, hgrn_lb_logits, hgrn_norm, w_attn_out, w_hgrn_out, w_out, norm_ffn2_pre, norm_ffn2_post, w_ffn2_gate, w_ffn2_up, w_ffn2_down):
    depth = w_in.shape[0]
    assert depth == 1 and hgrn_lb_logits.shape[0] == 2, "single-layer stack only"
    batch, seq, _ = x_prompt.shape
    n_s = x_sample.shape[0]
    assert x_sample.shape[1] == 1 and seq % WINDOW == 0

    xp = x_prompt.reshape(batch * seq, D_MODEL)
    xs = x_sample.reshape(n_s, D_MODEL)
    ck = cache_k[0].reshape(n_s, WINDOW, ATTN_KV_W)
    cv = cache_v[0].reshape(n_s, WINDOW, ATTN_KV_W)
    s0 = state_hgrn[0].reshape(n_s, HG_W, HG_D)

    xp, xs = _ffn_call(xp, xs, norm_ffn1_pre, norm_ffn1_post,
                       w_ffn1_gate[0], w_ffn1_up[0], w_ffn1_down[0])

    mix_w = (attn_sinks, norm_mix_pre, norm_mix_post, hgrn_lb_logits, hgrn_norm,
             w_in[0], w_attn_out[0], w_hgrn_out[0], w_out[0])
    xp, nkp, nvp, nsp = _mix_prompt_call(xp, batch, *mix_w)
    xs, nks, nvs, nss = _mix_sample_call(xs, ck, cv, s0, *mix_w)

    xp, xs = _ffn_call(xp, xs, norm_ffn2_pre, norm_ffn2_post,
                       w_ffn2_gate[0], w_ffn2_up[0], w_ffn2_down[0])

    kv_shape = (1, -1, WINDOW, ATTN_KV_HEADS, HEAD_DIM)
    st_shape = (1, -1, HG_HEADS, HG_D, HG_D)
    return (xp.reshape(batch, seq, D_MODEL), xs.reshape(n_s, 1, D_MODEL),
            nkp.reshape(kv_shape), nvp.reshape(kv_shape), nsp.reshape(st_shape),
            nks.reshape(kv_shape), nvs.reshape(kv_shape), nss.reshape(st_shape))
```

```python
import functools

import jax
import jax.numpy as jnp
from jax import lax
from jax.experimental import pallas as pl
from jax.experimental.pallas import tpu as pltpu

F32 = jnp.float32
BF16 = jnp.bfloat16

D_MODEL = 1024
FFN_DIM = 2816
HEAD_DIM = 64
ATTN_Q_HEADS = 8
ATTN_KV_HEADS = 2
WINDOW = 128
HG_HEADS = 4
HG_D = 128
EPS = 1e-6

ATTN_Q_W = ATTN_Q_HEADS * HEAD_DIM
ATTN_KV_W = ATTN_KV_HEADS * HEAD_DIM
HG_W = HG_HEADS * HG_D
OFF_QA = 0
OFF_KA = OFF_QA + ATTN_Q_W
OFF_VA = OFF_KA + ATTN_KV_W
OFF_QH = OFF_VA + ATTN_KV_W
OFF_FH = OFF_QH + HG_W
OFF_IH = OFF_FH + HG_W
OFF_GH = OFF_IH + HG_W
OFF_GA = OFF_GH + HG_W
OFF_GB = OFF_GA + D_MODEL
IN_WIDTH = OFF_GB + D_MODEL

LANES = 128
FFN_CHUNK = 256
N_FFN_CHUNKS = FFN_DIM // FFN_CHUNK
FFN_ROWS = 1024
MIX_ROWS = 512
NORM_PARTS = 2
LOOP_GATE_SLABS = 10
HG_CHUNK = 128
SAMPLE_TILE = 8
VMEM_LIMIT = 56 * 1024 * 1024

NT_DIMS = (((1,), (1,)), ((), ()))
TN_DIMS = (((0,), (0,)), ((), ()))


def _rms(x, g):
    return x * lax.rsqrt(jnp.mean(x * x, axis=-1, keepdims=True) + EPS) * g


def _dot(a, b):
    return jnp.dot(a, b, preferred_element_type=F32)


STREAM_SLOTS = 3
STREAM_CHUNK_BYTES = 1 << 20


def _stream_cast(src_hbm, dst_ref):
    n_rows, n_cols = src_hbm.shape
    rc = max(8, min(n_rows, STREAM_CHUNK_BYTES // (4 * n_cols) // 8 * 8))
    while n_rows % rc:
        rc -= 8
    n = n_rows // rc
    slots = min(STREAM_SLOTS, n)

    def body(stage, sem):
        def copy(c):
            return pltpu.make_async_copy(src_hbm.at[pl.ds(c * rc, rc), :], stage.at[c % slots],
                                         sem.at[c % slots])
        for c in range(slots):
            copy(c).start()
        for c in range(n):
            copy(c).wait()
            dst_ref[c * rc:(c + 1) * rc, :] = stage[c % slots].astype(BF16)
            if c + slots < n:
                copy(c + slots).start()

    pl.run_scoped(body, pltpu.VMEM((slots, rc, n_cols), F32), pltpu.SemaphoreType.DMA((slots,)))


HBM_SPEC = pl.BlockSpec(memory_space=pl.ANY)


def _const_spec(shape):
    zeros = (0,) * len(shape)
    return pl.BlockSpec(shape, lambda *_: zeros, pipeline_mode=pl.Buffered(1))


def _ffn_tile(x, pre, post, wg_ref, wu_ref, wd_ref):
    h = _rms(x, pre).astype(BF16)
    acc = None
    for c in range(N_FFN_CHUNKS):
        cols = slice(c * FFN_CHUNK, (c + 1) * FFN_CHUNK)
        g = _dot(h, wg_ref[:, cols])
        u = _dot(h, wu_ref[:, cols])
        a = (g * jax.nn.sigmoid(g) * u).astype(BF16)
        d = _dot(a, wd_ref[cols, :])
        acc = d if acc is None else acc + d
    return x + 0.5 * _rms(acc, post)


def _ffn_kernel(n_prompt_steps, xp_ref, xs_ref, pre_ref, post_ref, wg_hbm, wu_hbm, wd_hbm,
                yp_ref, ys_ref, wg_ref, wu_ref, wd_ref):
    i = pl.program_id(0)

    @pl.when(i == 0)
    def _():
        _stream_cast(wg_hbm, wg_ref)
        _stream_cast(wu_hbm, wu_ref)
        _stream_cast(wd_hbm, wd_ref)

    @pl.when(i < n_prompt_steps)
    def _():
        yp_ref[...] = _ffn_tile(xp_ref[...], pre_ref[...], post_ref[...], wg_ref, wu_ref, wd_ref)

    @pl.when(i == n_prompt_steps)
    def _():
        ys_ref[...] = _ffn_tile(xs_ref[...], pre_ref[...], post_ref[...], wg_ref, wu_ref, wd_ref)


def _ffn_call(xp, xs, pre, post, wg, wu, wd):
    n_p, n_s = xp.shape[0], xs.shape[0]
    rows = min(FFN_ROWS, n_p)
    steps = n_p // rows
    prompt_spec = pl.BlockSpec((rows, D_MODEL), lambda i: (jnp.minimum(i, steps - 1), 0))
    sample_spec = pl.BlockSpec((n_s, D_MODEL), lambda i: (0, 0))
    return pl.pallas_call(
        functools.partial(_ffn_kernel, steps),
        grid=(steps + 1,),
        in_specs=[prompt_spec, sample_spec,
                  _const_spec((1, D_MODEL)), _const_spec((1, D_MODEL)),
                  HBM_SPEC, HBM_SPEC, HBM_SPEC],
        out_specs=[prompt_spec, sample_spec],
        out_shape=[jax.ShapeDtypeStruct(xp.shape, F32), jax.ShapeDtypeStruct(xs.shape, F32)],
        scratch_shapes=[pltpu.VMEM(wg.shape, BF16), pltpu.VMEM(wu.shape, BF16),
                        pltpu.VMEM(wd.shape, BF16)],
        compiler_params=pltpu.CompilerParams(
            dimension_semantics=("arbitrary",), vmem_limit_bytes=VMEM_LIMIT),
        name="ffn_half",
    )(xp, xs, pre, post, wg, wu, wd)


def _forget_lower_bound(lbl):
    l0, l1 = lbl[0:1, :], lbl[1:2, :]
    m = jnp.maximum(l0, l1)
    e0, e1 = jnp.exp(l0 - m), jnp.exp(l1 - m)
    return e0 / (e0 + e1)


def _lane_is_low(shape):
    return lax.broadcasted_iota(jnp.int32, shape, len(shape) - 1) < HEAD_DIM


def _dup_kv(x):
    swapped = pltpu.roll(x, HEAD_DIM, axis=1)
    low = _lane_is_low(x.shape)
    return jnp.where(low, x, swapped), jnp.where(low, swapped, x)


def _merge_out(h, w_ref, off, gates_from):
    return jax.nn.sigmoid(gates_from(off, D_MODEL)) * _dot(h.astype(BF16), w_ref[...])


def _pair_levels(c):
    t = jnp.arange(c, dtype=jnp.int32)[:, None]
    s = jnp.arange(c, dtype=jnp.int32)[None, :]
    x = jnp.maximum(t ^ s, 1)
    lvl = (31 - lax.clz(x)).astype(jnp.int32)
    return jnp.where(t > s, lvl, -1)


LOG2E = 1.4426950408889634


def _head(a, hd):
    return a[:, hd * HG_D:(hd + 1) * HG_D]


HG_LEVELS = HG_CHUNK.bit_length() - 1


def _chunk_rows(ci):
    return slice(ci * HG_CHUNK, (ci + 1) * HG_CHUNK)


def _hgrn_log_decay(g_ref, b_ref, n_chunks):
    c = HG_CHUNK
    row = lax.broadcasted_iota(jnp.int32, (c, c), 0)
    col = lax.broadcasted_iota(jnp.int32, (c, c), 1)
    tril = (col <= row).astype(F32)
    for ci in range(n_chunks):
        r = _chunk_rows(ci)
        b_ref[r, :] = jnp.dot(tril, g_ref[r, :], precision=lax.Precision.HIGHEST,
                              preferred_element_type=F32) * LOG2E


def _hgrn_factor_jobs(q_ref, k_ref, b_ref, x_ref, qe_ref, ke_ref, n_chunks):
    c = HG_CHUNK
    t = lax.broadcasted_iota(jnp.int32, (c, 1), 0)
    last = {}

    def level_job(ci, lvl):
        def run():
            n = 1 << lvl
            r = _chunk_rows(ci)
            bc = b_ref[r, :]
            second = (t & n) != 0
            if n < 8:
                prev = last.get(ci, bc)
                edge = jnp.where(second, pltpu.roll(prev, n, axis=0), prev)
                if 2 * n < 8:
                    last[ci] = jnp.where(second, prev, pltpu.roll(prev, c - n, axis=0))
            else:
                edge = jnp.concatenate(
                    [jnp.broadcast_to(bc[p + n - 1:p + n, :], (2 * n, bc.shape[1]))
                     for p in range(0, c, 2 * n)], axis=0)
            w = jnp.exp2(jnp.where(second, bc - edge, edge - bc))
            x_ref[lvl, r, :] = (jnp.where(second, q_ref[r, :], k_ref[r, :]) * w).astype(BF16)
        return run

    def state_job(ci):
        def run():
            r = _chunk_rows(ci)
            bc = b_ref[r, :]
            qe_ref[r, :] = (q_ref[r, :] * jnp.exp2(bc)).astype(BF16)
            ke_ref[r, :] = (k_ref[r, :] * jnp.exp2(bc[c - 1:c, :] - bc)).astype(BF16)
        return run

    jobs = [level_job(ci, lvl) for lvl in range(HG_LEVELS) for ci in range(n_chunks)]
    return jobs + [state_job(ci) for ci in range(n_chunks)]


def _hgrn_pair_jobs(x_ref, levels, n_chunks):
    c = HG_CHUNK
    heads, chunks = range(HG_HEADS), range(n_chunks)
    a = [[jnp.zeros((c, c), F32) for _ in heads] for _ in chunks]

    def job(lvl, ci):
        def run():
            x = x_ref[lvl, _chunk_rows(ci), :]
            for hd in heads:
                al = lax.dot_general(_head(x, hd), _head(x, hd), NT_DIMS,
                                     preferred_element_type=F32)
                a[ci][hd] = jnp.where(levels == lvl, al, a[ci][hd])
        return run
    return [job(lvl, ci) for lvl in range(HG_LEVELS) for ci in chunks], a


def _hgrn_finish(q_ref, k_ref, v_ref, b_ref, a, qe_ref, ke_ref, st, n_chunks):
    c = HG_CHUNK
    heads, chunks = range(HG_HEADS), range(n_chunks)
    local = []
    for ci in chunks:
        r = _chunk_rows(ci)
        q, k, v = q_ref[r, :], k_ref[r, :], v_ref[r, :]
        qk, vb = q * k, v.astype(BF16)
        outs = []
        for hd in heads:
            diag = jnp.sum(_head(qk, hd), axis=1, keepdims=True)
            outs.append(_dot(a[ci][hd].astype(BF16), _head(vb, hd)) + diag * _head(v, hd))
        local.append(outs)
        yield

    result = []
    for ci in chunks:
        r = _chunk_rows(ci)
        qe, ke, vb = qe_ref[r, :], ke_ref[r, :], v_ref[r, :].astype(BF16)
        decay = jnp.exp2(b_ref[ci * c + c - 1:ci * c + c, :])
        outs, new = [], []
        for hd in heads:
            outs.append(local[ci][hd] + lax.dot_general(_head(qe, hd), st[hd].astype(BF16), NT_DIMS,
                                                        preferred_element_type=F32))
            new.append(st[hd] * _head(decay, hd)
                       + lax.dot_general(_head(vb, hd), _head(ke, hd), TN_DIMS,
                                         preferred_element_type=F32))
        st = new
        result.append(outs)
        yield
    return result, st


def _mix_prompt_kernel(sinks_ref, x_ref, pre_ref, post_ref, lbl_ref, hgn_ref, lvl_ref,
                       win_hbm, wao_hbm, who_hbm, wo_hbm,
                       y_ref, nk_ref, nv_ref, ns_ref,
                       win_ref, wao_ref, who_ref, wo_ref,
                       kbuf, vbuf, st_ref, qh_ref, kh_ref, vh_ref, lf_ref, oa_ref, oh_ref, g_ref,
                       b_ref, xl_ref, qe_ref, ke_ref):
    j = pl.program_id(1)
    rows = x_ref.shape[0]
    n_blocks = rows // WINDOW

    @pl.when((pl.program_id(0) == 0) & (j == 0))
    def _():
        for src, dst in ((win_hbm, win_ref), (wao_hbm, wao_ref), (who_hbm, who_ref),
                         (wo_hbm, wo_ref)):
            _stream_cast(src, dst)

    @pl.when(j == 0)
    def _():
        kbuf[0:WINDOW, :] = jnp.zeros((WINDOW, LANES), F32)
        vbuf[0:WINDOW, :] = jnp.zeros((WINDOW, LANES), F32)
        st_ref[...] = jnp.zeros(st_ref.shape, F32)

    part_rows = [slice(r0, r0 + rows // NORM_PARTS) for r0 in range(0, rows, rows // NORM_PARTS)]
    h_parts, qkv_parts = [], []
    for r in part_rows:
        hp = _rms(x_ref[r, :], pre_ref[...]).astype(BF16)
        h_parts.append(hp)
        qkv_parts.append(_dot(hp, win_ref[:, OFF_QA:OFF_QH]))
    h = jnp.concatenate(h_parts, axis=0)
    qkv = jnp.concatenate(qkv_parts, axis=0)

    def proj(off, width):
        return _dot(h, win_ref[:, off:off + width])

    lb = _forget_lower_bound(lbl_ref[...])
    n_chunks = rows // HG_CHUNK

    def proj_slab(off):
        def run():
            z = proj(off, FFN_CHUNK)
            if off < OFF_FH:
                cols = slice(off - OFF_QH, off - OFF_QH + FFN_CHUNK)
                qh_ref[:, cols] = z * jax.nn.sigmoid(z)
            elif off < OFF_IH:
                cols = slice(off - OFF_FH, off - OFF_FH + FFN_CHUNK)
                lbs = lb[:, cols]
                lf_ref[:, cols] = jnp.log(lbs + (1.0 - lbs) * jax.nn.sigmoid(z))
                kh_ref[:, cols] = (1.0 - lbs) * jax.nn.sigmoid(-z)
            elif off < OFF_GH:
                cols = slice(off - OFF_IH, off - OFF_IH + FFN_CHUNK)
                vh_ref[:, cols] = z
            else:
                cols = slice(off - OFF_GH, off - OFF_GH + FFN_CHUNK)
                g_ref[:, cols] = z
        return run
    hgrn_slabs = [proj_slab(off) for off in range(OFF_QH, OFF_GH, FFN_CHUNK)]
    gate_slabs = [proj_slab(off) for off in range(OFF_GH, IN_WIDTH, FFN_CHUNK)]
    jobs = _hgrn_factor_jobs(qh_ref, kh_ref, b_ref, xl_ref, qe_ref, ke_ref, n_chunks)
    pair_jobs, in_chunk = _hgrn_pair_jobs(xl_ref, lvl_ref[...], n_chunks)

    qa = qkv[:, OFF_QA:OFF_KA] * (HEAD_DIM ** -0.5)
    ka = qkv[:, OFF_KA:OFF_VA]
    va = qkv[:, OFF_VA:OFF_QH]
    kbuf[WINDOW:WINDOW + rows, :] = ka
    vbuf[WINDOW:WINDOW + rows, :] = va

    qi = lax.broadcasted_iota(jnp.int32, (WINDOW, 2 * WINDOW), 0)
    kj = lax.broadcasted_iota(jnp.int32, (WINDOW, 2 * WINDOW), 1)
    band = (kj > qi) & (kj <= qi + WINDOW)
    low = _lane_is_low((WINDOW, LANES))
    top = lax.broadcasted_iota(jnp.int32, (2 * WINDOW, 1), 0) < WINDOW
    n_slabs = ATTN_Q_HEADS // 2
    group = ATTN_Q_HEADS // ATTN_KV_HEADS
    kds = [_dup_kv(kbuf[n * WINDOW:(n + 2) * WINDOW, :]) for n in range(n_blocks)]
    vds = [_dup_kv(vbuf[n * WINDOW:(n + 2) * WINDOW, :]) for n in range(n_blocks)]

    def scores(n, slab):
        qs = qa[n * WINDOW:(n + 1) * WINDOW, slab * LANES:(slab + 1) * LANES]
        q2 = jnp.concatenate([jnp.where(low, qs, 0.0), jnp.where(low, 0.0, qs)], axis=0)
        return lax.dot_general(q2.astype(BF16), kds[n][(2 * slab) // group].astype(BF16), NT_DIMS,
                               preferred_element_type=F32)

    def attend(n, slab, s):
        valid = band & ((j > 0) | (kj >= WINDOW)) if n == 0 else band
        s = jnp.where(jnp.concatenate([valid, valid], axis=0), s, -jnp.inf)
        sk = jnp.where(top, sinks_ref[0, 2 * slab], sinks_ref[0, 2 * slab + 1])
        m = jnp.maximum(jnp.max(s, axis=1, keepdims=True), sk)
        p = jnp.exp(s - m)
        den = jnp.sum(p, axis=1, keepdims=True) + jnp.exp(sk - m)
        o2 = _dot(p.astype(BF16), vds[n][(2 * slab) // group].astype(BF16)) / den
        oa_ref[n * WINDOW:(n + 1) * WINDOW, slab * LANES:(slab + 1) * LANES] = (
            jnp.where(low, o2[0:WINDOW], o2[WINDOW:2 * WINDOW]))

    order = [(n, slab) for n in range(n_blocks) for slab in range(n_slabs)]
    early = len(hgrn_slabs)
    late_gates = gate_slabs[LOOP_GATE_SLABS:]
    gate_slabs = gate_slabs[:LOOP_GATE_SLABS]
    pending = scores(*order[0])
    ready_pairs = 0
    for idx, (n, slab) in enumerate(order):
        nxt = scores(*order[idx + 1]) if idx + 1 < len(order) else None
        if idx < early:
            hgrn_slabs[idx]()
        else:
            if idx == early:
                _hgrn_log_decay(lf_ref, b_ref, n_chunks)
            if gate_slabs:
                gate_slabs.pop(0)()
            for _ in range(min(len(pair_jobs), ready_pairs)):
                pair_jobs.pop(0)()
            ready_pairs = 0
            for _ in range(-(-len(jobs) // (len(order) - 1 - idx)) if idx + 1 < len(order) else 0):
                if jobs:
                    jobs.pop(0)()
                    ready_pairs += 1
        attend(n, slab, pending)
        pending = nxt
    kbuf[0:WINDOW, :] = kbuf[rows:rows + WINDOW, :]
    vbuf[0:WINDOW, :] = vbuf[rows:rows + WINDOW, :]
    assert not jobs and not hgrn_slabs[early:]
    late = gate_slabs + late_gates
    for run in pair_jobs:
        run()

    nk_ref[0] = ka[rows - WINDOW:rows, :]
    nv_ref[0] = va[rows - WINDOW:rows, :]

    hgn = hgn_ref[...]
    st0 = [st_ref[hd * HG_D:(hd + 1) * HG_D, :] for hd in range(HG_HEADS)]
    finish = _hgrn_finish(qh_ref, kh_ref, vh_ref, b_ref, in_chunk, qe_ref, ke_ref, st0, n_chunks)
    while True:
        try:
            next(finish)
        except StopIteration as done:
            outs, states = done.value
            break
        if late:
            late.pop(0)()
    for run in late:
        run()
    for hd in range(HG_HEADS):
        cs = slice(hd * HG_D, (hd + 1) * HG_D)
        st_ref[cs, :] = states[hd]
        ns_ref[0, cs, :] = states[hd].T
        for ci in range(n_chunks):
            oh_ref[ci * HG_CHUNK:(ci + 1) * HG_CHUNK, cs] = _rms(outs[ci][hd], hgn)

    gh = g_ref[:, 0:HG_W]
    oh = oh_ref[...] * (gh * jax.nn.sigmoid(gh))

    def gate(off, width):
        return g_ref[:, off - OFF_GH:off - OFF_GH + width]
    m = (_merge_out(oa_ref[...], wao_ref, OFF_GA, gate)
         + _merge_out(oh, who_ref, OFF_GB, gate))
    mb = m.astype(BF16)
    for r in part_rows:
        y_ref[r, :] = x_ref[r, :] + _rms(_dot(mb[r, :], wo_ref[...]), post_ref[...])


def _mix_prompt_call(x, batch, sinks, pre, post, lbl, hgn, win, wao, who, wo):
    seq = x.shape[0] // batch
    rows = min(MIX_ROWS, seq)
    steps = seq // rows
    x_spec = pl.BlockSpec((rows, D_MODEL), lambda b, j: (b * steps + j, 0))
    kv_spec = pl.BlockSpec((1, WINDOW, ATTN_KV_W), lambda b, j: (b, 0, 0))
    st_spec = pl.BlockSpec((1, HG_W, HG_D), lambda b, j: (b, 0, 0))
    act = pltpu.VMEM((rows, HG_W), F32)
    return pl.pallas_call(
        _mix_prompt_kernel,
        grid=(batch, steps),
        in_specs=[pl.BlockSpec(memory_space=pltpu.SMEM), x_spec,
                  _const_spec((1, D_MODEL)), _const_spec((1, D_MODEL)),
                  _const_spec(lbl.shape), _const_spec((1, HG_D)),
                  _const_spec((HG_CHUNK, HG_CHUNK)),
                  HBM_SPEC, HBM_SPEC, HBM_SPEC, HBM_SPEC],
        out_specs=[x_spec, kv_spec, kv_spec, st_spec],
        out_shape=[jax.ShapeDtypeStruct(x.shape, F32),
                   jax.ShapeDtypeStruct((batch, WINDOW, ATTN_KV_W), F32),
                   jax.ShapeDtypeStruct((batch, WINDOW, ATTN_KV_W), F32),
                   jax.ShapeDtypeStruct((batch, HG_W, HG_D), F32)],
        scratch_shapes=[pltpu.VMEM(win.shape, BF16), pltpu.VMEM(wao.shape, BF16),
                        pltpu.VMEM(who.shape, BF16), pltpu.VMEM(wo.shape, BF16),
                        pltpu.VMEM((rows + WINDOW, ATTN_KV_W), F32),
                        pltpu.VMEM((rows + WINDOW, ATTN_KV_W), F32),
                        pltpu.VMEM((HG_W, HG_D), F32),
                        act, act, act, act,
                        pltpu.VMEM((rows, ATTN_Q_W), F32), act,
                        pltpu.VMEM((rows, IN_WIDTH - OFF_GH), F32),
                        act, pltpu.VMEM((HG_LEVELS, rows, HG_W), BF16),
                        pltpu.VMEM((rows, HG_W), BF16), pltpu.VMEM((rows, HG_W), BF16)],
        compiler_params=pltpu.CompilerParams(
            dimension_semantics=("arbitrary", "arbitrary"), vmem_limit_bytes=VMEM_LIMIT),
        name="mix_prompt",
    )(sinks, x, pre, post, lbl, hgn, _pair_levels(HG_CHUNK), win, wao, who, wo)


def _row_select(rows_list):
    n = -(-len(rows_list) // 8) * 8
    lanes = rows_list[0].shape[1]
    ridx = lax.broadcasted_iota(jnp.int32, (n, lanes), 0)
    out = jnp.zeros((n, lanes), F32)
    for i, r in enumerate(rows_list):
        out = jnp.where(ridx == i, jnp.broadcast_to(r, (n, lanes)), out)
    return out


def _mix_sample_kernel(sinks_ref, x_ref, ck_ref, cv_ref, s0_ref, pre_ref, post_ref, lbl_ref, hgn_ref,
                       win_hbm, wao_hbm, who_hbm, wo_hbm,
                       y_ref, nk_ref, nv_ref, ns_ref,
                       win_ref, wao_ref, who_ref, wo_ref,
                       h_ref, qa_ref, ka_ref, va_ref, qh_ref, kh_ref, fh_ref, vh_ref, oa_ref, oh_ref):
    i = pl.program_id(0)
    tile = ck_ref.shape[0]

    def proj(off, width):
        return _dot(h_ref[...], win_ref[:, off:off + width])

    @pl.when(i == 0)
    def _():
        for src, dst in ((win_hbm, win_ref), (wao_hbm, wao_ref), (who_hbm, who_ref),
                         (wo_hbm, wo_ref)):
            _stream_cast(src, dst)
        h_ref[...] = _rms(x_ref[...], pre_ref[...]).astype(BF16)
        qa_ref[...] = proj(OFF_QA, ATTN_Q_W) * (HEAD_DIM ** -0.5)
        ka_ref[...] = proj(OFF_KA, ATTN_KV_W)
        va_ref[...] = proj(OFF_VA, ATTN_KV_W)
        lb = _forget_lower_bound(lbl_ref[...])
        qh = proj(OFF_QH, HG_W)
        qh_ref[...] = qh * jax.nn.sigmoid(qh)
        fp = proj(OFF_FH, HG_W)
        fh_ref[...] = jnp.exp(jnp.log(lb + (1.0 - lb) * jax.nn.sigmoid(fp)))
        kh_ref[...] = (1.0 - lb) * jax.nn.sigmoid(-fp)
        vh_ref[...] = proj(OFF_IH, HG_W)

    low = _lane_is_low((1, LANES))
    pos = lax.broadcasted_iota(jnp.int32, (WINDOW, LANES), 0)
    hrow = lax.broadcasted_iota(jnp.int32, (ATTN_Q_HEADS, 1), 0)
    sk = jnp.zeros((ATTN_Q_HEADS, 1), F32)
    for hd in range(ATTN_Q_HEADS):
        sk = jnp.where(hrow == hd, sinks_ref[0, hd], sk)
    group = ATTN_Q_HEADS // ATTN_KV_HEADS
    hgn = hgn_ref[...]

    r8 = pl.ds(pl.multiple_of(i * tile, tile), tile)
    qa8, ka8, va8 = qa_ref[r8, :], ka_ref[r8, :], va_ref[r8, :]
    fh8, kh8, qh8, vh8 = fh_ref[r8, :], kh_ref[r8, :], qh_ref[r8, :], vh_ref[r8, :]
    oa_rows = [[] for _ in range(ATTN_Q_HEADS // 2)]
    oh_rows = [[] for _ in range(HG_HEADS)]
    seqs = range(tile)
    kws, vws, q8s = [], [], []
    for bi in seqs:
        r1 = slice(bi, bi + 1)
        kw = jnp.where(pos == WINDOW - 1, ka8[r1, :], pltpu.roll(ck_ref[bi], WINDOW - 1, axis=0))
        vw = jnp.where(pos == WINDOW - 1, va8[r1, :], pltpu.roll(cv_ref[bi], WINDOW - 1, axis=0))
        nk_ref[bi] = kw
        nv_ref[bi] = vw
        kws.append(kw.astype(BF16))
        vws.append(vw.astype(BF16))
        qrows = []
        for hd in range(ATTN_Q_HEADS):
            slab = qa8[r1, (hd // 2) * LANES:(hd // 2 + 1) * LANES]
            in_place = (hd % 2) == (hd // group)
            src = slab if in_place else pltpu.roll(slab, HEAD_DIM, axis=1)
            on_kv_lanes = low if hd // group == 0 else jnp.logical_not(low)
            qrows.append(jnp.where(on_kv_lanes, src, 0.0))
        q8s.append(_row_select(qrows).astype(BF16))
    scores = [lax.dot_general(q8s[bi], kws[bi], NT_DIMS, preferred_element_type=F32) for bi in seqs]
    probs = []
    for s in scores:
        m = jnp.maximum(jnp.max(s, axis=1, keepdims=True), sk)
        p = jnp.exp(s - m)
        probs.append((p / (jnp.sum(p, axis=1, keepdims=True) + jnp.exp(sk - m))).astype(BF16))
    for bi in seqs:
        o8 = _dot(probs[bi], vws[bi])
        for slab in range(ATTN_Q_HEADS // 2):
            kv = (2 * slab) // group
            even, odd = o8[2 * slab:2 * slab + 1, :], o8[2 * slab + 1:2 * slab + 2, :]
            if kv == 0:
                out = jnp.where(low, even, pltpu.roll(odd, HEAD_DIM, axis=1))
            else:
                out = jnp.where(low, pltpu.roll(even, HEAD_DIM, axis=1), odd)
            oa_rows[slab].append(out)

    def columns(a8, hd):
        pad = jnp.zeros((HG_D - tile, HG_D), F32)
        return jnp.concatenate([_head(a8, hd), pad], axis=0).T
    seq_row = lax.broadcasted_iota(jnp.int32, (HG_D, HG_D), 0)
    for hd in range(HG_HEADS):
        rs = slice(hd * HG_D, (hd + 1) * HG_D)
        f_cols = columns(fh8, hd)
        k_cols = columns(kh8, hd).astype(BF16)
        v_rows = jnp.concatenate([_head(vh8, hd), jnp.zeros((HG_D - tile, HG_D), F32)], axis=0)
        q_rows = _head(qh8, hd).astype(BF16)
        outers = [_dot(k_cols, jnp.where(seq_row == bi, v_rows, 0.0).astype(BF16))
                  for bi in seqs]
        for bi in seqs:
            f_col = jnp.broadcast_to(f_cols[:, bi:bi + 1], (HG_D, HG_D))
            ns_ref[bi, rs, :] = f_col * s0_ref[bi, rs, :] + outers[bi]
        outs = [_dot(q_rows, ns_ref[bi, rs, :].astype(BF16)) for bi in seqs]
        for bi in seqs:
            oh_rows[hd].append(_rms(outs[bi][bi:bi + 1, :], hgn))
    for slab in range(ATTN_Q_HEADS // 2):
        oa_ref[r8, slab * LANES:(slab + 1) * LANES] = _row_select(oa_rows[slab])
    for hd in range(HG_HEADS):
        oh_ref[r8, hd * HG_D:(hd + 1) * HG_D] = _row_select(oh_rows[hd])

    @pl.when(i == pl.num_programs(0) - 1)
    def _():
        gh = proj(OFF_GH, HG_W)
        oh = oh_ref[...] * (gh * jax.nn.sigmoid(gh))
        m = (_merge_out(oa_ref[...], wao_ref, OFF_GA, proj)
             + _merge_out(oh, who_ref, OFF_GB, proj))
        y_ref[...] = x_ref[...] + _rms(_dot(m.astype(BF16), wo_ref[...]), post_ref[...])


def _mix_sample_call(x, ck, cv, s0, sinks, pre, post, lbl, hgn, win, wao, who, wo):
    n = x.shape[0]
    tile = min(SAMPLE_TILE, n)
    full = pl.BlockSpec((n, D_MODEL), lambda i: (0, 0))
    kv_spec = pl.BlockSpec((tile, WINDOW, ATTN_KV_W), lambda i: (i, 0, 0))
    st_spec = pl.BlockSpec((tile, HG_W, HG_D), lambda i: (i, 0, 0))
    act = pltpu.VMEM((n, HG_W), F32)
    kv_act = pltpu.VMEM((n, ATTN_KV_W), F32)
    return pl.pallas_call(
        _mix_sample_kernel,
        grid=(n // tile,),
        in_specs=[pl.BlockSpec(memory_space=pltpu.SMEM), full, kv_spec, kv_spec, st_spec,
                  _const_spec((1, D_MODEL)), _const_spec((1, D_MODEL)),
                  _const_spec(lbl.shape), _const_spec((1, HG_D)),
                  HBM_SPEC, HBM_SPEC, HBM_SPEC, HBM_SPEC],
        out_specs=[full, kv_spec, kv_spec, st_spec],
        out_shape=[jax.ShapeDtypeStruct(x.shape, F32),
                   jax.ShapeDtypeStruct(ck.shape, F32),
                   jax.ShapeDtypeStruct(cv.shape, F32),
                   jax.ShapeDtypeStruct(s0.shape, F32)],
        scratch_shapes=[pltpu.VMEM(win.shape, BF16), pltpu.VMEM(wao.shape, BF16),
                        pltpu.VMEM(who.shape, BF16), pltpu.VMEM(wo.shape, BF16),
                        pltpu.VMEM((n, D_MODEL), BF16),
                        pltpu.VMEM((n, ATTN_Q_W), F32), kv_act, kv_act,
                        act, act, act, act,
                        pltpu.VMEM((n, ATTN_Q_W), F32), act],
        compiler_params=pltpu.CompilerParams(
            dimension_semantics=("arbitrary",), vmem_limit_bytes=VMEM_LIMIT),
        name="mix_sample",
    )(sinks, x, ck, cv, s0, pre, post, lbl, hgn, win, wao, who, wo)


def kernel(x_prompt, x_sample, cache_k, cache_v, state_hgrn, norm_ffn1_pre, norm_ffn1_post, w_ffn1_gate, w_ffn1_up, w_ffn1_down, norm_mix_pre, norm_mix_post, w_in, attn_sinks, hgrn_lb_logits, hgrn_norm, w_attn_out, w_hgrn_out, w_out, norm_ffn2_pre, norm_ffn2_post, w_ffn2_gate, w_ffn2_up, w_ffn2_down):
    depth = w_in.shape[0]
    assert depth == 1 and hgrn_lb_logits.shape[0] == 2, "single-layer stack only"
    batch, seq, _ = x_prompt.shape
    n_s = x_sample.shape[0]
    assert x_sample.shape[1] == 1 and seq % WINDOW == 0

    xp = x_prompt.reshape(batch * seq, D_MODEL)
    xs = x_sample.reshape(n_s, D_MODEL)
    ck = cache_k[0].reshape(n_s, WINDOW, ATTN_KV_W)
    cv = cache_v[0].reshape(n_s, WINDOW, ATTN_KV_W)
    s0 = state_hgrn[0].reshape(n_s, HG_W, HG_D)

    xp, xs = _ffn_call(xp, xs, norm_ffn1_pre, norm_ffn1_post,
                       w_ffn1_gate[0], w_ffn1_up[0], w_ffn1_down[0])

    mix_w = (attn_sinks, norm_mix_pre, norm_mix_post, hgrn_lb_logits, hgrn_norm,
             w_in[0], w_attn_out[0], w_hgrn_out[0], w_out[0])
    xp, nkp, nvp, nsp = _mix_prompt_call(xp, batch, *mix_w)
    xs, nks, nvs, nss = _mix_sample_call(xs, ck, cv, s0, *mix_w)

    xp, xs = _ffn_call(xp, xs, norm_ffn2_pre, norm_ffn2_post,
                       w_ffn2_gate[0], w_ffn2_up[0], w_ffn2_down[0])

    kv_shape = (1, -1, WINDOW, ATTN_KV_HEADS, HEAD_DIM)
    st_shape = (1, -1, HG_HEADS, HG_D, HG_D)
    return (xp.reshape(batch, seq, D_MODEL), xs.reshape(n_s, 1, D_MODEL),
            nkp.reshape(kv_shape), nvp.reshape(kv_shape), nsp.reshape(st_shape),
            nks.reshape(kv_shape), nvs.reshape(kv_shape), nss.reshape(st_shape))
```

```python
import functools

import jax
import jax.numpy as jnp
from jax import lax
from jax.experimental import pallas as pl
from jax.experimental.pallas import tpu as pltpu

F32 = jnp.float32
BF16 = jnp.bfloat16

D_MODEL = 1024
FFN_DIM = 2816
HEAD_DIM = 64
ATTN_Q_HEADS = 8
ATTN_KV_HEADS = 2
WINDOW = 128
HG_HEADS = 4
HG_D = 128
EPS = 1e-6

ATTN_Q_W = ATTN_Q_HEADS * HEAD_DIM
ATTN_KV_W = ATTN_KV_HEADS * HEAD_DIM
HG_W = HG_HEADS * HG_D
OFF_QA = 0
OFF_KA = OFF_QA + ATTN_Q_W
OFF_VA = OFF_KA + ATTN_KV_W
OFF_QH = OFF_VA + ATTN_KV_W
OFF_FH = OFF_QH + HG_W
OFF_IH = OFF_FH + HG_W
OFF_GH = OFF_IH + HG_W
OFF_GA = OFF_GH + HG_W
OFF_GB = OFF_GA + D_MODEL
IN_WIDTH = OFF_GB + D_MODEL

LANES = 128
FFN_CHUNK = 256
N_FFN_CHUNKS = FFN_DIM // FFN_CHUNK
FFN_ROWS = 512
MIX_ROWS = 512
NORM_PARTS = 2
LOOP_GATE_SLABS = 10
HG_CHUNK = 128
SAMPLE_TILE = 8
VMEM_LIMIT = 56 * 1024 * 1024

NT_DIMS = (((1,), (1,)), ((), ()))
TN_DIMS = (((0,), (0,)), ((), ()))


def _rms(x, g):
    return x * lax.rsqrt(jnp.mean(x * x, axis=-1, keepdims=True) + EPS) * g


def _dot(a, b):
    return jnp.dot(a, b, preferred_element_type=F32)


STREAM_SLOTS = 3
STREAM_CHUNK_BYTES = 1 << 20


def _stream_cast(src_hbm, dst_ref):
    n_rows, n_cols = src_hbm.shape
    rc = max(8, min(n_rows, STREAM_CHUNK_BYTES // (4 * n_cols) // 8 * 8))
    while n_rows % rc:
        rc -= 8
    n = n_rows // rc
    slots = min(STREAM_SLOTS, n)

    def body(stage, sem):
        def copy(c):
            return pltpu.make_async_copy(src_hbm.at[pl.ds(c * rc, rc), :], stage.at[c % slots],
                                         sem.at[c % slots])
        for c in range(slots):
            copy(c).start()
        for c in range(n):
            copy(c).wait()
            dst_ref[c * rc:(c + 1) * rc, :] = stage[c % slots].astype(BF16)
            if c + slots < n:
                copy(c + slots).start()

    pl.run_scoped(body, pltpu.VMEM((slots, rc, n_cols), F32), pltpu.SemaphoreType.DMA((slots,)))


HBM_SPEC = pl.BlockSpec(memory_space=pl.ANY)


def _const_spec(shape):
    zeros = (0,) * len(shape)
    return pl.BlockSpec(shape, lambda *_: zeros, pipeline_mode=pl.Buffered(1))


def _ffn_hidden(h, c, wg_ref, wu_ref):
    cols = slice(c * FFN_CHUNK, (c + 1) * FFN_CHUNK)
    g = _dot(h, wg_ref[:, cols])
    u = _dot(h, wu_ref[:, cols])
    return (g * jax.nn.sigmoid(g) * u).astype(BF16)


def _ffn_chunk(h, c, wg_ref, wu_ref, wd_ref):
    return _dot(_ffn_hidden(h, c, wg_ref, wu_ref), wd_ref[c * FFN_CHUNK:(c + 1) * FFN_CHUNK, :])


def _ffn_kernel(n_prompt_steps, xp_ref, xn_ref, xs_ref, pre_ref, post_ref, wg_hbm, wu_hbm, wd_hbm,
                yp_ref, ys_ref, wg_ref, wu_ref, wd_ref, h_ref, acc_ref):
    i = pl.program_id(0)
    slot = i % 2
    w = (wg_ref, wu_ref, wd_ref)

    @pl.when(i == 0)
    def _():
        _stream_cast(wg_hbm, wg_ref)
        _stream_cast(wu_hbm, wu_ref)
        _stream_cast(wd_hbm, wd_ref)
        h0 = _rms(xp_ref[...], pre_ref[...]).astype(BF16)
        h_ref[0] = h0
        acc_ref[...] = _ffn_chunk(h0, 0, *w)

    @pl.when(i < n_prompt_steps)
    def _():
        h = h_ref[slot]
        acc = acc_ref[...]
        for c in range(1, N_FFN_CHUNKS):
            acc = acc + _ffn_chunk(h, c, *w)
            if c == N_FFN_CHUNKS // 2:
                h_next = _rms(xn_ref[...], pre_ref[...]).astype(BF16)
                h_ref[1 - slot] = h_next
                hidden0 = _ffn_hidden(h_next, 0, wg_ref, wu_ref)
        yp_ref[...] = xp_ref[...] + 0.5 * _rms(acc, post_ref[...])
        acc_ref[...] = _dot(hidden0, wd_ref[0:FFN_CHUNK, :])

    @pl.when(i == n_prompt_steps)
    def _():
        h = _rms(xs_ref[...], pre_ref[...]).astype(BF16)
        acc = _ffn_chunk(h, 0, *w)
        for c in range(1, N_FFN_CHUNKS):
            acc = acc + _ffn_chunk(h, c, *w)
        ys_ref[...] = xs_ref[...] + 0.5 * _rms(acc, post_ref[...])


def _ffn_call(xp, xs, pre, post, wg, wu, wd):
    n_p, n_s = xp.shape[0], xs.shape[0]
    rows = min(FFN_ROWS, n_p)
    steps = n_p // rows
    prompt_spec = pl.BlockSpec((rows, D_MODEL), lambda i: (jnp.minimum(i, steps - 1), 0))
    next_spec = pl.BlockSpec((rows, D_MODEL), lambda i: (jnp.minimum(i + 1, steps - 1), 0))
    sample_spec = pl.BlockSpec((n_s, D_MODEL), lambda i: (0, 0))
    return pl.pallas_call(
        functools.partial(_ffn_kernel, steps),
        grid=(steps + 1,),
        in_specs=[prompt_spec, next_spec, sample_spec,
                  _const_spec((1, D_MODEL)), _const_spec((1, D_MODEL)),
                  HBM_SPEC, HBM_SPEC, HBM_SPEC],
        out_specs=[prompt_spec, sample_spec],
        out_shape=[jax.ShapeDtypeStruct(xp.shape, F32), jax.ShapeDtypeStruct(xs.shape, F32)],
        scratch_shapes=[pltpu.VMEM(wg.shape, BF16), pltpu.VMEM(wu.shape, BF16),
                        pltpu.VMEM(wd.shape, BF16), pltpu.VMEM((2, rows, D_MODEL), BF16),
                        pltpu.VMEM((rows, D_MODEL), F32)],
        compiler_params=pltpu.CompilerParams(
            dimension_semantics=("arbitrary",), vmem_limit_bytes=VMEM_LIMIT),
        name="ffn_half",
    )(xp, xp, xs, pre, post, wg, wu, wd)


def _forget_lower_bound(lbl):
    l0, l1 = lbl[0:1, :], lbl[1:2, :]
    m = jnp.maximum(l0, l1)
    e0, e1 = jnp.exp(l0 - m), jnp.exp(l1 - m)
    return e0 / (e0 + e1)


def _lane_is_low(shape):
    return lax.broadcasted_iota(jnp.int32, shape, len(shape) - 1) < HEAD_DIM


def _dup_kv(x):
    swapped = pltpu.roll(x, HEAD_DIM, axis=1)
    low = _lane_is_low(x.shape)
    return jnp.where(low, x, swapped), jnp.where(low, swapped, x)


def _merge_out(h, w_ref, off, gates_from):
    return jax.nn.sigmoid(gates_from(off, D_MODEL)) * _dot(h.astype(BF16), w_ref[...])


def _pair_levels(c):
    t = jnp.arange(c, dtype=jnp.int32)[:, None]
    s = jnp.arange(c, dtype=jnp.int32)[None, :]
    x = jnp.maximum(t ^ s, 1)
    lvl = (31 - lax.clz(x)).astype(jnp.int32)
    return jnp.where(t > s, lvl, -1)


LOG2E = 1.4426950408889634


def _head(a, hd):
    return a[:, hd * HG_D:(hd + 1) * HG_D]


HG_LEVELS = HG_CHUNK.bit_length() - 1


def _chunk_rows(ci):
    return slice(ci * HG_CHUNK, (ci + 1) * HG_CHUNK)


def _hgrn_log_decay(g_ref, b_ref, n_chunks):
    c = HG_CHUNK
    row = lax.broadcasted_iota(jnp.int32, (c, c), 0)
    col = lax.broadcasted_iota(jnp.int32, (c, c), 1)
    tril = (col <= row).astype(F32)
    for ci in range(n_chunks):
        r = _chunk_rows(ci)
        b_ref[r, :] = jnp.dot(tril, g_ref[r, :], precision=lax.Precision.HIGHEST,
                              preferred_element_type=F32) * LOG2E


def _hgrn_factor_jobs(q_ref, k_ref, b_ref, x_ref, qe_ref, ke_ref, n_chunks):
    c = HG_CHUNK
    t = lax.broadcasted_iota(jnp.int32, (c, 1), 0)
    last = {}

    def level_job(ci, lvl):
        def run():
            n = 1 << lvl
            r = _chunk_rows(ci)
            bc = b_ref[r, :]
            second = (t & n) != 0
            if n < 8:
                prev = last.get(ci, bc)
                edge = jnp.where(second, pltpu.roll(prev, n, axis=0), prev)
                if 2 * n < 8:
                    last[ci] = jnp.where(second, prev, pltpu.roll(prev, c - n, axis=0))
            else:
                edge = jnp.concatenate(
                    [jnp.broadcast_to(bc[p + n - 1:p + n, :], (2 * n, bc.shape[1]))
                     for p in range(0, c, 2 * n)], axis=0)
            w = jnp.exp2(jnp.where(second, bc - edge, edge - bc))
            x_ref[lvl, r, :] = (jnp.where(second, q_ref[r, :], k_ref[r, :]) * w).astype(BF16)
        return run

    def state_job(ci):
        def run():
            r = _chunk_rows(ci)
            bc = b_ref[r, :]
            qe_ref[r, :] = (q_ref[r, :] * jnp.exp2(bc)).astype(BF16)
            ke_ref[r, :] = (k_ref[r, :] * jnp.exp2(bc[c - 1:c, :] - bc)).astype(BF16)
        return run

    jobs = [level_job(ci, lvl) for lvl in range(HG_LEVELS) for ci in range(n_chunks)]
    return jobs + [state_job(ci) for ci in range(n_chunks)]


def _hgrn_pair_jobs(x_ref, levels, n_chunks):
    c = HG_CHUNK
    heads, chunks = range(HG_HEADS), range(n_chunks)
    a = [[jnp.zeros((c, c), F32) for _ in heads] for _ in chunks]

    def job(lvl, ci):
        def run():
            x = x_ref[lvl, _chunk_rows(ci), :]
            for hd in heads:
                al = lax.dot_general(_head(x, hd), _head(x, hd), NT_DIMS,
                                     preferred_element_type=F32)
                a[ci][hd] = jnp.where(levels == lvl, al, a[ci][hd])
        return run
    return [job(lvl, ci) for lvl in range(HG_LEVELS) for ci in chunks], a


def _hgrn_finish(q_ref, k_ref, v_ref, b_ref, a, qe_ref, ke_ref, st, n_chunks):
    c = HG_CHUNK
    heads, chunks = range(HG_HEADS), range(n_chunks)
    local = []
    for ci in chunks:
        r = _chunk_rows(ci)
        q, k, v = q_ref[r, :], k_ref[r, :], v_ref[r, :]
        qk, vb = q * k, v.astype(BF16)
        outs = []
        for hd in heads:
            diag = jnp.sum(_head(qk, hd), axis=1, keepdims=True)
            outs.append(_dot(a[ci][hd].astype(BF16), _head(vb, hd)) + diag * _head(v, hd))
        local.append(outs)
        yield

    result = []
    for ci in chunks:
        r = _chunk_rows(ci)
        qe, ke, vb = qe_ref[r, :], ke_ref[r, :], v_ref[r, :].astype(BF16)
        decay = jnp.exp2(b_ref[ci * c + c - 1:ci * c + c, :])
        outs, new = [], []
        for hd in heads:
            outs.append(local[ci][hd] + lax.dot_general(_head(qe, hd), st[hd].astype(BF16), NT_DIMS,
                                                        preferred_element_type=F32))
            new.append(st[hd] * _head(decay, hd)
                       + lax.dot_general(_head(vb, hd), _head(ke, hd), TN_DIMS,
                                         preferred_element_type=F32))
        st = new
        result.append(outs)
        yield
    return result, st


def _mix_prompt_kernel(sinks_ref, x_ref, pre_ref, post_ref, lbl_ref, hgn_ref, lvl_ref,
                       win_hbm, wao_hbm, who_hbm, wo_hbm,
                       y_ref, nk_ref, nv_ref, ns_ref,
                       win_ref, wao_ref, who_ref, wo_ref,
                       kbuf, vbuf, st_ref, qh_ref, kh_ref, vh_ref, lf_ref, oa_ref, oh_ref, g_ref,
                       b_ref, xl_ref, qe_ref, ke_ref):
    j = pl.program_id(1)
    rows = x_ref.shape[0]
    n_blocks = rows // WINDOW

    @pl.when((pl.program_id(0) == 0) & (j == 0))
    def _():
        for src, dst in ((win_hbm, win_ref), (wao_hbm, wao_ref), (who_hbm, who_ref),
                         (wo_hbm, wo_ref)):
            _stream_cast(src, dst)

    @pl.when(j == 0)
    def _():
        kbuf[0:WINDOW, :] = jnp.zeros((WINDOW, LANES), F32)
        vbuf[0:WINDOW, :] = jnp.zeros((WINDOW, LANES), F32)
        st_ref[...] = jnp.zeros(st_ref.shape, F32)

    part_rows = [slice(r0, r0 + rows // NORM_PARTS) for r0 in range(0, rows, rows // NORM_PARTS)]
    h_parts, qkv_parts = [], []
    for r in part_rows:
        hp = _rms(x_ref[r, :], pre_ref[...]).astype(BF16)
        h_parts.append(hp)
        qkv_parts.append(_dot(hp, win_ref[:, OFF_QA:OFF_QH]))
    h = jnp.concatenate(h_parts, axis=0)
    qkv = jnp.concatenate(qkv_parts, axis=0)

    def proj(off, width):
        return _dot(h, win_ref[:, off:off + width])

    lb = _forget_lower_bound(lbl_ref[...])
    n_chunks = rows // HG_CHUNK

    def proj_slab(off):
        def run():
            z = proj(off, FFN_CHUNK)
            if off < OFF_FH:
                cols = slice(off - OFF_QH, off - OFF_QH + FFN_CHUNK)
                qh_ref[:, cols] = z * jax.nn.sigmoid(z)
            elif off < OFF_IH:
                cols = slice(off - OFF_FH, off - OFF_FH + FFN_CHUNK)
                lbs = lb[:, cols]
                lf_ref[:, cols] = jnp.log(lbs + (1.0 - lbs) * jax.nn.sigmoid(z))
                kh_ref[:, cols] = (1.0 - lbs) * jax.nn.sigmoid(-z)
            elif off < OFF_GH:
                cols = slice(off - OFF_IH, off - OFF_IH + FFN_CHUNK)
                vh_ref[:, cols] = z
            else:
                cols = slice(off - OFF_GH, off - OFF_GH + FFN_CHUNK)
                g_ref[:, cols] = z
        return run
    hgrn_slabs = [proj_slab(off) for off in range(OFF_QH, OFF_GH, FFN_CHUNK)]
    gate_slabs = [proj_slab(off) for off in range(OFF_GH, IN_WIDTH, FFN_CHUNK)]
    jobs = _hgrn_factor_jobs(qh_ref, kh_ref, b_ref, xl_ref, qe_ref, ke_ref, n_chunks)
    pair_jobs, in_chunk = _hgrn_pair_jobs(xl_ref, lvl_ref[...], n_chunks)

    qa = qkv[:, OFF_QA:OFF_KA] * (HEAD_DIM ** -0.5)
    ka = qkv[:, OFF_KA:OFF_VA]
    va = qkv[:, OFF_VA:OFF_QH]
    kbuf[WINDOW:WINDOW + rows, :] = ka
    vbuf[WINDOW:WINDOW + rows, :] = va

    qi = lax.broadcasted_iota(jnp.int32, (WINDOW, 2 * WINDOW), 0)
    kj = lax.broadcasted_iota(jnp.int32, (WINDOW, 2 * WINDOW), 1)
    band = (kj > qi) & (kj <= qi + WINDOW)
    low = _lane_is_low((WINDOW, LANES))
    top = lax.broadcasted_iota(jnp.int32, (2 * WINDOW, 1), 0) < WINDOW
    n_slabs = ATTN_Q_HEADS // 2
    group = ATTN_Q_HEADS // ATTN_KV_HEADS
    kds = [_dup_kv(kbuf[n * WINDOW:(n + 2) * WINDOW, :]) for n in range(n_blocks)]
    vds = [_dup_kv(vbuf[n * WINDOW:(n + 2) * WINDOW, :]) for n in range(n_blocks)]

    def scores(n, slab):
        qs = qa[n * WINDOW:(n + 1) * WINDOW, slab * LANES:(slab + 1) * LANES]
        q2 = jnp.concatenate([jnp.where(low, qs, 0.0), jnp.where(low, 0.0, qs)], axis=0)
        return lax.dot_general(q2.astype(BF16), kds[n][(2 * slab) // group].astype(BF16), NT_DIMS,
                               preferred_element_type=F32)

    def attend(n, slab, s):
        valid = band & ((j > 0) | (kj >= WINDOW)) if n == 0 else band
        s = jnp.where(jnp.concatenate([valid, valid], axis=0), s, -jnp.inf)
        sk = jnp.where(top, sinks_ref[0, 2 * slab], sinks_ref[0, 2 * slab + 1])
        m = jnp.maximum(jnp.max(s, axis=1, keepdims=True), sk)
        p = jnp.exp(s - m)
        den = jnp.sum(p, axis=1, keepdims=True) + jnp.exp(sk - m)
        o2 = _dot(p.astype(BF16), vds[n][(2 * slab) // group].astype(BF16)) / den
        oa_ref[n * WINDOW:(n + 1) * WINDOW, slab * LANES:(slab + 1) * LANES] = (
            jnp.where(low, o2[0:WINDOW], o2[WINDOW:2 * WINDOW]))

    order = [(n, slab) for n in range(n_blocks) for slab in range(n_slabs)]
    early = len(hgrn_slabs)
    late_gates = gate_slabs[LOOP_GATE_SLABS:]
    gate_slabs = gate_slabs[:LOOP_GATE_SLABS]
    pending = scores(*order[0])
    ready_pairs = 0
    for idx, (n, slab) in enumerate(order):
        nxt = scores(*order[idx + 1]) if idx + 1 < len(order) else None
        if idx < early:
            hgrn_slabs[idx]()
        else:
            if idx == early:
                _hgrn_log_decay(lf_ref, b_ref, n_chunks)
            if gate_slabs:
                gate_slabs.pop(0)()
            for _ in range(min(len(pair_jobs), ready_pairs)):
                pair_jobs.pop(0)()
            ready_pairs = 0
            for _ in range(-(-len(jobs) // (len(order) - 1 - idx)) if idx + 1 < len(order) else 0):
                if jobs:
                    jobs.pop(0)()
                    ready_pairs += 1
        attend(n, slab, pending)
        pending = nxt
    kbuf[0:WINDOW, :] = kbuf[rows:rows + WINDOW, :]
    vbuf[0:WINDOW, :] = vbuf[rows:rows + WINDOW, :]
    assert not jobs and not hgrn_slabs[early:]
    late = gate_slabs + late_gates
    for run in pair_jobs:
        run()

    nk_ref[0] = ka[rows - WINDOW:rows, :]
    nv_ref[0] = va[rows - WINDOW:rows, :]

    hgn = hgn_ref[...]
    st0 = [st_ref[hd * HG_D:(hd + 1) * HG_D, :] for hd in range(HG_HEADS)]
    finish = _hgrn_finish(qh_ref, kh_ref, vh_ref, b_ref, in_chunk, qe_ref, ke_ref, st0, n_chunks)
    while True:
        try:
            next(finish)
        except StopIteration as done:
            outs, states = done.value
            break
        if late:
            late.pop(0)()
    for run in late:
        run()
    for hd in range(HG_HEADS):
        cs = slice(hd * HG_D, (hd + 1) * HG_D)
        st_ref[cs, :] = states[hd]
        ns_ref[0, cs, :] = states[hd].T
        for ci in range(n_chunks):
            oh_ref[ci * HG_CHUNK:(ci + 1) * HG_CHUNK, cs] = _rms(outs[ci][hd], hgn)

    gh = g_ref[:, 0:HG_W]
    oh = oh_ref[...] * (gh * jax.nn.sigmoid(gh))

    def gate(off, width):
        return g_ref[:, off - OFF_GH:off - OFF_GH + width]
    m = (_merge_out(oa_ref[...], wao_ref, OFF_GA, gate)
         + _merge_out(oh, who_ref, OFF_GB, gate))
    mb = m.astype(BF16)
    for r in part_rows:
        y_ref[r, :] = x_ref[r, :] + _rms(_dot(mb[r, :], wo_ref[...]), post_ref[...])


def _mix_prompt_call(x, batch, sinks, pre, post, lbl, hgn, win, wao, who, wo):
    seq = x.shape[0] // batch
    rows = min(MIX_ROWS, seq)
    steps = seq // rows
    x_spec = pl.BlockSpec((rows, D_MODEL), lambda b, j: (b * steps + j, 0))
    kv_spec = pl.BlockSpec((1, WINDOW, ATTN_KV_W), lambda b, j: (b, 0, 0))
    st_spec = pl.BlockSpec((1, HG_W, HG_D), lambda b, j: (b, 0, 0))
    act = pltpu.VMEM((rows, HG_W), F32)
    return pl.pallas_call(
        _mix_prompt_kernel,
        grid=(batch, steps),
        in_specs=[pl.BlockSpec(memory_space=pltpu.SMEM), x_spec,
                  _const_spec((1, D_MODEL)), _const_spec((1, D_MODEL)),
                  _const_spec(lbl.shape), _const_spec((1, HG_D)),
                  _const_spec((HG_CHUNK, HG_CHUNK)),
                  HBM_SPEC, HBM_SPEC, HBM_SPEC, HBM_SPEC],
        out_specs=[x_spec, kv_spec, kv_spec, st_spec],
        out_shape=[jax.ShapeDtypeStruct(x.shape, F32),
                   jax.ShapeDtypeStruct((batch, WINDOW, ATTN_KV_W), F32),
                   jax.ShapeDtypeStruct((batch, WINDOW, ATTN_KV_W), F32),
                   jax.ShapeDtypeStruct((batch, HG_W, HG_D), F32)],
        scratch_shapes=[pltpu.VMEM(win.shape, BF16), pltpu.VMEM(wao.shape, BF16),
                        pltpu.VMEM(who.shape, BF16), pltpu.VMEM(wo.shape, BF16),
                        pltpu.VMEM((rows + WINDOW, ATTN_KV_W), F32),
                        pltpu.VMEM((rows + WINDOW, ATTN_KV_W), F32),
                        pltpu.VMEM((HG_W, HG_D), F32),
                        act, act, act, act,
                        pltpu.VMEM((rows, ATTN_Q_W), F32), act,
                        pltpu.VMEM((rows, IN_WIDTH - OFF_GH), F32),
                        act, pltpu.VMEM((HG_LEVELS, rows, HG_W), BF16),
                        pltpu.VMEM((rows, HG_W), BF16), pltpu.VMEM((rows, HG_W), BF16)],
        compiler_params=pltpu.CompilerParams(
            dimension_semantics=("arbitrary", "arbitrary"), vmem_limit_bytes=VMEM_LIMIT),
        name="mix_prompt",
    )(sinks, x, pre, post, lbl, hgn, _pair_levels(HG_CHUNK), win, wao, who, wo)


def _row_select(rows_list):
    n = -(-len(rows_list) // 8) * 8
    lanes = rows_list[0].shape[1]
    ridx = lax.broadcasted_iota(jnp.int32, (n, lanes), 0)
    out = jnp.zeros((n, lanes), F32)
    for i, r in enumerate(rows_list):
        out = jnp.where(ridx == i, jnp.broadcast_to(r, (n, lanes)), out)
    return out


def _mix_sample_kernel(sinks_ref, x_ref, ck_ref, cv_ref, s0_ref, pre_ref, post_ref, lbl_ref, hgn_ref,
                       win_hbm, wao_hbm, who_hbm, wo_hbm,
                       y_ref, nk_ref, nv_ref, ns_ref,
                       win_ref, wao_ref, who_ref, wo_ref,
                       h_ref, qa_ref, ka_ref, va_ref, qh_ref, kh_ref, fh_ref, vh_ref, oa_ref, oh_ref):
    i = pl.program_id(0)
    tile = ck_ref.shape[0]

    def proj(off, width):
        return _dot(h_ref[...], win_ref[:, off:off + width])

    @pl.when(i == 0)
    def _():
        for src, dst in ((win_hbm, win_ref), (wao_hbm, wao_ref), (who_hbm, who_ref),
                         (wo_hbm, wo_ref)):
            _stream_cast(src, dst)
        h_ref[...] = _rms(x_ref[...], pre_ref[...]).astype(BF16)
        qa_ref[...] = proj(OFF_QA, ATTN_Q_W) * (HEAD_DIM ** -0.5)
        ka_ref[...] = proj(OFF_KA, ATTN_KV_W)
        va_ref[...] = proj(OFF_VA, ATTN_KV_W)
        lb = _forget_lower_bound(lbl_ref[...])
        qh = proj(OFF_QH, HG_W)
        qh_ref[...] = qh * jax.nn.sigmoid(qh)
        fp = proj(OFF_FH, HG_W)
        fh_ref[...] = jnp.exp(jnp.log(lb + (1.0 - lb) * jax.nn.sigmoid(fp)))
        kh_ref[...] = (1.0 - lb) * jax.nn.sigmoid(-fp)
        vh_ref[...] = proj(OFF_IH, HG_W)

    low = _lane_is_low((1, LANES))
    pos = lax.broadcasted_iota(jnp.int32, (WINDOW, LANES), 0)
    hrow = lax.broadcasted_iota(jnp.int32, (ATTN_Q_HEADS, 1), 0)
    sk = jnp.zeros((ATTN_Q_HEADS, 1), F32)
    for hd in range(ATTN_Q_HEADS):
        sk = jnp.where(hrow == hd, sinks_ref[0, hd], sk)
    group = ATTN_Q_HEADS // ATTN_KV_HEADS
    hgn = hgn_ref[...]

    r8 = pl.ds(pl.multiple_of(i * tile, tile), tile)
    qa8, ka8, va8 = qa_ref[r8, :], ka_ref[r8, :], va_ref[r8, :]
    fh8, kh8, qh8, vh8 = fh_ref[r8, :], kh_ref[r8, :], qh_ref[r8, :], vh_ref[r8, :]
    oa_rows = [[] for _ in range(ATTN_Q_HEADS // 2)]
    oh_rows = [[] for _ in range(HG_HEADS)]
    seqs = range(tile)
    kws, vws, q8s = [], [], []
    for bi in seqs:
        r1 = slice(bi, bi + 1)
        kw = jnp.where(pos == WINDOW - 1, ka8[r1, :], pltpu.roll(ck_ref[bi], WINDOW - 1, axis=0))
        vw = jnp.where(pos == WINDOW - 1, va8[r1, :], pltpu.roll(cv_ref[bi], WINDOW - 1, axis=0))
        nk_ref[bi] = kw
        nv_ref[bi] = vw
        kws.append(kw.astype(BF16))
        vws.append(vw.astype(BF16))
        qrows = []
        for hd in range(ATTN_Q_HEADS):
            slab = qa8[r1, (hd // 2) * LANES:(hd // 2 + 1) * LANES]
            in_place = (hd % 2) == (hd // group)
            src = slab if in_place else pltpu.roll(slab, HEAD_DIM, axis=1)
            on_kv_lanes = low if hd // group == 0 else jnp.logical_not(low)
            qrows.append(jnp.where(on_kv_lanes, src, 0.0))
        q8s.append(_row_select(qrows).astype(BF16))
    scores = [lax.dot_general(q8s[bi], kws[bi], NT_DIMS, preferred_element_type=F32) for bi in seqs]
    probs = []
    for s in scores:
        m = jnp.maximum(jnp.max(s, axis=1, keepdims=True), sk)
        p = jnp.exp(s - m)
        probs.append((p / (jnp.sum(p, axis=1, keepdims=True) + jnp.exp(sk - m))).astype(BF16))
    for bi in seqs:
        o8 = _dot(probs[bi], vws[bi])
        for slab in range(ATTN_Q_HEADS // 2):
            kv = (2 * slab) // group
            even, odd = o8[2 * slab:2 * slab + 1, :], o8[2 * slab + 1:2 * slab + 2, :]
            if kv == 0:
                out = jnp.where(low, even, pltpu.roll(odd, HEAD_DIM, axis=1))
            else:
                out = jnp.where(low, pltpu.roll(even, HEAD_DIM, axis=1), odd)
            oa_rows[slab].append(out)

    def columns(a8, hd):
        pad = jnp.zeros((HG_D - tile, HG_D), F32)
        return jnp.concatenate([_head(a8, hd), pad], axis=0).T
    seq_row = lax.broadcasted_iota(jnp.int32, (HG_D, HG_D), 0)
    for hd in range(HG_HEADS):
        rs = slice(hd * HG_D, (hd + 1) * HG_D)
        f_cols = columns(fh8, hd)
        k_cols = columns(kh8, hd).astype(BF16)
        v_rows = jnp.concatenate([_head(vh8, hd), jnp.zeros((HG_D - tile, HG_D), F32)], axis=0)
        q_rows = _head(qh8, hd).astype(BF16)
        outers = [_dot(k_cols, jnp.where(seq_row == bi, v_rows, 0.0).astype(BF16))
                  for bi in seqs]
        for bi in seqs:
            f_col = jnp.broadcast_to(f_cols[:, bi:bi + 1], (HG_D, HG_D))
            ns_ref[bi, rs, :] = f_col * s0_ref[bi, rs, :] + outers[bi]
        outs = [_dot(q_rows, ns_ref[bi, rs, :].astype(BF16)) for bi in seqs]
        for bi in seqs:
            oh_rows[hd].append(_rms(outs[bi][bi:bi + 1, :], hgn))
    for slab in range(ATTN_Q_HEADS // 2):
        oa_ref[r8, slab * LANES:(slab + 1) * LANES] = _row_select(oa_rows[slab])
    for hd in range(HG_HEADS):
        oh_ref[r8, hd * HG_D:(hd + 1) * HG_D] = _row_select(oh_rows[hd])

    @pl.when(i == pl.num_programs(0) - 1)
    def _():
        gh = proj(OFF_GH, HG_W)
        oh = oh_ref[...] * (gh * jax.nn.sigmoid(gh))
        m = (_merge_out(oa_ref[...], wao_ref, OFF_GA, proj)
             + _merge_out(oh, who_ref, OFF_GB, proj))
        y_ref[...] = x_ref[...] + _rms(_dot(m.astype(BF16), wo_ref[...]), post_ref[...])


def _mix_sample_call(x, ck, cv, s0, sinks, pre, post, lbl, hgn, win, wao, who, wo):
    n = x.shape[0]
    tile = min(SAMPLE_TILE, n)
    full = pl.BlockSpec((n, D_MODEL), lambda i: (0, 0))
    kv_spec = pl.BlockSpec((tile, WINDOW, ATTN_KV_W), lambda i: (i, 0, 0))
    st_spec = pl.BlockSpec((tile, HG_W, HG_D), lambda i: (i, 0, 0))
    act = pltpu.VMEM((n, HG_W), F32)
    kv_act = pltpu.VMEM((n, ATTN_KV_W), F32)
    return pl.pallas_call(
        _mix_sample_kernel,
        grid=(n // tile,),
        in_specs=[pl.BlockSpec(memory_space=pltpu.SMEM), full, kv_spec, kv_spec, st_spec,
                  _const_spec((1, D_MODEL)), _const_spec((1, D_MODEL)),
                  _const_spec(lbl.shape), _const_spec((1, HG_D)),
                  HBM_SPEC, HBM_SPEC, HBM_SPEC, HBM_SPEC],
        out_specs=[full, kv_spec, kv_spec, st_spec],
        out_shape=[jax.ShapeDtypeStruct(x.shape, F32),
                   jax.ShapeDtypeStruct(ck.shape, F32),
                   jax.ShapeDtypeStruct(cv.shape, F32),
                   jax.ShapeDtypeStruct(s0.shape, F32)],
        scratch_shapes=[pltpu.VMEM(win.shape, BF16), pltpu.VMEM(wao.shape, BF16),
                        pltpu.VMEM(who.shape, BF16), pltpu.VMEM(wo.shape, BF16),
                        pltpu.VMEM((n, D_MODEL), BF16),
                        pltpu.VMEM((n, ATTN_Q_W), F32), kv_act, kv_act,
                        act, act, act, act,
                        pltpu.VMEM((n, ATTN_Q_W), F32), act],
        compiler_params=pltpu.CompilerParams(
            dimension_semantics=("arbitrary",), vmem_limit_bytes=VMEM_LIMIT),
        name="mix_sample",
    )(sinks, x, ck, cv, s0, pre, post, lbl, hgn, win, wao, who, wo)


def kernel(x_prompt, x_sample, cache_k, cache_v, state_hgrn, norm_ffn1_pre, norm_ffn1_post, w_ffn1_gate, w_ffn1_up, w_ffn1_down, norm_mix_pre, norm_mix_post, w_in, attn_sinks, hgrn_lb_logits, hgrn_norm, w_attn_out, w_hgrn_out, w_out, norm_ffn2_pre, norm_ffn2_post, w_ffn2_gate, w_ffn2_up, w_ffn2_down):
    depth = w_in.shape[0]
    assert depth == 1 and hgrn_lb_logits.shape[0] == 2, "single-layer stack only"
    batch, seq, _ = x_prompt.shape
    n_s = x_sample.shape[0]
    assert x_sample.shape[1] == 1 and seq % WINDOW == 0

    xp = x_prompt.reshape(batch * seq, D_MODEL)
    xs = x_sample.reshape(n_s, D_MODEL)
    ck = cache_k[0].reshape(n_s, WINDOW, ATTN_KV_W)
    cv = cache_v[0].reshape(n_s, WINDOW, ATTN_KV_W)
    s0 = state_hgrn[0].reshape(n_s, HG_W, HG_D)

    xp, xs = _ffn_call(xp, xs, norm_ffn1_pre, norm_ffn1_post,
                       w_ffn1_gate[0], w_ffn1_up[0], w_ffn1_down[0])

    mix_w = (attn_sinks, norm_mix_pre, norm_mix_post, hgrn_lb_logits, hgrn_norm,
             w_in[0], w_attn_out[0], w_hgrn_out[0], w_out[0])
    xp, nkp, nvp, nsp = _mix_prompt_call(xp, batch, *mix_w)
    xs, nks, nvs, nss = _mix_sample_call(xs, ck, cv, s0, *mix_w)

    xp, xs = _ffn_call(xp, xs, norm_ffn2_pre, norm_ffn2_post,
                       w_ffn2_gate[0], w_ffn2_up[0], w_ffn2_down[0])

    kv_shape = (1, -1, WINDOW, ATTN_KV_HEADS, HEAD_DIM)
    st_shape = (1, -1, HG_HEADS, HG_D, HG_D)
    return (xp.reshape(batch, seq, D_MODEL), xs.reshape(n_s, 1, D_MODEL),
            nkp.reshape(kv_shape), nvp.reshape(kv_shape), nsp.reshape(st_shape),
            nks.reshape(kv_shape), nvs.reshape(kv_shape), nss.reshape(st_shape))
```

```python
import functools

import jax
import jax.numpy as jnp
from jax import lax
from jax.experimental import pallas as pl
from jax.experimental.pallas import tpu as pltpu

F32 = jnp.float32
BF16 = jnp.bfloat16

D_MODEL = 1024
FFN_DIM = 2816
HEAD_DIM = 64
ATTN_Q_HEADS = 8
ATTN_KV_HEADS = 2
WINDOW = 128
HG_HEADS = 4
HG_D = 128
EPS = 1e-6

ATTN_Q_W = ATTN_Q_HEADS * HEAD_DIM
ATTN_KV_W = ATTN_KV_HEADS * HEAD_DIM
HG_W = HG_HEADS * HG_D
OFF_QA = 0
OFF_KA = OFF_QA + ATTN_Q_W
OFF_VA = OFF_KA + ATTN_KV_W
OFF_QH = OFF_VA + ATTN_KV_W
OFF_FH = OFF_QH + HG_W
OFF_IH = OFF_FH + HG_W
OFF_GH = OFF_IH + HG_W
OFF_GA = OFF_GH + HG_W
OFF_GB = OFF_GA + D_MODEL
IN_WIDTH = OFF_GB + D_MODEL

LANES = 128
FFN_CHUNK = 256
N_FFN_CHUNKS = FFN_DIM // FFN_CHUNK
FFN_ROWS = 1024
MIX_ROWS = 512
NORM_PARTS = 2
LOOP_GATE_SLABS = 10
HG_CHUNK = 128
SAMPLE_TILE = 8
VMEM_LIMIT = 56 * 1024 * 1024

NT_DIMS = (((1,), (1,)), ((), ()))
TN_DIMS = (((0,), (0,)), ((), ()))


def _rms(x, g):
    return x * lax.rsqrt(jnp.mean(x * x, axis=-1, keepdims=True) + EPS) * g


def _dot(a, b):
    return jnp.dot(a, b, preferred_element_type=F32)


STREAM_SLOTS = 3
STREAM_CHUNK_BYTES = 1 << 20


def _stream_cast(src_hbm, dst_ref):
    n_rows, n_cols = src_hbm.shape
    rc = max(8, min(n_rows, STREAM_CHUNK_BYTES // (4 * n_cols) // 8 * 8))
    while n_rows % rc:
        rc -= 8
    n = n_rows // rc
    slots = min(STREAM_SLOTS, n)

    def body(stage, sem):
        def copy(c):
            return pltpu.make_async_copy(src_hbm.at[pl.ds(c * rc, rc), :], stage.at[c % slots],
                                         sem.at[c % slots])
        for c in range(slots):
            copy(c).start()
        for c in range(n):
            copy(c).wait()
            dst_ref[c * rc:(c + 1) * rc, :] = stage[c % slots].astype(BF16)
            if c + slots < n:
                copy(c + slots).start()

    pl.run_scoped(body, pltpu.VMEM((slots, rc, n_cols), F32), pltpu.SemaphoreType.DMA((slots,)))


HBM_SPEC = pl.BlockSpec(memory_space=pl.ANY)


def _const_spec(shape):
    zeros = (0,) * len(shape)
    return pl.BlockSpec(shape, lambda *_: zeros, pipeline_mode=pl.Buffered(1))


def _ffn_tile(x, pre, post, wg_ref, wu_ref, wd_ref):
    h = _rms(x, pre).astype(BF16)
    acc = None
    for c in range(N_FFN_CHUNKS):
        cols = slice(c * FFN_CHUNK, (c + 1) * FFN_CHUNK)
        g = _dot(h, wg_ref[:, cols])
        u = _dot(h, wu_ref[:, cols])
        a = (g * jax.nn.sigmoid(g) * u).astype(BF16)
        d = _dot(a, wd_ref[cols, :])
        acc = d if acc is None else acc + d
    return x + 0.5 * _rms(acc, post)


def _ffn_kernel(n_prompt_steps, xp_ref, xs_ref, pre_ref, post_ref, wg_hbm, wu_hbm, wd_hbm,
                yp_ref, ys_ref, wg_ref, wu_ref, wd_ref):
    i = pl.program_id(0)

    @pl.when(i == 0)
    def _():
        _stream_cast(wg_hbm, wg_ref)
        _stream_cast(wu_hbm, wu_ref)
        _stream_cast(wd_hbm, wd_ref)

    @pl.when(i < n_prompt_steps)
    def _():
        yp_ref[...] = _ffn_tile(xp_ref[...], pre_ref[...], post_ref[...], wg_ref, wu_ref, wd_ref)

    @pl.when(i == n_prompt_steps)
    def _():
        ys_ref[...] = _ffn_tile(xs_ref[...], pre_ref[...], post_ref[...], wg_ref, wu_ref, wd_ref)


def _ffn_call(xp, xs, pre, post, wg, wu, wd):
    n_p, n_s = xp.shape[0], xs.shape[0]
    rows = min(FFN_ROWS, n_p)
    steps = n_p // rows
    prompt_spec = pl.BlockSpec((rows, D_MODEL), lambda i: (jnp.minimum(i, steps - 1), 0))
    sample_spec = pl.BlockSpec((n_s, D_MODEL), lambda i: (0, 0))
    return pl.pallas_call(
        functools.partial(_ffn_kernel, steps),
        grid=(steps + 1,),
        in_specs=[prompt_spec, sample_spec,
                  _const_spec((1, D_MODEL)), _const_spec((1, D_MODEL)),
                  HBM_SPEC, HBM_SPEC, HBM_SPEC],
        out_specs=[prompt_spec, sample_spec],
        out_shape=[jax.ShapeDtypeStruct(xp.shape, F32), jax.ShapeDtypeStruct(xs.shape, F32)],
        scratch_shapes=[pltpu.VMEM(wg.shape, BF16), pltpu.VMEM(wu.shape, BF16),
                        pltpu.VMEM(wd.shape, BF16)],
        compiler_params=pltpu.CompilerParams(
            dimension_semantics=("arbitrary",), vmem_limit_bytes=VMEM_LIMIT),
        name="ffn_half",
    )(xp, xs, pre, post, wg, wu, wd)


def _forget_lower_bound(lbl):
    l0, l1 = lbl[0:1, :], lbl[1:2, :]
    m = jnp.maximum(l0, l1)
    e0, e1 = jnp.exp(l0 - m), jnp.exp(l1 - m)
    return e0 / (e0 + e1)


def _lane_is_low(shape):
    return lax.broadcasted_iota(jnp.int32, shape, len(shape) - 1) < HEAD_DIM


def _dup_kv(x):
    swapped = pltpu.roll(x, HEAD_DIM, axis=1)
    low = _lane_is_low(x.shape)
    return jnp.where(low, x, swapped), jnp.where(low, swapped, x)


def _merge_out(h, w_ref, off, gates_from):
    return jax.nn.sigmoid(gates_from(off, D_MODEL)) * _dot(h.astype(BF16), w_ref[...])


def _pair_levels(c):
    t = jnp.arange(c, dtype=jnp.int32)[:, None]
    s = jnp.arange(c, dtype=jnp.int32)[None, :]
    x = jnp.maximum(t ^ s, 1)
    lvl = (31 - lax.clz(x)).astype(jnp.int32)
    return jnp.where(t > s, lvl, -1)


LOG2E = 1.4426950408889634


def _head(a, hd):
    return a[:, hd * HG_D:(hd + 1) * HG_D]


HG_LEVELS = HG_CHUNK.bit_length() - 1


def _chunk_rows(ci):
    return slice(ci * HG_CHUNK, (ci + 1) * HG_CHUNK)


def _hgrn_log_decay(g_ref, b_ref, n_chunks):
    c = HG_CHUNK
    width = g_ref.shape[1]
    row = lax.broadcasted_iota(jnp.int32, (c, c), 0)
    col = lax.broadcasted_iota(jnp.int32, (c, c), 1)
    tril = (col <= row).astype(BF16)
    for ci in range(n_chunks):
        r = _chunk_rows(ci)
        g = g_ref[r, :]
        hi = g.astype(BF16)
        rest = g - hi.astype(F32)
        mid = rest.astype(BF16)
        lo = (rest - mid.astype(F32)).astype(BF16)
        sums = _dot(tril, jnp.concatenate([hi, mid, lo], axis=1))
        b_ref[r, :] = (sums[:, 0:width] + sums[:, width:2 * width] + sums[:, 2 * width:]) * LOG2E


def _hgrn_factor_jobs(q_ref, k_ref, b_ref, x_ref, qe_ref, ke_ref, n_chunks):
    c = HG_CHUNK
    t = lax.broadcasted_iota(jnp.int32, (c, 1), 0)
    last = {}

    def level_job(ci, lvl):
        def run():
            n = 1 << lvl
            r = _chunk_rows(ci)
            bc = b_ref[r, :]
            second = (t & n) != 0
            if n < 8:
                prev = last.get(ci, bc)
                edge = jnp.where(second, pltpu.roll(prev, n, axis=0), prev)
                if 2 * n < 8:
                    last[ci] = jnp.where(second, prev, pltpu.roll(prev, c - n, axis=0))
            else:
                edge = jnp.concatenate(
                    [jnp.broadcast_to(bc[p + n - 1:p + n, :], (2 * n, bc.shape[1]))
                     for p in range(0, c, 2 * n)], axis=0)
            w = jnp.exp2(jnp.where(second, bc - edge, edge - bc))
            x_ref[lvl, r, :] = (jnp.where(second, q_ref[r, :], k_ref[r, :]) * w).astype(BF16)
        return run

    def state_job(ci):
        def run():
            r = _chunk_rows(ci)
            bc = b_ref[r, :]
            qe_ref[r, :] = (q_ref[r, :] * jnp.exp2(bc)).astype(BF16)
            ke_ref[r, :] = (k_ref[r, :] * jnp.exp2(bc[c - 1:c, :] - bc)).astype(BF16)
        return run

    jobs = [level_job(ci, lvl) for lvl in range(HG_LEVELS) for ci in range(n_chunks)]
    return jobs + [state_job(ci) for ci in range(n_chunks)]


def _hgrn_pair_jobs(x_ref, levels, n_chunks):
    c = HG_CHUNK
    heads, chunks = range(HG_HEADS), range(n_chunks)
    a = [[jnp.zeros((c, c), F32) for _ in heads] for _ in chunks]

    def job(lvl, ci):
        def run():
            x = x_ref[lvl, _chunk_rows(ci), :]
            for hd in heads:
                al = lax.dot_general(_head(x, hd), _head(x, hd), NT_DIMS,
                                     preferred_element_type=F32)
                a[ci][hd] = jnp.where(levels == lvl, al, a[ci][hd])
        return run
    return [job(lvl, ci) for lvl in range(HG_LEVELS) for ci in chunks], a


def _hgrn_finish(q_ref, k_ref, v_ref, b_ref, a, qe_ref, ke_ref, st, n_chunks):
    c = HG_CHUNK
    heads, chunks = range(HG_HEADS), range(n_chunks)
    local = []
    for ci in chunks:
        r = _chunk_rows(ci)
        q, k, v = q_ref[r, :], k_ref[r, :], v_ref[r, :]
        qk, vb = q * k, v.astype(BF16)
        outs = []
        for hd in heads:
            diag = jnp.sum(_head(qk, hd), axis=1, keepdims=True)
            outs.append(_dot(a[ci][hd].astype(BF16), _head(vb, hd)) + diag * _head(v, hd))
        local.append(outs)
        yield

    result = []
    for ci in chunks:
        r = _chunk_rows(ci)
        qe, ke, vb = qe_ref[r, :], ke_ref[r, :], v_ref[r, :].astype(BF16)
        decay = jnp.exp2(b_ref[ci * c + c - 1:ci * c + c, :])
        outs, new = [], []
        for hd in heads:
            outs.append(local[ci][hd] + lax.dot_general(_head(qe, hd), st[hd].astype(BF16), NT_DIMS,
                                                        preferred_element_type=F32))
            new.append(st[hd] * _head(decay, hd)
                       + lax.dot_general(_head(vb, hd), _head(ke, hd), TN_DIMS,
                                         preferred_element_type=F32))
        st = new
        result.append(outs)
        yield
    return result, st


def _mix_prompt_kernel(sinks_ref, x_ref, pre_ref, post_ref, lbl_ref, hgn_ref, lvl_ref,
                       win_hbm, wao_hbm, who_hbm, wo_hbm,
                       y_ref, nk_ref, nv_ref, ns_ref,
                       win_ref, wao_ref, who_ref, wo_ref,
                       kbuf, vbuf, st_ref, qh_ref, kh_ref, vh_ref, lf_ref, oa_ref, oh_ref, g_ref,
                       b_ref, xl_ref, qe_ref, ke_ref):
    j = pl.program_id(1)
    rows = x_ref.shape[0]
    n_blocks = rows // WINDOW

    @pl.when((pl.program_id(0) == 0) & (j == 0))
    def _():
        for src, dst in ((win_hbm, win_ref), (wao_hbm, wao_ref), (who_hbm, who_ref),
                         (wo_hbm, wo_ref)):
            _stream_cast(src, dst)

    @pl.when(j == 0)
    def _():
        kbuf[0:WINDOW, :] = jnp.zeros((WINDOW, LANES), F32)
        vbuf[0:WINDOW, :] = jnp.zeros((WINDOW, LANES), F32)
        st_ref[...] = jnp.zeros(st_ref.shape, F32)

    part_rows = [slice(r0, r0 + rows // NORM_PARTS) for r0 in range(0, rows, rows // NORM_PARTS)]
    h_parts, qkv_parts = [], []
    for r in part_rows:
        hp = _rms(x_ref[r, :], pre_ref[...]).astype(BF16)
        h_parts.append(hp)
        qkv_parts.append(_dot(hp, win_ref[:, OFF_QA:OFF_QH]))
    h = jnp.concatenate(h_parts, axis=0)
    qkv = jnp.concatenate(qkv_parts, axis=0)

    def proj(off, width):
        return _dot(h, win_ref[:, off:off + width])

    lb = _forget_lower_bound(lbl_ref[...])
    n_chunks = rows // HG_CHUNK

    def proj_slab(off):
        def run():
            z = proj(off, FFN_CHUNK)
            if off < OFF_FH:
                cols = slice(off - OFF_QH, off - OFF_QH + FFN_CHUNK)
                qh_ref[:, cols] = z * jax.nn.sigmoid(z)
            elif off < OFF_IH:
                cols = slice(off - OFF_FH, off - OFF_FH + FFN_CHUNK)
                lbs = lb[:, cols]
                lf_ref[:, cols] = jnp.log(lbs + (1.0 - lbs) * jax.nn.sigmoid(z))
                kh_ref[:, cols] = (1.0 - lbs) * jax.nn.sigmoid(-z)
            elif off < OFF_GH:
                cols = slice(off - OFF_IH, off - OFF_IH + FFN_CHUNK)
                vh_ref[:, cols] = z
            else:
                cols = slice(off - OFF_GH, off - OFF_GH + FFN_CHUNK)
                g_ref[:, cols] = z
        return run
    hgrn_slabs = [proj_slab(off) for off in range(OFF_QH, OFF_GH, FFN_CHUNK)]
    gate_slabs = [proj_slab(off) for off in range(OFF_GH, IN_WIDTH, FFN_CHUNK)]
    jobs = _hgrn_factor_jobs(qh_ref, kh_ref, b_ref, xl_ref, qe_ref, ke_ref, n_chunks)
    pair_jobs, in_chunk = _hgrn_pair_jobs(xl_ref, lvl_ref[...], n_chunks)

    qa = qkv[:, OFF_QA:OFF_KA] * (HEAD_DIM ** -0.5)
    ka = qkv[:, OFF_KA:OFF_VA]
    va = qkv[:, OFF_VA:OFF_QH]
    kbuf[WINDOW:WINDOW + rows, :] = ka
    vbuf[WINDOW:WINDOW + rows, :] = va

    qi = lax.broadcasted_iota(jnp.int32, (WINDOW, 2 * WINDOW), 0)
    kj = lax.broadcasted_iota(jnp.int32, (WINDOW, 2 * WINDOW), 1)
    band = (kj > qi) & (kj <= qi + WINDOW)
    low = _lane_is_low((WINDOW, LANES))
    top = lax.broadcasted_iota(jnp.int32, (2 * WINDOW, 1), 0) < WINDOW
    n_slabs = ATTN_Q_HEADS // 2
    group = ATTN_Q_HEADS // ATTN_KV_HEADS
    kds = [_dup_kv(kbuf[n * WINDOW:(n + 2) * WINDOW, :]) for n in range(n_blocks)]
    vds = [_dup_kv(vbuf[n * WINDOW:(n + 2) * WINDOW, :]) for n in range(n_blocks)]

    def scores(n, slab):
        qs = qa[n * WINDOW:(n + 1) * WINDOW, slab * LANES:(slab + 1) * LANES]
        q2 = jnp.concatenate([jnp.where(low, qs, 0.0), jnp.where(low, 0.0, qs)], axis=0)
        return lax.dot_general(q2.astype(BF16), kds[n][(2 * slab) // group].astype(BF16), NT_DIMS,
                               preferred_element_type=F32)

    def attend(n, slab, s):
        valid = band & ((j > 0) | (kj >= WINDOW)) if n == 0 else band
        s = jnp.where(jnp.concatenate([valid, valid], axis=0), s, -jnp.inf)
        sk = jnp.where(top, sinks_ref[0, 2 * slab], sinks_ref[0, 2 * slab + 1])
        m = jnp.maximum(jnp.max(s, axis=1, keepdims=True), sk)
        p = jnp.exp(s - m)
        den = jnp.sum(p, axis=1, keepdims=True) + jnp.exp(sk - m)
        o2 = _dot(p.astype(BF16), vds[n][(2 * slab) // group].astype(BF16)) / den
        oa_ref[n * WINDOW:(n + 1) * WINDOW, slab * LANES:(slab + 1) * LANES] = (
            jnp.where(low, o2[0:WINDOW], o2[WINDOW:2 * WINDOW]))

    order = [(n, slab) for n in range(n_blocks) for slab in range(n_slabs)]
    early = len(hgrn_slabs)
    late_gates = gate_slabs[LOOP_GATE_SLABS:]
    gate_slabs = gate_slabs[:LOOP_GATE_SLABS]
    pending = scores(*order[0])
    ready_pairs = 0
    for idx, (n, slab) in enumerate(order):
        nxt = scores(*order[idx + 1]) if idx + 1 < len(order) else None
        if idx < early:
            hgrn_slabs[idx]()
        else:
            if idx == early:
                _hgrn_log_decay(lf_ref, b_ref, n_chunks)
            if gate_slabs:
                gate_slabs.pop(0)()
            for _ in range(min(len(pair_jobs), ready_pairs)):
                pair_jobs.pop(0)()
            ready_pairs = 0
            for _ in range(-(-len(jobs) // (len(order) - 1 - idx)) if idx + 1 < len(order) else 0):
                if jobs:
                    jobs.pop(0)()
                    ready_pairs += 1
        attend(n, slab, pending)
        pending = nxt
    kbuf[0:WINDOW, :] = kbuf[rows:rows + WINDOW, :]
    vbuf[0:WINDOW, :] = vbuf[rows:rows + WINDOW, :]
    assert not jobs and not hgrn_slabs[early:]
    late = gate_slabs + late_gates
    for run in pair_jobs:
        run()

    nk_ref[0] = ka[rows - WINDOW:rows, :]
    nv_ref[0] = va[rows - WINDOW:rows, :]

    hgn = hgn_ref[...]
    st0 = [st_ref[hd * HG_D:(hd + 1) * HG_D, :] for hd in range(HG_HEADS)]
    finish = _hgrn_finish(qh_ref, kh_ref, vh_ref, b_ref, in_chunk, qe_ref, ke_ref, st0, n_chunks)
    while True:
        try:
            next(finish)
        except StopIteration as done:
            outs, states = done.value
            break
        if late:
            late.pop(0)()
    for run in late:
        run()
    for hd in range(HG_HEADS):
        cs = slice(hd * HG_D, (hd + 1) * HG_D)
        st_ref[cs, :] = states[hd]
        ns_ref[0, cs, :] = states[hd].T
        for ci in range(n_chunks):
            oh_ref[ci * HG_CHUNK:(ci + 1) * HG_CHUNK, cs] = _rms(outs[ci][hd], hgn)

    gh = g_ref[:, 0:HG_W]
    oh = oh_ref[...] * (gh * jax.nn.sigmoid(gh))

    def gate(off, width):
        return g_ref[:, off - OFF_GH:off - OFF_GH + width]
    m = (_merge_out(oa_ref[...], wao_ref, OFF_GA, gate)
         + _merge_out(oh, who_ref, OFF_GB, gate))
    mb = m.astype(BF16)
    for r in part_rows:
        y_ref[r, :] = x_ref[r, :] + _rms(_dot(mb[r, :], wo_ref[...]), post_ref[...])


def _mix_prompt_call(x, batch, sinks, pre, post, lbl, hgn, win, wao, who, wo):
    seq = x.shape[0] // batch
    rows = min(MIX_ROWS, seq)
    steps = seq // rows
    x_spec = pl.BlockSpec((rows, D_MODEL), lambda b, j: (b * steps + j, 0))
    kv_spec = pl.BlockSpec((1, WINDOW, ATTN_KV_W), lambda b, j: (b, 0, 0))
    st_spec = pl.BlockSpec((1, HG_W, HG_D), lambda b, j: (b, 0, 0))
    act = pltpu.VMEM((rows, HG_W), F32)
    return pl.pallas_call(
        _mix_prompt_kernel,
        grid=(batch, steps),
        in_specs=[pl.BlockSpec(memory_space=pltpu.SMEM), x_spec,
                  _const_spec((1, D_MODEL)), _const_spec((1, D_MODEL)),
                  _const_spec(lbl.shape), _const_spec((1, HG_D)),
                  _const_spec((HG_CHUNK, HG_CHUNK)),
                  HBM_SPEC, HBM_SPEC, HBM_SPEC, HBM_SPEC],
        out_specs=[x_spec, kv_spec, kv_spec, st_spec],
        out_shape=[jax.ShapeDtypeStruct(x.shape, F32),
                   jax.ShapeDtypeStruct((batch, WINDOW, ATTN_KV_W), F32),
                   jax.ShapeDtypeStruct((batch, WINDOW, ATTN_KV_W), F32),
                   jax.ShapeDtypeStruct((batch, HG_W, HG_D), F32)],
        scratch_shapes=[pltpu.VMEM(win.shape, BF16), pltpu.VMEM(wao.shape, BF16),
                        pltpu.VMEM(who.shape, BF16), pltpu.VMEM(wo.shape, BF16),
                        pltpu.VMEM((rows + WINDOW, ATTN_KV_W), F32),
                        pltpu.VMEM((rows + WINDOW, ATTN_KV_W), F32),
                        pltpu.VMEM((HG_W, HG_D), F32),
                        act, act, act, act,
                        pltpu.VMEM((rows, ATTN_Q_W), F32), act,
                        pltpu.VMEM((rows, IN_WIDTH - OFF_GH), F32),
                        act, pltpu.VMEM((HG_LEVELS, rows, HG_W), BF16),
                        pltpu.VMEM((rows, HG_W), BF16), pltpu.VMEM((rows, HG_W), BF16)],
        compiler_params=pltpu.CompilerParams(
            dimension_semantics=("arbitrary", "arbitrary"), vmem_limit_bytes=VMEM_LIMIT),
        name="mix_prompt",
    )(sinks, x, pre, post, lbl, hgn, _pair_levels(HG_CHUNK), win, wao, who, wo)


def _row_select(rows_list):
    n = -(-len(rows_list) // 8) * 8
    lanes = rows_list[0].shape[1]
    ridx = lax.broadcasted_iota(jnp.int32, (n, lanes), 0)
    out = jnp.zeros((n, lanes), F32)
    for i, r in enumerate(rows_list):
        out = jnp.where(ridx == i, jnp.broadcast_to(r, (n, lanes)), out)
    return out


def _mix_sample_kernel(sinks_ref, x_ref, ck_ref, cv_ref, s0_ref, pre_ref, post_ref, lbl_ref, hgn_ref,
                       win_hbm, wao_hbm, who_hbm, wo_hbm,
                       y_ref, nk_ref, nv_ref, ns_ref,
                       win_ref, wao_ref, who_ref, wo_ref,
                       h_ref, qa_ref, ka_ref, va_ref, qh_ref, kh_ref, fh_ref, vh_ref, oa_ref, oh_ref):
    i = pl.program_id(0)
    tile = ck_ref.shape[0]

    def proj(off, width):
        return _dot(h_ref[...], win_ref[:, off:off + width])

    @pl.when(i == 0)
    def _():
        for src, dst in ((win_hbm, win_ref), (wao_hbm, wao_ref), (who_hbm, who_ref),
                         (wo_hbm, wo_ref)):
            _stream_cast(src, dst)
        h_ref[...] = _rms(x_ref[...], pre_ref[...]).astype(BF16)
        qa_ref[...] = proj(OFF_QA, ATTN_Q_W) * (HEAD_DIM ** -0.5)
        ka_ref[...] = proj(OFF_KA, ATTN_KV_W)
        va_ref[...] = proj(OFF_VA, ATTN_KV_W)
        lb = _forget_lower_bound(lbl_ref[...])
        qh = proj(OFF_QH, HG_W)
        qh_ref[...] = qh * jax.nn.sigmoid(qh)
        fp = proj(OFF_FH, HG_W)
        fh_ref[...] = jnp.exp(jnp.log(lb + (1.0 - lb) * jax.nn.sigmoid(fp)))
        kh_ref[...] = (1.0 - lb) * jax.nn.sigmoid(-fp)
        vh_ref[...] = proj(OFF_IH, HG_W)

    low = _lane_is_low((1, LANES))
    pos = lax.broadcasted_iota(jnp.int32, (WINDOW, LANES), 0)
    hrow = lax.broadcasted_iota(jnp.int32, (ATTN_Q_HEADS, 1), 0)
    sk = jnp.zeros((ATTN_Q_HEADS, 1), F32)
    for hd in range(ATTN_Q_HEADS):
        sk = jnp.where(hrow == hd, sinks_ref[0, hd], sk)
    group = ATTN_Q_HEADS // ATTN_KV_HEADS
    hgn = hgn_ref[...]

    r8 = pl.ds(pl.multiple_of(i * tile, tile), tile)
    qa8, ka8, va8 = qa_ref[r8, :], ka_ref[r8, :], va_ref[r8, :]
    fh8, kh8, qh8, vh8 = fh_ref[r8, :], kh_ref[r8, :], qh_ref[r8, :], vh_ref[r8, :]
    oa_rows = [[] for _ in range(ATTN_Q_HEADS // 2)]
    oh_rows = [[] for _ in range(HG_HEADS)]
    seqs = range(tile)
    kws, vws, q8s = [], [], []
    for bi in seqs:
        r1 = slice(bi, bi + 1)
        kw = jnp.where(pos == WINDOW - 1, ka8[r1, :], pltpu.roll(ck_ref[bi], WINDOW - 1, axis=0))
        vw = jnp.where(pos == WINDOW - 1, va8[r1, :], pltpu.roll(cv_ref[bi], WINDOW - 1, axis=0))
        nk_ref[bi] = kw
        nv_ref[bi] = vw
        kws.append(kw.astype(BF16))
        vws.append(vw.astype(BF16))
        qrows = []
        for hd in range(ATTN_Q_HEADS):
            slab = qa8[r1, (hd // 2) * LANES:(hd // 2 + 1) * LANES]
            in_place = (hd % 2) == (hd // group)
            src = slab if in_place else pltpu.roll(slab, HEAD_DIM, axis=1)
            on_kv_lanes = low if hd // group == 0 else jnp.logical_not(low)
            qrows.append(jnp.where(on_kv_lanes, src, 0.0))
        q8s.append(_row_select(qrows).astype(BF16))
    scores = [lax.dot_general(q8s[bi], kws[bi], NT_DIMS, preferred_element_type=F32) for bi in seqs]
    probs = []
    for s in scores:
        m = jnp.maximum(jnp.max(s, axis=1, keepdims=True), sk)
        p = jnp.exp(s - m)
        probs.append((p / (jnp.sum(p, axis=1, keepdims=True) + jnp.exp(sk - m))).astype(BF16))
    for bi in seqs:
        o8 = _dot(probs[bi], vws[bi])
        for slab in range(ATTN_Q_HEADS // 2):
            kv = (2 * slab) // group
            even, odd = o8[2 * slab:2 * slab + 1, :], o8[2 * slab + 1:2 * slab + 2, :]
            if kv == 0:
                out = jnp.where(low, even, pltpu.roll(odd, HEAD_DIM, axis=1))
            else:
                out = jnp.where(low, pltpu.roll(even, HEAD_DIM, axis=1), odd)
            oa_rows[slab].append(out)

    def columns(a8, hd):
        pad = jnp.zeros((HG_D - tile, HG_D), F32)
        return jnp.concatenate([_head(a8, hd), pad], axis=0).T
    seq_row = lax.broadcasted_iota(jnp.int32, (HG_D, HG_D), 0)
    for hd in range(HG_HEADS):
        rs = slice(hd * HG_D, (hd + 1) * HG_D)
        f_cols = columns(fh8, hd)
        k_cols = columns(kh8, hd).astype(BF16)
        v_rows = jnp.concatenate([_head(vh8, hd), jnp.zeros((HG_D - tile, HG_D), F32)], axis=0)
        q_rows = _head(qh8, hd).astype(BF16)
        outers = [_dot(k_cols, jnp.where(seq_row == bi, v_rows, 0.0).astype(BF16))
                  for bi in seqs]
        for bi in seqs:
            f_col = jnp.broadcast_to(f_cols[:, bi:bi + 1], (HG_D, HG_D))
            ns_ref[bi, rs, :] = f_col * s0_ref[bi, rs, :] + outers[bi]
        outs = [_dot(q_rows, ns_ref[bi, rs, :].astype(BF16)) for bi in seqs]
        for bi in seqs:
            oh_rows[hd].append(_rms(outs[bi][bi:bi + 1, :], hgn))
    for slab in range(ATTN_Q_HEADS // 2):
        oa_ref[r8, slab * LANES:(slab + 1) * LANES] = _row_select(oa_rows[slab])
    for hd in range(HG_HEADS):
        oh_ref[r8, hd * HG_D:(hd + 1) * HG_D] = _row_select(oh_rows[hd])

    @pl.when(i == pl.num_programs(0) - 1)
    def _():
        gh = proj(OFF_GH, HG_W)
        oh = oh_ref[...] * (gh * jax.nn.sigmoid(gh))
        m = (_merge_out(oa_ref[...], wao_ref, OFF_GA, proj)
             + _merge_out(oh, who_ref, OFF_GB, proj))
        y_ref[...] = x_ref[...] + _rms(_dot(m.astype(BF16), wo_ref[...]), post_ref[...])


def _mix_sample_call(x, ck, cv, s0, sinks, pre, post, lbl, hgn, win, wao, who, wo):
    n = x.shape[0]
    tile = min(SAMPLE_TILE, n)
    full = pl.BlockSpec((n, D_MODEL), lambda i: (0, 0))
    kv_spec = pl.BlockSpec((tile, WINDOW, ATTN_KV_W), lambda i: (i, 0, 0))
    st_spec = pl.BlockSpec((tile, HG_W, HG_D), lambda i: (i, 0, 0))
    act = pltpu.VMEM((n, HG_W), F32)
    kv_act = pltpu.VMEM((n, ATTN_KV_W), F32)
    return pl.pallas_call(
        _mix_sample_kernel,
        grid=(n // tile,),
        in_specs=[pl.BlockSpec(memory_space=pltpu.SMEM), full, kv_spec, kv_spec, st_spec,
                  _const_spec((1, D_MODEL)), _const_spec((1, D_MODEL)),
                  _const_spec(lbl.shape), _const_spec((1, HG_D)),
                  HBM_SPEC, HBM_SPEC, HBM_SPEC, HBM_SPEC],
        out_specs=[full, kv_spec, kv_spec, st_spec],
        out_shape=[jax.ShapeDtypeStruct(x.shape, F32),
                   jax.ShapeDtypeStruct(ck.shape, F32),
                   jax.ShapeDtypeStruct(cv.shape, F32),
                   jax.ShapeDtypeStruct(s0.shape, F32)],
        scratch_shapes=[pltpu.VMEM(win.shape, BF16), pltpu.VMEM(wao.shape, BF16),
                        pltpu.VMEM(who.shape, BF16), pltpu.VMEM(wo.shape, BF16),
                        pltpu.VMEM((n, D_MODEL), BF16),
                        pltpu.VMEM((n, ATTN_Q_W), F32), kv_act, kv_act,
                        act, act, act, act,
                        pltpu.VMEM((n, ATTN_Q_W), F32), act],
        compiler_params=pltpu.CompilerParams(
            dimension_semantics=("arbitrary",), vmem_limit_bytes=VMEM_LIMIT),
        name="mix_sample",
    )(sinks, x, ck, cv, s0, pre, post, lbl, hgn, win, wao, who, wo)


def kernel(x_prompt, x_sample, cache_k, cache_v, state_hgrn, norm_ffn1_pre, norm_ffn1_post, w_ffn1_gate, w_ffn1_up, w_ffn1_down, norm_mix_pre, norm_mix_post, w_in, attn_sinks, hgrn_lb_logits, hgrn_norm, w_attn_out, w_hgrn_out, w_out, norm_ffn2_pre, norm_ffn2_post, w_ffn2_gate, w_ffn2_up, w_ffn2_down):
    depth = w_in.shape[0]
    assert depth == 1 and hgrn_lb_logits.shape[0] == 2, "single-layer stack only"
    batch, seq, _ = x_prompt.shape
    n_s = x_sample.shape[0]
    assert x_sample.shape[1] == 1 and seq % WINDOW == 0

    xp = x_prompt.reshape(batch * seq, D_MODEL)
    xs = x_sample.reshape(n_s, D_MODEL)
    ck = cache_k[0].reshape(n_s, WINDOW, ATTN_KV_W)
    cv = cache_v[0].reshape(n_s, WINDOW, ATTN_KV_W)
    s0 = state_hgrn[0].reshape(n_s, HG_W, HG_D)

    xp, xs = _ffn_call(xp, xs, norm_ffn1_pre, norm_ffn1_post,
                       w_ffn1_gate[0], w_ffn1_up[0], w_ffn1_down[0])

    mix_w = (attn_sinks, norm_mix_pre, norm_mix_post, hgrn_lb_logits, hgrn_norm,
             w_in[0], w_attn_out[0], w_hgrn_out[0], w_out[0])
    xp, nkp, nvp, nsp = _mix_prompt_call(xp, batch, *mix_w)
    xs, nks, nvs, nss = _mix_sample_call(xs, ck, cv, s0, *mix_w)

    xp, xs = _ffn_call(xp, xs, norm_ffn2_pre, norm_ffn2_post,
                       w_ffn2_gate[0], w_ffn2_up[0], w_ffn2_down[0])

    kv_shape = (1, -1, WINDOW, ATTN_KV_HEADS, HEAD_DIM)
    st_shape = (1, -1, HG_HEADS, HG_D, HG_D)
    return (xp.reshape(batch, seq, D_MODEL), xs.reshape(n_s, 1, D_MODEL),
            nkp.reshape(kv_shape), nvp.reshape(kv_shape), nsp.reshape(st_shape),
            nks.reshape(kv_shape), nvs.reshape(kv_shape), nss.reshape(st_shape))
```

```python
import functools

import jax
import jax.numpy as jnp
from jax import lax
from jax.experimental import pallas as pl
from jax.experimental.pallas import tpu as pltpu

F32 = jnp.float32
BF16 = jnp.bfloat16

D_MODEL = 1024
FFN_DIM = 2816
HEAD_DIM = 64
ATTN_Q_HEADS = 8
ATTN_KV_HEADS = 2
WINDOW = 128
HG_HEADS = 4
HG_D = 128
EPS = 1e-6

ATTN_Q_W = ATTN_Q_HEADS * HEAD_DIM
ATTN_KV_W = ATTN_KV_HEADS * HEAD_DIM
HG_W = HG_HEADS * HG_D
OFF_QA = 0
OFF_KA = OFF_QA + ATTN_Q_W
OFF_VA = OFF_KA + ATTN_KV_W
OFF_QH = OFF_VA + ATTN_KV_W
OFF_FH = OFF_QH + HG_W
OFF_IH = OFF_FH + HG_W
OFF_GH = OFF_IH + HG_W
OFF_GA = OFF_GH + HG_W
OFF_GB = OFF_GA + D_MODEL
IN_WIDTH = OFF_GB + D_MODEL

LANES = 128
SUBLANES = 8
FFN_CHUNK = 256
N_FFN_CHUNKS = FFN_DIM // FFN_CHUNK
FFN_ROWS = 1024
MIX_ROWS = 512
NORM_PARTS = 2
LOOP_GATE_SLABS = 10
HG_CHUNK = 128
SAMPLE_TILE = 8
VMEM_LIMIT = 56 * 1024 * 1024

NT_DIMS = (((1,), (1,)), ((), ()))
TN_DIMS = (((0,), (0,)), ((), ()))


def _rms(x, g):
    return x * lax.rsqrt(jnp.mean(x * x, axis=-1, keepdims=True) + EPS) * g


def _dot(a, b):
    return jnp.dot(a, b, preferred_element_type=F32)


STREAM_SLOTS = 3
STREAM_CHUNK_BYTES = 1 << 20


def _stream_cast(src_hbm, dst_ref):
    n_rows, n_cols = src_hbm.shape
    rc = max(8, min(n_rows, STREAM_CHUNK_BYTES // (4 * n_cols) // 8 * 8))
    while n_rows % rc:
        rc -= 8
    n = n_rows // rc
    slots = min(STREAM_SLOTS, n)

    def body(stage, sem):
        def copy(c):
            return pltpu.make_async_copy(src_hbm.at[pl.ds(c * rc, rc), :], stage.at[c % slots],
                                         sem.at[c % slots])
        for c in range(slots):
            copy(c).start()
        for c in range(n):
            copy(c).wait()
            dst_ref[c * rc:(c + 1) * rc, :] = stage[c % slots].astype(BF16)
            if c + slots < n:
                copy(c + slots).start()

    pl.run_scoped(body, pltpu.VMEM((slots, rc, n_cols), F32), pltpu.SemaphoreType.DMA((slots,)))


HBM_SPEC = pl.BlockSpec(memory_space=pl.ANY)


def _const_spec(shape):
    zeros = (0,) * len(shape)
    return pl.BlockSpec(shape, lambda *_: zeros, pipeline_mode=pl.Buffered(1))


def _ffn_tile(x, pre, post, wg_ref, wu_ref, wd_ref):
    h = _rms(x, pre).astype(BF16)
    acc = None
    for c in range(N_FFN_CHUNKS):
        cols = slice(c * FFN_CHUNK, (c + 1) * FFN_CHUNK)
        g = _dot(h, wg_ref[:, cols])
        u = _dot(h, wu_ref[:, cols])
        a = (g * jax.nn.sigmoid(g) * u).astype(BF16)
        d = _dot(a, wd_ref[cols, :])
        acc = d if acc is None else acc + d
    return x + 0.5 * _rms(acc, post)


def _ffn_kernel(n_prompt_steps, xp_ref, xs_ref, pre_ref, post_ref, wg_hbm, wu_hbm, wd_hbm,
                yp_ref, ys_ref, wg_ref, wu_ref, wd_ref):
    i = pl.program_id(0)

    @pl.when(i == 0)
    def _():
        _stream_cast(wg_hbm, wg_ref)
        _stream_cast(wu_hbm, wu_ref)
        _stream_cast(wd_hbm, wd_ref)

    @pl.when(i < n_prompt_steps)
    def _():
        yp_ref[...] = _ffn_tile(xp_ref[...], pre_ref[...], post_ref[...], wg_ref, wu_ref, wd_ref)

    @pl.when(i == n_prompt_steps)
    def _():
        ys_ref[...] = _ffn_tile(xs_ref[...], pre_ref[...], post_ref[...], wg_ref, wu_ref, wd_ref)


def _ffn_call(xp, xs, pre, post, wg, wu, wd):
    n_p, n_s = xp.shape[0], xs.shape[0]
    rows = min(FFN_ROWS, n_p)
    steps = n_p // rows
    prompt_spec = pl.BlockSpec((rows, D_MODEL), lambda i: (jnp.minimum(i, steps - 1), 0))
    sample_spec = pl.BlockSpec((n_s, D_MODEL), lambda i: (0, 0))
    return pl.pallas_call(
        functools.partial(_ffn_kernel, steps),
        grid=(steps + 1,),
        in_specs=[prompt_spec, sample_spec,
                  _const_spec((1, D_MODEL)), _const_spec((1, D_MODEL)),
                  HBM_SPEC, HBM_SPEC, HBM_SPEC],
        out_specs=[prompt_spec, sample_spec],
        out_shape=[jax.ShapeDtypeStruct(xp.shape, F32), jax.ShapeDtypeStruct(xs.shape, F32)],
        scratch_shapes=[pltpu.VMEM(wg.shape, BF16), pltpu.VMEM(wu.shape, BF16),
                        pltpu.VMEM(wd.shape, BF16)],
        compiler_params=pltpu.CompilerParams(
            dimension_semantics=("arbitrary",), vmem_limit_bytes=VMEM_LIMIT),
        name="ffn_half",
    )(xp, xs, pre, post, wg, wu, wd)


def _forget_lower_bound(lbl):
    l0, l1 = lbl[0:1, :], lbl[1:2, :]
    m = jnp.maximum(l0, l1)
    e0, e1 = jnp.exp(l0 - m), jnp.exp(l1 - m)
    return e0 / (e0 + e1)


def _lane_is_low(shape):
    return lax.broadcasted_iota(jnp.int32, shape, len(shape) - 1) < HEAD_DIM


def _dup_kv(x):
    swapped = pltpu.roll(x, HEAD_DIM, axis=1)
    low = _lane_is_low(x.shape)
    return jnp.where(low, x, swapped), jnp.where(low, swapped, x)


def _merge_out(h, w_ref, off, gates_from):
    return jax.nn.sigmoid(gates_from(off, D_MODEL)) * _dot(h.astype(BF16), w_ref[...])


def _pair_levels(c):
    t = jnp.arange(c, dtype=jnp.int32)[:, None]
    s = jnp.arange(c, dtype=jnp.int32)[None, :]
    x = jnp.maximum(t ^ s, 1)
    lvl = (31 - lax.clz(x)).astype(jnp.int32)
    return jnp.where(t > s, lvl, -1)


LOG2E = 1.4426950408889634


def _head(a, hd):
    return a[:, hd * HG_D:(hd + 1) * HG_D]


HG_LEVELS = HG_CHUNK.bit_length() - 1


def _chunk_rows(ci):
    return slice(ci * HG_CHUNK, (ci + 1) * HG_CHUNK)


def _hgrn_log_decay(g_ref, b_ref, n_chunks):
    c = HG_CHUNK
    width = g_ref.shape[1]
    row = lax.broadcasted_iota(jnp.int32, (c, c), 0)
    col = lax.broadcasted_iota(jnp.int32, (c, c), 1)
    tril = (col <= row).astype(BF16)
    for ci in range(n_chunks):
        r = _chunk_rows(ci)
        g = g_ref[r, :]
        hi = g.astype(BF16)
        rest = g - hi.astype(F32)
        mid = rest.astype(BF16)
        lo = (rest - mid.astype(F32)).astype(BF16)
        sums = _dot(tril, jnp.concatenate([hi, mid, lo], axis=1))
        b_ref[r, :] = (sums[:, 0:width] + sums[:, width:2 * width] + sums[:, 2 * width:]) * LOG2E


def _hgrn_factor_jobs(q_ref, k_ref, b_ref, x_ref, qe_ref, ke_ref, n_chunks):
    c = HG_CHUNK
    t = lax.broadcasted_iota(jnp.int32, (c, 1), 0)
    last = {}

    def level_job(ci, lvl):
        def run():
            n = 1 << lvl
            r = _chunk_rows(ci)
            bc = b_ref[r, :]
            second = (t & n) != 0
            if n < SUBLANES:
                prev = last.get(ci, bc)
                tiles = prev.reshape(c // SUBLANES, SUBLANES, prev.shape[1])
                second3 = second.reshape(c // SUBLANES, SUBLANES, 1)
                edge = jnp.where(second3, pltpu.roll(tiles, n, axis=1), tiles).reshape(prev.shape)
                if 2 * n < SUBLANES:
                    last[ci] = jnp.where(second3, tiles, pltpu.roll(tiles, SUBLANES - n, axis=1)
                                         ).reshape(prev.shape)
            else:
                edge = jnp.concatenate(
                    [jnp.broadcast_to(bc[p + n - 1:p + n, :], (2 * n, bc.shape[1]))
                     for p in range(0, c, 2 * n)], axis=0)
            w = jnp.exp2(jnp.where(second, bc - edge, edge - bc))
            x_ref[lvl, r, :] = (jnp.where(second, q_ref[r, :], k_ref[r, :]) * w).astype(BF16)
        return run

    def state_job(ci):
        def run():
            r = _chunk_rows(ci)
            bc = b_ref[r, :]
            qe_ref[r, :] = (q_ref[r, :] * jnp.exp2(bc)).astype(BF16)
            ke_ref[r, :] = (k_ref[r, :] * jnp.exp2(bc[c - 1:c, :] - bc)).astype(BF16)
        return run

    jobs = [level_job(ci, lvl) for lvl in range(HG_LEVELS) for ci in range(n_chunks)]
    return jobs + [state_job(ci) for ci in range(n_chunks)]


def _hgrn_pair_jobs(x_ref, levels, n_chunks):
    c = HG_CHUNK
    heads, chunks = range(HG_HEADS), range(n_chunks)
    a = [[jnp.zeros((c, c), F32) for _ in heads] for _ in chunks]

    def job(lvl, ci):
        def run():
            x = x_ref[lvl, _chunk_rows(ci), :]
            for hd in heads:
                al = lax.dot_general(_head(x, hd), _head(x, hd), NT_DIMS,
                                     preferred_element_type=F32)
                a[ci][hd] = jnp.where(levels == lvl, al, a[ci][hd])
        return run
    return [job(lvl, ci) for lvl in range(HG_LEVELS) for ci in chunks], a


def _hgrn_finish(q_ref, k_ref, v_ref, b_ref, a, qe_ref, ke_ref, st, n_chunks):
    c = HG_CHUNK
    heads, chunks = range(HG_HEADS), range(n_chunks)
    local = []
    for ci in chunks:
        r = _chunk_rows(ci)
        q, k, v = q_ref[r, :], k_ref[r, :], v_ref[r, :]
        qk, vb = q * k, v.astype(BF16)
        outs = []
        for hd in heads:
            diag = jnp.sum(_head(qk, hd), axis=1, keepdims=True)
            outs.append(_dot(a[ci][hd].astype(BF16), _head(vb, hd)) + diag * _head(v, hd))
        local.append(outs)
        yield

    result = []
    for ci in chunks:
        r = _chunk_rows(ci)
        qe, ke, vb = qe_ref[r, :], ke_ref[r, :], v_ref[r, :].astype(BF16)
        decay = jnp.exp2(b_ref[ci * c + c - 1:ci * c + c, :])
        outs, new = [], []
        for hd in heads:
            outs.append(local[ci][hd] + lax.dot_general(_head(qe, hd), st[hd].astype(BF16), NT_DIMS,
                                                        preferred_element_type=F32))
            new.append(st[hd] * _head(decay, hd)
                       + lax.dot_general(_head(vb, hd), _head(ke, hd), TN_DIMS,
                                         preferred_element_type=F32))
        st = new
        result.append(outs)
        yield
    return result, st


def _mix_prompt_kernel(sinks_ref, x_ref, pre_ref, post_ref, lbl_ref, hgn_ref, lvl_ref,
                       win_hbm, wao_hbm, who_hbm, wo_hbm,
                       y_ref, nk_ref, nv_ref, ns_ref,
                       win_ref, wao_ref, who_ref, wo_ref,
                       kbuf, vbuf, st_ref, qh_ref, kh_ref, vh_ref, lf_ref, oa_ref, oh_ref, g_ref,
                       b_ref, xl_ref, qe_ref, ke_ref):
    j = pl.program_id(1)
    rows = x_ref.shape[0]
    n_blocks = rows // WINDOW

    @pl.when((pl.program_id(0) == 0) & (j == 0))
    def _():
        for src, dst in ((win_hbm, win_ref), (wao_hbm, wao_ref), (who_hbm, who_ref),
                         (wo_hbm, wo_ref)):
            _stream_cast(src, dst)

    @pl.when(j == 0)
    def _():
        kbuf[0:WINDOW, :] = jnp.zeros((WINDOW, LANES), F32)
        vbuf[0:WINDOW, :] = jnp.zeros((WINDOW, LANES), F32)
        st_ref[...] = jnp.zeros(st_ref.shape, F32)

    part_rows = [slice(r0, r0 + rows // NORM_PARTS) for r0 in range(0, rows, rows // NORM_PARTS)]
    h_parts, qkv_parts = [], []
    for r in part_rows:
        hp = _rms(x_ref[r, :], pre_ref[...]).astype(BF16)
        h_parts.append(hp)
        qkv_parts.append(_dot(hp, win_ref[:, OFF_QA:OFF_QH]))
    h = jnp.concatenate(h_parts, axis=0)
    qkv = jnp.concatenate(qkv_parts, axis=0)

    def proj(off, width):
        return _dot(h, win_ref[:, off:off + width])

    lb = _forget_lower_bound(lbl_ref[...])
    n_chunks = rows // HG_CHUNK

    def proj_slab(off):
        def run():
            z = proj(off, FFN_CHUNK)
            if off < OFF_FH:
                cols = slice(off - OFF_QH, off - OFF_QH + FFN_CHUNK)
                qh_ref[:, cols] = z * jax.nn.sigmoid(z)
            elif off < OFF_IH:
                cols = slice(off - OFF_FH, off - OFF_FH + FFN_CHUNK)
                lbs = lb[:, cols]
                lf_ref[:, cols] = jnp.log(lbs + (1.0 - lbs) * jax.nn.sigmoid(z))
                kh_ref[:, cols] = (1.0 - lbs) * jax.nn.sigmoid(-z)
            elif off < OFF_GH:
                cols = slice(off - OFF_IH, off - OFF_IH + FFN_CHUNK)
                vh_ref[:, cols] = z
            else:
                cols = slice(off - OFF_GH, off - OFF_GH + FFN_CHUNK)
                g_ref[:, cols] = z
        return run
    hgrn_slabs = [proj_slab(off) for off in range(OFF_QH, OFF_GH, FFN_CHUNK)]
    gate_slabs = [proj_slab(off) for off in range(OFF_GH, IN_WIDTH, FFN_CHUNK)]
    jobs = _hgrn_factor_jobs(qh_ref, kh_ref, b_ref, xl_ref, qe_ref, ke_ref, n_chunks)
    pair_jobs, in_chunk = _hgrn_pair_jobs(xl_ref, lvl_ref[...], n_chunks)

    qa = qkv[:, OFF_QA:OFF_KA] * (HEAD_DIM ** -0.5)
    ka = qkv[:, OFF_KA:OFF_VA]
    va = qkv[:, OFF_VA:OFF_QH]
    kbuf[WINDOW:WINDOW + rows, :] = ka
    vbuf[WINDOW:WINDOW + rows, :] = va

    qi = lax.broadcasted_iota(jnp.int32, (WINDOW, 2 * WINDOW), 0)
    kj = lax.broadcasted_iota(jnp.int32, (WINDOW, 2 * WINDOW), 1)
    band = (kj > qi) & (kj <= qi + WINDOW)
    low = _lane_is_low((WINDOW, LANES))
    top = lax.broadcasted_iota(jnp.int32, (2 * WINDOW, 1), 0) < WINDOW
    n_slabs = ATTN_Q_HEADS // 2
    group = ATTN_Q_HEADS // ATTN_KV_HEADS
    kds = [_dup_kv(kbuf[n * WINDOW:(n + 2) * WINDOW, :]) for n in range(n_blocks)]
    vds = [_dup_kv(vbuf[n * WINDOW:(n + 2) * WINDOW, :]) for n in range(n_blocks)]

    def scores(n, slab):
        qs = qa[n * WINDOW:(n + 1) * WINDOW, slab * LANES:(slab + 1) * LANES]
        q2 = jnp.concatenate([jnp.where(low, qs, 0.0), jnp.where(low, 0.0, qs)], axis=0)
        return lax.dot_general(q2.astype(BF16), kds[n][(2 * slab) // group].astype(BF16), NT_DIMS,
                               preferred_element_type=F32)

    def attend(n, slab, s):
        valid = band & ((j > 0) | (kj >= WINDOW)) if n == 0 else band
        s = jnp.where(jnp.concatenate([valid, valid], axis=0), s, -jnp.inf)
        sk = jnp.where(top, sinks_ref[0, 2 * slab], sinks_ref[0, 2 * slab + 1])
        m = jnp.maximum(jnp.max(s, axis=1, keepdims=True), sk)
        p = jnp.exp(s - m)
        den = jnp.sum(p, axis=1, keepdims=True) + jnp.exp(sk - m)
        o2 = _dot(p.astype(BF16), vds[n][(2 * slab) // group].astype(BF16)) / den
        oa_ref[n * WINDOW:(n + 1) * WINDOW, slab * LANES:(slab + 1) * LANES] = (
            jnp.where(low, o2[0:WINDOW], o2[WINDOW:2 * WINDOW]))

    order = [(n, slab) for n in range(n_blocks) for slab in range(n_slabs)]
    early = len(hgrn_slabs)
    late_gates = gate_slabs[LOOP_GATE_SLABS:]
    gate_slabs = gate_slabs[:LOOP_GATE_SLABS]
    pending = scores(*order[0])
    ready_pairs = 0
    for idx, (n, slab) in enumerate(order):
        nxt = scores(*order[idx + 1]) if idx + 1 < len(order) else None
        if idx < early:
            hgrn_slabs[idx]()
        else:
            if idx == early:
                _hgrn_log_decay(lf_ref, b_ref, n_chunks)
            if gate_slabs:
                gate_slabs.pop(0)()
            for _ in range(min(len(pair_jobs), ready_pairs)):
                pair_jobs.pop(0)()
            ready_pairs = 0
            for _ in range(-(-len(jobs) // (len(order) - 1 - idx)) if idx + 1 < len(order) else 0):
                if jobs:
                    jobs.pop(0)()
                    ready_pairs += 1
        attend(n, slab, pending)
        pending = nxt
    kbuf[0:WINDOW, :] = kbuf[rows:rows + WINDOW, :]
    vbuf[0:WINDOW, :] = vbuf[rows:rows + WINDOW, :]
    assert not jobs and not hgrn_slabs[early:]
    late = gate_slabs + late_gates
    for run in pair_jobs:
        run()

    nk_ref[0] = ka[rows - WINDOW:rows, :]
    nv_ref[0] = va[rows - WINDOW:rows, :]

    hgn = hgn_ref[...]
    st0 = [st_ref[hd * HG_D:(hd + 1) * HG_D, :] for hd in range(HG_HEADS)]
    finish = _hgrn_finish(qh_ref, kh_ref, vh_ref, b_ref, in_chunk, qe_ref, ke_ref, st0, n_chunks)
    while True:
        try:
            next(finish)
        except StopIteration as done:
            outs, states = done.value
            break
        if late:
            late.pop(0)()
    for run in late:
        run()
    for hd in range(HG_HEADS):
        cs = slice(hd * HG_D, (hd + 1) * HG_D)
        st_ref[cs, :] = states[hd]
        ns_ref[0, cs, :] = states[hd].T
        for ci in range(n_chunks):
            oh_ref[ci * HG_CHUNK:(ci + 1) * HG_CHUNK, cs] = _rms(outs[ci][hd], hgn)

    gh = g_ref[:, 0:HG_W]
    oh = oh_ref[...] * (gh * jax.nn.sigmoid(gh))

    def gate(off, width):
        return g_ref[:, off - OFF_GH:off - OFF_GH + width]
    m = (_merge_out(oa_ref[...], wao_ref, OFF_GA, gate)
         + _merge_out(oh, who_ref, OFF_GB, gate))
    mb = m.astype(BF16)
    for r in part_rows:
        y_ref[r, :] = x_ref[r, :] + _rms(_dot(mb[r, :], wo_ref[...]), post_ref[...])


def _mix_prompt_call(x, batch, sinks, pre, post, lbl, hgn, win, wao, who, wo):
    seq = x.shape[0] // batch
    rows = min(MIX_ROWS, seq)
    steps = seq // rows
    x_spec = pl.BlockSpec((rows, D_MODEL), lambda b, j: (b * steps + j, 0))
    kv_spec = pl.BlockSpec((1, WINDOW, ATTN_KV_W), lambda b, j: (b, 0, 0))
    st_spec = pl.BlockSpec((1, HG_W, HG_D), lambda b, j: (b, 0, 0))
    act = pltpu.VMEM((rows, HG_W), F32)
    return pl.pallas_call(
        _mix_prompt_kernel,
        grid=(batch, steps),
        in_specs=[pl.BlockSpec(memory_space=pltpu.SMEM), x_spec,
                  _const_spec((1, D_MODEL)), _const_spec((1, D_MODEL)),
                  _const_spec(lbl.shape), _const_spec((1, HG_D)),
                  _const_spec((HG_CHUNK, HG_CHUNK)),
                  HBM_SPEC, HBM_SPEC, HBM_SPEC, HBM_SPEC],
        out_specs=[x_spec, kv_spec, kv_spec, st_spec],
        out_shape=[jax.ShapeDtypeStruct(x.shape, F32),
                   jax.ShapeDtypeStruct((batch, WINDOW, ATTN_KV_W), F32),
                   jax.ShapeDtypeStruct((batch, WINDOW, ATTN_KV_W), F32),
                   jax.ShapeDtypeStruct((batch, HG_W, HG_D), F32)],
        scratch_shapes=[pltpu.VMEM(win.shape, BF16), pltpu.VMEM(wao.shape, BF16),
                        pltpu.VMEM(who.shape, BF16), pltpu.VMEM(wo.shape, BF16),
                        pltpu.VMEM((rows + WINDOW, ATTN_KV_W), F32),
                        pltpu.VMEM((rows + WINDOW, ATTN_KV_W), F32),
                        pltpu.VMEM((HG_W, HG_D), F32),
                        act, act, act, act,
                        pltpu.VMEM((rows, ATTN_Q_W), F32), act,
                        pltpu.VMEM((rows, IN_WIDTH - OFF_GH), F32),
                        act, pltpu.VMEM((HG_LEVELS, rows, HG_W), BF16),
                        pltpu.VMEM((rows, HG_W), BF16), pltpu.VMEM((rows, HG_W), BF16)],
        compiler_params=pltpu.CompilerParams(
            dimension_semantics=("arbitrary", "arbitrary"), vmem_limit_bytes=VMEM_LIMIT),
        name="mix_prompt",
    )(sinks, x, pre, post, lbl, hgn, _pair_levels(HG_CHUNK), win, wao, who, wo)


def _row_select(rows_list):
    n = -(-len(rows_list) // 8) * 8
    lanes = rows_list[0].shape[1]
    ridx = lax.broadcasted_iota(jnp.int32, (n, lanes), 0)
    out = jnp.zeros((n, lanes), F32)
    for i, r in enumerate(rows_list):
        out = jnp.where(ridx == i, jnp.broadcast_to(r, (n, lanes)), out)
    return out


def _mix_sample_kernel(sinks_ref, x_ref, ck_ref, cv_ref, s0_ref, pre_ref, post_ref, lbl_ref, hgn_ref,
                       win_hbm, wao_hbm, who_hbm, wo_hbm,
                       y_ref, nk_ref, nv_ref, ns_ref,
                       win_ref, wao_ref, who_ref, wo_ref,
                       h_ref, qa_ref, ka_ref, va_ref, qh_ref, kh_ref, fh_ref, vh_ref, oa_ref, oh_ref):
    i = pl.program_id(0)
    tile = ck_ref.shape[0]

    def proj(off, width):
        return _dot(h_ref[...], win_ref[:, off:off + width])

    @pl.when(i == 0)
    def _():
        for src, dst in ((win_hbm, win_ref), (wao_hbm, wao_ref), (who_hbm, who_ref),
                         (wo_hbm, wo_ref)):
            _stream_cast(src, dst)
        h_ref[...] = _rms(x_ref[...], pre_ref[...]).astype(BF16)
        qa_ref[...] = proj(OFF_QA, ATTN_Q_W) * (HEAD_DIM ** -0.5)
        ka_ref[...] = proj(OFF_KA, ATTN_KV_W)
        va_ref[...] = proj(OFF_VA, ATTN_KV_W)
        lb = _forget_lower_bound(lbl_ref[...])
        qh = proj(OFF_QH, HG_W)
        qh_ref[...] = qh * jax.nn.sigmoid(qh)
        fp = proj(OFF_FH, HG_W)
        fh_ref[...] = jnp.exp(jnp.log(lb + (1.0 - lb) * jax.nn.sigmoid(fp)))
        kh_ref[...] = (1.0 - lb) * jax.nn.sigmoid(-fp)
        vh_ref[...] = proj(OFF_IH, HG_W)

    low = _lane_is_low((1, LANES))
    pos = lax.broadcasted_iota(jnp.int32, (WINDOW, LANES), 0)
    hrow = lax.broadcasted_iota(jnp.int32, (ATTN_Q_HEADS, 1), 0)
    sk = jnp.zeros((ATTN_Q_HEADS, 1), F32)
    for hd in range(ATTN_Q_HEADS):
        sk = jnp.where(hrow == hd, sinks_ref[0, hd], sk)
    group = ATTN_Q_HEADS // ATTN_KV_HEADS
    hgn = hgn_ref[...]

    r8 = pl.ds(pl.multiple_of(i * tile, tile), tile)
    qa8, ka8, va8 = qa_ref[r8, :], ka_ref[r8, :], va_ref[r8, :]
    fh8, kh8, qh8, vh8 = fh_ref[r8, :], kh_ref[r8, :], qh_ref[r8, :], vh_ref[r8, :]
    oa_rows = [[] for _ in range(ATTN_Q_HEADS // 2)]
    oh_rows = [[] for _ in range(HG_HEADS)]
    seqs = range(tile)
    kws, vws, q8s = [], [], []
    for bi in seqs:
        r1 = slice(bi, bi + 1)
        kw = jnp.where(pos == WINDOW - 1, ka8[r1, :], pltpu.roll(ck_ref[bi], WINDOW - 1, axis=0))
        vw = jnp.where(pos == WINDOW - 1, va8[r1, :], pltpu.roll(cv_ref[bi], WINDOW - 1, axis=0))
        nk_ref[bi] = kw
        nv_ref[bi] = vw
        kws.append(kw.astype(BF16))
        vws.append(vw.astype(BF16))
        qrows = []
        for hd in range(ATTN_Q_HEADS):
            slab = qa8[r1, (hd // 2) * LANES:(hd // 2 + 1) * LANES]
            in_place = (hd % 2) == (hd // group)
            src = slab if in_place else pltpu.roll(slab, HEAD_DIM, axis=1)
            on_kv_lanes = low if hd // group == 0 else jnp.logical_not(low)
            qrows.append(jnp.where(on_kv_lanes, src, 0.0))
        q8s.append(_row_select(qrows).astype(BF16))
    scores = [lax.dot_general(q8s[bi], kws[bi], NT_DIMS, preferred_element_type=F32) for bi in seqs]
    probs = []
    for s in scores:
        m = jnp.maximum(jnp.max(s, axis=1, keepdims=True), sk)
        p = jnp.exp(s - m)
        probs.append((p / (jnp.sum(p, axis=1, keepdims=True) + jnp.exp(sk - m))).astype(BF16))
    for bi in seqs:
        o8 = _dot(probs[bi], vws[bi])
        for slab in range(ATTN_Q_HEADS // 2):
            kv = (2 * slab) // group
            even, odd = o8[2 * slab:2 * slab + 1, :], o8[2 * slab + 1:2 * slab + 2, :]
            if kv == 0:
                out = jnp.where(low, even, pltpu.roll(odd, HEAD_DIM, axis=1))
            else:
                out = jnp.where(low, pltpu.roll(even, HEAD_DIM, axis=1), odd)
            oa_rows[slab].append(out)

    def columns(a8, hd):
        pad = jnp.zeros((HG_D - tile, HG_D), F32)
        return jnp.concatenate([_head(a8, hd), pad], axis=0).T
    seq_row = lax.broadcasted_iota(jnp.int32, (HG_D, HG_D), 0)
    for hd in range(HG_HEADS):
        rs = slice(hd * HG_D, (hd + 1) * HG_D)
        f_cols = columns(fh8, hd)
        k_cols = columns(kh8, hd).astype(BF16)
        v_rows = jnp.concatenate([_head(vh8, hd), jnp.zeros((HG_D - tile, HG_D), F32)], axis=0)
        q_rows = _head(qh8, hd).astype(BF16)
        outers = [_dot(k_cols, jnp.where(seq_row == bi, v_rows, 0.0).astype(BF16))
                  for bi in seqs]
        for bi in seqs:
            f_col = jnp.broadcast_to(f_cols[:, bi:bi + 1], (HG_D, HG_D))
            ns_ref[bi, rs, :] = f_col * s0_ref[bi, rs, :] + outers[bi]
        outs = [_dot(q_rows, ns_ref[bi, rs, :].astype(BF16)) for bi in seqs]
        for bi in seqs:
            oh_rows[hd].append(_rms(outs[bi][bi:bi + 1, :], hgn))
    for slab in range(ATTN_Q_HEADS // 2):
        oa_ref[r8, slab * LANES:(slab + 1) * LANES] = _row_select(oa_rows[slab])
    for hd in range(HG_HEADS):
        oh_ref[r8, hd * HG_D:(hd + 1) * HG_D] = _row_select(oh_rows[hd])

    @pl.when(i == pl.num_programs(0) - 1)
    def _():
        gh = proj(OFF_GH, HG_W)
        oh = oh_ref[...] * (gh * jax.nn.sigmoid(gh))
        m = (_merge_out(oa_ref[...], wao_ref, OFF_GA, proj)
             + _merge_out(oh, who_ref, OFF_GB, proj))
        y_ref[...] = x_ref[...] + _rms(_dot(m.astype(BF16), wo_ref[...]), post_ref[...])


def _mix_sample_call(x, ck, cv, s0, sinks, pre, post, lbl, hgn, win, wao, who, wo):
    n = x.shape[0]
    tile = min(SAMPLE_TILE, n)
    full = pl.BlockSpec((n, D_MODEL), lambda i: (0, 0))
    kv_spec = pl.BlockSpec((tile, WINDOW, ATTN_KV_W), lambda i: (i, 0, 0))
    st_spec = pl.BlockSpec((tile, HG_W, HG_D), lambda i: (i, 0, 0))
    act = pltpu.VMEM((n, HG_W), F32)
    kv_act = pltpu.VMEM((n, ATTN_KV_W), F32)
    return pl.pallas_call(
        _mix_sample_kernel,
        grid=(n // tile,),
        in_specs=[pl.BlockSpec(memory_space=pltpu.SMEM), full, kv_spec, kv_spec, st_spec,
                  _const_spec((1, D_MODEL)), _const_spec((1, D_MODEL)),
                  _const_spec(lbl.shape), _const_spec((1, HG_D)),
                  HBM_SPEC, HBM_SPEC, HBM_SPEC, HBM_SPEC],
        out_specs=[full, kv_spec, kv_spec, st_spec],
        out_shape=[jax.ShapeDtypeStruct(x.shape, F32),
                   jax.ShapeDtypeStruct(ck.shape, F32),
                   jax.ShapeDtypeStruct(cv.shape, F32),
                   jax.ShapeDtypeStruct(s0.shape, F32)],
        scratch_shapes=[pltpu.VMEM(win.shape, BF16), pltpu.VMEM(wao.shape, BF16),
                        pltpu.VMEM(who.shape, BF16), pltpu.VMEM(wo.shape, BF16),
                        pltpu.VMEM((n, D_MODEL), BF16),
                        pltpu.VMEM((n, ATTN_Q_W), F32), kv_act, kv_act,
                        act, act, act, act,
                        pltpu.VMEM((n, ATTN_Q_W), F32), act],
        compiler_params=pltpu.CompilerParams(
            dimension_semantics=("arbitrary",), vmem_limit_bytes=VMEM_LIMIT),
        name="mix_sample",
    )(sinks, x, ck, cv, s0, pre, post, lbl, hgn, win, wao, who, wo)


def kernel(x_prompt, x_sample, cache_k, cache_v, state_hgrn, norm_ffn1_pre, norm_ffn1_post, w_ffn1_gate, w_ffn1_up, w_ffn1_down, norm_mix_pre, norm_mix_post, w_in, attn_sinks, hgrn_lb_logits, hgrn_norm, w_attn_out, w_hgrn_out, w_out, norm_ffn2_pre, norm_ffn2_post, w_ffn2_gate, w_ffn2_up, w_ffn2_down):
    depth = w_in.shape[0]
    assert depth == 1 and hgrn_lb_logits.shape[0] == 2, "single-layer stack only"
    batch, seq, _ = x_prompt.shape
    n_s = x_sample.shape[0]
    assert x_sample.shape[1] == 1 and seq % WINDOW == 0

    xp = x_prompt.reshape(batch * seq, D_MODEL)
    xs = x_sample.reshape(n_s, D_MODEL)
    ck = cache_k[0].reshape(n_s, WINDOW, ATTN_KV_W)
    cv = cache_v[0].reshape(n_s, WINDOW, ATTN_KV_W)
    s0 = state_hgrn[0].reshape(n_s, HG_W, HG_D)

    xp, xs = _ffn_call(xp, xs, norm_ffn1_pre, norm_ffn1_post,
                       w_ffn1_gate[0], w_ffn1_up[0], w_ffn1_down[0])

    mix_w = (attn_sinks, norm_mix_pre, norm_mix_post, hgrn_lb_logits, hgrn_norm,
             w_in[0], w_attn_out[0], w_hgrn_out[0], w_out[0])
    xp, nkp, nvp, nsp = _mix_prompt_call(xp, batch, *mix_w)
    xs, nks, nvs, nss = _mix_sample_call(xs, ck, cv, s0, *mix_w)

    xp, xs = _ffn_call(xp, xs, norm_ffn2_pre, norm_ffn2_post,
                       w_ffn2_gate[0], w_ffn2_up[0], w_ffn2_down[0])

    kv_shape = (1, -1, WINDOW, ATTN_KV_HEADS, HEAD_DIM)
    st_shape = (1, -1, HG_HEADS, HG_D, HG_D)
    return (xp.reshape(batch, seq, D_MODEL), xs.reshape(n_s, 1, D_MODEL),
            nkp.reshape(kv_shape), nvp.reshape(kv_shape), nsp.reshape(st_shape),
            nks.reshape(kv_shape), nvs.reshape(kv_shape), nss.reshape(st_shape))
```

```python
import functools

import jax
import jax.numpy as jnp
from jax import lax
from jax.experimental import pallas as pl
from jax.experimental.pallas import tpu as pltpu

F32 = jnp.float32
BF16 = jnp.bfloat16

D_MODEL = 1024
FFN_DIM = 2816
HEAD_DIM = 64
ATTN_Q_HEADS = 8
ATTN_KV_HEADS = 2
WINDOW = 128
HG_HEADS = 4
HG_D = 128
EPS = 1e-6

ATTN_Q_W = ATTN_Q_HEADS * HEAD_DIM
ATTN_KV_W = ATTN_KV_HEADS * HEAD_DIM
HG_W = HG_HEADS * HG_D
OFF_QA = 0
OFF_KA = OFF_QA + ATTN_Q_W
OFF_VA = OFF_KA + ATTN_KV_W
OFF_QH = OFF_VA + ATTN_KV_W
OFF_FH = OFF_QH + HG_W
OFF_IH = OFF_FH + HG_W
OFF_GH = OFF_IH + HG_W
OFF_GA = OFF_GH + HG_W
OFF_GB = OFF_GA + D_MODEL
IN_WIDTH = OFF_GB + D_MODEL

LANES = 128
SUBLANES = 8
FFN_CHUNK = 256
N_FFN_CHUNKS = FFN_DIM // FFN_CHUNK
FFN_ROWS = 1024
MIX_ROWS = 512
NORM_PARTS = 2
LOOP_GATE_SLABS = 10
HG_CHUNK = 128
SAMPLE_TILE = 8
VMEM_LIMIT = 56 * 1024 * 1024

NT_DIMS = (((1,), (1,)), ((), ()))
TN_DIMS = (((0,), (0,)), ((), ()))


def _rms(x, g):
    return x * lax.rsqrt(jnp.mean(x * x, axis=-1, keepdims=True) + EPS) * g


def _dot(a, b):
    return jnp.dot(a, b, preferred_element_type=F32)


STREAM_SLOTS = 4
STREAM_CHUNK_BYTES = 2 << 20


def _stream_cast(src_hbm, dst_ref):
    n_rows, n_cols = src_hbm.shape
    row_bytes = n_cols * jnp.dtype(F32).itemsize
    rc = max(SUBLANES, min(n_rows, STREAM_CHUNK_BYTES // row_bytes // SUBLANES * SUBLANES))
    while n_rows % rc:
        rc -= SUBLANES
    n = n_rows // rc
    slots = min(STREAM_SLOTS, n)

    def body(stage, sem):
        def copy(c):
            return pltpu.make_async_copy(src_hbm.at[pl.ds(c * rc, rc), :], stage.at[c % slots],
                                         sem.at[c % slots])
        for c in range(slots):
            copy(c).start()
        for c in range(n):
            copy(c).wait()
            dst_ref[c * rc:(c + 1) * rc, :] = stage[c % slots].astype(BF16)
            if c + slots < n:
                copy(c + slots).start()

    pl.run_scoped(body, pltpu.VMEM((slots, rc, n_cols), F32), pltpu.SemaphoreType.DMA((slots,)))


HBM_SPEC = pl.BlockSpec(memory_space=pl.ANY)


def _const_spec(shape):
    zeros = (0,) * len(shape)
    return pl.BlockSpec(shape, lambda *_: zeros, pipeline_mode=pl.Buffered(1))


def _ffn_tile(x, pre, post, wg_ref, wu_ref, wd_ref):
    h = _rms(x, pre).astype(BF16)
    acc = None
    for c in range(N_FFN_CHUNKS):
        cols = slice(c * FFN_CHUNK, (c + 1) * FFN_CHUNK)
        g = _dot(h, wg_ref[:, cols])
        u = _dot(h, wu_ref[:, cols])
        a = (g * jax.nn.sigmoid(g) * u).astype(BF16)
        d = _dot(a, wd_ref[cols, :])
        acc = d if acc is None else acc + d
    return x + 0.5 * _rms(acc, post)


def _ffn_kernel(n_prompt_steps, xp_ref, xs_ref, pre_ref, post_ref, wg_hbm, wu_hbm, wd_hbm,
                yp_ref, ys_ref, wg_ref, wu_ref, wd_ref):
    i = pl.program_id(0)

    @pl.when(i == 0)
    def _():
        _stream_cast(wg_hbm, wg_ref)
        _stream_cast(wu_hbm, wu_ref)
        _stream_cast(wd_hbm, wd_ref)

    @pl.when(i < n_prompt_steps)
    def _():
        yp_ref[...] = _ffn_tile(xp_ref[...], pre_ref[...], post_ref[...], wg_ref, wu_ref, wd_ref)

    @pl.when(i == n_prompt_steps)
    def _():
        ys_ref[...] = _ffn_tile(xs_ref[...], pre_ref[...], post_ref[...], wg_ref, wu_ref, wd_ref)


def _ffn_call(xp, xs, pre, post, wg, wu, wd):
    n_p, n_s = xp.shape[0], xs.shape[0]
    rows = min(FFN_ROWS, n_p)
    steps = n_p // rows
    prompt_spec = pl.BlockSpec((rows, D_MODEL), lambda i: (jnp.minimum(i, steps - 1), 0))
    sample_spec = pl.BlockSpec((n_s, D_MODEL), lambda i: (0, 0))
    return pl.pallas_call(
        functools.partial(_ffn_kernel, steps),
        grid=(steps + 1,),
        in_specs=[prompt_spec, sample_spec,
                  _const_spec((1, D_MODEL)), _const_spec((1, D_MODEL)),
                  HBM_SPEC, HBM_SPEC, HBM_SPEC],
        out_specs=[prompt_spec, sample_spec],
        out_shape=[jax.ShapeDtypeStruct(xp.shape, F32), jax.ShapeDtypeStruct(xs.shape, F32)],
        scratch_shapes=[pltpu.VMEM(wg.shape, BF16), pltpu.VMEM(wu.shape, BF16),
                        pltpu.VMEM(wd.shape, BF16)],
        compiler_params=pltpu.CompilerParams(
            dimension_semantics=("arbitrary",), vmem_limit_bytes=VMEM_LIMIT),
        name="ffn_half",
    )(xp, xs, pre, post, wg, wu, wd)


def _forget_lower_bound(lbl):
    l0, l1 = lbl[0:1, :], lbl[1:2, :]
    m = jnp.maximum(l0, l1)
    e0, e1 = jnp.exp(l0 - m), jnp.exp(l1 - m)
    return e0 / (e0 + e1)


def _lane_is_low(shape):
    return lax.broadcasted_iota(jnp.int32, shape, len(shape) - 1) < HEAD_DIM


def _dup_kv(x):
    swapped = pltpu.roll(x, HEAD_DIM, axis=1)
    low = _lane_is_low(x.shape)
    return jnp.where(low, x, swapped), jnp.where(low, swapped, x)


def _merge_out(h, w_ref, off, gates_from):
    return jax.nn.sigmoid(gates_from(off, D_MODEL)) * _dot(h.astype(BF16), w_ref[...])


def _pair_levels(c):
    t = jnp.arange(c, dtype=jnp.int32)[:, None]
    s = jnp.arange(c, dtype=jnp.int32)[None, :]
    x = jnp.maximum(t ^ s, 1)
    lvl = (31 - lax.clz(x)).astype(jnp.int32)
    return jnp.where(t > s, lvl, -1)


LOG2E = 1.4426950408889634


def _head(a, hd):
    return a[:, hd * HG_D:(hd + 1) * HG_D]


HG_LEVELS = HG_CHUNK.bit_length() - 1


def _chunk_rows(ci):
    return slice(ci * HG_CHUNK, (ci + 1) * HG_CHUNK)


def _hgrn_log_decay(g_ref, b_ref, n_chunks):
    c = HG_CHUNK
    width = g_ref.shape[1]
    row = lax.broadcasted_iota(jnp.int32, (c, c), 0)
    col = lax.broadcasted_iota(jnp.int32, (c, c), 1)
    tril = (col <= row).astype(BF16)
    for ci in range(n_chunks):
        r = _chunk_rows(ci)
        g = g_ref[r, :]
        hi = g.astype(BF16)
        rest = g - hi.astype(F32)
        mid = rest.astype(BF16)
        lo = (rest - mid.astype(F32)).astype(BF16)
        sums = _dot(tril, jnp.concatenate([hi, mid, lo], axis=1))
        b_ref[r, :] = (sums[:, 0:width] + sums[:, width:2 * width] + sums[:, 2 * width:]) * LOG2E


def _hgrn_factor_jobs(q_ref, k_ref, b_ref, x_ref, qe_ref, ke_ref, n_chunks):
    c = HG_CHUNK
    t = lax.broadcasted_iota(jnp.int32, (c, 1), 0)
    last = {}

    def level_job(ci, lvl):
        def run():
            n = 1 << lvl
            r = _chunk_rows(ci)
            bc = b_ref[r, :]
            second = (t & n) != 0
            if n < SUBLANES:
                prev = last.get(ci, bc)
                tiles = prev.reshape(c // SUBLANES, SUBLANES, prev.shape[1])
                second3 = second.reshape(c // SUBLANES, SUBLANES, 1)
                edge = jnp.where(second3, pltpu.roll(tiles, n, axis=1), tiles).reshape(prev.shape)
                if 2 * n < SUBLANES:
                    last[ci] = jnp.where(second3, tiles, pltpu.roll(tiles, SUBLANES - n, axis=1)
                                         ).reshape(prev.shape)
            else:
                edge = jnp.concatenate(
                    [jnp.broadcast_to(bc[p + n - 1:p + n, :], (2 * n, bc.shape[1]))
                     for p in range(0, c, 2 * n)], axis=0)
            w = jnp.exp2(jnp.where(second, bc - edge, edge - bc))
            x_ref[lvl, r, :] = (jnp.where(second, q_ref[r, :], k_ref[r, :]) * w).astype(BF16)
        return run

    def state_job(ci):
        def run():
            r = _chunk_rows(ci)
            bc = b_ref[r, :]
            qe_ref[r, :] = (q_ref[r, :] * jnp.exp2(bc)).astype(BF16)
            ke_ref[r, :] = (k_ref[r, :] * jnp.exp2(bc[c - 1:c, :] - bc)).astype(BF16)
        return run

    jobs = [level_job(ci, lvl) for lvl in range(HG_LEVELS) for ci in range(n_chunks)]
    return jobs + [state_job(ci) for ci in range(n_chunks)]


def _hgrn_pair_jobs(x_ref, levels, n_chunks):
    c = HG_CHUNK
    heads, chunks = range(HG_HEADS), range(n_chunks)
    a = [[jnp.zeros((c, c), F32) for _ in heads] for _ in chunks]

    def job(lvl, ci):
        def run():
            x = x_ref[lvl, _chunk_rows(ci), :]
            for hd in heads:
                al = lax.dot_general(_head(x, hd), _head(x, hd), NT_DIMS,
                                     preferred_element_type=F32)
                a[ci][hd] = jnp.where(levels == lvl, al, a[ci][hd])
        return run
    return [job(lvl, ci) for lvl in range(HG_LEVELS) for ci in chunks], a


def _hgrn_finish(q_ref, k_ref, v_ref, b_ref, a, qe_ref, ke_ref, st, n_chunks):
    c = HG_CHUNK
    heads, chunks = range(HG_HEADS), range(n_chunks)
    local = []
    for ci in chunks:
        r = _chunk_rows(ci)
        q, k, v = q_ref[r, :], k_ref[r, :], v_ref[r, :]
        qk, vb = q * k, v.astype(BF16)
        outs = []
        for hd in heads:
            diag = jnp.sum(_head(qk, hd), axis=1, keepdims=True)
            outs.append(_dot(a[ci][hd].astype(BF16), _head(vb, hd)) + diag * _head(v, hd))
        local.append(outs)
        yield

    result = []
    for ci in chunks:
        r = _chunk_rows(ci)
        qe, ke, vb = qe_ref[r, :], ke_ref[r, :], v_ref[r, :].astype(BF16)
        decay = jnp.exp2(b_ref[ci * c + c - 1:ci * c + c, :])
        outs, new = [], []
        for hd in heads:
            outs.append(local[ci][hd] + lax.dot_general(_head(qe, hd), st[hd].astype(BF16), NT_DIMS,
                                                        preferred_element_type=F32))
            new.append(st[hd] * _head(decay, hd)
                       + lax.dot_general(_head(vb, hd), _head(ke, hd), TN_DIMS,
                                         preferred_element_type=F32))
        st = new
        result.append(outs)
        yield
    return result, st


def _mix_prompt_kernel(sinks_ref, x_ref, pre_ref, post_ref, lbl_ref, hgn_ref, lvl_ref,
                       win_hbm, wao_hbm, who_hbm, wo_hbm,
                       y_ref, nk_ref, nv_ref, ns_ref,
                       win_ref, wao_ref, who_ref, wo_ref,
                       kbuf, vbuf, st_ref, qh_ref, kh_ref, vh_ref, lf_ref, oa_ref, oh_ref, g_ref,
                       b_ref, xl_ref, qe_ref, ke_ref):
    j = pl.program_id(1)
    rows = x_ref.shape[0]
    n_blocks = rows // WINDOW

    @pl.when((pl.program_id(0) == 0) & (j == 0))
    def _():
        for src, dst in ((win_hbm, win_ref), (wao_hbm, wao_ref), (who_hbm, who_ref),
                         (wo_hbm, wo_ref)):
            _stream_cast(src, dst)

    @pl.when(j == 0)
    def _():
        kbuf[0:WINDOW, :] = jnp.zeros((WINDOW, LANES), F32)
        vbuf[0:WINDOW, :] = jnp.zeros((WINDOW, LANES), F32)
        st_ref[...] = jnp.zeros(st_ref.shape, F32)

    part_rows = [slice(r0, r0 + rows // NORM_PARTS) for r0 in range(0, rows, rows // NORM_PARTS)]
    h_parts, qkv_parts = [], []
    for r in part_rows:
        hp = _rms(x_ref[r, :], pre_ref[...]).astype(BF16)
        h_parts.append(hp)
        qkv_parts.append(_dot(hp, win_ref[:, OFF_QA:OFF_QH]))
    h = jnp.concatenate(h_parts, axis=0)
    qkv = jnp.concatenate(qkv_parts, axis=0)

    def proj(off, width):
        return _dot(h, win_ref[:, off:off + width])

    lb = _forget_lower_bound(lbl_ref[...])
    n_chunks = rows // HG_CHUNK

    def proj_slab(off):
        def run():
            z = proj(off, FFN_CHUNK)
            if off < OFF_FH:
                cols = slice(off - OFF_QH, off - OFF_QH + FFN_CHUNK)
                qh_ref[:, cols] = z * jax.nn.sigmoid(z)
            elif off < OFF_IH:
                cols = slice(off - OFF_FH, off - OFF_FH + FFN_CHUNK)
                lbs = lb[:, cols]
                lf_ref[:, cols] = jnp.log(lbs + (1.0 - lbs) * jax.nn.sigmoid(z))
                kh_ref[:, cols] = (1.0 - lbs) * jax.nn.sigmoid(-z)
            elif off < OFF_GH:
                cols = slice(off - OFF_IH, off - OFF_IH + FFN_CHUNK)
                vh_ref[:, cols] = z
            else:
                cols = slice(off - OFF_GH, off - OFF_GH + FFN_CHUNK)
                g_ref[:, cols] = z
        return run
    hgrn_slabs = [proj_slab(off) for off in range(OFF_QH, OFF_GH, FFN_CHUNK)]
    gate_slabs = [proj_slab(off) for off in range(OFF_GH, IN_WIDTH, FFN_CHUNK)]
    jobs = _hgrn_factor_jobs(qh_ref, kh_ref, b_ref, xl_ref, qe_ref, ke_ref, n_chunks)
    pair_jobs, in_chunk = _hgrn_pair_jobs(xl_ref, lvl_ref[...], n_chunks)

    qa = qkv[:, OFF_QA:OFF_KA] * (HEAD_DIM ** -0.5)
    ka = qkv[:, OFF_KA:OFF_VA]
    va = qkv[:, OFF_VA:OFF_QH]
    kbuf[WINDOW:WINDOW + rows, :] = ka
    vbuf[WINDOW:WINDOW + rows, :] = va

    qi = lax.broadcasted_iota(jnp.int32, (WINDOW, 2 * WINDOW), 0)
    kj = lax.broadcasted_iota(jnp.int32, (WINDOW, 2 * WINDOW), 1)
    band = (kj > qi) & (kj <= qi + WINDOW)
    low = _lane_is_low((WINDOW, LANES))
    top = lax.broadcasted_iota(jnp.int32, (2 * WINDOW, 1), 0) < WINDOW
    n_slabs = ATTN_Q_HEADS // 2
    group = ATTN_Q_HEADS // ATTN_KV_HEADS
    kds = [_dup_kv(kbuf[n * WINDOW:(n + 2) * WINDOW, :]) for n in range(n_blocks)]
    vds = [_dup_kv(vbuf[n * WINDOW:(n + 2) * WINDOW, :]) for n in range(n_blocks)]

    def scores(n, slab):
        qs = qa[n * WINDOW:(n + 1) * WINDOW, slab * LANES:(slab + 1) * LANES]
        q2 = jnp.concatenate([jnp.where(low, qs, 0.0), jnp.where(low, 0.0, qs)], axis=0)
        return lax.dot_general(q2.astype(BF16), kds[n][(2 * slab) // group].astype(BF16), NT_DIMS,
                               preferred_element_type=F32)

    def attend(n, slab, s):
        valid = band & ((j > 0) | (kj >= WINDOW)) if n == 0 else band
        s = jnp.where(jnp.concatenate([valid, valid], axis=0), s, -jnp.inf)
        sk = jnp.where(top, sinks_ref[0, 2 * slab], sinks_ref[0, 2 * slab + 1])
        m = jnp.maximum(jnp.max(s, axis=1, keepdims=True), sk)
        p = jnp.exp(s - m)
        den = jnp.sum(p, axis=1, keepdims=True) + jnp.exp(sk - m)
        o2 = _dot(p.astype(BF16), vds[n][(2 * slab) // group].astype(BF16)) / den
        oa_ref[n * WINDOW:(n + 1) * WINDOW, slab * LANES:(slab + 1) * LANES] = (
            jnp.where(low, o2[0:WINDOW], o2[WINDOW:2 * WINDOW]))

    order = [(n, slab) for n in range(n_blocks) for slab in range(n_slabs)]
    early = len(hgrn_slabs)
    late_gates = gate_slabs[LOOP_GATE_SLABS:]
    gate_slabs = gate_slabs[:LOOP_GATE_SLABS]
    pending = scores(*order[0])
    ready_pairs = 0
    for idx, (n, slab) in enumerate(order):
        nxt = scores(*order[idx + 1]) if idx + 1 < len(order) else None
        if idx < early:
            hgrn_slabs[idx]()
        else:
            if idx == early:
                _hgrn_log_decay(lf_ref, b_ref, n_chunks)
            if gate_slabs:
                gate_slabs.pop(0)()
            for _ in range(min(len(pair_jobs), ready_pairs)):
                pair_jobs.pop(0)()
            ready_pairs = 0
            for _ in range(-(-len(jobs) // (len(order) - 1 - idx)) if idx + 1 < len(order) else 0):
                if jobs:
                    jobs.pop(0)()
                    ready_pairs += 1
        attend(n, slab, pending)
        pending = nxt
    kbuf[0:WINDOW, :] = kbuf[rows:rows + WINDOW, :]
    vbuf[0:WINDOW, :] = vbuf[rows:rows + WINDOW, :]
    assert not jobs and not hgrn_slabs[early:]
    late = gate_slabs + late_gates
    for run in pair_jobs:
        run()

    nk_ref[0] = ka[rows - WINDOW:rows, :]
    nv_ref[0] = va[rows - WINDOW:rows, :]

    hgn = hgn_ref[...]
    st0 = [st_ref[hd * HG_D:(hd + 1) * HG_D, :] for hd in range(HG_HEADS)]
    finish = _hgrn_finish(qh_ref, kh_ref, vh_ref, b_ref, in_chunk, qe_ref, ke_ref, st0, n_chunks)
    while True:
        try:
            next(finish)
        except StopIteration as done:
            outs, states = done.value
            break
        if late:
            late.pop(0)()
    for run in late:
        run()
    for hd in range(HG_HEADS):
        cs = slice(hd * HG_D, (hd + 1) * HG_D)
        st_ref[cs, :] = states[hd]
        ns_ref[0, cs, :] = states[hd].T
        for ci in range(n_chunks):
            oh_ref[ci * HG_CHUNK:(ci + 1) * HG_CHUNK, cs] = _rms(outs[ci][hd], hgn)

    gh = g_ref[:, 0:HG_W]
    oh = oh_ref[...] * (gh * jax.nn.sigmoid(gh))

    def gate(off, width):
        return g_ref[:, off - OFF_GH:off - OFF_GH + width]
    m = (_merge_out(oa_ref[...], wao_ref, OFF_GA, gate)
         + _merge_out(oh, who_ref, OFF_GB, gate))
    mb = m.astype(BF16)
    for r in part_rows:
        y_ref[r, :] = x_ref[r, :] + _rms(_dot(mb[r, :], wo_ref[...]), post_ref[...])


def _mix_prompt_call(x, batch, sinks, pre, post, lbl, hgn, win, wao, who, wo):
    seq = x.shape[0] // batch
    rows = min(MIX_ROWS, seq)
    steps = seq // rows
    x_spec = pl.BlockSpec((rows, D_MODEL), lambda b, j: (b * steps + j, 0))
    kv_spec = pl.BlockSpec((1, WINDOW, ATTN_KV_W), lambda b, j: (b, 0, 0))
    st_spec = pl.BlockSpec((1, HG_W, HG_D), lambda b, j: (b, 0, 0))
    act = pltpu.VMEM((rows, HG_W), F32)
    return pl.pallas_call(
        _mix_prompt_kernel,
        grid=(batch, steps),
        in_specs=[pl.BlockSpec(memory_space=pltpu.SMEM), x_spec,
                  _const_spec((1, D_MODEL)), _const_spec((1, D_MODEL)),
                  _const_spec(lbl.shape), _const_spec((1, HG_D)),
                  _const_spec((HG_CHUNK, HG_CHUNK)),
                  HBM_SPEC, HBM_SPEC, HBM_SPEC, HBM_SPEC],
        out_specs=[x_spec, kv_spec, kv_spec, st_spec],
        out_shape=[jax.ShapeDtypeStruct(x.shape, F32),
                   jax.ShapeDtypeStruct((batch, WINDOW, ATTN_KV_W), F32),
                   jax.ShapeDtypeStruct((batch, WINDOW, ATTN_KV_W), F32),
                   jax.ShapeDtypeStruct((batch, HG_W, HG_D), F32)],
        scratch_shapes=[pltpu.VMEM(win.shape, BF16), pltpu.VMEM(wao.shape, BF16),
                        pltpu.VMEM(who.shape, BF16), pltpu.VMEM(wo.shape, BF16),
                        pltpu.VMEM((rows + WINDOW, ATTN_KV_W), F32),
                        pltpu.VMEM((rows + WINDOW, ATTN_KV_W), F32),
                        pltpu.VMEM((HG_W, HG_D), F32),
                        act, act, act, act,
                        pltpu.VMEM((rows, ATTN_Q_W), F32), act,
                        pltpu.VMEM((rows, IN_WIDTH - OFF_GH), F32),
                        act, pltpu.VMEM((HG_LEVELS, rows, HG_W), BF16),
                        pltpu.VMEM((rows, HG_W), BF16), pltpu.VMEM((rows, HG_W), BF16)],
        compiler_params=pltpu.CompilerParams(
            dimension_semantics=("arbitrary", "arbitrary"), vmem_limit_bytes=VMEM_LIMIT),
        name="mix_prompt",
    )(sinks, x, pre, post, lbl, hgn, _pair_levels(HG_CHUNK), win, wao, who, wo)


def _row_select(rows_list):
    n = -(-len(rows_list) // 8) * 8
    lanes = rows_list[0].shape[1]
    ridx = lax.broadcasted_iota(jnp.int32, (n, lanes), 0)
    out = jnp.zeros((n, lanes), F32)
    for i, r in enumerate(rows_list):
        out = jnp.where(ridx == i, jnp.broadcast_to(r, (n, lanes)), out)
    return out


def _mix_sample_kernel(sinks_ref, x_ref, ck_ref, cv_ref, s0_ref, pre_ref, post_ref, lbl_ref, hgn_ref,
                       win_hbm, wao_hbm, who_hbm, wo_hbm,
                       y_ref, nk_ref, nv_ref, ns_ref,
                       win_ref, wao_ref, who_ref, wo_ref,
                       h_ref, qa_ref, ka_ref, va_ref, qh_ref, kh_ref, fh_ref, vh_ref, oa_ref, oh_ref):
    i = pl.program_id(0)
    tile = ck_ref.shape[0]

    def proj(off, width):
        return _dot(h_ref[...], win_ref[:, off:off + width])

    @pl.when(i == 0)
    def _():
        for src, dst in ((win_hbm, win_ref), (wao_hbm, wao_ref), (who_hbm, who_ref),
                         (wo_hbm, wo_ref)):
            _stream_cast(src, dst)
        h_ref[...] = _rms(x_ref[...], pre_ref[...]).astype(BF16)
        qa_ref[...] = proj(OFF_QA, ATTN_Q_W) * (HEAD_DIM ** -0.5)
        ka_ref[...] = proj(OFF_KA, ATTN_KV_W)
        va_ref[...] = proj(OFF_VA, ATTN_KV_W)
        lb = _forget_lower_bound(lbl_ref[...])
        qh = proj(OFF_QH, HG_W)
        qh_ref[...] = qh * jax.nn.sigmoid(qh)
        fp = proj(OFF_FH, HG_W)
        fh_ref[...] = jnp.exp(jnp.log(lb + (1.0 - lb) * jax.nn.sigmoid(fp)))
        kh_ref[...] = (1.0 - lb) * jax.nn.sigmoid(-fp)
        vh_ref[...] = proj(OFF_IH, HG_W)

    low = _lane_is_low((1, LANES))
    pos = lax.broadcasted_iota(jnp.int32, (WINDOW, LANES), 0)
    hrow = lax.broadcasted_iota(jnp.int32, (ATTN_Q_HEADS, 1), 0)
    sk = jnp.zeros((ATTN_Q_HEADS, 1), F32)
    for hd in range(ATTN_Q_HEADS):
        sk = jnp.where(hrow == hd, sinks_ref[0, hd], sk)
    group = ATTN_Q_HEADS // ATTN_KV_HEADS
    hgn = hgn_ref[...]

    r8 = pl.ds(pl.multiple_of(i * tile, tile), tile)
    qa8, ka8, va8 = qa_ref[r8, :], ka_ref[r8, :], va_ref[r8, :]
    fh8, kh8, qh8, vh8 = fh_ref[r8, :], kh_ref[r8, :], qh_ref[r8, :], vh_ref[r8, :]
    oa_rows = [[] for _ in range(ATTN_Q_HEADS // 2)]
    oh_rows = [[] for _ in range(HG_HEADS)]
    seqs = range(tile)
    kws, vws, q8s = [], [], []
    for bi in seqs:
        r1 = slice(bi, bi + 1)
        kw = jnp.where(pos == WINDOW - 1, ka8[r1, :], pltpu.roll(ck_ref[bi], WINDOW - 1, axis=0))
        vw = jnp.where(pos == WINDOW - 1, va8[r1, :], pltpu.roll(cv_ref[bi], WINDOW - 1, axis=0))
        nk_ref[bi] = kw
        nv_ref[bi] = vw
        kws.append(kw.astype(BF16))
        vws.append(vw.astype(BF16))
        qrows = []
        for hd in range(ATTN_Q_HEADS):
            slab = qa8[r1, (hd // 2) * LANES:(hd // 2 + 1) * LANES]
            in_place = (hd % 2) == (hd // group)
            src = slab if in_place else pltpu.roll(slab, HEAD_DIM, axis=1)
            on_kv_lanes = low if hd // group == 0 else jnp.logical_not(low)
            qrows.append(jnp.where(on_kv_lanes, src, 0.0))
        q8s.append(_row_select(qrows).astype(BF16))
    scores = [lax.dot_general(q8s[bi], kws[bi], NT_DIMS, preferred_element_type=F32) for bi in seqs]
    probs = []
    for s in scores:
        m = jnp.maximum(jnp.max(s, axis=1, keepdims=True), sk)
        p = jnp.exp(s - m)
        probs.append((p / (jnp.sum(p, axis=1, keepdims=True) + jnp.exp(sk - m))).astype(BF16))
    for bi in seqs:
        o8 = _dot(probs[bi], vws[bi])
        for slab in range(ATTN_Q_HEADS // 2):
            kv = (2 * slab) // group
            even, odd = o8[2 * slab:2 * slab + 1, :], o8[2 * slab + 1:2 * slab + 2, :]
            if kv == 0:
                out = jnp.where(low, even, pltpu.roll(odd, HEAD_DIM, axis=1))
            else:
                out = jnp.where(low, pltpu.roll(even, HEAD_DIM, axis=1), odd)
            oa_rows[slab].append(out)

    def columns(a8, hd):
        pad = jnp.zeros((HG_D - tile, HG_D), F32)
        return jnp.concatenate([_head(a8, hd), pad], axis=0).T
    seq_row = lax.broadcasted_iota(jnp.int32, (HG_D, HG_D), 0)
    for hd in range(HG_HEADS):
        rs = slice(hd * HG_D, (hd + 1) * HG_D)
        f_cols = columns(fh8, hd)
        k_cols = columns(kh8, hd).astype(BF16)
        v_rows = jnp.concatenate([_head(vh8, hd), jnp.zeros((HG_D - tile, HG_D), F32)], axis=0)
        q_rows = _head(qh8, hd).astype(BF16)
        outers = [_dot(k_cols, jnp.where(seq_row == bi, v_rows, 0.0).astype(BF16))
                  for bi in seqs]
        for bi in seqs:
            f_col = jnp.broadcast_to(f_cols[:, bi:bi + 1], (HG_D, HG_D))
            ns_ref[bi, rs, :] = f_col * s0_ref[bi, rs, :] + outers[bi]
        outs = [_dot(q_rows, ns_ref[bi, rs, :].astype(BF16)) for bi in seqs]
        for bi in seqs:
            oh_rows[hd].append(_rms(outs[bi][bi:bi + 1, :], hgn))
    for slab in range(ATTN_Q_HEADS // 2):
        oa_ref[r8, slab * LANES:(slab + 1) * LANES] = _row_select(oa_rows[slab])
    for hd in range(HG_HEADS):
        oh_ref[r8, hd * HG_D:(hd + 1) * HG_D] = _row_select(oh_rows[hd])

    @pl.when(i == pl.num_programs(0) - 1)
    def _():
        gh = proj(OFF_GH, HG_W)
        oh = oh_ref[...] * (gh * jax.nn.sigmoid(gh))
        m = (_merge_out(oa_ref[...], wao_ref, OFF_GA, proj)
             + _merge_out(oh, who_ref, OFF_GB, proj))
        y_ref[...] = x_ref[...] + _rms(_dot(m.astype(BF16), wo_ref[...]), post_ref[...])


def _mix_sample_call(x, ck, cv, s0, sinks, pre, post, lbl, hgn, win, wao, who, wo):
    n = x.shape[0]
    tile = min(SAMPLE_TILE, n)
    full = pl.BlockSpec((n, D_MODEL), lambda i: (0, 0))
    kv_spec = pl.BlockSpec((tile, WINDOW, ATTN_KV_W), lambda i: (i, 0, 0))
    st_spec = pl.BlockSpec((tile, HG_W, HG_D), lambda i: (i, 0, 0))
    act = pltpu.VMEM((n, HG_W), F32)
    kv_act = pltpu.VMEM((n, ATTN_KV_W), F32)
    return pl.pallas_call(
        _mix_sample_kernel,
        grid=(n // tile,),
        in_specs=[pl.BlockSpec(memory_space=pltpu.SMEM), full, kv_spec, kv_spec, st_spec,
                  _const_spec((1, D_MODEL)), _const_spec((1, D_MODEL)),
                  _const_spec(lbl.shape), _const_spec((1, HG_D)),
                  HBM_SPEC, HBM_SPEC, HBM_SPEC, HBM_SPEC],
        out_specs=[full, kv_spec, kv_spec, st_spec],
        out_shape=[jax.ShapeDtypeStruct(x.shape, F32),
                   jax.ShapeDtypeStruct(ck.shape, F32),
                   jax.ShapeDtypeStruct(cv.shape, F32),
                   jax.ShapeDtypeStruct(s0.shape, F32)],
        scratch_shapes=[pltpu.VMEM(win.shape, BF16), pltpu.VMEM(wao.shape, BF16),
                        pltpu.VMEM(who.shape, BF16), pltpu.VMEM(wo.shape, BF16),
                        pltpu.VMEM((n, D_MODEL), BF16),
                        pltpu.VMEM((n, ATTN_Q_W), F32), kv_act, kv_act,
                        act, act, act, act,
                        pltpu.VMEM((n, ATTN_Q_W), F32), act],
        compiler_params=pltpu.CompilerParams(
            dimension_semantics=("arbitrary",), vmem_limit_bytes=VMEM_LIMIT),
        name="mix_sample",
    )(sinks, x, ck, cv, s0, pre, post, lbl, hgn, win, wao, who, wo)


def kernel(x_prompt, x_sample, cache_k, cache_v, state_hgrn, norm_ffn1_pre, norm_ffn1_post, w_ffn1_gate, w_ffn1_up, w_ffn1_down, norm_mix_pre, norm_mix_post, w_in, attn_sinks, hgrn_lb_logits, hgrn_norm, w_attn_out, w_hgrn_out, w_out, norm_ffn2_pre, norm_ffn2_post, w_ffn2_gate, w_ffn2_up, w_ffn2_down):
    depth = w_in.shape[0]
    assert depth == 1 and hgrn_lb_logits.shape[0] == 2, "single-layer stack only"
    batch, seq, _ = x_prompt.shape
    n_s = x_sample.shape[0]
    assert x_sample.shape[1] == 1 and seq % WINDOW == 0

    xp = x_prompt.reshape(batch * seq, D_MODEL)
    xs = x_sample.reshape(n_s, D_MODEL)
    ck = cache_k[0].reshape(n_s, WINDOW, ATTN_KV_W)
    cv = cache_v[0].reshape(n_s, WINDOW, ATTN_KV_W)
    s0 = state_hgrn[0].reshape(n_s, HG_W, HG_D)

    xp, xs = _ffn_call(xp, xs, norm_ffn1_pre, norm_ffn1_post,
                       w_ffn1_gate[0], w_ffn1_up[0], w_ffn1_down[0])

    mix_w = (attn_sinks, norm_mix_pre, norm_mix_post, hgrn_lb_logits, hgrn_norm,
             w_in[0], w_attn_out[0], w_hgrn_out[0], w_out[0])
    xp, nkp, nvp, nsp = _mix_prompt_call(xp, batch, *mix_w)
    xs, nks, nvs, nss = _mix_sample_call(xs, ck, cv, s0, *mix_w)

    xp, xs = _ffn_call(xp, xs, norm_ffn2_pre, norm_ffn2_post,
                       w_ffn2_gate[0], w_ffn2_up[0], w_ffn2_down[0])

    kv_shape = (1, -1, WINDOW, ATTN_KV_HEADS, HEAD_DIM)
    st_shape = (1, -1, HG_HEADS, HG_D, HG_D)
    return (xp.reshape(batch, seq, D_MODEL), xs.reshape(n_s, 1, D_MODEL),
            nkp.reshape(kv_shape), nvp.reshape(kv_shape), nsp.reshape(st_shape),
            nks.reshape(kv_shape), nvs.reshape(kv_shape), nss.reshape(st_shape))
```

```python
import functools

import jax
import jax.numpy as jnp
from jax import lax
from jax.experimental import pallas as pl
from jax.experimental.pallas import tpu as pltpu

F32 = jnp.float32
BF16 = jnp.bfloat16

D_MODEL = 1024
FFN_DIM = 2816
HEAD_DIM = 64
ATTN_Q_HEADS = 8
ATTN_KV_HEADS = 2
WINDOW = 128
HG_HEADS = 4
HG_D = 128
EPS = 1e-6

ATTN_Q_W = ATTN_Q_HEADS * HEAD_DIM
ATTN_KV_W = ATTN_KV_HEADS * HEAD_DIM
HG_W = HG_HEADS * HG_D
OFF_QA = 0
OFF_KA = OFF_QA + ATTN_Q_W
OFF_VA = OFF_KA + ATTN_KV_W
OFF_QH = OFF_VA + ATTN_KV_W
OFF_FH = OFF_QH + HG_W
OFF_IH = OFF_FH + HG_W
OFF_GH = OFF_IH + HG_W
OFF_GA = OFF_GH + HG_W
OFF_GB = OFF_GA + D_MODEL
IN_WIDTH = OFF_GB + D_MODEL

LANES = 128
SUBLANES = 8
FFN_CHUNK = 256
N_FFN_CHUNKS = FFN_DIM // FFN_CHUNK
FFN_ROWS = 1024
MIX_ROWS = 512
NORM_PARTS = 2
LOOP_GATE_SLABS = 10
HG_CHUNK = 128
SAMPLE_TILE = 8
VMEM_LIMIT = 56 * 1024 * 1024

NT_DIMS = (((1,), (1,)), ((), ()))
TN_DIMS = (((0,), (0,)), ((), ()))


def _rms(x, g):
    return x * lax.rsqrt(jnp.mean(x * x, axis=-1, keepdims=True) + EPS) * g


def _dot(a, b):
    return jnp.dot(a, b, preferred_element_type=F32)


STREAM_SLOTS = 4
STREAM_CHUNK_BYTES = 2 << 20


def _stream_cast(src_hbm, dst_ref):
    n_rows, n_cols = src_hbm.shape
    row_bytes = n_cols * jnp.dtype(F32).itemsize
    rc = max(SUBLANES, min(n_rows, STREAM_CHUNK_BYTES // row_bytes // SUBLANES * SUBLANES))
    while n_rows % rc:
        rc -= SUBLANES
    n = n_rows // rc
    slots = min(STREAM_SLOTS, n)

    def body(stage, sem):
        def copy(c):
            return pltpu.make_async_copy(src_hbm.at[pl.ds(c * rc, rc), :], stage.at[c % slots],
                                         sem.at[c % slots])
        for c in range(slots):
            copy(c).start()
        for c in range(n):
            copy(c).wait()
            dst_ref[c * rc:(c + 1) * rc, :] = stage[c % slots].astype(BF16)
            if c + slots < n:
                copy(c + slots).start()

    pl.run_scoped(body, pltpu.VMEM((slots, rc, n_cols), F32), pltpu.SemaphoreType.DMA((slots,)))


HBM_SPEC = pl.BlockSpec(memory_space=pl.ANY)


def _const_spec(shape):
    zeros = (0,) * len(shape)
    return pl.BlockSpec(shape, lambda *_: zeros, pipeline_mode=pl.Buffered(1))


def _ffn_tile(x, pre, post, wg_ref, wu_ref, wd_ref):
    h = _rms(x, pre).astype(BF16)
    acc = None
    for c in range(N_FFN_CHUNKS):
        cols = slice(c * FFN_CHUNK, (c + 1) * FFN_CHUNK)
        g = _dot(h, wg_ref[:, cols])
        u = _dot(h, wu_ref[:, cols])
        a = (g * jax.nn.sigmoid(g) * u).astype(BF16)
        d = _dot(a, wd_ref[cols, :])
        acc = d if acc is None else acc + d
    return x + 0.5 * _rms(acc, post)


def _ffn_kernel(n_prompt_steps, xp_ref, xs_ref, pre_ref, post_ref, wg_hbm, wu_hbm, wd_hbm,
                yp_ref, ys_ref, wg_ref, wu_ref, wd_ref):
    i = pl.program_id(0)

    @pl.when(i == 0)
    def _():
        _stream_cast(wg_hbm, wg_ref)
        _stream_cast(wu_hbm, wu_ref)
        _stream_cast(wd_hbm, wd_ref)

    @pl.when(i < n_prompt_steps)
    def _():
        yp_ref[...] = _ffn_tile(xp_ref[...], pre_ref[...], post_ref[...], wg_ref, wu_ref, wd_ref)

    @pl.when(i == n_prompt_steps)
    def _():
        ys_ref[...] = _ffn_tile(xs_ref[...], pre_ref[...], post_ref[...], wg_ref, wu_ref, wd_ref)


def _ffn_call(xp, xs, pre, post, wg, wu, wd):
    n_p, n_s = xp.shape[0], xs.shape[0]
    rows = min(FFN_ROWS, n_p)
    steps = n_p // rows
    prompt_spec = pl.BlockSpec((rows, D_MODEL), lambda i: (jnp.minimum(i, steps - 1), 0))
    sample_spec = pl.BlockSpec((n_s, D_MODEL), lambda i: (0, 0))
    return pl.pallas_call(
        functools.partial(_ffn_kernel, steps),
        grid=(steps + 1,),
        in_specs=[prompt_spec, sample_spec,
                  _const_spec((1, D_MODEL)), _const_spec((1, D_MODEL)),
                  HBM_SPEC, HBM_SPEC, HBM_SPEC],
        out_specs=[prompt_spec, sample_spec],
        out_shape=[jax.ShapeDtypeStruct(xp.shape, F32), jax.ShapeDtypeStruct(xs.shape, F32)],
        scratch_shapes=[pltpu.VMEM(wg.shape, BF16), pltpu.VMEM(wu.shape, BF16),
                        pltpu.VMEM(wd.shape, BF16)],
        compiler_params=pltpu.CompilerParams(
            dimension_semantics=("arbitrary",), vmem_limit_bytes=VMEM_LIMIT),
        name="ffn_half",
    )(xp, xs, pre, post, wg, wu, wd)


def _forget_lower_bound(lbl):
    l0, l1 = lbl[0:1, :], lbl[1:2, :]
    m = jnp.maximum(l0, l1)
    e0, e1 = jnp.exp(l0 - m), jnp.exp(l1 - m)
    return e0 / (e0 + e1)


def _lane_is_low(shape):
    return lax.broadcasted_iota(jnp.int32, shape, len(shape) - 1) < HEAD_DIM


def _dup_kv(x):
    swapped = pltpu.roll(x, HEAD_DIM, axis=1)
    low = _lane_is_low(x.shape)
    return jnp.where(low, x, swapped), jnp.where(low, swapped, x)


def _merge_out(h, w_ref, off, gates_from):
    return jax.nn.sigmoid(gates_from(off, D_MODEL)) * _dot(h.astype(BF16), w_ref[...])


def _pair_levels(c):
    t = jnp.arange(c, dtype=jnp.int32)[:, None]
    s = jnp.arange(c, dtype=jnp.int32)[None, :]
    x = jnp.maximum(t ^ s, 1)
    lvl = (31 - lax.clz(x)).astype(jnp.int32)
    return jnp.where(t > s, lvl, -1)


LOG2E = 1.4426950408889634


def _head(a, hd):
    return a[:, hd * HG_D:(hd + 1) * HG_D]


HG_LEVELS = HG_CHUNK.bit_length() - 1


def _chunk_rows(ci):
    return slice(ci * HG_CHUNK, (ci + 1) * HG_CHUNK)


def _hgrn_log_decay(g_ref, b_ref, n_chunks):
    c = HG_CHUNK
    width = g_ref.shape[1]
    row = lax.broadcasted_iota(jnp.int32, (c, c), 0)
    col = lax.broadcasted_iota(jnp.int32, (c, c), 1)
    tril = (col <= row).astype(BF16)
    for ci in range(n_chunks):
        r = _chunk_rows(ci)
        g = g_ref[r, :]
        hi = g.astype(BF16)
        rest = g - hi.astype(F32)
        mid = rest.astype(BF16)
        lo = (rest - mid.astype(F32)).astype(BF16)
        sums = _dot(tril, jnp.concatenate([hi, mid, lo], axis=1))
        b_ref[r, :] = (sums[:, 0:width] + sums[:, width:2 * width] + sums[:, 2 * width:]) * LOG2E


def _hgrn_factor_jobs(q_ref, k_ref, b_ref, x_ref, qe_ref, ke_ref, n_chunks):
    c = HG_CHUNK
    t = lax.broadcasted_iota(jnp.int32, (c, 1), 0)
    last = {}

    def level_job(ci, lvl):
        def run():
            n = 1 << lvl
            r = _chunk_rows(ci)
            bc = b_ref[r, :]
            second = (t & n) != 0
            if n < SUBLANES:
                prev = last.get(ci, bc)
                tiles = prev.reshape(c // SUBLANES, SUBLANES, prev.shape[1])
                second3 = second.reshape(c // SUBLANES, SUBLANES, 1)
                edge = jnp.where(second3, pltpu.roll(tiles, n, axis=1), tiles).reshape(prev.shape)
                if 2 * n < SUBLANES:
                    last[ci] = jnp.where(second3, tiles, pltpu.roll(tiles, SUBLANES - n, axis=1)
                                         ).reshape(prev.shape)
            else:
                edge = jnp.concatenate(
                    [jnp.broadcast_to(bc[p + n - 1:p + n, :], (2 * n, bc.shape[1]))
                     for p in range(0, c, 2 * n)], axis=0)
            w = jnp.exp2(jnp.where(second, bc - edge, edge - bc))
            x_ref[lvl, r, :] = (jnp.where(second, q_ref[r, :], k_ref[r, :]) * w).astype(BF16)
        return run

    def state_job(ci):
        def run():
            r = _chunk_rows(ci)
            bc = b_ref[r, :]
            qe_ref[r, :] = (q_ref[r, :] * jnp.exp2(bc)).astype(BF16)
            ke_ref[r, :] = (k_ref[r, :] * jnp.exp2(bc[c - 1:c, :] - bc)).astype(BF16)
        return run

    jobs = [level_job(ci, lvl) for lvl in range(HG_LEVELS) for ci in range(n_chunks)]
    return jobs + [state_job(ci) for ci in range(n_chunks)]


def _hgrn_pair_jobs(x_ref, levels, n_chunks):
    c = HG_CHUNK
    heads, chunks = range(HG_HEADS), range(n_chunks)
    a = [[jnp.zeros((c, c), F32) for _ in heads] for _ in chunks]

    def job(lvl, ci):
        def run():
            x = x_ref[lvl, _chunk_rows(ci), :]
            for hd in heads:
                al = lax.dot_general(_head(x, hd), _head(x, hd), NT_DIMS,
                                     preferred_element_type=F32)
                a[ci][hd] = jnp.where(levels == lvl, al, a[ci][hd])
        return run
    return [job(lvl, ci) for lvl in range(HG_LEVELS) for ci in chunks], a


def _hgrn_finish(q_ref, k_ref, v_ref, b_ref, a, qe_ref, ke_ref, st, n_chunks):
    c = HG_CHUNK
    heads, chunks = range(HG_HEADS), range(n_chunks)
    local = []
    for ci in chunks:
        r = _chunk_rows(ci)
        q, k, v = q_ref[r, :], k_ref[r, :], v_ref[r, :]
        qk, vb = q * k, v.astype(BF16)
        outs = []
        for hd in heads:
            diag = jnp.sum(_head(qk, hd), axis=1, keepdims=True)
            outs.append(_dot(a[ci][hd].astype(BF16), _head(vb, hd)) + diag * _head(v, hd))
        local.append(outs)
        yield

    result = []
    for ci in chunks:
        r = _chunk_rows(ci)
        qe, ke, vb = qe_ref[r, :], ke_ref[r, :], v_ref[r, :].astype(BF16)
        decay = jnp.exp2(b_ref[ci * c + c - 1:ci * c + c, :])
        outs, new = [], []
        for hd in heads:
            outs.append(local[ci][hd] + lax.dot_general(_head(qe, hd), st[hd].astype(BF16), NT_DIMS,
                                                        preferred_element_type=F32))
            new.append(st[hd] * _head(decay, hd)
                       + lax.dot_general(_head(vb, hd), _head(ke, hd), TN_DIMS,
                                         preferred_element_type=F32))
        st = new
        result.append(outs)
        yield
    return result, st


def _mix_prompt_kernel(sinks_ref, x_ref, pre_ref, post_ref, lbl_ref, hgn_ref, lvl_ref,
                       win_hbm, wao_hbm, who_hbm, wo_hbm,
                       y_ref, nk_ref, nv_ref, ns_ref,
                       win_ref, wao_ref, who_ref, wo_ref,
                       kbuf, vbuf, st_ref, qh_ref, kh_ref, vh_ref, lf_ref, oa_ref, oh_ref, g_ref,
                       b_ref, xl_ref, qe_ref, ke_ref):
    j = pl.program_id(1)
    rows = x_ref.shape[0]
    n_blocks = rows // WINDOW

    @pl.when((pl.program_id(0) == 0) & (j == 0))
    def _():
        for src, dst in ((win_hbm, win_ref), (wao_hbm, wao_ref), (who_hbm, who_ref),
                         (wo_hbm, wo_ref)):
            _stream_cast(src, dst)

    @pl.when(j == 0)
    def _():
        kbuf[0:WINDOW, :] = jnp.zeros((WINDOW, LANES), F32)
        vbuf[0:WINDOW, :] = jnp.zeros((WINDOW, LANES), F32)
        st_ref[...] = jnp.zeros(st_ref.shape, F32)

    part_rows = [slice(r0, r0 + rows // NORM_PARTS) for r0 in range(0, rows, rows // NORM_PARTS)]
    h_parts, qkv_parts = [], []
    for r in part_rows:
        hp = _rms(x_ref[r, :], pre_ref[...]).astype(BF16)
        h_parts.append(hp)
        qkv_parts.append(_dot(hp, win_ref[:, OFF_QA:OFF_QH]))
    h = jnp.concatenate(h_parts, axis=0)
    qkv = jnp.concatenate(qkv_parts, axis=0)

    def proj(off, width):
        return _dot(h, win_ref[:, off:off + width])

    lb = _forget_lower_bound(lbl_ref[...])
    n_chunks = rows // HG_CHUNK

    def proj_slab(off):
        def run():
            z = proj(off, FFN_CHUNK)
            if off < OFF_FH:
                cols = slice(off - OFF_QH, off - OFF_QH + FFN_CHUNK)
                qh_ref[:, cols] = z * jax.nn.sigmoid(z)
            elif off < OFF_IH:
                cols = slice(off - OFF_FH, off - OFF_FH + FFN_CHUNK)
                lbs = lb[:, cols]
                lf_ref[:, cols] = jnp.log(lbs + (1.0 - lbs) * jax.nn.sigmoid(z))
                kh_ref[:, cols] = (1.0 - lbs) * jax.nn.sigmoid(-z)
            elif off < OFF_GH:
                cols = slice(off - OFF_IH, off - OFF_IH + FFN_CHUNK)
                vh_ref[:, cols] = z
            else:
                cols = slice(off - OFF_GH, off - OFF_GH + FFN_CHUNK)
                g_ref[:, cols] = z
        return run
    hgrn_slabs = [proj_slab(off) for off in range(OFF_QH, OFF_GH, FFN_CHUNK)]
    gate_slabs = [proj_slab(off) for off in range(OFF_GH, IN_WIDTH, FFN_CHUNK)]
    jobs = _hgrn_factor_jobs(qh_ref, kh_ref, b_ref, xl_ref, qe_ref, ke_ref, n_chunks)
    pair_jobs, in_chunk = _hgrn_pair_jobs(xl_ref, lvl_ref[...], n_chunks)

    qa = qkv[:, OFF_QA:OFF_KA] * (HEAD_DIM ** -0.5)
    ka = qkv[:, OFF_KA:OFF_VA]
    va = qkv[:, OFF_VA:OFF_QH]
    kbuf[WINDOW:WINDOW + rows, :] = ka
    vbuf[WINDOW:WINDOW + rows, :] = va

    qi = lax.broadcasted_iota(jnp.int32, (WINDOW, 2 * WINDOW), 0)
    kj = lax.broadcasted_iota(jnp.int32, (WINDOW, 2 * WINDOW), 1)
    band = (kj > qi) & (kj <= qi + WINDOW)
    low = _lane_is_low((WINDOW, LANES))
    top = lax.broadcasted_iota(jnp.int32, (2 * WINDOW, 1), 0) < WINDOW
    n_slabs = ATTN_Q_HEADS // 2
    group = ATTN_Q_HEADS // ATTN_KV_HEADS
    kds = [_dup_kv(kbuf[n * WINDOW:(n + 2) * WINDOW, :]) for n in range(n_blocks)]
    vds = [_dup_kv(vbuf[n * WINDOW:(n + 2) * WINDOW, :]) for n in range(n_blocks)]

    def scores(n, slab):
        qs = qa[n * WINDOW:(n + 1) * WINDOW, slab * LANES:(slab + 1) * LANES]
        q2 = jnp.concatenate([jnp.where(low, qs, 0.0), jnp.where(low, 0.0, qs)], axis=0)
        return lax.dot_general(q2.astype(BF16), kds[n][(2 * slab) // group].astype(BF16), NT_DIMS,
                               preferred_element_type=F32)

    def attend(n, slab, s):
        valid = band & ((j > 0) | (kj >= WINDOW)) if n == 0 else band
        s = jnp.where(jnp.concatenate([valid, valid], axis=0), s, -jnp.inf)
        sk = jnp.where(top, sinks_ref[0, 2 * slab], sinks_ref[0, 2 * slab + 1])
        m = jnp.maximum(jnp.max(s, axis=1, keepdims=True), sk)
        p = jnp.exp(s - m)
        den = jnp.sum(p, axis=1, keepdims=True) + jnp.exp(sk - m)
        o2 = _dot(p.astype(BF16), vds[n][(2 * slab) // group].astype(BF16)) / den
        oa_ref[n * WINDOW:(n + 1) * WINDOW, slab * LANES:(slab + 1) * LANES] = (
            jnp.where(low, o2[0:WINDOW], o2[WINDOW:2 * WINDOW]))

    order = [(n, slab) for n in range(n_blocks) for slab in range(n_slabs)]
    early = len(hgrn_slabs)
    late_gates = gate_slabs[LOOP_GATE_SLABS:]
    gate_slabs = gate_slabs[:LOOP_GATE_SLABS]
    pending = scores(*order[0])
    ready_pairs = 0
    for idx, (n, slab) in enumerate(order):
        nxt = scores(*order[idx + 1]) if idx + 1 < len(order) else None
        if idx < early:
            hgrn_slabs[idx]()
        else:
            if idx == early:
                _hgrn_log_decay(lf_ref, b_ref, n_chunks)
            if gate_slabs:
                gate_slabs.pop(0)()
            for _ in range(min(len(pair_jobs), ready_pairs)):
                pair_jobs.pop(0)()
            ready_pairs = 0
            for _ in range(-(-len(jobs) // (len(order) - 1 - idx)) if idx + 1 < len(order) else 0):
                if jobs:
                    jobs.pop(0)()
                    ready_pairs += 1
        attend(n, slab, pending)
        pending = nxt
    kbuf[0:WINDOW, :] = kbuf[rows:rows + WINDOW, :]
    vbuf[0:WINDOW, :] = vbuf[rows:rows + WINDOW, :]
    assert not jobs and not hgrn_slabs[early:]
    late = gate_slabs + late_gates
    for run in pair_jobs:
        run()

    nk_ref[0] = ka[rows - WINDOW:rows, :]
    nv_ref[0] = va[rows - WINDOW:rows, :]

    hgn = hgn_ref[...]
    st0 = [st_ref[hd * HG_D:(hd + 1) * HG_D, :] for hd in range(HG_HEADS)]
    finish = _hgrn_finish(qh_ref, kh_ref, vh_ref, b_ref, in_chunk, qe_ref, ke_ref, st0, n_chunks)
    while True:
        try:
            next(finish)
        except StopIteration as done:
            outs, states = done.value
            break
        if late:
            late.pop(0)()
    for run in late:
        run()
    for hd in range(HG_HEADS):
        cs = slice(hd * HG_D, (hd + 1) * HG_D)
        st_ref[cs, :] = states[hd]
        ns_ref[0, cs, :] = states[hd].T
        for ci in range(n_chunks):
            oh_ref[ci * HG_CHUNK:(ci + 1) * HG_CHUNK, cs] = _rms(outs[ci][hd], hgn)

    gh = g_ref[:, 0:HG_W]
    oh = oh_ref[...] * (gh * jax.nn.sigmoid(gh))

    def gate(off, width):
        return g_ref[:, off - OFF_GH:off - OFF_GH + width]
    m = (_merge_out(oa_ref[...], wao_ref, OFF_GA, gate)
         + _merge_out(oh, who_ref, OFF_GB, gate))
    mb = m.astype(BF16)
    for r in part_rows:
        y_ref[r, :] = x_ref[r, :] + _rms(_dot(mb[r, :], wo_ref[...]), post_ref[...])


def _mix_prompt_call(x, batch, sinks, pre, post, lbl, hgn, win, wao, who, wo):
    seq = x.shape[0] // batch
    rows = min(MIX_ROWS, seq)
    steps = seq // rows
    x_spec = pl.BlockSpec((rows, D_MODEL), lambda b, j: (b * steps + j, 0))
    kv_spec = pl.BlockSpec((1, WINDOW, ATTN_KV_W), lambda b, j: (b, 0, 0))
    st_spec = pl.BlockSpec((1, HG_W, HG_D), lambda b, j: (b, 0, 0))
    act = pltpu.VMEM((rows, HG_W), F32)
    return pl.pallas_call(
        _mix_prompt_kernel,
        grid=(batch, steps),
        in_specs=[pl.BlockSpec(memory_space=pltpu.SMEM), x_spec,
                  _const_spec((1, D_MODEL)), _const_spec((1, D_MODEL)),
                  _const_spec(lbl.shape), _const_spec((1, HG_D)),
                  _const_spec((HG_CHUNK, HG_CHUNK)),
                  HBM_SPEC, HBM_SPEC, HBM_SPEC, HBM_SPEC],
        out_specs=[x_spec, kv_spec, kv_spec, st_spec],
        out_shape=[jax.ShapeDtypeStruct(x.shape, F32),
                   jax.ShapeDtypeStruct((batch, WINDOW, ATTN_KV_W), F32),
                   jax.ShapeDtypeStruct((batch, WINDOW, ATTN_KV_W), F32),
                   jax.ShapeDtypeStruct((batch, HG_W, HG_D), F32)],
        scratch_shapes=[pltpu.VMEM(win.shape, BF16), pltpu.VMEM(wao.shape, BF16),
                        pltpu.VMEM(who.shape, BF16), pltpu.VMEM(wo.shape, BF16),
                        pltpu.VMEM((rows + WINDOW, ATTN_KV_W), F32),
                        pltpu.VMEM((rows + WINDOW, ATTN_KV_W), F32),
                        pltpu.VMEM((HG_W, HG_D), F32),
                        act, act, act, act,
                        pltpu.VMEM((rows, ATTN_Q_W), F32), act,
                        pltpu.VMEM((rows, IN_WIDTH - OFF_GH), F32),
                        act, pltpu.VMEM((HG_LEVELS, rows, HG_W), BF16),
                        pltpu.VMEM((rows, HG_W), BF16), pltpu.VMEM((rows, HG_W), BF16)],
        compiler_params=pltpu.CompilerParams(
            dimension_semantics=("arbitrary", "arbitrary"), vmem_limit_bytes=VMEM_LIMIT),
        name="mix_prompt",
    )(sinks, x, pre, post, lbl, hgn, _pair_levels(HG_CHUNK), win, wao, who, wo)


def _row_select(rows_list):
    n = -(-len(rows_list) // 8) * 8
    lanes = rows_list[0].shape[1]
    ridx = lax.broadcasted_iota(jnp.int32, (n, lanes), 0)
    out = jnp.zeros((n, lanes), F32)
    for i, r in enumerate(rows_list):
        out = jnp.where(ridx == i, jnp.broadcast_to(r, (n, lanes)), out)
    return out


def _mix_sample_kernel(sinks_ref, x_ref, ck_ref, cv_ref, s0_ref, pre_ref, post_ref, lbl_ref, hgn_ref,
                       win_hbm, wao_hbm, who_hbm, wo_hbm,
                       y_ref, nk_ref, nv_ref, ns_ref,
                       win_ref, wao_ref, who_ref, wo_ref,
                       h_ref, qa_ref, ka_ref, va_ref, qh_ref, kh_ref, fh_ref, vh_ref, oa_ref, oh_ref):
    i = pl.program_id(0)
    tile = ck_ref.shape[0]

    def proj(off, width):
        return _dot(h_ref[...], win_ref[:, off:off + width])

    @pl.when(i == 0)
    def _():
        for src, dst in ((win_hbm, win_ref), (wao_hbm, wao_ref), (who_hbm, who_ref),
                         (wo_hbm, wo_ref)):
            _stream_cast(src, dst)
        h_ref[...] = _rms(x_ref[...], pre_ref[...]).astype(BF16)
        qa_ref[...] = proj(OFF_QA, ATTN_Q_W) * (HEAD_DIM ** -0.5)
        ka_ref[...] = proj(OFF_KA, ATTN_KV_W)
        va_ref[...] = proj(OFF_VA, ATTN_KV_W)
        lb = _forget_lower_bound(lbl_ref[...])
        qh = proj(OFF_QH, HG_W)
        qh_ref[...] = qh * jax.nn.sigmoid(qh)
        fp = proj(OFF_FH, HG_W)
        fh_ref[...] = jnp.exp(jnp.log(lb + (1.0 - lb) * jax.nn.sigmoid(fp)))
        kh_ref[...] = (1.0 - lb) * jax.nn.sigmoid(-fp)
        vh_ref[...] = proj(OFF_IH, HG_W)

    low = _lane_is_low((1, LANES))
    newest = lax.broadcasted_iota(jnp.int32, (ATTN_KV_W, WINDOW), 1) == WINDOW - 1
    hrow = lax.broadcasted_iota(jnp.int32, (ATTN_Q_HEADS, 1), 0)
    sk = jnp.zeros((ATTN_Q_HEADS, 1), F32)
    for hd in range(ATTN_Q_HEADS):
        sk = jnp.where(hrow == hd, sinks_ref[0, hd], sk)
    group = ATTN_Q_HEADS // ATTN_KV_HEADS
    hgn = hgn_ref[...]

    r8 = pl.ds(pl.multiple_of(i * tile, tile), tile)
    qa8, ka8, va8 = qa_ref[r8, :], ka_ref[r8, :], va_ref[r8, :]
    fh8, kh8, qh8, vh8 = fh_ref[r8, :], kh_ref[r8, :], qh_ref[r8, :], vh_ref[r8, :]
    oa_rows = [[] for _ in range(ATTN_Q_HEADS // 2)]
    oh_rows = [[] for _ in range(HG_HEADS)]
    seqs = range(tile)
    def as_columns(rows8):
        pad = jnp.zeros((LANES - tile, rows8.shape[1]), F32)
        return jnp.concatenate([rows8, pad], axis=0).T
    k_cols, v_cols = as_columns(ka8), as_columns(va8)
    kws, vws, q8s = [], [], []
    for bi in seqs:
        r1 = slice(bi, bi + 1)
        kw = jnp.where(newest, pltpu.roll(k_cols, WINDOW - 1 - bi, axis=1),
                       pltpu.roll(ck_ref[bi], WINDOW - 1, axis=1))
        vw = jnp.where(newest, pltpu.roll(v_cols, WINDOW - 1 - bi, axis=1),
                       pltpu.roll(cv_ref[bi], WINDOW - 1, axis=1))
        nk_ref[bi] = kw
        nv_ref[bi] = vw
        kws.append(kw.astype(BF16))
        vws.append(vw.astype(BF16))
        qrows = []
        for hd in range(ATTN_Q_HEADS):
            slab = qa8[r1, (hd // 2) * LANES:(hd // 2 + 1) * LANES]
            in_place = (hd % 2) == (hd // group)
            src = slab if in_place else pltpu.roll(slab, HEAD_DIM, axis=1)
            on_kv_lanes = low if hd // group == 0 else jnp.logical_not(low)
            qrows.append(jnp.where(on_kv_lanes, src, 0.0))
        q8s.append(_row_select(qrows).astype(BF16))
    scores = [_dot(q8s[bi], kws[bi]) for bi in seqs]
    probs = []
    for s in scores:
        m = jnp.maximum(jnp.max(s, axis=1, keepdims=True), sk)
        p = jnp.exp(s - m)
        probs.append((p / (jnp.sum(p, axis=1, keepdims=True) + jnp.exp(sk - m))).astype(BF16))
    for bi in seqs:
        o8 = lax.dot_general(probs[bi], vws[bi], NT_DIMS, preferred_element_type=F32)
        for slab in range(ATTN_Q_HEADS // 2):
            kv = (2 * slab) // group
            even, odd = o8[2 * slab:2 * slab + 1, :], o8[2 * slab + 1:2 * slab + 2, :]
            if kv == 0:
                out = jnp.where(low, even, pltpu.roll(odd, HEAD_DIM, axis=1))
            else:
                out = jnp.where(low, pltpu.roll(even, HEAD_DIM, axis=1), odd)
            oa_rows[slab].append(out)

    def columns(a8, hd):
        pad = jnp.zeros((HG_D - tile, HG_D), F32)
        return jnp.concatenate([_head(a8, hd), pad], axis=0).T
    seq_row = lax.broadcasted_iota(jnp.int32, (HG_D, HG_D), 0)
    for hd in range(HG_HEADS):
        rs = slice(hd * HG_D, (hd + 1) * HG_D)
        f_cols = columns(fh8, hd)
        k_cols = columns(kh8, hd).astype(BF16)
        v_rows = jnp.concatenate([_head(vh8, hd), jnp.zeros((HG_D - tile, HG_D), F32)], axis=0)
        q_rows = _head(qh8, hd).astype(BF16)
        outers = [_dot(k_cols, jnp.where(seq_row == bi, v_rows, 0.0).astype(BF16))
                  for bi in seqs]
        for bi in seqs:
            f_col = jnp.broadcast_to(f_cols[:, bi:bi + 1], (HG_D, HG_D))
            ns_ref[bi, rs, :] = f_col * s0_ref[bi, rs, :] + outers[bi]
        outs = [_dot(q_rows, ns_ref[bi, rs, :].astype(BF16)) for bi in seqs]
        for bi in seqs:
            oh_rows[hd].append(_rms(outs[bi][bi:bi + 1, :], hgn))
    for slab in range(ATTN_Q_HEADS // 2):
        oa_ref[r8, slab * LANES:(slab + 1) * LANES] = _row_select(oa_rows[slab])
    for hd in range(HG_HEADS):
        oh_ref[r8, hd * HG_D:(hd + 1) * HG_D] = _row_select(oh_rows[hd])

    @pl.when(i == pl.num_programs(0) - 1)
    def _():
        gh = proj(OFF_GH, HG_W)
        oh = oh_ref[...] * (gh * jax.nn.sigmoid(gh))
        m = (_merge_out(oa_ref[...], wao_ref, OFF_GA, proj)
             + _merge_out(oh, who_ref, OFF_GB, proj))
        y_ref[...] = x_ref[...] + _rms(_dot(m.astype(BF16), wo_ref[...]), post_ref[...])


def _mix_sample_call(x, ck, cv, s0, sinks, pre, post, lbl, hgn, win, wao, who, wo):
    n = x.shape[0]
    tile = min(SAMPLE_TILE, n)
    full = pl.BlockSpec((n, D_MODEL), lambda i: (0, 0))
    kv_spec = pl.BlockSpec((tile, ATTN_KV_W, WINDOW), lambda i: (i, 0, 0))
    st_spec = pl.BlockSpec((tile, HG_W, HG_D), lambda i: (i, 0, 0))
    act = pltpu.VMEM((n, HG_W), F32)
    kv_act = pltpu.VMEM((n, ATTN_KV_W), F32)
    return pl.pallas_call(
        _mix_sample_kernel,
        grid=(n // tile,),
        in_specs=[pl.BlockSpec(memory_space=pltpu.SMEM), full, kv_spec, kv_spec, st_spec,
                  _const_spec((1, D_MODEL)), _const_spec((1, D_MODEL)),
                  _const_spec(lbl.shape), _const_spec((1, HG_D)),
                  HBM_SPEC, HBM_SPEC, HBM_SPEC, HBM_SPEC],
        out_specs=[full, kv_spec, kv_spec, st_spec],
        out_shape=[jax.ShapeDtypeStruct(x.shape, F32),
                   jax.ShapeDtypeStruct(ck.shape, F32),
                   jax.ShapeDtypeStruct(cv.shape, F32),
                   jax.ShapeDtypeStruct(s0.shape, F32)],
        scratch_shapes=[pltpu.VMEM(win.shape, BF16), pltpu.VMEM(wao.shape, BF16),
                        pltpu.VMEM(who.shape, BF16), pltpu.VMEM(wo.shape, BF16),
                        pltpu.VMEM((n, D_MODEL), BF16),
                        pltpu.VMEM((n, ATTN_Q_W), F32), kv_act, kv_act,
                        act, act, act, act,
                        pltpu.VMEM((n, ATTN_Q_W), F32), act],
        compiler_params=pltpu.CompilerParams(
            dimension_semantics=("arbitrary",), vmem_limit_bytes=VMEM_LIMIT),
        name="mix_sample",
    )(sinks, x, ck, cv, s0, pre, post, lbl, hgn, win, wao, who, wo)


def kernel(x_prompt, x_sample, cache_k, cache_v, state_hgrn, norm_ffn1_pre, norm_ffn1_post, w_ffn1_gate, w_ffn1_up, w_ffn1_down, norm_mix_pre, norm_mix_post, w_in, attn_sinks, hgrn_lb_logits, hgrn_norm, w_attn_out, w_hgrn_out, w_out, norm_ffn2_pre, norm_ffn2_post, w_ffn2_gate, w_ffn2_up, w_ffn2_down):
    depth = w_in.shape[0]
    assert depth == 1 and hgrn_lb_logits.shape[0] == 2, "single-layer stack only"
    batch, seq, _ = x_prompt.shape
    n_s = x_sample.shape[0]
    assert x_sample.shape[1] == 1 and seq % WINDOW == 0

    xp = x_prompt.reshape(batch * seq, D_MODEL)
    xs = x_sample.reshape(n_s, D_MODEL)
    ck = jnp.swapaxes(cache_k[0].reshape(n_s, WINDOW, ATTN_KV_W), 1, 2)
    cv = jnp.swapaxes(cache_v[0].reshape(n_s, WINDOW, ATTN_KV_W), 1, 2)
    s0 = state_hgrn[0].reshape(n_s, HG_W, HG_D)

    xp, xs = _ffn_call(xp, xs, norm_ffn1_pre, norm_ffn1_post,
                       w_ffn1_gate[0], w_ffn1_up[0], w_ffn1_down[0])

    mix_w = (attn_sinks, norm_mix_pre, norm_mix_post, hgrn_lb_logits, hgrn_norm,
             w_in[0], w_attn_out[0], w_hgrn_out[0], w_out[0])
    xp, nkp, nvp, nsp = _mix_prompt_call(xp, batch, *mix_w)
    xs, nks, nvs, nss = _mix_sample_call(xs, ck, cv, s0, *mix_w)

    xp, xs = _ffn_call(xp, xs, norm_ffn2_pre, norm_ffn2_post,
                       w_ffn2_gate[0], w_ffn2_up[0], w_ffn2_down[0])

    kv_shape = (1, -1, WINDOW, ATTN_KV_HEADS, HEAD_DIM)
    st_shape = (1, -1, HG_HEADS, HG_D, HG_D)
    return (xp.reshape(batch, seq, D_MODEL), xs.reshape(n_s, 1, D_MODEL),
            nkp.reshape(kv_shape), nvp.reshape(kv_shape), nsp.reshape(st_shape),
            jnp.swapaxes(nks, 1, 2).reshape(kv_shape), jnp.swapaxes(nvs, 1, 2).reshape(kv_shape),
            nss.reshape(st_shape))
```

```python
import functools

import jax
import jax.numpy as jnp
from jax import lax
from jax.experimental import pallas as pl
from jax.experimental.pallas import tpu as pltpu

F32 = jnp.float32
BF16 = jnp.bfloat16

D_MODEL = 1024
FFN_DIM = 2816
HEAD_DIM = 64
ATTN_Q_HEADS = 8
ATTN_KV_HEADS = 2
WINDOW = 128
HG_HEADS = 4
HG_D = 128
EPS = 1e-6

ATTN_Q_W = ATTN_Q_HEADS * HEAD_DIM
ATTN_KV_W = ATTN_KV_HEADS * HEAD_DIM
HG_W = HG_HEADS * HG_D
OFF_QA = 0
OFF_KA = OFF_QA + ATTN_Q_W
OFF_VA = OFF_KA + ATTN_KV_W
OFF_QH = OFF_VA + ATTN_KV_W
OFF_FH = OFF_QH + HG_W
OFF_IH = OFF_FH + HG_W
OFF_GH = OFF_IH + HG_W
OFF_GA = OFF_GH + HG_W
OFF_GB = OFF_GA + D_MODEL
IN_WIDTH = OFF_GB + D_MODEL

LANES = 128
SUBLANES = 8
FFN_CHUNK = 256
N_FFN_CHUNKS = FFN_DIM // FFN_CHUNK
FFN_ROWS = 1024
MIX_ROWS = 512
NORM_PARTS = 2
LOOP_GATE_SLABS = 10
HG_CHUNK = 128
SAMPLE_TILE = 8
VMEM_LIMIT = 56 * 1024 * 1024

NT_DIMS = (((1,), (1,)), ((), ()))
TN_DIMS = (((0,), (0,)), ((), ()))


def _rms(x, g):
    return x * lax.rsqrt(jnp.mean(x * x, axis=-1, keepdims=True) + EPS) * g


def _dot(a, b):
    return jnp.dot(a, b, preferred_element_type=F32)


STREAM_SLOTS = 4
STREAM_CHUNK_BYTES = 2 << 20


def _stream_cast(src_hbm, dst_ref):
    n_rows, n_cols = src_hbm.shape
    row_bytes = n_cols * jnp.dtype(F32).itemsize
    rc = max(SUBLANES, min(n_rows, STREAM_CHUNK_BYTES // row_bytes // SUBLANES * SUBLANES))
    while n_rows % rc:
        rc -= SUBLANES
    n = n_rows // rc
    slots = min(STREAM_SLOTS, n)

    def body(stage, sem):
        def copy(c):
            return pltpu.make_async_copy(src_hbm.at[pl.ds(c * rc, rc), :], stage.at[c % slots],
                                         sem.at[c % slots])
        for c in range(slots):
            copy(c).start()
        for c in range(n):
            copy(c).wait()
            dst_ref[c * rc:(c + 1) * rc, :] = stage[c % slots].astype(BF16)
            if c + slots < n:
                copy(c + slots).start()

    pl.run_scoped(body, pltpu.VMEM((slots, rc, n_cols), F32), pltpu.SemaphoreType.DMA((slots,)))


HBM_SPEC = pl.BlockSpec(memory_space=pl.ANY)


def _const_spec(shape):
    zeros = (0,) * len(shape)
    return pl.BlockSpec(shape, lambda *_: zeros, pipeline_mode=pl.Buffered(1))


def _ffn_tile(x, pre, post, wg_ref, wu_ref, wd_ref):
    h = _rms(x, pre).astype(BF16)
    acc = None
    for c in range(N_FFN_CHUNKS):
        cols = slice(c * FFN_CHUNK, (c + 1) * FFN_CHUNK)
        g = _dot(h, wg_ref[:, cols])
        u = _dot(h, wu_ref[:, cols])
        a = (g * jax.nn.sigmoid(g) * u).astype(BF16)
        d = _dot(a, wd_ref[cols, :])
        acc = d if acc is None else acc + d
    return x + 0.5 * _rms(acc, post)


def _ffn_kernel(n_prompt_steps, xp_ref, xs_ref, pre_ref, post_ref, wg_hbm, wu_hbm, wd_hbm,
                yp_ref, ys_ref, wg_ref, wu_ref, wd_ref):
    i = pl.program_id(0)

    @pl.when(i == 0)
    def _():
        _stream_cast(wg_hbm, wg_ref)
        _stream_cast(wu_hbm, wu_ref)
        _stream_cast(wd_hbm, wd_ref)

    @pl.when(i < n_prompt_steps)
    def _():
        yp_ref[...] = _ffn_tile(xp_ref[...], pre_ref[...], post_ref[...], wg_ref, wu_ref, wd_ref)

    @pl.when(i == n_prompt_steps)
    def _():
        ys_ref[...] = _ffn_tile(xs_ref[...], pre_ref[...], post_ref[...], wg_ref, wu_ref, wd_ref)


def _ffn_call(xp, xs, pre, post, wg, wu, wd):
    n_p, n_s = xp.shape[0], xs.shape[0]
    rows = min(FFN_ROWS, n_p)
    steps = n_p // rows
    prompt_spec = pl.BlockSpec((rows, D_MODEL), lambda i: (jnp.minimum(i, steps - 1), 0))
    sample_spec = pl.BlockSpec((n_s, D_MODEL), lambda i: (0, 0))
    return pl.pallas_call(
        functools.partial(_ffn_kernel, steps),
        grid=(steps + 1,),
        in_specs=[prompt_spec, sample_spec,
                  _const_spec((1, D_MODEL)), _const_spec((1, D_MODEL)),
                  HBM_SPEC, HBM_SPEC, HBM_SPEC],
        out_specs=[prompt_spec, sample_spec],
        out_shape=[jax.ShapeDtypeStruct(xp.shape, F32), jax.ShapeDtypeStruct(xs.shape, F32)],
        scratch_shapes=[pltpu.VMEM(wg.shape, BF16), pltpu.VMEM(wu.shape, BF16),
                        pltpu.VMEM(wd.shape, BF16)],
        compiler_params=pltpu.CompilerParams(
            dimension_semantics=("arbitrary",), vmem_limit_bytes=VMEM_LIMIT),
        name="ffn_half",
    )(xp, xs, pre, post, wg, wu, wd)


def _forget_lower_bound(lbl):
    l0, l1 = lbl[0:1, :], lbl[1:2, :]
    m = jnp.maximum(l0, l1)
    e0, e1 = jnp.exp(l0 - m), jnp.exp(l1 - m)
    return e0 / (e0 + e1)


def _lane_is_low(shape):
    return lax.broadcasted_iota(jnp.int32, shape, len(shape) - 1) < HEAD_DIM


def _dup_kv(x):
    swapped = pltpu.roll(x, HEAD_DIM, axis=1)
    low = _lane_is_low(x.shape)
    return jnp.where(low, x, swapped), jnp.where(low, swapped, x)


def _merge_out(h, w_ref, off, gates_from):
    return jax.nn.sigmoid(gates_from(off, D_MODEL)) * _dot(h.astype(BF16), w_ref[...])


def _pair_levels(c):
    t = jnp.arange(c, dtype=jnp.int32)[:, None]
    s = jnp.arange(c, dtype=jnp.int32)[None, :]
    x = jnp.maximum(t ^ s, 1)
    lvl = (31 - lax.clz(x)).astype(jnp.int32)
    return jnp.where(t > s, lvl, -1)


LOG2E = 1.4426950408889634


def _head(a, hd):
    return a[:, hd * HG_D:(hd + 1) * HG_D]


HG_LEVELS = HG_CHUNK.bit_length() - 1


def _chunk_rows(ci):
    return slice(ci * HG_CHUNK, (ci + 1) * HG_CHUNK)


def _hgrn_log_decay(g_ref, b_ref, n_chunks):
    c = HG_CHUNK
    width = g_ref.shape[1]
    row = lax.broadcasted_iota(jnp.int32, (c, c), 0)
    col = lax.broadcasted_iota(jnp.int32, (c, c), 1)
    tril = (col <= row).astype(BF16)
    for ci in range(n_chunks):
        r = _chunk_rows(ci)
        g = g_ref[r, :]
        hi = g.astype(BF16)
        rest = g - hi.astype(F32)
        mid = rest.astype(BF16)
        lo = (rest - mid.astype(F32)).astype(BF16)
        sums = _dot(tril, jnp.concatenate([hi, mid, lo], axis=1))
        b_ref[r, :] = (sums[:, 0:width] + sums[:, width:2 * width] + sums[:, 2 * width:]) * LOG2E


def _hgrn_factor_jobs(q_ref, k_ref, b_ref, x_ref, qe_ref, ke_ref, n_chunks):
    c = HG_CHUNK
    t = lax.broadcasted_iota(jnp.int32, (c, 1), 0)
    last = {}

    def level_job(ci, lvl):
        def run():
            n = 1 << lvl
            r = _chunk_rows(ci)
            bc = b_ref[r, :]
            second = (t & n) != 0
            if n < SUBLANES:
                prev = last.get(ci, bc)
                tiles = prev.reshape(c // SUBLANES, SUBLANES, prev.shape[1])
                second3 = second.reshape(c // SUBLANES, SUBLANES, 1)
                edge = jnp.where(second3, pltpu.roll(tiles, n, axis=1), tiles).reshape(prev.shape)
                if 2 * n < SUBLANES:
                    last[ci] = jnp.where(second3, tiles, pltpu.roll(tiles, SUBLANES - n, axis=1)
                                         ).reshape(prev.shape)
            else:
                edge = jnp.concatenate(
                    [jnp.broadcast_to(bc[p + n - 1:p + n, :], (2 * n, bc.shape[1]))
                     for p in range(0, c, 2 * n)], axis=0)
            w = jnp.exp2(jnp.where(second, bc - edge, edge - bc))
            x_ref[lvl, r, :] = (jnp.where(second, q_ref[r, :], k_ref[r, :]) * w).astype(BF16)
        return run

    def state_job(ci):
        def run():
            r = _chunk_rows(ci)
            bc = b_ref[r, :]
            qe_ref[r, :] = (q_ref[r, :] * jnp.exp2(bc)).astype(BF16)
            ke_ref[r, :] = (k_ref[r, :] * jnp.exp2(bc[c - 1:c, :] - bc)).astype(BF16)
        return run

    jobs = [level_job(ci, lvl) for lvl in range(HG_LEVELS) for ci in range(n_chunks)]
    return jobs + [state_job(ci) for ci in range(n_chunks)]


def _hgrn_pair_jobs(x_ref, levels, n_chunks):
    c = HG_CHUNK
    heads, chunks = range(HG_HEADS), range(n_chunks)
    a = [[jnp.zeros((c, c), F32) for _ in heads] for _ in chunks]

    def job(lvl, ci):
        def run():
            x = x_ref[lvl, _chunk_rows(ci), :]
            for hd in heads:
                al = lax.dot_general(_head(x, hd), _head(x, hd), NT_DIMS,
                                     preferred_element_type=F32)
                a[ci][hd] = jnp.where(levels == lvl, al, a[ci][hd])
        return run
    return [job(lvl, ci) for lvl in range(HG_LEVELS) for ci in chunks], a


def _hgrn_finish(q_ref, k_ref, v_ref, b_ref, a, qe_ref, ke_ref, st, n_chunks):
    c = HG_CHUNK
    heads, chunks = range(HG_HEADS), range(n_chunks)
    local = []
    for ci in chunks:
        r = _chunk_rows(ci)
        q, k, v = q_ref[r, :], k_ref[r, :], v_ref[r, :]
        qk, vb = q * k, v.astype(BF16)
        outs = []
        for hd in heads:
            diag = jnp.sum(_head(qk, hd), axis=1, keepdims=True)
            outs.append(_dot(a[ci][hd].astype(BF16), _head(vb, hd)) + diag * _head(v, hd))
        local.append(outs)
        yield

    result = []
    for ci in chunks:
        r = _chunk_rows(ci)
        qe, ke, vb = qe_ref[r, :], ke_ref[r, :], v_ref[r, :].astype(BF16)
        decay = jnp.exp2(b_ref[ci * c + c - 1:ci * c + c, :])
        outs, new = [], []
        for hd in heads:
            outs.append(local[ci][hd] + lax.dot_general(_head(qe, hd), st[hd].astype(BF16), NT_DIMS,
                                                        preferred_element_type=F32))
            new.append(st[hd] * _head(decay, hd)
                       + lax.dot_general(_head(vb, hd), _head(ke, hd), TN_DIMS,
                                         preferred_element_type=F32))
        st = new
        result.append(outs)
        yield
    return result, st


def _mix_prompt_kernel(sinks_ref, x_ref, pre_ref, post_ref, lbl_ref, hgn_ref, lvl_ref,
                       win_hbm, wao_hbm, who_hbm, wo_hbm,
                       y_ref, nk_ref, nv_ref, ns_ref,
                       win_ref, wao_ref, who_ref, wo_ref,
                       kbuf, vbuf, st_ref, qh_ref, kh_ref, vh_ref, lf_ref, oa_ref, oh_ref, g_ref,
                       b_ref, xl_ref, qe_ref, ke_ref):
    j = pl.program_id(1)
    rows = x_ref.shape[0]
    n_blocks = rows // WINDOW

    @pl.when((pl.program_id(0) == 0) & (j == 0))
    def _():
        for src, dst in ((win_hbm, win_ref), (wao_hbm, wao_ref), (who_hbm, who_ref),
                         (wo_hbm, wo_ref)):
            _stream_cast(src, dst)

    @pl.when(j == 0)
    def _():
        kbuf[0:WINDOW, :] = jnp.zeros((WINDOW, LANES), F32)
        vbuf[0:WINDOW, :] = jnp.zeros((WINDOW, LANES), F32)
        st_ref[...] = jnp.zeros(st_ref.shape, F32)

    part_rows = [slice(r0, r0 + rows // NORM_PARTS) for r0 in range(0, rows, rows // NORM_PARTS)]
    h_parts, qkv_parts = [], []
    for r in part_rows:
        hp = _rms(x_ref[r, :], pre_ref[...]).astype(BF16)
        h_parts.append(hp)
        qkv_parts.append(_dot(hp, win_ref[:, OFF_QA:OFF_QH]))
    h = jnp.concatenate(h_parts, axis=0)
    qkv = jnp.concatenate(qkv_parts, axis=0)

    def proj(off, width):
        return _dot(h, win_ref[:, off:off + width])

    lb = _forget_lower_bound(lbl_ref[...])
    n_chunks = rows // HG_CHUNK

    def proj_slab(off):
        def run():
            z = proj(off, FFN_CHUNK)
            if off < OFF_FH:
                cols = slice(off - OFF_QH, off - OFF_QH + FFN_CHUNK)
                qh_ref[:, cols] = z * jax.nn.sigmoid(z)
            elif off < OFF_IH:
                cols = slice(off - OFF_FH, off - OFF_FH + FFN_CHUNK)
                lbs = lb[:, cols]
                lf_ref[:, cols] = jnp.log(lbs + (1.0 - lbs) * jax.nn.sigmoid(z))
                kh_ref[:, cols] = (1.0 - lbs) * jax.nn.sigmoid(-z)
            elif off < OFF_GH:
                cols = slice(off - OFF_IH, off - OFF_IH + FFN_CHUNK)
                vh_ref[:, cols] = z
            else:
                cols = slice(off - OFF_GH, off - OFF_GH + FFN_CHUNK)
                g_ref[:, cols] = z
        return run
    hgrn_slabs = [proj_slab(off) for off in range(OFF_QH, OFF_GH, FFN_CHUNK)]
    gate_slabs = [proj_slab(off) for off in range(OFF_GH, IN_WIDTH, FFN_CHUNK)]
    jobs = _hgrn_factor_jobs(qh_ref, kh_ref, b_ref, xl_ref, qe_ref, ke_ref, n_chunks)
    pair_jobs, in_chunk = _hgrn_pair_jobs(xl_ref, lvl_ref[...], n_chunks)

    qa = qkv[:, OFF_QA:OFF_KA] * (HEAD_DIM ** -0.5)
    ka = qkv[:, OFF_KA:OFF_VA]
    va = qkv[:, OFF_VA:OFF_QH]
    kbuf[WINDOW:WINDOW + rows, :] = ka
    vbuf[WINDOW:WINDOW + rows, :] = va

    qi = lax.broadcasted_iota(jnp.int32, (WINDOW, 2 * WINDOW), 0)
    kj = lax.broadcasted_iota(jnp.int32, (WINDOW, 2 * WINDOW), 1)
    band = (kj > qi) & (kj <= qi + WINDOW)
    low = _lane_is_low((WINDOW, LANES))
    top = lax.broadcasted_iota(jnp.int32, (2 * WINDOW, 1), 0) < WINDOW
    n_slabs = ATTN_Q_HEADS // 2
    group = ATTN_Q_HEADS // ATTN_KV_HEADS
    kds = [_dup_kv(kbuf[n * WINDOW:(n + 2) * WINDOW, :]) for n in range(n_blocks)]
    vds = [_dup_kv(vbuf[n * WINDOW:(n + 2) * WINDOW, :]) for n in range(n_blocks)]

    def scores(n, slab):
        qs = qa[n * WINDOW:(n + 1) * WINDOW, slab * LANES:(slab + 1) * LANES]
        q2 = jnp.concatenate([jnp.where(low, qs, 0.0), jnp.where(low, 0.0, qs)], axis=0)
        return lax.dot_general(q2.astype(BF16), kds[n][(2 * slab) // group].astype(BF16), NT_DIMS,
                               preferred_element_type=F32)

    def attend(n, slab, s):
        valid = band & ((j > 0) | (kj >= WINDOW)) if n == 0 else band
        s = jnp.where(jnp.concatenate([valid, valid], axis=0), s, -jnp.inf)
        sk = jnp.where(top, sinks_ref[0, 2 * slab], sinks_ref[0, 2 * slab + 1])
        m = jnp.maximum(jnp.max(s, axis=1, keepdims=True), sk)
        p = jnp.exp(s - m)
        den = jnp.sum(p, axis=1, keepdims=True) + jnp.exp(sk - m)
        o2 = _dot(p.astype(BF16), vds[n][(2 * slab) // group].astype(BF16)) / den
        oa_ref[n * WINDOW:(n + 1) * WINDOW, slab * LANES:(slab + 1) * LANES] = (
            jnp.where(low, o2[0:WINDOW], o2[WINDOW:2 * WINDOW]))

    order = [(n, slab) for n in range(n_blocks) for slab in range(n_slabs)]
    early = len(hgrn_slabs)
    late_gates = gate_slabs[LOOP_GATE_SLABS:]
    gate_slabs = gate_slabs[:LOOP_GATE_SLABS]
    pending = scores(*order[0])
    ready_pairs = 0
    for idx, (n, slab) in enumerate(order):
        nxt = scores(*order[idx + 1]) if idx + 1 < len(order) else None
        if idx < early:
            hgrn_slabs[idx]()
        else:
            if idx == early:
                _hgrn_log_decay(lf_ref, b_ref, n_chunks)
            if gate_slabs:
                gate_slabs.pop(0)()
            for _ in range(min(len(pair_jobs), ready_pairs)):
                pair_jobs.pop(0)()
            ready_pairs = 0
            for _ in range(-(-len(jobs) // (len(order) - 1 - idx)) if idx + 1 < len(order) else 0):
                if jobs:
                    jobs.pop(0)()
                    ready_pairs += 1
        attend(n, slab, pending)
        pending = nxt
    kbuf[0:WINDOW, :] = kbuf[rows:rows + WINDOW, :]
    vbuf[0:WINDOW, :] = vbuf[rows:rows + WINDOW, :]
    assert not jobs and not hgrn_slabs[early:]
    late = gate_slabs + late_gates
    for run in pair_jobs:
        run()

    nk_ref[0] = ka[rows - WINDOW:rows, :]
    nv_ref[0] = va[rows - WINDOW:rows, :]

    hgn = hgn_ref[...]
    st0 = [st_ref[hd * HG_D:(hd + 1) * HG_D, :] for hd in range(HG_HEADS)]
    finish = _hgrn_finish(qh_ref, kh_ref, vh_ref, b_ref, in_chunk, qe_ref, ke_ref, st0, n_chunks)
    while True:
        try:
            next(finish)
        except StopIteration as done:
            outs, states = done.value
            break
        if late:
            late.pop(0)()
    for run in late:
        run()
    for hd in range(HG_HEADS):
        cs = slice(hd * HG_D, (hd + 1) * HG_D)
        st_ref[cs, :] = states[hd]
        ns_ref[0, cs, :] = states[hd].T
        for ci in range(n_chunks):
            oh_ref[ci * HG_CHUNK:(ci + 1) * HG_CHUNK, cs] = _rms(outs[ci][hd], hgn)

    gh = g_ref[:, 0:HG_W]
    oh = oh_ref[...] * (gh * jax.nn.sigmoid(gh))

    def gate(off, width):
        return g_ref[:, off - OFF_GH:off - OFF_GH + width]
    m = (_merge_out(oa_ref[...], wao_ref, OFF_GA, gate)
         + _merge_out(oh, who_ref, OFF_GB, gate))
    mb = m.astype(BF16)
    for r in part_rows:
        y_ref[r, :] = x_ref[r, :] + _rms(_dot(mb[r, :], wo_ref[...]), post_ref[...])


def _mix_prompt_call(x, batch, sinks, pre, post, lbl, hgn, win, wao, who, wo):
    seq = x.shape[0] // batch
    rows = min(MIX_ROWS, seq)
    steps = seq // rows
    x_spec = pl.BlockSpec((rows, D_MODEL), lambda b, j: (b * steps + j, 0))
    kv_spec = pl.BlockSpec((1, WINDOW, ATTN_KV_W), lambda b, j: (b, 0, 0))
    st_spec = pl.BlockSpec((1, HG_W, HG_D), lambda b, j: (b, 0, 0))
    act = pltpu.VMEM((rows, HG_W), F32)
    return pl.pallas_call(
        _mix_prompt_kernel,
        grid=(batch, steps),
        in_specs=[pl.BlockSpec(memory_space=pltpu.SMEM), x_spec,
                  _const_spec((1, D_MODEL)), _const_spec((1, D_MODEL)),
                  _const_spec(lbl.shape), _const_spec((1, HG_D)),
                  _const_spec((HG_CHUNK, HG_CHUNK)),
                  HBM_SPEC, HBM_SPEC, HBM_SPEC, HBM_SPEC],
        out_specs=[x_spec, kv_spec, kv_spec, st_spec],
        out_shape=[jax.ShapeDtypeStruct(x.shape, F32),
                   jax.ShapeDtypeStruct((batch, WINDOW, ATTN_KV_W), F32),
                   jax.ShapeDtypeStruct((batch, WINDOW, ATTN_KV_W), F32),
                   jax.ShapeDtypeStruct((batch, HG_W, HG_D), F32)],
        scratch_shapes=[pltpu.VMEM(win.shape, BF16), pltpu.VMEM(wao.shape, BF16),
                        pltpu.VMEM(who.shape, BF16), pltpu.VMEM(wo.shape, BF16),
                        pltpu.VMEM((rows + WINDOW, ATTN_KV_W), F32),
                        pltpu.VMEM((rows + WINDOW, ATTN_KV_W), F32),
                        pltpu.VMEM((HG_W, HG_D), F32),
                        act, act, act, act,
                        pltpu.VMEM((rows, ATTN_Q_W), F32), act,
                        pltpu.VMEM((rows, IN_WIDTH - OFF_GH), F32),
                        act, pltpu.VMEM((HG_LEVELS, rows, HG_W), BF16),
                        pltpu.VMEM((rows, HG_W), BF16), pltpu.VMEM((rows, HG_W), BF16)],
        compiler_params=pltpu.CompilerParams(
            dimension_semantics=("arbitrary", "arbitrary"), vmem_limit_bytes=VMEM_LIMIT),
        name="mix_prompt",
    )(sinks, x, pre, post, lbl, hgn, _pair_levels(HG_CHUNK), win, wao, who, wo)


def _row_select(rows_list):
    n = -(-len(rows_list) // 8) * 8
    lanes = rows_list[0].shape[1]
    ridx = lax.broadcasted_iota(jnp.int32, (n, lanes), 0)
    out = jnp.zeros((n, lanes), F32)
    for i, r in enumerate(rows_list):
        out = jnp.where(ridx == i, jnp.broadcast_to(r, (n, lanes)), out)
    return out


def _mix_sample_kernel(sinks_ref, x_ref, ck_ref, cv_ref, s0_ref, pre_ref, post_ref, lbl_ref, hgn_ref,
                       win_hbm, wao_hbm, who_hbm, wo_hbm,
                       y_ref, nk_ref, nv_ref, ns_ref,
                       win_ref, wao_ref, who_ref, wo_ref,
                       h_ref, qa_ref, ka_ref, va_ref, qh_ref, kh_ref, fh_ref, vh_ref, oa_ref, oh_ref):
    i = pl.program_id(0)
    tile = ck_ref.shape[0]

    def proj(off, width):
        return _dot(h_ref[...], win_ref[:, off:off + width])

    @pl.when(i == 0)
    def _():
        for src, dst in ((win_hbm, win_ref), (wao_hbm, wao_ref), (who_hbm, who_ref),
                         (wo_hbm, wo_ref)):
            _stream_cast(src, dst)
        h_ref[...] = _rms(x_ref[...], pre_ref[...]).astype(BF16)
        qa_ref[...] = proj(OFF_QA, ATTN_Q_W) * (HEAD_DIM ** -0.5)
        ka_ref[...] = proj(OFF_KA, ATTN_KV_W)
        va_ref[...] = proj(OFF_VA, ATTN_KV_W)
        lb = _forget_lower_bound(lbl_ref[...])
        qh = proj(OFF_QH, HG_W)
        qh_ref[...] = qh * jax.nn.sigmoid(qh)
        fp = proj(OFF_FH, HG_W)
        fh_ref[...] = jnp.exp(jnp.log(lb + (1.0 - lb) * jax.nn.sigmoid(fp)))
        kh_ref[...] = (1.0 - lb) * jax.nn.sigmoid(-fp)
        vh_ref[...] = proj(OFF_IH, HG_W)

    low = _lane_is_low((1, LANES))
    newest = lax.broadcasted_iota(jnp.int32, (ATTN_KV_W, WINDOW), 1) == WINDOW - 1
    hrow = lax.broadcasted_iota(jnp.int32, (ATTN_Q_HEADS, 1), 0)
    sk = jnp.zeros((ATTN_Q_HEADS, 1), F32)
    for hd in range(ATTN_Q_HEADS):
        sk = jnp.where(hrow == hd, sinks_ref[0, hd], sk)
    group = ATTN_Q_HEADS // ATTN_KV_HEADS
    hgn = hgn_ref[...]

    r8 = pl.ds(pl.multiple_of(i * tile, tile), tile)
    qa8, ka8, va8 = qa_ref[r8, :], ka_ref[r8, :], va_ref[r8, :]
    fh8, kh8, qh8, vh8 = fh_ref[r8, :], kh_ref[r8, :], qh_ref[r8, :], vh_ref[r8, :]
    oa_rows = [[] for _ in range(ATTN_Q_HEADS // 2)]
    oh_rows = [[] for _ in range(HG_HEADS)]
    seqs = range(tile)
    def as_columns(rows8):
        pad = jnp.zeros((LANES - tile, rows8.shape[1]), F32)
        return jnp.concatenate([rows8, pad], axis=0).T
    k_cols, v_cols = as_columns(ka8), as_columns(va8)
    kws, vws, q8s = [], [], []
    for bi in seqs:
        r1 = slice(bi, bi + 1)
        kw = jnp.where(newest, pltpu.roll(k_cols, WINDOW - 1 - bi, axis=1),
                       pltpu.roll(ck_ref[bi], WINDOW - 1, axis=1))
        vw = jnp.where(newest, pltpu.roll(v_cols, WINDOW - 1 - bi, axis=1),
                       pltpu.roll(cv_ref[bi], WINDOW - 1, axis=1))
        nk_ref[bi] = kw
        nv_ref[bi] = vw
        kws.append(kw.astype(BF16))
        vws.append(vw.astype(BF16))
        qrows = []
        for hd in range(ATTN_Q_HEADS):
            slab = qa8[r1, (hd // 2) * LANES:(hd // 2 + 1) * LANES]
            in_place = (hd % 2) == (hd // group)
            src = slab if in_place else pltpu.roll(slab, HEAD_DIM, axis=1)
            on_kv_lanes = low if hd // group == 0 else jnp.logical_not(low)
            qrows.append(jnp.where(on_kv_lanes, src, 0.0))
        q8s.append(_row_select(qrows).astype(BF16))
    scores = [_dot(q8s[bi], kws[bi]) for bi in seqs]
    probs = []
    for s in scores:
        m = jnp.maximum(jnp.max(s, axis=1, keepdims=True), sk)
        p = jnp.exp(s - m)
        probs.append((p / (jnp.sum(p, axis=1, keepdims=True) + jnp.exp(sk - m))).astype(BF16))
    for bi in seqs:
        o8 = lax.dot_general(probs[bi], vws[bi], NT_DIMS, preferred_element_type=F32)
        for slab in range(ATTN_Q_HEADS // 2):
            kv = (2 * slab) // group
            even, odd = o8[2 * slab:2 * slab + 1, :], o8[2 * slab + 1:2 * slab + 2, :]
            if kv == 0:
                out = jnp.where(low, even, pltpu.roll(odd, HEAD_DIM, axis=1))
            else:
                out = jnp.where(low, pltpu.roll(even, HEAD_DIM, axis=1), odd)
            oa_rows[slab].append(out)

    def columns(a8, hd):
        pad = jnp.zeros((HG_D - tile, HG_D), F32)
        return jnp.concatenate([_head(a8, hd), pad], axis=0).T
    seq_row = lax.broadcasted_iota(jnp.int32, (HG_D, HG_D), 0)
    heads = range(HG_HEADS)
    head_rows = [slice(hd * HG_D, (hd + 1) * HG_D) for hd in heads]
    f_cols = [columns(fh8, hd) for hd in heads]
    k_cols = [columns(kh8, hd).astype(BF16) for hd in heads]
    v_rows = [jnp.concatenate([_head(vh8, hd), jnp.zeros((HG_D - tile, HG_D), F32)], axis=0)
              for hd in heads]
    outers = [[_dot(k_cols[hd], jnp.where(seq_row == bi, v_rows[hd], 0.0).astype(BF16))
               for bi in seqs] for hd in heads]
    for hd in heads:
        for bi in seqs:
            f_col = jnp.broadcast_to(f_cols[hd][:, bi:bi + 1], (HG_D, HG_D))
            ns_ref[bi, head_rows[hd], :] = f_col * s0_ref[bi, head_rows[hd], :] + outers[hd][bi]
    outs = [[_dot(_head(qh8, hd).astype(BF16), ns_ref[bi, head_rows[hd], :].astype(BF16))
             for bi in seqs] for hd in heads]
    for hd in heads:
        for bi in seqs:
            oh_rows[hd].append(_rms(outs[hd][bi][bi:bi + 1, :], hgn))
    for slab in range(ATTN_Q_HEADS // 2):
        oa_ref[r8, slab * LANES:(slab + 1) * LANES] = _row_select(oa_rows[slab])
    for hd in range(HG_HEADS):
        oh_ref[r8, hd * HG_D:(hd + 1) * HG_D] = _row_select(oh_rows[hd])

    @pl.when(i == pl.num_programs(0) - 1)
    def _():
        gh = proj(OFF_GH, HG_W)
        oh = oh_ref[...] * (gh * jax.nn.sigmoid(gh))
        m = (_merge_out(oa_ref[...], wao_ref, OFF_GA, proj)
             + _merge_out(oh, who_ref, OFF_GB, proj))
        y_ref[...] = x_ref[...] + _rms(_dot(m.astype(BF16), wo_ref[...]), post_ref[...])


def _mix_sample_call(x, ck, cv, s0, sinks, pre, post, lbl, hgn, win, wao, who, wo):
    n = x.shape[0]
    tile = min(SAMPLE_TILE, n)
    full = pl.BlockSpec((n, D_MODEL), lambda i: (0, 0))
    kv_spec = pl.BlockSpec((tile, ATTN_KV_W, WINDOW), lambda i: (i, 0, 0))
    st_spec = pl.BlockSpec((tile, HG_W, HG_D), lambda i: (i, 0, 0))
    act = pltpu.VMEM((n, HG_W), F32)
    kv_act = pltpu.VMEM((n, ATTN_KV_W), F32)
    return pl.pallas_call(
        _mix_sample_kernel,
        grid=(n // tile,),
        in_specs=[pl.BlockSpec(memory_space=pltpu.SMEM), full, kv_spec, kv_spec, st_spec,
                  _const_spec((1, D_MODEL)), _const_spec((1, D_MODEL)),
                  _const_spec(lbl.shape), _const_spec((1, HG_D)),
                  HBM_SPEC, HBM_SPEC, HBM_SPEC, HBM_SPEC],
        out_specs=[full, kv_spec, kv_spec, st_spec],
        out_shape=[jax.ShapeDtypeStruct(x.shape, F32),
                   jax.ShapeDtypeStruct(ck.shape, F32),
                   jax.ShapeDtypeStruct(cv.shape, F32),
                   jax.ShapeDtypeStruct(s0.shape, F32)],
        scratch_shapes=[pltpu.VMEM(win.shape, BF16), pltpu.VMEM(wao.shape, BF16),
                        pltpu.VMEM(who.shape, BF16), pltpu.VMEM(wo.shape, BF16),
                        pltpu.VMEM((n, D_MODEL), BF16),
                        pltpu.VMEM((n, ATTN_Q_W), F32), kv_act, kv_act,
                        act, act, act, act,
                        pltpu.VMEM((n, ATTN_Q_W), F32), act],
        compiler_params=pltpu.CompilerParams(
            dimension_semantics=("arbitrary",), vmem_limit_bytes=VMEM_LIMIT),
        name="mix_sample",
    )(sinks, x, ck, cv, s0, pre, post, lbl, hgn, win, wao, who, wo)


def kernel(x_prompt, x_sample, cache_k, cache_v, state_hgrn, norm_ffn1_pre, norm_ffn1_post, w_ffn1_gate, w_ffn1_up, w_ffn1_down, norm_mix_pre, norm_mix_post, w_in, attn_sinks, hgrn_lb_logits, hgrn_norm, w_attn_out, w_hgrn_out, w_out, norm_ffn2_pre, norm_ffn2_post, w_ffn2_gate, w_ffn2_up, w_ffn2_down):
    depth = w_in.shape[0]
    assert depth == 1 and hgrn_lb_logits.shape[0] == 2, "single-layer stack only"
    batch, seq, _ = x_prompt.shape
    n_s = x_sample.shape[0]
    assert x_sample.shape[1] == 1 and seq % WINDOW == 0

    xp = x_prompt.reshape(batch * seq, D_MODEL)
    xs = x_sample.reshape(n_s, D_MODEL)
    ck = jnp.swapaxes(cache_k[0].reshape(n_s, WINDOW, ATTN_KV_W), 1, 2)
    cv = jnp.swapaxes(cache_v[0].reshape(n_s, WINDOW, ATTN_KV_W), 1, 2)
    s0 = state_hgrn[0].reshape(n_s, HG_W, HG_D)

    xp, xs = _ffn_call(xp, xs, norm_ffn1_pre, norm_ffn1_post,
                       w_ffn1_gate[0], w_ffn1_up[0], w_ffn1_down[0])

    mix_w = (attn_sinks, norm_mix_pre, norm_mix_post, hgrn_lb_logits, hgrn_norm,
             w_in[0], w_attn_out[0], w_hgrn_out[0], w_out[0])
    xp, nkp, nvp, nsp = _mix_prompt_call(xp, batch, *mix_w)
    xs, nks, nvs, nss = _mix_sample_call(xs, ck, cv, s0, *mix_w)

    xp, xs = _ffn_call(xp, xs, norm_ffn2_pre, norm_ffn2_post,
                       w_ffn2_gate[0], w_ffn2_up[0], w_ffn2_down[0])

    kv_shape = (1, -1, WINDOW, ATTN_KV_HEADS, HEAD_DIM)
    st_shape = (1, -1, HG_HEADS, HG_D, HG_D)
    return (xp.reshape(batch, seq, D_MODEL), xs.reshape(n_s, 1, D_MODEL),
            nkp.reshape(kv_shape), nvp.reshape(kv_shape), nsp.reshape(st_shape),
            jnp.swapaxes(nks, 1, 2).reshape(kv_shape), jnp.swapaxes(nvs, 1, 2).reshape(kv_shape),
            nss.reshape(st_shape))
```

```python
import functools

import jax
import jax.numpy as jnp
from jax import lax
from jax.experimental import pallas as pl
from jax.experimental.pallas import tpu as pltpu

F32 = jnp.float32
BF16 = jnp.bfloat16

D_MODEL = 1024
FFN_DIM = 2816
HEAD_DIM = 64
ATTN_Q_HEADS = 8
ATTN_KV_HEADS = 2
WINDOW = 128
HG_HEADS = 4
HG_D = 128
EPS = 1e-6

ATTN_Q_W = ATTN_Q_HEADS * HEAD_DIM
ATTN_KV_W = ATTN_KV_HEADS * HEAD_DIM
HG_W = HG_HEADS * HG_D
OFF_QA = 0
OFF_KA = OFF_QA + ATTN_Q_W
OFF_VA = OFF_KA + ATTN_KV_W
OFF_QH = OFF_VA + ATTN_KV_W
OFF_FH = OFF_QH + HG_W
OFF_IH = OFF_FH + HG_W
OFF_GH = OFF_IH + HG_W
OFF_GA = OFF_GH + HG_W
OFF_GB = OFF_GA + D_MODEL
IN_WIDTH = OFF_GB + D_MODEL

LANES = 128
SUBLANES = 8
FFN_CHUNK = 256
N_FFN_CHUNKS = FFN_DIM // FFN_CHUNK
FFN_ROWS = 1024
MIX_ROWS = 512
NORM_PARTS = 2
HG_CHUNK = 128
SAMPLE_TILE = 16
VMEM_LIMIT = 56 * 1024 * 1024

NT_DIMS = (((1,), (1,)), ((), ()))
TN_DIMS = (((0,), (0,)), ((), ()))


def _rms(x, g):
    return x * lax.rsqrt(jnp.mean(x * x, axis=-1, keepdims=True) + EPS) * g


def _dot(a, b):
    return jnp.dot(a, b, preferred_element_type=F32)


STREAM_SLOTS = 4
STREAM_CHUNK_BYTES = 2 << 20


def _stream_cast(src_hbm, dst_ref):
    n_rows, n_cols = src_hbm.shape
    row_bytes = n_cols * jnp.dtype(F32).itemsize
    rc = max(SUBLANES, min(n_rows, STREAM_CHUNK_BYTES // row_bytes // SUBLANES * SUBLANES))
    while n_rows % rc:
        rc -= SUBLANES
    n = n_rows // rc
    slots = min(STREAM_SLOTS, n)

    def body(stage, sem):
        def copy(c):
            return pltpu.make_async_copy(src_hbm.at[pl.ds(c * rc, rc), :], stage.at[c % slots],
                                         sem.at[c % slots])
        for c in range(slots):
            copy(c).start()
        for c in range(n):
            copy(c).wait()
            dst_ref[c * rc:(c + 1) * rc, :] = stage[c % slots].astype(BF16)
            if c + slots < n:
                copy(c + slots).start()

    pl.run_scoped(body, pltpu.VMEM((slots, rc, n_cols), F32), pltpu.SemaphoreType.DMA((slots,)))


HBM_SPEC = pl.BlockSpec(memory_space=pl.ANY)


def _const_spec(shape):
    zeros = (0,) * len(shape)
    return pl.BlockSpec(shape, lambda *_: zeros, pipeline_mode=pl.Buffered(1))


def _ffn_tile(x, pre, post, wg_ref, wu_ref, wd_ref):
    h = _rms(x, pre).astype(BF16)
    acc = None
    for c in range(N_FFN_CHUNKS):
        cols = slice(c * FFN_CHUNK, (c + 1) * FFN_CHUNK)
        g = _dot(h, wg_ref[:, cols])
        u = _dot(h, wu_ref[:, cols])
        a = (g * jax.nn.sigmoid(g) * u).astype(BF16)
        d = _dot(a, wd_ref[cols, :])
        acc = d if acc is None else acc + d
    return x + 0.5 * _rms(acc, post)


def _ffn_kernel(n_prompt_steps, xp_ref, xs_ref, pre_ref, post_ref, wg_hbm, wu_hbm, wd_hbm,
                yp_ref, ys_ref, wg_ref, wu_ref, wd_ref):
    i = pl.program_id(0)

    @pl.when(i == 0)
    def _():
        _stream_cast(wg_hbm, wg_ref)
        _stream_cast(wu_hbm, wu_ref)
        _stream_cast(wd_hbm, wd_ref)

    @pl.when(i < n_prompt_steps)
    def _():
        yp_ref[...] = _ffn_tile(xp_ref[...], pre_ref[...], post_ref[...], wg_ref, wu_ref, wd_ref)

    @pl.when(i == n_prompt_steps)
    def _():
        ys_ref[...] = _ffn_tile(xs_ref[...], pre_ref[...], post_ref[...], wg_ref, wu_ref, wd_ref)


def _ffn_call(xp, xs, pre, post, wg, wu, wd):
    n_p, n_s = xp.shape[0], xs.shape[0]
    rows = min(FFN_ROWS, n_p)
    steps = n_p // rows
    prompt_spec = pl.BlockSpec((rows, D_MODEL), lambda i: (jnp.minimum(i, steps - 1), 0))
    sample_spec = pl.BlockSpec((n_s, D_MODEL), lambda i: (0, 0))
    return pl.pallas_call(
        functools.partial(_ffn_kernel, steps),
        grid=(steps + 1,),
        in_specs=[prompt_spec, sample_spec,
                  _const_spec((1, D_MODEL)), _const_spec((1, D_MODEL)),
                  HBM_SPEC, HBM_SPEC, HBM_SPEC],
        out_specs=[prompt_spec, sample_spec],
        out_shape=[jax.ShapeDtypeStruct(xp.shape, F32), jax.ShapeDtypeStruct(xs.shape, F32)],
        scratch_shapes=[pltpu.VMEM(wg.shape, BF16), pltpu.VMEM(wu.shape, BF16),
                        pltpu.VMEM(wd.shape, BF16)],
        compiler_params=pltpu.CompilerParams(
            dimension_semantics=("arbitrary",), vmem_limit_bytes=VMEM_LIMIT),
        name="ffn_half",
    )(xp, xs, pre, post, wg, wu, wd)


def _forget_lower_bound(lbl):
    l0, l1 = lbl[0:1, :], lbl[1:2, :]
    m = jnp.maximum(l0, l1)
    e0, e1 = jnp.exp(l0 - m), jnp.exp(l1 - m)
    return e0 / (e0 + e1)


def _lane_is_low(shape):
    return lax.broadcasted_iota(jnp.int32, shape, len(shape) - 1) < HEAD_DIM


def _dup_kv(x):
    swapped = pltpu.roll(x, HEAD_DIM, axis=1)
    low = _lane_is_low(x.shape)
    return jnp.where(low, x, swapped), jnp.where(low, swapped, x)


def _split3(x):
    hi = x.astype(BF16)
    rest = x - hi.astype(F32)
    mid = rest.astype(BF16)
    return hi, mid, (rest - mid.astype(F32)).astype(BF16)


def _merge_out(h, w_ref, off, gates_from):
    return jax.nn.sigmoid(gates_from(off, D_MODEL)) * _dot(h.astype(BF16), w_ref[...])


def _pair_levels(c):
    t = jnp.arange(c, dtype=jnp.int32)[:, None]
    s = jnp.arange(c, dtype=jnp.int32)[None, :]
    x = jnp.maximum(t ^ s, 1)
    lvl = (31 - lax.clz(x)).astype(jnp.int32)
    return jnp.where(t > s, lvl, -1)


LOG2E = 1.4426950408889634


def _head(a, hd):
    return a[:, hd * HG_D:(hd + 1) * HG_D]


HG_LEVELS = HG_CHUNK.bit_length() - 1


def _chunk_rows(ci):
    return slice(ci * HG_CHUNK, (ci + 1) * HG_CHUNK)


def _hgrn_log_decay(g_ref, b_ref, n_chunks):
    c = HG_CHUNK
    width = g_ref.shape[1]
    row = lax.broadcasted_iota(jnp.int32, (c, c), 0)
    col = lax.broadcasted_iota(jnp.int32, (c, c), 1)
    tril = (col <= row).astype(BF16)
    for ci in range(n_chunks):
        r = _chunk_rows(ci)
        sums = _dot(tril, jnp.concatenate(_split3(g_ref[r, :]), axis=1))
        b_ref[r, :] = (sums[:, 0:width] + sums[:, width:2 * width] + sums[:, 2 * width:]) * LOG2E


def _hgrn_factor_jobs(q_ref, k_ref, b_ref, x_ref, qe_ref, ke_ref, n_chunks):
    c = HG_CHUNK
    t = lax.broadcasted_iota(jnp.int32, (c, 1), 0)
    last = {}

    def level_job(ci, lvl):
        def run():
            n = 1 << lvl
            r = _chunk_rows(ci)
            bc = b_ref[r, :]
            second = (t & n) != 0
            if n < SUBLANES:
                prev = last.get(ci, bc)
                tiles = prev.reshape(c // SUBLANES, SUBLANES, prev.shape[1])
                second3 = second.reshape(c // SUBLANES, SUBLANES, 1)
                edge = jnp.where(second3, pltpu.roll(tiles, n, axis=1), tiles).reshape(prev.shape)
                if 2 * n < SUBLANES:
                    last[ci] = jnp.where(second3, tiles, pltpu.roll(tiles, SUBLANES - n, axis=1)
                                         ).reshape(prev.shape)
            else:
                edge = jnp.concatenate(
                    [jnp.broadcast_to(bc[p + n - 1:p + n, :], (2 * n, bc.shape[1]))
                     for p in range(0, c, 2 * n)], axis=0)
            w = jnp.exp2(jnp.where(second, bc - edge, edge - bc))
            x_ref[lvl, r, :] = (jnp.where(second, q_ref[r, :], k_ref[r, :]) * w).astype(BF16)
        return run

    def state_job(ci):
        def run():
            r = _chunk_rows(ci)
            bc = b_ref[r, :]
            qe_ref[r, :] = (q_ref[r, :] * jnp.exp2(bc)).astype(BF16)
            ke_ref[r, :] = (k_ref[r, :] * jnp.exp2(bc[c - 1:c, :] - bc)).astype(BF16)
        return run

    jobs = [level_job(ci, lvl) for lvl in range(HG_LEVELS) for ci in range(n_chunks)]
    return jobs + [state_job(ci) for ci in range(n_chunks)]


def _hgrn_pair_jobs(x_ref, levels, n_chunks):
    c = HG_CHUNK
    heads, chunks = range(HG_HEADS), range(n_chunks)
    a = [[jnp.zeros((c, c), F32) for _ in heads] for _ in chunks]

    def job(lvl, ci):
        def run():
            x = x_ref[lvl, _chunk_rows(ci), :]
            for hd in heads:
                al = lax.dot_general(_head(x, hd), _head(x, hd), NT_DIMS,
                                     preferred_element_type=F32)
                a[ci][hd] = jnp.where(levels == lvl, al, a[ci][hd])
        return run
    return [job(lvl, ci) for lvl in range(HG_LEVELS) for ci in chunks], a


def _hgrn_finish(q_ref, k_ref, v_ref, b_ref, a, qe_ref, ke_ref, st, n_chunks):
    c = HG_CHUNK
    heads, chunks = range(HG_HEADS), range(n_chunks)
    local = []
    for ci in chunks:
        r = _chunk_rows(ci)
        q, k, v = q_ref[r, :], k_ref[r, :], v_ref[r, :]
        qk, vb = q * k, v.astype(BF16)
        outs = []
        for hd in heads:
            diag = jnp.sum(_head(qk, hd), axis=1, keepdims=True)
            outs.append(_dot(a[ci][hd].astype(BF16), _head(vb, hd)) + diag * _head(v, hd))
        local.append(outs)

    result = []
    for ci in chunks:
        r = _chunk_rows(ci)
        qe, ke, vb = qe_ref[r, :], ke_ref[r, :], v_ref[r, :].astype(BF16)
        decay = jnp.exp2(b_ref[ci * c + c - 1:ci * c + c, :])
        outs, new = [], []
        for hd in heads:
            outs.append(local[ci][hd] + lax.dot_general(_head(qe, hd), st[hd].astype(BF16), NT_DIMS,
                                                        preferred_element_type=F32))
            new.append(st[hd] * _head(decay, hd)
                       + lax.dot_general(_head(vb, hd), _head(ke, hd), TN_DIMS,
                                         preferred_element_type=F32))
        st = new
        result.append(outs)
    return result, st


def _mix_prompt_kernel(sinks_ref, x_ref, pre_ref, post_ref, lbl_ref, hgn_ref, lvl_ref,
                       win_hbm, wao_hbm, who_hbm, wo_hbm,
                       y_ref, nk_ref, nv_ref, ns_ref,
                       win_ref, wao_ref, who_ref, wo_ref,
                       kbuf, vbuf, st_ref, qh_ref, kh_ref, vh_ref, lf_ref, oa_ref, oh_ref, g_ref,
                       b_ref, xl_ref, qe_ref, ke_ref):
    j = pl.program_id(1)
    rows = x_ref.shape[0]
    n_blocks = rows // WINDOW

    @pl.when((pl.program_id(0) == 0) & (j == 0))
    def _():
        for src, dst in ((win_hbm, win_ref), (wao_hbm, wao_ref), (who_hbm, who_ref),
                         (wo_hbm, wo_ref)):
            _stream_cast(src, dst)

    @pl.when(j == 0)
    def _():
        kbuf[0:WINDOW, :] = jnp.zeros((WINDOW, LANES), F32)
        vbuf[0:WINDOW, :] = jnp.zeros((WINDOW, LANES), F32)
        st_ref[...] = jnp.zeros(st_ref.shape, F32)

    part_rows = [slice(r0, r0 + rows // NORM_PARTS) for r0 in range(0, rows, rows // NORM_PARTS)]
    h_parts, qkv_parts = [], []
    for r in part_rows:
        hp = _rms(x_ref[r, :], pre_ref[...]).astype(BF16)
        h_parts.append(hp)
        qkv_parts.append(_dot(hp, win_ref[:, OFF_QA:OFF_QH]))
    h = jnp.concatenate(h_parts, axis=0)
    qkv = jnp.concatenate(qkv_parts, axis=0)

    def proj(off, width):
        return _dot(h, win_ref[:, off:off + width])

    lb = _forget_lower_bound(lbl_ref[...])
    n_chunks = rows // HG_CHUNK

    def proj_slab(off):
        def run():
            z = proj(off, FFN_CHUNK)
            if off < OFF_FH:
                cols = slice(off - OFF_QH, off - OFF_QH + FFN_CHUNK)
                qh_ref[:, cols] = z * jax.nn.sigmoid(z)
            elif off < OFF_IH:
                cols = slice(off - OFF_FH, off - OFF_FH + FFN_CHUNK)
                lbs = lb[:, cols]
                lf_ref[:, cols] = jnp.log(lbs + (1.0 - lbs) * jax.nn.sigmoid(z))
                kh_ref[:, cols] = (1.0 - lbs) * jax.nn.sigmoid(-z)
            elif off < OFF_GH:
                cols = slice(off - OFF_IH, off - OFF_IH + FFN_CHUNK)
                vh_ref[:, cols] = z
            else:
                cols = slice(off - OFF_GH, off - OFF_GH + FFN_CHUNK)
                g_ref[:, cols] = z
        return run
    hgrn_slabs = [proj_slab(off) for off in range(OFF_QH, OFF_GH, FFN_CHUNK)]
    gate_slabs = [proj_slab(off) for off in range(OFF_GH, IN_WIDTH, FFN_CHUNK)]
    jobs = _hgrn_factor_jobs(qh_ref, kh_ref, b_ref, xl_ref, qe_ref, ke_ref, n_chunks)
    pair_jobs, in_chunk = _hgrn_pair_jobs(xl_ref, lvl_ref[...], n_chunks)

    qa = qkv[:, OFF_QA:OFF_KA] * (HEAD_DIM ** -0.5)
    ka = qkv[:, OFF_KA:OFF_VA]
    va = qkv[:, OFF_VA:OFF_QH]
    kbuf[WINDOW:WINDOW + rows, :] = ka
    vbuf[WINDOW:WINDOW + rows, :] = va

    qi = lax.broadcasted_iota(jnp.int32, (WINDOW, 2 * WINDOW), 0)
    kj = lax.broadcasted_iota(jnp.int32, (WINDOW, 2 * WINDOW), 1)
    band = (kj > qi) & (kj <= qi + WINDOW)
    low = _lane_is_low((WINDOW, LANES))
    top = lax.broadcasted_iota(jnp.int32, (2 * WINDOW, 1), 0) < WINDOW
    n_slabs = ATTN_Q_HEADS // 2
    group = ATTN_Q_HEADS // ATTN_KV_HEADS
    kds = [_dup_kv(kbuf[n * WINDOW:(n + 2) * WINDOW, :]) for n in range(n_blocks)]
    vds = [_dup_kv(vbuf[n * WINDOW:(n + 2) * WINDOW, :]) for n in range(n_blocks)]

    def scores(n, slab):
        qs = qa[n * WINDOW:(n + 1) * WINDOW, slab * LANES:(slab + 1) * LANES]
        q2 = jnp.concatenate([jnp.where(low, qs, 0.0), jnp.where(low, 0.0, qs)], axis=0)
        return lax.dot_general(q2.astype(BF16), kds[n][(2 * slab) // group].astype(BF16), NT_DIMS,
                               preferred_element_type=F32)

    def attend(n, slab, s):
        valid = band & ((j > 0) | (kj >= WINDOW)) if n == 0 else band
        s = jnp.where(jnp.concatenate([valid, valid], axis=0), s, -jnp.inf)
        sk = jnp.where(top, sinks_ref[0, 2 * slab], sinks_ref[0, 2 * slab + 1])
        m = jnp.maximum(jnp.max(s, axis=1, keepdims=True), sk)
        p = jnp.exp(s - m)
        den = jnp.sum(p, axis=1, keepdims=True) + jnp.exp(sk - m)
        o2 = _dot(p.astype(BF16), vds[n][(2 * slab) // group].astype(BF16)) / den
        oa_ref[n * WINDOW:(n + 1) * WINDOW, slab * LANES:(slab + 1) * LANES] = (
            jnp.where(low, o2[0:WINDOW], o2[WINDOW:2 * WINDOW]))

    order = [(n, slab) for n in range(n_blocks) for slab in range(n_slabs)]
    early = len(hgrn_slabs)
    assert len(order) >= early + len(gate_slabs)
    pending = scores(*order[0])
    ready_pairs = 0
    for idx, (n, slab) in enumerate(order):
        nxt = scores(*order[idx + 1]) if idx + 1 < len(order) else None
        if idx < early:
            hgrn_slabs[idx]()
        else:
            if idx == early:
                _hgrn_log_decay(lf_ref, b_ref, n_chunks)
            if gate_slabs:
                gate_slabs.pop(0)()
            for _ in range(min(len(pair_jobs), ready_pairs)):
                pair_jobs.pop(0)()
            ready_pairs = 0
            for _ in range(-(-len(jobs) // (len(order) - 1 - idx)) if idx + 1 < len(order) else 0):
                if jobs:
                    jobs.pop(0)()
                    ready_pairs += 1
        attend(n, slab, pending)
        pending = nxt
    kbuf[0:WINDOW, :] = kbuf[rows:rows + WINDOW, :]
    vbuf[0:WINDOW, :] = vbuf[rows:rows + WINDOW, :]
    assert not jobs and not gate_slabs
    for run in pair_jobs:
        run()

    nk_ref[0] = ka[rows - WINDOW:rows, :]
    nv_ref[0] = va[rows - WINDOW:rows, :]

    hgn = hgn_ref[...]
    st0 = [st_ref[hd * HG_D:(hd + 1) * HG_D, :] for hd in range(HG_HEADS)]
    outs, states = _hgrn_finish(qh_ref, kh_ref, vh_ref, b_ref, in_chunk, qe_ref, ke_ref, st0,
                                n_chunks)
    for hd in range(HG_HEADS):
        cs = slice(hd * HG_D, (hd + 1) * HG_D)
        st_ref[cs, :] = states[hd]
        ns_ref[0, cs, :] = states[hd].T
        for ci in range(n_chunks):
            oh_ref[ci * HG_CHUNK:(ci + 1) * HG_CHUNK, cs] = _rms(outs[ci][hd], hgn)

    gh = g_ref[:, 0:HG_W]
    oh = oh_ref[...] * (gh * jax.nn.sigmoid(gh))

    def gate(off, width):
        return g_ref[:, off - OFF_GH:off - OFF_GH + width]
    m = (_merge_out(oa_ref[...], wao_ref, OFF_GA, gate)
         + _merge_out(oh, who_ref, OFF_GB, gate))
    mb = m.astype(BF16)
    for r in part_rows:
        y_ref[r, :] = x_ref[r, :] + _rms(_dot(mb[r, :], wo_ref[...]), post_ref[...])


def _mix_prompt_call(x, batch, sinks, pre, post, lbl, hgn, win, wao, who, wo):
    seq = x.shape[0] // batch
    rows = min(MIX_ROWS, seq)
    steps = seq // rows
    x_spec = pl.BlockSpec((rows, D_MODEL), lambda b, j: (b * steps + j, 0))
    kv_spec = pl.BlockSpec((1, WINDOW, ATTN_KV_W), lambda b, j: (b, 0, 0))
    st_spec = pl.BlockSpec((1, HG_W, HG_D), lambda b, j: (b, 0, 0))
    act = pltpu.VMEM((rows, HG_W), F32)
    return pl.pallas_call(
        _mix_prompt_kernel,
        grid=(batch, steps),
        in_specs=[pl.BlockSpec(memory_space=pltpu.SMEM), x_spec,
                  _const_spec((1, D_MODEL)), _const_spec((1, D_MODEL)),
                  _const_spec(lbl.shape), _const_spec((1, HG_D)),
                  _const_spec((HG_CHUNK, HG_CHUNK)),
                  HBM_SPEC, HBM_SPEC, HBM_SPEC, HBM_SPEC],
        out_specs=[x_spec, kv_spec, kv_spec, st_spec],
        out_shape=[jax.ShapeDtypeStruct(x.shape, F32),
                   jax.ShapeDtypeStruct((batch, WINDOW, ATTN_KV_W), F32),
                   jax.ShapeDtypeStruct((batch, WINDOW, ATTN_KV_W), F32),
                   jax.ShapeDtypeStruct((batch, HG_W, HG_D), F32)],
        scratch_shapes=[pltpu.VMEM(win.shape, BF16), pltpu.VMEM(wao.shape, BF16),
                        pltpu.VMEM(who.shape, BF16), pltpu.VMEM(wo.shape, BF16),
                        pltpu.VMEM((rows + WINDOW, ATTN_KV_W), F32),
                        pltpu.VMEM((rows + WINDOW, ATTN_KV_W), F32),
                        pltpu.VMEM((HG_W, HG_D), F32),
                        act, act, act, act,
                        pltpu.VMEM((rows, ATTN_Q_W), F32), act,
                        pltpu.VMEM((rows, IN_WIDTH - OFF_GH), F32),
                        act, pltpu.VMEM((HG_LEVELS, rows, HG_W), BF16),
                        pltpu.VMEM((rows, HG_W), BF16), pltpu.VMEM((rows, HG_W), BF16)],
        compiler_params=pltpu.CompilerParams(
            dimension_semantics=("arbitrary", "arbitrary"), vmem_limit_bytes=VMEM_LIMIT),
        name="mix_prompt",
    )(sinks, x, pre, post, lbl, hgn, _pair_levels(HG_CHUNK), win, wao, who, wo)


def _row_select(rows_list):
    n = -(-len(rows_list) // 8) * 8
    lanes = rows_list[0].shape[1]
    ridx = lax.broadcasted_iota(jnp.int32, (n, lanes), 0)
    out = jnp.zeros((n, lanes), F32)
    for i, r in enumerate(rows_list):
        out = jnp.where(ridx == i, jnp.broadcast_to(r, (n, lanes)), out)
    return out


def _spread_rows(rows, expand):
    terms = jnp.concatenate(_split3(rows), axis=0)
    return lax.dot_general(terms, expand, TN_DIMS, preferred_element_type=F32)


def _lane_selectors(tile):
    seq = jnp.arange(3 * tile, dtype=jnp.int32)[:, None] % tile
    lane = jnp.arange(tile * LANES, dtype=jnp.int32)[None, :]
    every = lane // LANES == seq
    return every.astype(BF16), (every & (lane % LANES == LANES - 1)).astype(BF16)


def _mix_sample_kernel(sinks_ref, x_ref, ck_ref, cv_ref, s0_ref, pre_ref, post_ref, lbl_ref, hgn_ref,
                       every_ref, last_ref,
                       win_hbm, wao_hbm, who_hbm, wo_hbm,
                       y_ref, nk_ref, nv_ref, ns_ref,
                       win_ref, wao_ref, who_ref, wo_ref,
                       h_ref, qa_ref, ka_ref, va_ref, qh_ref, kh_ref, fh_ref, vh_ref, oa_ref, oh_ref):
    i = pl.program_id(0)
    tile = ck_ref.shape[0]

    def proj(off, width):
        return _dot(h_ref[...], win_ref[:, off:off + width])

    @pl.when(i == 0)
    def _():
        for src, dst in ((win_hbm, win_ref), (wao_hbm, wao_ref), (who_hbm, who_ref),
                         (wo_hbm, wo_ref)):
            _stream_cast(src, dst)
        h_ref[...] = _rms(x_ref[...], pre_ref[...]).astype(BF16)
        qa_ref[...] = proj(OFF_QA, ATTN_Q_W) * (HEAD_DIM ** -0.5)
        ka_ref[...] = proj(OFF_KA, ATTN_KV_W)
        va_ref[...] = proj(OFF_VA, ATTN_KV_W)
        lb = _forget_lower_bound(lbl_ref[...])
        qh = proj(OFF_QH, HG_W)
        qh_ref[...] = qh * jax.nn.sigmoid(qh)
        fp = proj(OFF_FH, HG_W)
        fh_ref[...] = jnp.exp(jnp.log(lb + (1.0 - lb) * jax.nn.sigmoid(fp)))
        kh_ref[...] = (1.0 - lb) * jax.nn.sigmoid(-fp)
        vh_ref[...] = proj(OFF_IH, HG_W)

    low = _lane_is_low((1, LANES))
    newest = lax.broadcasted_iota(jnp.int32, (ATTN_KV_W, WINDOW), 1) == WINDOW - 1
    hrow = lax.broadcasted_iota(jnp.int32, (ATTN_Q_HEADS, 1), 0)
    sk = jnp.zeros((ATTN_Q_HEADS, 1), F32)
    for hd in range(ATTN_Q_HEADS):
        sk = jnp.where(hrow == hd, sinks_ref[0, hd], sk)
    group = ATTN_Q_HEADS // ATTN_KV_HEADS
    hgn = hgn_ref[...]

    r8 = pl.ds(pl.multiple_of(i * tile, tile), tile)
    qa8, ka8, va8 = qa_ref[r8, :], ka_ref[r8, :], va_ref[r8, :]
    fh8, kh8, qh8, vh8 = fh_ref[r8, :], kh_ref[r8, :], qh_ref[r8, :], vh_ref[r8, :]
    oa_rows = [[] for _ in range(ATTN_Q_HEADS // 2)]
    oh_rows = [[] for _ in range(HG_HEADS)]
    seqs = range(tile)
    k_new = _spread_rows(ka8, last_ref[...])
    v_new = _spread_rows(va8, last_ref[...])
    kws, vws, q8s = [], [], []
    for bi in seqs:
        r1 = slice(bi, bi + 1)
        block = slice(bi * WINDOW, (bi + 1) * WINDOW)
        kw = jnp.where(newest, k_new[:, block], pltpu.roll(ck_ref[bi], WINDOW - 1, axis=1))
        vw = jnp.where(newest, v_new[:, block], pltpu.roll(cv_ref[bi], WINDOW - 1, axis=1))
        nk_ref[bi] = kw
        nv_ref[bi] = vw
        kws.append(kw.astype(BF16))
        vws.append(vw.astype(BF16))
        qrows = []
        for hd in range(ATTN_Q_HEADS):
            slab = qa8[r1, (hd // 2) * LANES:(hd // 2 + 1) * LANES]
            in_place = (hd % 2) == (hd // group)
            src = slab if in_place else pltpu.roll(slab, HEAD_DIM, axis=1)
            on_kv_lanes = low if hd // group == 0 else jnp.logical_not(low)
            qrows.append(jnp.where(on_kv_lanes, src, 0.0))
        q8s.append(_row_select(qrows).astype(BF16))
    scores = [_dot(q8s[bi], kws[bi]) for bi in seqs]
    probs = []
    for s in scores:
        m = jnp.maximum(jnp.max(s, axis=1, keepdims=True), sk)
        p = jnp.exp(s - m)
        probs.append((p / (jnp.sum(p, axis=1, keepdims=True) + jnp.exp(sk - m))).astype(BF16))
    for bi in seqs:
        o8 = lax.dot_general(probs[bi], vws[bi], NT_DIMS, preferred_element_type=F32)
        for slab in range(ATTN_Q_HEADS // 2):
            kv = (2 * slab) // group
            even, odd = o8[2 * slab:2 * slab + 1, :], o8[2 * slab + 1:2 * slab + 2, :]
            if kv == 0:
                out = jnp.where(low, even, pltpu.roll(odd, HEAD_DIM, axis=1))
            else:
                out = jnp.where(low, pltpu.roll(even, HEAD_DIM, axis=1), odd)
            oa_rows[slab].append(out)

    def columns(a8, hd):
        pad = jnp.zeros((HG_D - tile, HG_D), F32)
        return jnp.concatenate([_head(a8, hd), pad], axis=0).T
    seq_row = lax.broadcasted_iota(jnp.int32, (HG_D, HG_D), 0)
    heads = range(HG_HEADS)
    head_rows = [slice(hd * HG_D, (hd + 1) * HG_D) for hd in heads]
    f_cols = [_spread_rows(_head(fh8, hd), every_ref[...]) for hd in heads]
    k_cols = [columns(kh8, hd).astype(BF16) for hd in heads]
    v_rows = [jnp.concatenate([_head(vh8, hd), jnp.zeros((HG_D - tile, HG_D), F32)], axis=0)
              for hd in heads]
    outers = [[_dot(k_cols[hd], jnp.where(seq_row == bi, v_rows[hd], 0.0).astype(BF16))
               for bi in seqs] for hd in heads]
    for hd in heads:
        for bi in seqs:
            f_col = f_cols[hd][:, bi * HG_D:(bi + 1) * HG_D]
            ns_ref[bi, head_rows[hd], :] = f_col * s0_ref[bi, head_rows[hd], :] + outers[hd][bi]
    outs = [[_dot(_head(qh8, hd).astype(BF16), ns_ref[bi, head_rows[hd], :].astype(BF16))
             for bi in seqs] for hd in heads]
    for hd in heads:
        for bi in seqs:
            oh_rows[hd].append(_rms(outs[hd][bi][bi:bi + 1, :], hgn))
    for slab in range(ATTN_Q_HEADS // 2):
        oa_ref[r8, slab * LANES:(slab + 1) * LANES] = _row_select(oa_rows[slab])
    for hd in range(HG_HEADS):
        oh_ref[r8, hd * HG_D:(hd + 1) * HG_D] = _row_select(oh_rows[hd])

    @pl.when(i == pl.num_programs(0) - 1)
    def _():
        gh = proj(OFF_GH, HG_W)
        oh = oh_ref[...] * (gh * jax.nn.sigmoid(gh))
        m = (_merge_out(oa_ref[...], wao_ref, OFF_GA, proj)
             + _merge_out(oh, who_ref, OFF_GB, proj))
        y_ref[...] = x_ref[...] + _rms(_dot(m.astype(BF16), wo_ref[...]), post_ref[...])


def _mix_sample_call(x, ck, cv, s0, sinks, pre, post, lbl, hgn, win, wao, who, wo):
    n = x.shape[0]
    tile = min(SAMPLE_TILE, n)
    full = pl.BlockSpec((n, D_MODEL), lambda i: (0, 0))
    kv_spec = pl.BlockSpec((tile, ATTN_KV_W, WINDOW), lambda i: (i, 0, 0))
    st_spec = pl.BlockSpec((tile, HG_W, HG_D), lambda i: (i, 0, 0))
    act = pltpu.VMEM((n, HG_W), F32)
    kv_act = pltpu.VMEM((n, ATTN_KV_W), F32)
    every, last = _lane_selectors(tile)
    return pl.pallas_call(
        _mix_sample_kernel,
        grid=(n // tile,),
        in_specs=[pl.BlockSpec(memory_space=pltpu.SMEM), full, kv_spec, kv_spec, st_spec,
                  _const_spec((1, D_MODEL)), _const_spec((1, D_MODEL)),
                  _const_spec(lbl.shape), _const_spec((1, HG_D)),
                  _const_spec(every.shape), _const_spec(last.shape),
                  HBM_SPEC, HBM_SPEC, HBM_SPEC, HBM_SPEC],
        out_specs=[full, kv_spec, kv_spec, st_spec],
        out_shape=[jax.ShapeDtypeStruct(x.shape, F32),
                   jax.ShapeDtypeStruct(ck.shape, F32),
                   jax.ShapeDtypeStruct(cv.shape, F32),
                   jax.ShapeDtypeStruct(s0.shape, F32)],
        scratch_shapes=[pltpu.VMEM(win.shape, BF16), pltpu.VMEM(wao.shape, BF16),
                        pltpu.VMEM(who.shape, BF16), pltpu.VMEM(wo.shape, BF16),
                        pltpu.VMEM((n, D_MODEL), BF16),
                        pltpu.VMEM((n, ATTN_Q_W), F32), kv_act, kv_act,
                        act, act, act, act,
                        pltpu.VMEM((n, ATTN_Q_W), F32), act],
        compiler_params=pltpu.CompilerParams(
            dimension_semantics=("arbitrary",), vmem_limit_bytes=VMEM_LIMIT),
        name="mix_sample",
    )(sinks, x, ck, cv, s0, pre, post, lbl, hgn, every, last, win, wao, who, wo)


def kernel(x_prompt, x_sample, cache_k, cache_v, state_hgrn, norm_ffn1_pre, norm_ffn1_post, w_ffn1_gate, w_ffn1_up, w_ffn1_down, norm_mix_pre, norm_mix_post, w_in, attn_sinks, hgrn_lb_logits, hgrn_norm, w_attn_out, w_hgrn_out, w_out, norm_ffn2_pre, norm_ffn2_post, w_ffn2_gate, w_ffn2_up, w_ffn2_down):
    depth = w_in.shape[0]
    assert depth == 1 and hgrn_lb_logits.shape[0] == 2, "single-layer stack only"
    batch, seq, _ = x_prompt.shape
    n_s = x_sample.shape[0]
    assert x_sample.shape[1] == 1 and seq % WINDOW == 0

    xp = x_prompt.reshape(batch * seq, D_MODEL)
    xs = x_sample.reshape(n_s, D_MODEL)
    ck = jnp.swapaxes(cache_k[0].reshape(n_s, WINDOW, ATTN_KV_W), 1, 2)
    cv = jnp.swapaxes(cache_v[0].reshape(n_s, WINDOW, ATTN_KV_W), 1, 2)
    s0 = state_hgrn[0].reshape(n_s, HG_W, HG_D)

    xp, xs = _ffn_call(xp, xs, norm_ffn1_pre, norm_ffn1_post,
                       w_ffn1_gate[0], w_ffn1_up[0], w_ffn1_down[0])

    mix_w = (attn_sinks, norm_mix_pre, norm_mix_post, hgrn_lb_logits, hgrn_norm,
             w_in[0], w_attn_out[0], w_hgrn_out[0], w_out[0])
    xp, nkp, nvp, nsp = _mix_prompt_call(xp, batch, *mix_w)
    xs, nks, nvs, nss = _mix_sample_call(xs, ck, cv, s0, *mix_w)

    xp, xs = _ffn_call(xp, xs, norm_ffn2_pre, norm_ffn2_post,
                       w_ffn2_gate[0], w_ffn2_up[0], w_ffn2_down[0])

    kv_shape = (1, -1, WINDOW, ATTN_KV_HEADS, HEAD_DIM)
    st_shape = (1, -1, HG_HEADS, HG_D, HG_D)
    return (xp.reshape(batch, seq, D_MODEL), xs.reshape(n_s, 1, D_MODEL),
            nkp.reshape(kv_shape), nvp.reshape(kv_shape), nsp.reshape(st_shape),
            jnp.swapaxes(nks, 1, 2).reshape(kv_shape), jnp.swapaxes(nvs, 1, 2).reshape(kv_shape),
            nss.reshape(st_shape))
```

```python
import functools

import jax
import jax.numpy as jnp
from jax import lax
from jax.experimental import pallas as pl
from jax.experimental.pallas import tpu as pltpu

F32 = jnp.float32
BF16 = jnp.bfloat16

D_MODEL = 1024
FFN_DIM = 2816
HEAD_DIM = 64
ATTN_Q_HEADS = 8
ATTN_KV_HEADS = 2
WINDOW = 128
HG_HEADS = 4
HG_D = 128
EPS = 1e-6

ATTN_Q_W = ATTN_Q_HEADS * HEAD_DIM
ATTN_KV_W = ATTN_KV_HEADS * HEAD_DIM
HG_W = HG_HEADS * HG_D
OFF_QA = 0
OFF_KA = OFF_QA + ATTN_Q_W
OFF_VA = OFF_KA + ATTN_KV_W
OFF_QH = OFF_VA + ATTN_KV_W
OFF_FH = OFF_QH + HG_W
OFF_IH = OFF_FH + HG_W
OFF_GH = OFF_IH + HG_W
OFF_GA = OFF_GH + HG_W
OFF_GB = OFF_GA + D_MODEL
IN_WIDTH = OFF_GB + D_MODEL

LANES = 128
SUBLANES = 8
FFN_CHUNK = 256
N_FFN_CHUNKS = FFN_DIM // FFN_CHUNK
FFN_ROWS = 1024
MIX_ROWS = 512
NORM_PARTS = 2
HG_CHUNK = 128
SAMPLE_TILE = 16
VMEM_LIMIT = 56 * 1024 * 1024

NT_DIMS = (((1,), (1,)), ((), ()))
TN_DIMS = (((0,), (0,)), ((), ()))


def _rms(x, g):
    return x * lax.rsqrt(jnp.mean(x * x, axis=-1, keepdims=True) + EPS) * g


def _dot(a, b):
    return jnp.dot(a, b, preferred_element_type=F32)


STREAM_SLOTS = 4
STREAM_CHUNK_BYTES = 2 << 20


def _stream_cast(src_hbm, dst_ref):
    n_rows, n_cols = src_hbm.shape
    row_bytes = n_cols * jnp.dtype(F32).itemsize
    rc = max(SUBLANES, min(n_rows, STREAM_CHUNK_BYTES // row_bytes // SUBLANES * SUBLANES))
    while n_rows % rc:
        rc -= SUBLANES
    n = n_rows // rc
    slots = min(STREAM_SLOTS, n)

    def body(stage, sem):
        def copy(c):
            return pltpu.make_async_copy(src_hbm.at[pl.ds(c * rc, rc), :], stage.at[c % slots],
                                         sem.at[c % slots])
        for c in range(slots):
            copy(c).start()
        for c in range(n):
            copy(c).wait()
            dst_ref[c * rc:(c + 1) * rc, :] = stage[c % slots].astype(BF16)
            if c + slots < n:
                copy(c + slots).start()

    pl.run_scoped(body, pltpu.VMEM((slots, rc, n_cols), F32), pltpu.SemaphoreType.DMA((slots,)))


HBM_SPEC = pl.BlockSpec(memory_space=pl.ANY)


def _const_spec(shape):
    zeros = (0,) * len(shape)
    return pl.BlockSpec(shape, lambda *_: zeros, pipeline_mode=pl.Buffered(1))


def _ffn_tile(x, pre, post, wg_ref, wu_ref, wd_ref):
    h = _rms(x, pre).astype(BF16)
    acc = None
    for c in range(N_FFN_CHUNKS):
        cols = slice(c * FFN_CHUNK, (c + 1) * FFN_CHUNK)
        g = _dot(h, wg_ref[:, cols])
        u = _dot(h, wu_ref[:, cols])
        a = (g * jax.nn.sigmoid(g) * u).astype(BF16)
        d = _dot(a, wd_ref[cols, :])
        acc = d if acc is None else acc + d
    return x + 0.5 * _rms(acc, post)


def _ffn_kernel(n_prompt_steps, xp_ref, xs_ref, pre_ref, post_ref, wg_hbm, wu_hbm, wd_hbm,
                yp_ref, ys_ref, wg_ref, wu_ref, wd_ref):
    i = pl.program_id(0)

    @pl.when(i == 0)
    def _():
        _stream_cast(wg_hbm, wg_ref)
        _stream_cast(wu_hbm, wu_ref)
        _stream_cast(wd_hbm, wd_ref)

    @pl.when(i < n_prompt_steps)
    def _():
        yp_ref[...] = _ffn_tile(xp_ref[...], pre_ref[...], post_ref[...], wg_ref, wu_ref, wd_ref)

    @pl.when(i == n_prompt_steps)
    def _():
        ys_ref[...] = _ffn_tile(xs_ref[...], pre_ref[...], post_ref[...], wg_ref, wu_ref, wd_ref)


def _ffn_call(xp, xs, pre, post, wg, wu, wd):
    n_p, n_s = xp.shape[0], xs.shape[0]
    rows = min(FFN_ROWS, n_p)
    steps = n_p // rows
    prompt_spec = pl.BlockSpec((rows, D_MODEL), lambda i: (jnp.minimum(i, steps - 1), 0))
    sample_spec = pl.BlockSpec((n_s, D_MODEL), lambda i: (0, 0))
    return pl.pallas_call(
        functools.partial(_ffn_kernel, steps),
        grid=(steps + 1,),
        in_specs=[prompt_spec, sample_spec,
                  _const_spec((1, D_MODEL)), _const_spec((1, D_MODEL)),
                  HBM_SPEC, HBM_SPEC, HBM_SPEC],
        out_specs=[prompt_spec, sample_spec],
        out_shape=[jax.ShapeDtypeStruct(xp.shape, F32), jax.ShapeDtypeStruct(xs.shape, F32)],
        scratch_shapes=[pltpu.VMEM(wg.shape, BF16), pltpu.VMEM(wu.shape, BF16),
                        pltpu.VMEM(wd.shape, BF16)],
        compiler_params=pltpu.CompilerParams(
            dimension_semantics=("arbitrary",), vmem_limit_bytes=VMEM_LIMIT),
        name="ffn_half",
    )(xp, xs, pre, post, wg, wu, wd)


def _forget_lower_bound(lbl):
    l0, l1 = lbl[0:1, :], lbl[1:2, :]
    m = jnp.maximum(l0, l1)
    e0, e1 = jnp.exp(l0 - m), jnp.exp(l1 - m)
    return e0 / (e0 + e1)


def _lane_is_low(shape):
    return lax.broadcasted_iota(jnp.int32, shape, len(shape) - 1) < HEAD_DIM


def _dup_kv(x):
    swapped = pltpu.roll(x, HEAD_DIM, axis=1)
    low = _lane_is_low(x.shape)
    return jnp.where(low, x, swapped), jnp.where(low, swapped, x)


def _merge_out(h, w_ref, off, gates_from):
    return jax.nn.sigmoid(gates_from(off, D_MODEL)) * _dot(h.astype(BF16), w_ref[...])


def _pair_levels(c):
    t = jnp.arange(c, dtype=jnp.int32)[:, None]
    s = jnp.arange(c, dtype=jnp.int32)[None, :]
    x = jnp.maximum(t ^ s, 1)
    lvl = (31 - lax.clz(x)).astype(jnp.int32)
    return jnp.where(t > s, lvl, -1)


LOG2E = 1.4426950408889634


def _head(a, hd):
    return a[:, hd * HG_D:(hd + 1) * HG_D]


HG_LEVELS = HG_CHUNK.bit_length() - 1


def _chunk_rows(ci):
    return slice(ci * HG_CHUNK, (ci + 1) * HG_CHUNK)


def _hgrn_log_decay(g_ref, b_ref, n_chunks):
    c = HG_CHUNK
    width = g_ref.shape[1]
    row = lax.broadcasted_iota(jnp.int32, (c, c), 0)
    col = lax.broadcasted_iota(jnp.int32, (c, c), 1)
    tril = (col <= row).astype(BF16)
    for ci in range(n_chunks):
        r = _chunk_rows(ci)
        g = g_ref[r, :]
        hi = g.astype(BF16)
        rest = g - hi.astype(F32)
        mid = rest.astype(BF16)
        lo = (rest - mid.astype(F32)).astype(BF16)
        sums = _dot(tril, jnp.concatenate([hi, mid, lo], axis=1))
        b_ref[r, :] = (sums[:, 0:width] + sums[:, width:2 * width] + sums[:, 2 * width:]) * LOG2E


def _hgrn_factor_jobs(q_ref, k_ref, b_ref, x_ref, qe_ref, ke_ref, n_chunks):
    c = HG_CHUNK
    t = lax.broadcasted_iota(jnp.int32, (c, 1), 0)
    last = {}

    def level_job(ci, lvl):
        def run():
            n = 1 << lvl
            r = _chunk_rows(ci)
            bc = b_ref[r, :]
            second = (t & n) != 0
            if n < SUBLANES:
                prev = last.get(ci, bc)
                tiles = prev.reshape(c // SUBLANES, SUBLANES, prev.shape[1])
                second3 = second.reshape(c // SUBLANES, SUBLANES, 1)
                edge = jnp.where(second3, pltpu.roll(tiles, n, axis=1), tiles).reshape(prev.shape)
                if 2 * n < SUBLANES:
                    last[ci] = jnp.where(second3, tiles, pltpu.roll(tiles, SUBLANES - n, axis=1)
                                         ).reshape(prev.shape)
            else:
                edge = jnp.concatenate(
                    [jnp.broadcast_to(bc[p + n - 1:p + n, :], (2 * n, bc.shape[1]))
                     for p in range(0, c, 2 * n)], axis=0)
            w = jnp.exp2(jnp.where(second, bc - edge, edge - bc))
            x_ref[lvl, r, :] = (jnp.where(second, q_ref[r, :], k_ref[r, :]) * w).astype(BF16)
        return run

    def state_job(ci):
        def run():
            r = _chunk_rows(ci)
            bc = b_ref[r, :]
            qe_ref[r, :] = (q_ref[r, :] * jnp.exp2(bc)).astype(BF16)
            ke_ref[r, :] = (k_ref[r, :] * jnp.exp2(bc[c - 1:c, :] - bc)).astype(BF16)
        return run

    jobs = [level_job(ci, lvl) for lvl in range(HG_LEVELS) for ci in range(n_chunks)]
    return jobs + [state_job(ci) for ci in range(n_chunks)]


def _hgrn_pair_jobs(x_ref, levels, n_chunks):
    c = HG_CHUNK
    heads, chunks = range(HG_HEADS), range(n_chunks)
    a = [[jnp.zeros((c, c), F32) for _ in heads] for _ in chunks]

    def job(lvl, ci):
        def run():
            x = x_ref[lvl, _chunk_rows(ci), :]
            for hd in heads:
                al = lax.dot_general(_head(x, hd), _head(x, hd), NT_DIMS,
                                     preferred_element_type=F32)
                a[ci][hd] = jnp.where(levels == lvl, al, a[ci][hd])
        return run
    return [job(lvl, ci) for lvl in range(HG_LEVELS) for ci in chunks], a


def _hgrn_finish(q_ref, k_ref, v_ref, b_ref, a, qe_ref, ke_ref, st, n_chunks):
    c = HG_CHUNK
    heads, chunks = range(HG_HEADS), range(n_chunks)
    local = []
    for ci in chunks:
        r = _chunk_rows(ci)
        q, k, v = q_ref[r, :], k_ref[r, :], v_ref[r, :]
        qk, vb = q * k, v.astype(BF16)
        outs = []
        for hd in heads:
            diag = jnp.sum(_head(qk, hd), axis=1, keepdims=True)
            outs.append(_dot(a[ci][hd].astype(BF16), _head(vb, hd)) + diag * _head(v, hd))
        local.append(outs)

    result = []
    for ci in chunks:
        r = _chunk_rows(ci)
        qe, ke, vb = qe_ref[r, :], ke_ref[r, :], v_ref[r, :].astype(BF16)
        decay = jnp.exp2(b_ref[ci * c + c - 1:ci * c + c, :])
        outs, new = [], []
        for hd in heads:
            outs.append(local[ci][hd] + lax.dot_general(_head(qe, hd), st[hd].astype(BF16), NT_DIMS,
                                                        preferred_element_type=F32))
            new.append(st[hd] * _head(decay, hd)
                       + lax.dot_general(_head(vb, hd), _head(ke, hd), TN_DIMS,
                                         preferred_element_type=F32))
        st = new
        result.append(outs)
    return result, st


def _mix_prompt_kernel(sinks_ref, x_ref, pre_ref, post_ref, lbl_ref, hgn_ref, lvl_ref,
                       win_hbm, wao_hbm, who_hbm, wo_hbm,
                       y_ref, nk_ref, nv_ref, ns_ref,
                       win_ref, wao_ref, who_ref, wo_ref,
                       kbuf, vbuf, st_ref, qh_ref, kh_ref, vh_ref, lf_ref, oa_ref, oh_ref, g_ref,
                       b_ref, xl_ref, qe_ref, ke_ref):
    j = pl.program_id(1)
    rows = x_ref.shape[0]
    n_blocks = rows // WINDOW

    @pl.when((pl.program_id(0) == 0) & (j == 0))
    def _():
        for src, dst in ((win_hbm, win_ref), (wao_hbm, wao_ref), (who_hbm, who_ref),
                         (wo_hbm, wo_ref)):
            _stream_cast(src, dst)

    @pl.when(j == 0)
    def _():
        kbuf[0:WINDOW, :] = jnp.zeros((WINDOW, LANES), F32)
        vbuf[0:WINDOW, :] = jnp.zeros((WINDOW, LANES), F32)
        st_ref[...] = jnp.zeros(st_ref.shape, F32)

    part_rows = [slice(r0, r0 + rows // NORM_PARTS) for r0 in range(0, rows, rows // NORM_PARTS)]
    h_parts, qkv_parts = [], []
    for r in part_rows:
        hp = _rms(x_ref[r, :], pre_ref[...]).astype(BF16)
        h_parts.append(hp)
        qkv_parts.append(_dot(hp, win_ref[:, OFF_QA:OFF_QH]))
    h = jnp.concatenate(h_parts, axis=0)
    qkv = jnp.concatenate(qkv_parts, axis=0)

    def proj(off, width):
        return _dot(h, win_ref[:, off:off + width])

    lb = _forget_lower_bound(lbl_ref[...])
    n_chunks = rows // HG_CHUNK

    def proj_slab(off):
        def run():
            z = proj(off, FFN_CHUNK)
            if off < OFF_FH:
                cols = slice(off - OFF_QH, off - OFF_QH + FFN_CHUNK)
                qh_ref[:, cols] = z * jax.nn.sigmoid(z)
            elif off < OFF_IH:
                cols = slice(off - OFF_FH, off - OFF_FH + FFN_CHUNK)
                lbs = lb[:, cols]
                lf_ref[:, cols] = jnp.log(lbs + (1.0 - lbs) * jax.nn.sigmoid(z))
                kh_ref[:, cols] = (1.0 - lbs) * jax.nn.sigmoid(-z)
            elif off < OFF_GH:
                cols = slice(off - OFF_IH, off - OFF_IH + FFN_CHUNK)
                vh_ref[:, cols] = z
            else:
                cols = slice(off - OFF_GH, off - OFF_GH + FFN_CHUNK)
                g_ref[:, cols] = z
        return run
    hgrn_slabs = [proj_slab(off) for off in range(OFF_QH, OFF_GH, FFN_CHUNK)]
    gate_slabs = [proj_slab(off) for off in range(OFF_GH, IN_WIDTH, FFN_CHUNK)]
    jobs = _hgrn_factor_jobs(qh_ref, kh_ref, b_ref, xl_ref, qe_ref, ke_ref, n_chunks)
    pair_jobs, in_chunk = _hgrn_pair_jobs(xl_ref, lvl_ref[...], n_chunks)

    qa = qkv[:, OFF_QA:OFF_KA] * (HEAD_DIM ** -0.5)
    ka = qkv[:, OFF_KA:OFF_VA]
    va = qkv[:, OFF_VA:OFF_QH]
    kbuf[WINDOW:WINDOW + rows, :] = ka
    vbuf[WINDOW:WINDOW + rows, :] = va

    qi = lax.broadcasted_iota(jnp.int32, (WINDOW, 2 * WINDOW), 0)
    kj = lax.broadcasted_iota(jnp.int32, (WINDOW, 2 * WINDOW), 1)
    band = (kj > qi) & (kj <= qi + WINDOW)
    low = _lane_is_low((WINDOW, LANES))
    top = lax.broadcasted_iota(jnp.int32, (2 * WINDOW, 1), 0) < WINDOW
    n_slabs = ATTN_Q_HEADS // 2
    group = ATTN_Q_HEADS // ATTN_KV_HEADS
    kds = [_dup_kv(kbuf[n * WINDOW:(n + 2) * WINDOW, :]) for n in range(n_blocks)]
    vds = [_dup_kv(vbuf[n * WINDOW:(n + 2) * WINDOW, :]) for n in range(n_blocks)]

    def scores(n, slab):
        qs = qa[n * WINDOW:(n + 1) * WINDOW, slab * LANES:(slab + 1) * LANES]
        q2 = jnp.concatenate([jnp.where(low, qs, 0.0), jnp.where(low, 0.0, qs)], axis=0)
        return lax.dot_general(q2.astype(BF16), kds[n][(2 * slab) // group].astype(BF16), NT_DIMS,
                               preferred_element_type=F32)

    def attend(n, slab, s):
        valid = band & ((j > 0) | (kj >= WINDOW)) if n == 0 else band
        s = jnp.where(jnp.concatenate([valid, valid], axis=0), s, -jnp.inf)
        sk = jnp.where(top, sinks_ref[0, 2 * slab], sinks_ref[0, 2 * slab + 1])
        m = jnp.maximum(jnp.max(s, axis=1, keepdims=True), sk)
        p = jnp.exp(s - m)
        den = jnp.sum(p, axis=1, keepdims=True) + jnp.exp(sk - m)
        o2 = _dot(p.astype(BF16), vds[n][(2 * slab) // group].astype(BF16)) / den
        oa_ref[n * WINDOW:(n + 1) * WINDOW, slab * LANES:(slab + 1) * LANES] = (
            jnp.where(low, o2[0:WINDOW], o2[WINDOW:2 * WINDOW]))

    order = [(n, slab) for n in range(n_blocks) for slab in range(n_slabs)]
    early = len(hgrn_slabs)
    assert len(order) >= early + len(gate_slabs)
    pending = scores(*order[0])
    ready_pairs = 0
    for idx, (n, slab) in enumerate(order):
        nxt = scores(*order[idx + 1]) if idx + 1 < len(order) else None
        if idx < early:
            hgrn_slabs[idx]()
        else:
            if idx == early:
                _hgrn_log_decay(lf_ref, b_ref, n_chunks)
            if gate_slabs:
                gate_slabs.pop(0)()
            for _ in range(min(len(pair_jobs), ready_pairs)):
                pair_jobs.pop(0)()
            ready_pairs = 0
            for _ in range(-(-len(jobs) // (len(order) - 1 - idx)) if idx + 1 < len(order) else 0):
                if jobs:
                    jobs.pop(0)()
                    ready_pairs += 1
        attend(n, slab, pending)
        pending = nxt
    kbuf[0:WINDOW, :] = kbuf[rows:rows + WINDOW, :]
    vbuf[0:WINDOW, :] = vbuf[rows:rows + WINDOW, :]
    assert not jobs and not gate_slabs
    for run in pair_jobs:
        run()

    hgn = hgn_ref[...]
    st0 = [st_ref[hd * HG_D:(hd + 1) * HG_D, :] for hd in range(HG_HEADS)]
    outs, states = _hgrn_finish(qh_ref, kh_ref, vh_ref, b_ref, in_chunk, qe_ref, ke_ref, st0,
                                n_chunks)
    for hd in range(HG_HEADS):
        cs = slice(hd * HG_D, (hd + 1) * HG_D)
        st_ref[cs, :] = states[hd]
        for ci in range(n_chunks):
            oh_ref[ci * HG_CHUNK:(ci + 1) * HG_CHUNK, cs] = _rms(outs[ci][hd], hgn)

    gh = g_ref[:, 0:HG_W]
    oh = oh_ref[...] * (gh * jax.nn.sigmoid(gh))

    def gate(off, width):
        return g_ref[:, off - OFF_GH:off - OFF_GH + width]
    m = (_merge_out(oa_ref[...], wao_ref, OFF_GA, gate)
         + _merge_out(oh, who_ref, OFF_GB, gate))
    mb = m.astype(BF16)
    for r in part_rows:
        y_ref[r, :] = x_ref[r, :] + _rms(_dot(mb[r, :], wo_ref[...]), post_ref[...])

    @pl.when(j == pl.num_programs(1) - 1)
    def _():
        nk_ref[0] = kbuf[0:WINDOW, :].T
        nv_ref[0] = vbuf[0:WINDOW, :].T
        for hd in range(HG_HEADS):
            cs = slice(hd * HG_D, (hd + 1) * HG_D)
            ns_ref[0, cs, :] = st_ref[cs, :].T


def _mix_prompt_call(x, batch, sinks, pre, post, lbl, hgn, win, wao, who, wo):
    seq = x.shape[0] // batch
    rows = min(MIX_ROWS, seq)
    steps = seq // rows
    x_spec = pl.BlockSpec((rows, D_MODEL), lambda b, j: (b * steps + j, 0))
    kv_spec = pl.BlockSpec((1, ATTN_KV_W, WINDOW), lambda b, j: (b, 0, 0))
    st_spec = pl.BlockSpec((1, HG_W, HG_D), lambda b, j: (b, 0, 0))
    act = pltpu.VMEM((rows, HG_W), F32)
    return pl.pallas_call(
        _mix_prompt_kernel,
        grid=(batch, steps),
        in_specs=[pl.BlockSpec(memory_space=pltpu.SMEM), x_spec,
                  _const_spec((1, D_MODEL)), _const_spec((1, D_MODEL)),
                  _const_spec(lbl.shape), _const_spec((1, HG_D)),
                  _const_spec((HG_CHUNK, HG_CHUNK)),
                  HBM_SPEC, HBM_SPEC, HBM_SPEC, HBM_SPEC],
        out_specs=[x_spec, kv_spec, kv_spec, st_spec],
        out_shape=[jax.ShapeDtypeStruct(x.shape, F32),
                   jax.ShapeDtypeStruct((batch, ATTN_KV_W, WINDOW), F32),
                   jax.ShapeDtypeStruct((batch, ATTN_KV_W, WINDOW), F32),
                   jax.ShapeDtypeStruct((batch, HG_W, HG_D), F32)],
        scratch_shapes=[pltpu.VMEM(win.shape, BF16), pltpu.VMEM(wao.shape, BF16),
                        pltpu.VMEM(who.shape, BF16), pltpu.VMEM(wo.shape, BF16),
                        pltpu.VMEM((rows + WINDOW, ATTN_KV_W), F32),
                        pltpu.VMEM((rows + WINDOW, ATTN_KV_W), F32),
                        pltpu.VMEM((HG_W, HG_D), F32),
                        act, act, act, act,
                        pltpu.VMEM((rows, ATTN_Q_W), F32), act,
                        pltpu.VMEM((rows, IN_WIDTH - OFF_GH), F32),
                        act, pltpu.VMEM((HG_LEVELS, rows, HG_W), BF16),
                        pltpu.VMEM((rows, HG_W), BF16), pltpu.VMEM((rows, HG_W), BF16)],
        compiler_params=pltpu.CompilerParams(
            dimension_semantics=("arbitrary", "arbitrary"), vmem_limit_bytes=VMEM_LIMIT),
        name="mix_prompt",
    )(sinks, x, pre, post, lbl, hgn, _pair_levels(HG_CHUNK), win, wao, who, wo)


def _row_select(rows_list):
    n = -(-len(rows_list) // 8) * 8
    lanes = rows_list[0].shape[1]
    ridx = lax.broadcasted_iota(jnp.int32, (n, lanes), 0)
    out = jnp.zeros((n, lanes), F32)
    for i, r in enumerate(rows_list):
        out = jnp.where(ridx == i, jnp.broadcast_to(r, (n, lanes)), out)
    return out


def _mix_sample_kernel(sinks_ref, x_ref, ck_ref, cv_ref, s0_ref, pre_ref, post_ref, lbl_ref, hgn_ref,
                       win_hbm, wao_hbm, who_hbm, wo_hbm,
                       y_ref, nk_ref, nv_ref, ns_ref,
                       win_ref, wao_ref, who_ref, wo_ref,
                       h_ref, qa_ref, ka_ref, va_ref, qh_ref, kh_ref, fh_ref, vh_ref, oa_ref, oh_ref):
    i = pl.program_id(0)
    tile = ck_ref.shape[0]

    def proj(off, width):
        return _dot(h_ref[...], win_ref[:, off:off + width])

    @pl.when(i == 0)
    def _():
        for src, dst in ((win_hbm, win_ref), (wao_hbm, wao_ref), (who_hbm, who_ref),
                         (wo_hbm, wo_ref)):
            _stream_cast(src, dst)
        h_ref[...] = _rms(x_ref[...], pre_ref[...]).astype(BF16)
        qa_ref[...] = proj(OFF_QA, ATTN_Q_W) * (HEAD_DIM ** -0.5)
        ka_ref[...] = proj(OFF_KA, ATTN_KV_W)
        va_ref[...] = proj(OFF_VA, ATTN_KV_W)
        lb = _forget_lower_bound(lbl_ref[...])
        qh = proj(OFF_QH, HG_W)
        qh_ref[...] = qh * jax.nn.sigmoid(qh)
        fp = proj(OFF_FH, HG_W)
        fh_ref[...] = jnp.exp(jnp.log(lb + (1.0 - lb) * jax.nn.sigmoid(fp)))
        kh_ref[...] = (1.0 - lb) * jax.nn.sigmoid(-fp)
        vh_ref[...] = proj(OFF_IH, HG_W)

    low = _lane_is_low((1, LANES))
    newest = lax.broadcasted_iota(jnp.int32, (ATTN_KV_W, WINDOW), 1) == WINDOW - 1
    hrow = lax.broadcasted_iota(jnp.int32, (ATTN_Q_HEADS, 1), 0)
    sk = jnp.zeros((ATTN_Q_HEADS, 1), F32)
    for hd in range(ATTN_Q_HEADS):
        sk = jnp.where(hrow == hd, sinks_ref[0, hd], sk)
    group = ATTN_Q_HEADS // ATTN_KV_HEADS
    hgn = hgn_ref[...]

    r8 = pl.ds(pl.multiple_of(i * tile, tile), tile)
    qa8, ka8, va8 = qa_ref[r8, :], ka_ref[r8, :], va_ref[r8, :]
    fh8, kh8, qh8, vh8 = fh_ref[r8, :], kh_ref[r8, :], qh_ref[r8, :], vh_ref[r8, :]
    oa_rows = [[] for _ in range(ATTN_Q_HEADS // 2)]
    oh_rows = [[] for _ in range(HG_HEADS)]
    seqs = range(tile)
    def as_columns(rows8):
        pad = jnp.zeros((LANES - tile, rows8.shape[1]), F32)
        return jnp.concatenate([rows8, pad], axis=0).T
    k_cols, v_cols = as_columns(ka8), as_columns(va8)
    kws, vws, q8s = [], [], []
    for bi in seqs:
        r1 = slice(bi, bi + 1)
        kw = jnp.where(newest, pltpu.roll(k_cols, WINDOW - 1 - bi, axis=1),
                       pltpu.roll(ck_ref[bi], WINDOW - 1, axis=1))
        vw = jnp.where(newest, pltpu.roll(v_cols, WINDOW - 1 - bi, axis=1),
                       pltpu.roll(cv_ref[bi], WINDOW - 1, axis=1))
        nk_ref[bi] = kw
        nv_ref[bi] = vw
        kws.append(kw.astype(BF16))
        vws.append(vw.astype(BF16))
        qrows = []
        for hd in range(ATTN_Q_HEADS):
            slab = qa8[r1, (hd // 2) * LANES:(hd // 2 + 1) * LANES]
            in_place = (hd % 2) == (hd // group)
            src = slab if in_place else pltpu.roll(slab, HEAD_DIM, axis=1)
            on_kv_lanes = low if hd // group == 0 else jnp.logical_not(low)
            qrows.append(jnp.where(on_kv_lanes, src, 0.0))
        q8s.append(_row_select(qrows).astype(BF16))
    scores = [_dot(q8s[bi], kws[bi]) for bi in seqs]
    probs = []
    for s in scores:
        m = jnp.maximum(jnp.max(s, axis=1, keepdims=True), sk)
        p = jnp.exp(s - m)
        probs.append((p / (jnp.sum(p, axis=1, keepdims=True) + jnp.exp(sk - m))).astype(BF16))
    for bi in seqs:
        o8 = lax.dot_general(probs[bi], vws[bi], NT_DIMS, preferred_element_type=F32)
        for slab in range(ATTN_Q_HEADS // 2):
            kv = (2 * slab) // group
            even, odd = o8[2 * slab:2 * slab + 1, :], o8[2 * slab + 1:2 * slab + 2, :]
            if kv == 0:
                out = jnp.where(low, even, pltpu.roll(odd, HEAD_DIM, axis=1))
            else:
                out = jnp.where(low, pltpu.roll(even, HEAD_DIM, axis=1), odd)
            oa_rows[slab].append(out)

    def columns(a8, hd):
        pad = jnp.zeros((HG_D - tile, HG_D), F32)
        return jnp.concatenate([_head(a8, hd), pad], axis=0).T
    seq_row = lax.broadcasted_iota(jnp.int32, (HG_D, HG_D), 0)
    heads = range(HG_HEADS)
    head_rows = [slice(hd * HG_D, (hd + 1) * HG_D) for hd in heads]
    f_cols = [columns(fh8, hd) for hd in heads]
    k_cols = [columns(kh8, hd).astype(BF16) for hd in heads]
    v_rows = [jnp.concatenate([_head(vh8, hd), jnp.zeros((HG_D - tile, HG_D), F32)], axis=0)
              for hd in heads]
    outers = [[_dot(k_cols[hd], jnp.where(seq_row == bi, v_rows[hd], 0.0).astype(BF16))
               for bi in seqs] for hd in heads]
    for hd in heads:
        for bi in seqs:
            f_col = jnp.broadcast_to(f_cols[hd][:, bi:bi + 1], (HG_D, HG_D))
            ns_ref[bi, head_rows[hd], :] = f_col * s0_ref[bi, head_rows[hd], :] + outers[hd][bi]
    outs = [[_dot(_head(qh8, hd).astype(BF16), ns_ref[bi, head_rows[hd], :].astype(BF16))
             for bi in seqs] for hd in heads]
    for hd in heads:
        for bi in seqs:
            oh_rows[hd].append(_rms(outs[hd][bi][bi:bi + 1, :], hgn))
    for slab in range(ATTN_Q_HEADS // 2):
        oa_ref[r8, slab * LANES:(slab + 1) * LANES] = _row_select(oa_rows[slab])
    for hd in range(HG_HEADS):
        oh_ref[r8, hd * HG_D:(hd + 1) * HG_D] = _row_select(oh_rows[hd])

    @pl.when(i == pl.num_programs(0) - 1)
    def _():
        gh = proj(OFF_GH, HG_W)
        oh = oh_ref[...] * (gh * jax.nn.sigmoid(gh))
        m = (_merge_out(oa_ref[...], wao_ref, OFF_GA, proj)
             + _merge_out(oh, who_ref, OFF_GB, proj))
        y_ref[...] = x_ref[...] + _rms(_dot(m.astype(BF16), wo_ref[...]), post_ref[...])


def _mix_sample_call(x, ck, cv, s0, sinks, pre, post, lbl, hgn, win, wao, who, wo):
    n = x.shape[0]
    tile = min(SAMPLE_TILE, n)
    full = pl.BlockSpec((n, D_MODEL), lambda i: (0, 0))
    kv_spec = pl.BlockSpec((tile, ATTN_KV_W, WINDOW), lambda i: (i, 0, 0))
    st_spec = pl.BlockSpec((tile, HG_W, HG_D), lambda i: (i, 0, 0))
    act = pltpu.VMEM((n, HG_W), F32)
    kv_act = pltpu.VMEM((n, ATTN_KV_W), F32)
    return pl.pallas_call(
        _mix_sample_kernel,
        grid=(n // tile,),
        in_specs=[pl.BlockSpec(memory_space=pltpu.SMEM), full, kv_spec, kv_spec, st_spec,
                  _const_spec((1, D_MODEL)), _const_spec((1, D_MODEL)),
                  _const_spec(lbl.shape), _const_spec((1, HG_D)),
                  HBM_SPEC, HBM_SPEC, HBM_SPEC, HBM_SPEC],
        out_specs=[full, kv_spec, kv_spec, st_spec],
        out_shape=[jax.ShapeDtypeStruct(x.shape, F32),
                   jax.ShapeDtypeStruct(ck.shape, F32),
                   jax.ShapeDtypeStruct(cv.shape, F32),
                   jax.ShapeDtypeStruct(s0.shape, F32)],
        scratch_shapes=[pltpu.VMEM(win.shape, BF16), pltpu.VMEM(wao.shape, BF16),
                        pltpu.VMEM(who.shape, BF16), pltpu.VMEM(wo.shape, BF16),
                        pltpu.VMEM((n, D_MODEL), BF16),
                        pltpu.VMEM((n, ATTN_Q_W), F32), kv_act, kv_act,
                        act, act, act, act,
                        pltpu.VMEM((n, ATTN_Q_W), F32), act],
        compiler_params=pltpu.CompilerParams(
            dimension_semantics=("arbitrary",), vmem_limit_bytes=VMEM_LIMIT),
        name="mix_sample",
    )(sinks, x, ck, cv, s0, pre, post, lbl, hgn, win, wao, who, wo)


def kernel(x_prompt, x_sample, cache_k, cache_v, state_hgrn, norm_ffn1_pre, norm_ffn1_post, w_ffn1_gate, w_ffn1_up, w_ffn1_down, norm_mix_pre, norm_mix_post, w_in, attn_sinks, hgrn_lb_logits, hgrn_norm, w_attn_out, w_hgrn_out, w_out, norm_ffn2_pre, norm_ffn2_post, w_ffn2_gate, w_ffn2_up, w_ffn2_down):
    depth = w_in.shape[0]
    assert depth == 1 and hgrn_lb_logits.shape[0] == 2, "single-layer stack only"
    batch, seq, _ = x_prompt.shape
    n_s = x_sample.shape[0]
    assert x_sample.shape[1] == 1 and seq % WINDOW == 0

    xp = x_prompt.reshape(batch * seq, D_MODEL)
    xs = x_sample.reshape(n_s, D_MODEL)
    ck = jnp.swapaxes(cache_k[0].reshape(n_s, WINDOW, ATTN_KV_W), 1, 2)
    cv = jnp.swapaxes(cache_v[0].reshape(n_s, WINDOW, ATTN_KV_W), 1, 2)
    s0 = state_hgrn[0].reshape(n_s, HG_W, HG_D)

    xp, xs = _ffn_call(xp, xs, norm_ffn1_pre, norm_ffn1_post,
                       w_ffn1_gate[0], w_ffn1_up[0], w_ffn1_down[0])

    mix_w = (attn_sinks, norm_mix_pre, norm_mix_post, hgrn_lb_logits, hgrn_norm,
             w_in[0], w_attn_out[0], w_hgrn_out[0], w_out[0])
    xp, nkp, nvp, nsp = _mix_prompt_call(xp, batch, *mix_w)
    xs, nks, nvs, nss = _mix_sample_call(xs, ck, cv, s0, *mix_w)

    xp, xs = _ffn_call(xp, xs, norm_ffn2_pre, norm_ffn2_post,
                       w_ffn2_gate[0], w_ffn2_up[0], w_ffn2_down[0])

    kv_shape = (1, -1, WINDOW, ATTN_KV_HEADS, HEAD_DIM)
    st_shape = (1, -1, HG_HEADS, HG_D, HG_D)
    def window(t):
        return jnp.swapaxes(t, 1, 2).reshape(kv_shape)
    return (xp.reshape(batch, seq, D_MODEL), xs.reshape(n_s, 1, D_MODEL),
            window(nkp), window(nvp), nsp.reshape(st_shape),
            window(nks), window(nvs), nss.reshape(st_shape))
```

```python
import functools

import jax
import jax.numpy as jnp
from jax import lax
from jax.experimental import pallas as pl
from jax.experimental.pallas import tpu as pltpu

F32 = jnp.float32
BF16 = jnp.bfloat16

D_MODEL = 1024
FFN_DIM = 2816
HEAD_DIM = 64
ATTN_Q_HEADS = 8
ATTN_KV_HEADS = 2
WINDOW = 128
HG_HEADS = 4
HG_D = 128
EPS = 1e-6

ATTN_Q_W = ATTN_Q_HEADS * HEAD_DIM
ATTN_KV_W = ATTN_KV_HEADS * HEAD_DIM
HG_W = HG_HEADS * HG_D
OFF_QA = 0
OFF_KA = OFF_QA + ATTN_Q_W
OFF_VA = OFF_KA + ATTN_KV_W
OFF_QH = OFF_VA + ATTN_KV_W
OFF_FH = OFF_QH + HG_W
OFF_IH = OFF_FH + HG_W
OFF_GH = OFF_IH + HG_W
OFF_GA = OFF_GH + HG_W
OFF_GB = OFF_GA + D_MODEL
IN_WIDTH = OFF_GB + D_MODEL

LANES = 128
SUBLANES = 8
FFN_CHUNK = 256
N_FFN_CHUNKS = FFN_DIM // FFN_CHUNK
FFN_ROWS = 1024
MIX_ROWS = 512
NORM_PARTS = 2
HG_CHUNK = 128
SAMPLE_TILE = 16
VMEM_LIMIT = 56 * 1024 * 1024

NT_DIMS = (((1,), (1,)), ((), ()))
TN_DIMS = (((0,), (0,)), ((), ()))


def _rms(x, g):
    return x * lax.rsqrt(jnp.mean(x * x, axis=-1, keepdims=True) + EPS) * g


def _dot(a, b):
    return jnp.dot(a, b, preferred_element_type=F32)


STREAM_SLOTS = 4
STREAM_CHUNK_BYTES = 2 << 20


def _stream_cast(src_hbm, dst_ref):
    n_rows, n_cols = src_hbm.shape
    row_bytes = n_cols * jnp.dtype(F32).itemsize
    rc = max(SUBLANES, min(n_rows, STREAM_CHUNK_BYTES // row_bytes // SUBLANES * SUBLANES))
    while n_rows % rc:
        rc -= SUBLANES
    n = n_rows // rc
    slots = min(STREAM_SLOTS, n)

    def body(stage, sem):
        def copy(c):
            return pltpu.make_async_copy(src_hbm.at[pl.ds(c * rc, rc), :], stage.at[c % slots],
                                         sem.at[c % slots])
        for c in range(slots):
            copy(c).start()
        for c in range(n):
            copy(c).wait()
            dst_ref[c * rc:(c + 1) * rc, :] = stage[c % slots].astype(BF16)
            if c + slots < n:
                copy(c + slots).start()

    pl.run_scoped(body, pltpu.VMEM((slots, rc, n_cols), F32), pltpu.SemaphoreType.DMA((slots,)))


HBM_SPEC = pl.BlockSpec(memory_space=pl.ANY)


def _const_spec(shape):
    zeros = (0,) * len(shape)
    return pl.BlockSpec(shape, lambda *_: zeros, pipeline_mode=pl.Buffered(1))


def _ffn_tile(x, pre, post, wg_ref, wu_ref, wd_ref):
    h = _rms(x, pre).astype(BF16)
    acc = None
    for c in range(N_FFN_CHUNKS):
        cols = slice(c * FFN_CHUNK, (c + 1) * FFN_CHUNK)
        g = _dot(h, wg_ref[:, cols])
        u = _dot(h, wu_ref[:, cols])
        a = (g * jax.nn.sigmoid(g) * u).astype(BF16)
        d = _dot(a, wd_ref[cols, :])
        acc = d if acc is None else acc + d
    return x + 0.5 * _rms(acc, post)


def _ffn_kernel(n_prompt_steps, xp_ref, xs_ref, pre_ref, post_ref, wg_hbm, wu_hbm, wd_hbm,
                yp_ref, ys_ref, wg_ref, wu_ref, wd_ref):
    i = pl.program_id(0)

    @pl.when(i == 0)
    def _():
        _stream_cast(wg_hbm, wg_ref)
        _stream_cast(wu_hbm, wu_ref)
        _stream_cast(wd_hbm, wd_ref)

    @pl.when(i < n_prompt_steps)
    def _():
        yp_ref[...] = _ffn_tile(xp_ref[...], pre_ref[...], post_ref[...], wg_ref, wu_ref, wd_ref)

    @pl.when(i == n_prompt_steps)
    def _():
        xs = xs_ref[...].reshape(xs_ref.shape[0], D_MODEL)
        ys = _ffn_tile(xs, pre_ref[...], post_ref[...], wg_ref, wu_ref, wd_ref)
        ys_ref[...] = ys.reshape(ys_ref.shape)


def _ffn_call(xp, xs, pre, post, wg, wu, wd, ys_shape):
    n_p, n_s = xp.shape[0], xs.shape[0]
    rows = min(FFN_ROWS, n_p)
    steps = n_p // rows
    prompt_spec = pl.BlockSpec((rows, D_MODEL), lambda i: (jnp.minimum(i, steps - 1), 0))

    def sample_spec(shape):
        zeros = (0,) * len(shape)
        return pl.BlockSpec(shape, lambda i: zeros)
    return pl.pallas_call(
        functools.partial(_ffn_kernel, steps),
        grid=(steps + 1,),
        in_specs=[prompt_spec, sample_spec(xs.shape),
                  _const_spec((1, D_MODEL)), _const_spec((1, D_MODEL)),
                  HBM_SPEC, HBM_SPEC, HBM_SPEC],
        out_specs=[prompt_spec, sample_spec(ys_shape)],
        out_shape=[jax.ShapeDtypeStruct(xp.shape, F32), jax.ShapeDtypeStruct(ys_shape, F32)],
        scratch_shapes=[pltpu.VMEM(wg.shape, BF16), pltpu.VMEM(wu.shape, BF16),
                        pltpu.VMEM(wd.shape, BF16)],
        compiler_params=pltpu.CompilerParams(
            dimension_semantics=("arbitrary",), vmem_limit_bytes=VMEM_LIMIT),
        name="ffn_half",
    )(xp, xs, pre, post, wg, wu, wd)


def _forget_lower_bound(lbl):
    l0, l1 = lbl[0:1, :], lbl[1:2, :]
    m = jnp.maximum(l0, l1)
    e0, e1 = jnp.exp(l0 - m), jnp.exp(l1 - m)
    return e0 / (e0 + e1)


def _lane_is_low(shape):
    return lax.broadcasted_iota(jnp.int32, shape, len(shape) - 1) < HEAD_DIM


def _dup_kv(x):
    swapped = pltpu.roll(x, HEAD_DIM, axis=1)
    low = _lane_is_low(x.shape)
    return jnp.where(low, x, swapped), jnp.where(low, swapped, x)


def _merge_out(h, w_ref, off, gates_from):
    return jax.nn.sigmoid(gates_from(off, D_MODEL)) * _dot(h.astype(BF16), w_ref[...])


def _pair_levels(c):
    t = jnp.arange(c, dtype=jnp.int32)[:, None]
    s = jnp.arange(c, dtype=jnp.int32)[None, :]
    x = jnp.maximum(t ^ s, 1)
    lvl = (31 - lax.clz(x)).astype(jnp.int32)
    return jnp.where(t > s, lvl, -1)


LOG2E = 1.4426950408889634


def _head(a, hd):
    return a[:, hd * HG_D:(hd + 1) * HG_D]


HG_LEVELS = HG_CHUNK.bit_length() - 1


def _chunk_rows(ci):
    return slice(ci * HG_CHUNK, (ci + 1) * HG_CHUNK)


def _hgrn_log_decay(g_ref, b_ref, n_chunks):
    c = HG_CHUNK
    width = g_ref.shape[1]
    row = lax.broadcasted_iota(jnp.int32, (c, c), 0)
    col = lax.broadcasted_iota(jnp.int32, (c, c), 1)
    tril = (col <= row).astype(BF16)
    for ci in range(n_chunks):
        r = _chunk_rows(ci)
        g = g_ref[r, :]
        hi = g.astype(BF16)
        rest = g - hi.astype(F32)
        mid = rest.astype(BF16)
        lo = (rest - mid.astype(F32)).astype(BF16)
        sums = _dot(tril, jnp.concatenate([hi, mid, lo], axis=1))
        b_ref[r, :] = (sums[:, 0:width] + sums[:, width:2 * width] + sums[:, 2 * width:]) * LOG2E


def _hgrn_factor_jobs(q_ref, k_ref, b_ref, x_ref, qe_ref, ke_ref, n_chunks):
    c = HG_CHUNK
    t = lax.broadcasted_iota(jnp.int32, (c, 1), 0)
    last = {}

    def level_job(ci, lvl):
        def run():
            n = 1 << lvl
            r = _chunk_rows(ci)
            bc = b_ref[r, :]
            second = (t & n) != 0
            if n < SUBLANES:
                prev = last.get(ci, bc)
                tiles = prev.reshape(c // SUBLANES, SUBLANES, prev.shape[1])
                second3 = second.reshape(c // SUBLANES, SUBLANES, 1)
                edge = jnp.where(second3, pltpu.roll(tiles, n, axis=1), tiles).reshape(prev.shape)
                if 2 * n < SUBLANES:
                    last[ci] = jnp.where(second3, tiles, pltpu.roll(tiles, SUBLANES - n, axis=1)
                                         ).reshape(prev.shape)
            else:
                edge = jnp.concatenate(
                    [jnp.broadcast_to(bc[p + n - 1:p + n, :], (2 * n, bc.shape[1]))
                     for p in range(0, c, 2 * n)], axis=0)
            w = jnp.exp2(jnp.where(second, bc - edge, edge - bc))
            x_ref[lvl, r, :] = (jnp.where(second, q_ref[r, :], k_ref[r, :]) * w).astype(BF16)
        return run

    def state_job(ci):
        def run():
            r = _chunk_rows(ci)
            bc = b_ref[r, :]
            qe_ref[r, :] = (q_ref[r, :] * jnp.exp2(bc)).astype(BF16)
            ke_ref[r, :] = (k_ref[r, :] * jnp.exp2(bc[c - 1:c, :] - bc)).astype(BF16)
        return run

    jobs = [level_job(ci, lvl) for lvl in range(HG_LEVELS) for ci in range(n_chunks)]
    return jobs + [state_job(ci) for ci in range(n_chunks)]


def _hgrn_pair_jobs(x_ref, levels, n_chunks):
    c = HG_CHUNK
    heads, chunks = range(HG_HEADS), range(n_chunks)
    a = [[jnp.zeros((c, c), F32) for _ in heads] for _ in chunks]

    def job(lvl, ci):
        def run():
            x = x_ref[lvl, _chunk_rows(ci), :]
            for hd in heads:
                al = lax.dot_general(_head(x, hd), _head(x, hd), NT_DIMS,
                                     preferred_element_type=F32)
                a[ci][hd] = jnp.where(levels == lvl, al, a[ci][hd])
        return run
    return [job(lvl, ci) for lvl in range(HG_LEVELS) for ci in chunks], a


def _hgrn_finish(q_ref, k_ref, v_ref, b_ref, a, qe_ref, ke_ref, st, n_chunks):
    c = HG_CHUNK
    heads, chunks = range(HG_HEADS), range(n_chunks)
    local = []
    for ci in chunks:
        r = _chunk_rows(ci)
        q, k, v = q_ref[r, :], k_ref[r, :], v_ref[r, :]
        qk, vb = q * k, v.astype(BF16)
        outs = []
        for hd in heads:
            diag = jnp.sum(_head(qk, hd), axis=1, keepdims=True)
            outs.append(_dot(a[ci][hd].astype(BF16), _head(vb, hd)) + diag * _head(v, hd))
        local.append(outs)

    result = []
    for ci in chunks:
        r = _chunk_rows(ci)
        qe, ke, vb = qe_ref[r, :], ke_ref[r, :], v_ref[r, :].astype(BF16)
        decay = jnp.exp2(b_ref[ci * c + c - 1:ci * c + c, :])
        outs, new = [], []
        for hd in heads:
            outs.append(local[ci][hd] + lax.dot_general(_head(qe, hd), st[hd].astype(BF16), NT_DIMS,
                                                        preferred_element_type=F32))
            new.append(st[hd] * _head(decay, hd)
                       + lax.dot_general(_head(vb, hd), _head(ke, hd), TN_DIMS,
                                         preferred_element_type=F32))
        st = new
        result.append(outs)
    return result, st


def _mix_prompt_kernel(sinks_ref, x_ref, pre_ref, post_ref, lbl_ref, hgn_ref, lvl_ref,
                       win_hbm, wao_hbm, who_hbm, wo_hbm,
                       y_ref, nk_ref, nv_ref, ns_ref,
                       win_ref, wao_ref, who_ref, wo_ref,
                       kbuf, vbuf, st_ref, qh_ref, kh_ref, vh_ref, lf_ref, oa_ref, oh_ref, g_ref,
                       b_ref, xl_ref, qe_ref, ke_ref):
    j = pl.program_id(1)
    rows = x_ref.shape[0]
    n_blocks = rows // WINDOW

    @pl.when((pl.program_id(0) == 0) & (j == 0))
    def _():
        for src, dst in ((win_hbm, win_ref), (wao_hbm, wao_ref), (who_hbm, who_ref),
                         (wo_hbm, wo_ref)):
            _stream_cast(src, dst)

    @pl.when(j == 0)
    def _():
        kbuf[0:WINDOW, :] = jnp.zeros((WINDOW, LANES), F32)
        vbuf[0:WINDOW, :] = jnp.zeros((WINDOW, LANES), F32)
        st_ref[...] = jnp.zeros(st_ref.shape, F32)

    part_rows = [slice(r0, r0 + rows // NORM_PARTS) for r0 in range(0, rows, rows // NORM_PARTS)]
    h_parts, qkv_parts = [], []
    for r in part_rows:
        hp = _rms(x_ref[r, :], pre_ref[...]).astype(BF16)
        h_parts.append(hp)
        qkv_parts.append(_dot(hp, win_ref[:, OFF_QA:OFF_QH]))
    h = jnp.concatenate(h_parts, axis=0)
    qkv = jnp.concatenate(qkv_parts, axis=0)

    def proj(off, width):
        return _dot(h, win_ref[:, off:off + width])

    lb = _forget_lower_bound(lbl_ref[...])
    n_chunks = rows // HG_CHUNK

    def proj_slab(off):
        def run():
            z = proj(off, FFN_CHUNK)
            if off < OFF_FH:
                cols = slice(off - OFF_QH, off - OFF_QH + FFN_CHUNK)
                qh_ref[:, cols] = z * jax.nn.sigmoid(z)
            elif off < OFF_IH:
                cols = slice(off - OFF_FH, off - OFF_FH + FFN_CHUNK)
                lbs = lb[:, cols]
                lf_ref[:, cols] = jnp.log(lbs + (1.0 - lbs) * jax.nn.sigmoid(z))
                kh_ref[:, cols] = (1.0 - lbs) * jax.nn.sigmoid(-z)
            elif off < OFF_GH:
                cols = slice(off - OFF_IH, off - OFF_IH + FFN_CHUNK)
                vh_ref[:, cols] = z
            else:
                cols = slice(off - OFF_GH, off - OFF_GH + FFN_CHUNK)
                g_ref[:, cols] = z
        return run
    hgrn_slabs = [proj_slab(off) for off in range(OFF_QH, OFF_GH, FFN_CHUNK)]
    gate_slabs = [proj_slab(off) for off in range(OFF_GH, IN_WIDTH, FFN_CHUNK)]
    jobs = _hgrn_factor_jobs(qh_ref, kh_ref, b_ref, xl_ref, qe_ref, ke_ref, n_chunks)
    pair_jobs, in_chunk = _hgrn_pair_jobs(xl_ref, lvl_ref[...], n_chunks)

    qa = qkv[:, OFF_QA:OFF_KA] * (HEAD_DIM ** -0.5)
    ka = qkv[:, OFF_KA:OFF_VA]
    va = qkv[:, OFF_VA:OFF_QH]
    kbuf[WINDOW:WINDOW + rows, :] = ka
    vbuf[WINDOW:WINDOW + rows, :] = va

    qi = lax.broadcasted_iota(jnp.int32, (WINDOW, 2 * WINDOW), 0)
    kj = lax.broadcasted_iota(jnp.int32, (WINDOW, 2 * WINDOW), 1)
    band = (kj > qi) & (kj <= qi + WINDOW)
    low = _lane_is_low((WINDOW, LANES))
    top = lax.broadcasted_iota(jnp.int32, (2 * WINDOW, 1), 0) < WINDOW
    n_slabs = ATTN_Q_HEADS // 2
    group = ATTN_Q_HEADS // ATTN_KV_HEADS
    kds = [_dup_kv(kbuf[n * WINDOW:(n + 2) * WINDOW, :]) for n in range(n_blocks)]
    vds = [_dup_kv(vbuf[n * WINDOW:(n + 2) * WINDOW, :]) for n in range(n_blocks)]

    def scores(n, slab):
        qs = qa[n * WINDOW:(n + 1) * WINDOW, slab * LANES:(slab + 1) * LANES]
        q2 = jnp.concatenate([jnp.where(low, qs, 0.0), jnp.where(low, 0.0, qs)], axis=0)
        return lax.dot_general(q2.astype(BF16), kds[n][(2 * slab) // group].astype(BF16), NT_DIMS,
                               preferred_element_type=F32)

    def attend(n, slab, s):
        valid = band & ((j > 0) | (kj >= WINDOW)) if n == 0 else band
        s = jnp.where(jnp.concatenate([valid, valid], axis=0), s, -jnp.inf)
        sk = jnp.where(top, sinks_ref[0, 2 * slab], sinks_ref[0, 2 * slab + 1])
        m = jnp.maximum(jnp.max(s, axis=1, keepdims=True), sk)
        p = jnp.exp(s - m)
        den = jnp.sum(p, axis=1, keepdims=True) + jnp.exp(sk - m)
        o2 = _dot(p.astype(BF16), vds[n][(2 * slab) // group].astype(BF16)) / den
        oa_ref[n * WINDOW:(n + 1) * WINDOW, slab * LANES:(slab + 1) * LANES] = (
            jnp.where(low, o2[0:WINDOW], o2[WINDOW:2 * WINDOW]))

    order = [(n, slab) for n in range(n_blocks) for slab in range(n_slabs)]
    early = len(hgrn_slabs)
    assert len(order) >= early + len(gate_slabs)
    pending = scores(*order[0])
    ready_pairs = 0
    for idx, (n, slab) in enumerate(order):
        nxt = scores(*order[idx + 1]) if idx + 1 < len(order) else None
        if idx < early:
            hgrn_slabs[idx]()
        else:
            if idx == early:
                _hgrn_log_decay(lf_ref, b_ref, n_chunks)
            if gate_slabs:
                gate_slabs.pop(0)()
            for _ in range(min(len(pair_jobs), ready_pairs)):
                pair_jobs.pop(0)()
            ready_pairs = 0
            for _ in range(-(-len(jobs) // (len(order) - 1 - idx)) if idx + 1 < len(order) else 0):
                if jobs:
                    jobs.pop(0)()
                    ready_pairs += 1
        attend(n, slab, pending)
        pending = nxt
    kbuf[0:WINDOW, :] = kbuf[rows:rows + WINDOW, :]
    vbuf[0:WINDOW, :] = vbuf[rows:rows + WINDOW, :]
    assert not jobs and not gate_slabs
    for run in pair_jobs:
        run()

    hgn = hgn_ref[...]
    st0 = [st_ref[hd * HG_D:(hd + 1) * HG_D, :] for hd in range(HG_HEADS)]
    outs, states = _hgrn_finish(qh_ref, kh_ref, vh_ref, b_ref, in_chunk, qe_ref, ke_ref, st0,
                                n_chunks)
    for hd in range(HG_HEADS):
        cs = slice(hd * HG_D, (hd + 1) * HG_D)
        st_ref[cs, :] = states[hd]
        for ci in range(n_chunks):
            oh_ref[ci * HG_CHUNK:(ci + 1) * HG_CHUNK, cs] = _rms(outs[ci][hd], hgn)

    gh = g_ref[:, 0:HG_W]
    oh = oh_ref[...] * (gh * jax.nn.sigmoid(gh))

    def gate(off, width):
        return g_ref[:, off - OFF_GH:off - OFF_GH + width]
    m = (_merge_out(oa_ref[...], wao_ref, OFF_GA, gate)
         + _merge_out(oh, who_ref, OFF_GB, gate))
    mb = m.astype(BF16)
    for r in part_rows:
        y_ref[r, :] = x_ref[r, :] + _rms(_dot(mb[r, :], wo_ref[...]), post_ref[...])

    @pl.when(j == pl.num_programs(1) - 1)
    def _():
        nk_ref[0] = kbuf[0:WINDOW, :].T
        nv_ref[0] = vbuf[0:WINDOW, :].T
        for hd in range(HG_HEADS):
            cs = slice(hd * HG_D, (hd + 1) * HG_D)
            ns_ref[0, cs, :] = st_ref[cs, :].T


def _mix_prompt_call(x, batch, sinks, pre, post, lbl, hgn, win, wao, who, wo):
    seq = x.shape[0] // batch
    rows = min(MIX_ROWS, seq)
    steps = seq // rows
    x_spec = pl.BlockSpec((rows, D_MODEL), lambda b, j: (b * steps + j, 0))
    kv_spec = pl.BlockSpec((1, ATTN_KV_W, WINDOW), lambda b, j: (b, 0, 0))
    st_spec = pl.BlockSpec((1, HG_W, HG_D), lambda b, j: (b, 0, 0))
    act = pltpu.VMEM((rows, HG_W), F32)
    return pl.pallas_call(
        _mix_prompt_kernel,
        grid=(batch, steps),
        in_specs=[pl.BlockSpec(memory_space=pltpu.SMEM), x_spec,
                  _const_spec((1, D_MODEL)), _const_spec((1, D_MODEL)),
                  _const_spec(lbl.shape), _const_spec((1, HG_D)),
                  _const_spec((HG_CHUNK, HG_CHUNK)),
                  HBM_SPEC, HBM_SPEC, HBM_SPEC, HBM_SPEC],
        out_specs=[x_spec, kv_spec, kv_spec, st_spec],
        out_shape=[jax.ShapeDtypeStruct(x.shape, F32),
                   jax.ShapeDtypeStruct((batch, ATTN_KV_W, WINDOW), F32),
                   jax.ShapeDtypeStruct((batch, ATTN_KV_W, WINDOW), F32),
                   jax.ShapeDtypeStruct((batch, HG_W, HG_D), F32)],
        scratch_shapes=[pltpu.VMEM(win.shape, BF16), pltpu.VMEM(wao.shape, BF16),
                        pltpu.VMEM(who.shape, BF16), pltpu.VMEM(wo.shape, BF16),
                        pltpu.VMEM((rows + WINDOW, ATTN_KV_W), F32),
                        pltpu.VMEM((rows + WINDOW, ATTN_KV_W), F32),
                        pltpu.VMEM((HG_W, HG_D), F32),
                        act, act, act, act,
                        pltpu.VMEM((rows, ATTN_Q_W), F32), act,
                        pltpu.VMEM((rows, IN_WIDTH - OFF_GH), F32),
                        act, pltpu.VMEM((HG_LEVELS, rows, HG_W), BF16),
                        pltpu.VMEM((rows, HG_W), BF16), pltpu.VMEM((rows, HG_W), BF16)],
        compiler_params=pltpu.CompilerParams(
            dimension_semantics=("arbitrary", "arbitrary"), vmem_limit_bytes=VMEM_LIMIT),
        name="mix_prompt",
    )(sinks, x, pre, post, lbl, hgn, _pair_levels(HG_CHUNK), win, wao, who, wo)


def _row_select(rows_list):
    n = -(-len(rows_list) // 8) * 8
    lanes = rows_list[0].shape[1]
    ridx = lax.broadcasted_iota(jnp.int32, (n, lanes), 0)
    out = jnp.zeros((n, lanes), F32)
    for i, r in enumerate(rows_list):
        out = jnp.where(ridx == i, jnp.broadcast_to(r, (n, lanes)), out)
    return out


def _mix_sample_kernel(sinks_ref, x_ref, ck_ref, cv_ref, s0_ref, pre_ref, post_ref, lbl_ref, hgn_ref,
                       win_hbm, wao_hbm, who_hbm, wo_hbm,
                       y_ref, nk_ref, nv_ref, ns_ref,
                       win_ref, wao_ref, who_ref, wo_ref,
                       h_ref, qa_ref, ka_ref, va_ref, qh_ref, kh_ref, fh_ref, vh_ref, oa_ref, oh_ref):
    i = pl.program_id(0)
    tile = ck_ref.shape[0]

    def proj(off, width):
        return _dot(h_ref[...], win_ref[:, off:off + width])

    @pl.when(i == 0)
    def _():
        for src, dst in ((win_hbm, win_ref), (wao_hbm, wao_ref), (who_hbm, who_ref),
                         (wo_hbm, wo_ref)):
            _stream_cast(src, dst)
        h_ref[...] = _rms(x_ref[...], pre_ref[...]).astype(BF16)
        qa_ref[...] = proj(OFF_QA, ATTN_Q_W) * (HEAD_DIM ** -0.5)
        ka_ref[...] = proj(OFF_KA, ATTN_KV_W)
        va_ref[...] = proj(OFF_VA, ATTN_KV_W)
        lb = _forget_lower_bound(lbl_ref[...])
        qh = proj(OFF_QH, HG_W)
        qh_ref[...] = qh * jax.nn.sigmoid(qh)
        fp = proj(OFF_FH, HG_W)
        fh_ref[...] = jnp.exp(jnp.log(lb + (1.0 - lb) * jax.nn.sigmoid(fp)))
        kh_ref[...] = (1.0 - lb) * jax.nn.sigmoid(-fp)
        vh_ref[...] = proj(OFF_IH, HG_W)

    low = _lane_is_low((1, LANES))
    newest = lax.broadcasted_iota(jnp.int32, (ATTN_KV_W, WINDOW), 1) == WINDOW - 1
    hrow = lax.broadcasted_iota(jnp.int32, (ATTN_Q_HEADS, 1), 0)
    sk = jnp.zeros((ATTN_Q_HEADS, 1), F32)
    for hd in range(ATTN_Q_HEADS):
        sk = jnp.where(hrow == hd, sinks_ref[0, hd], sk)
    group = ATTN_Q_HEADS // ATTN_KV_HEADS
    hgn = hgn_ref[...]

    r8 = pl.ds(pl.multiple_of(i * tile, tile), tile)
    qa8, ka8, va8 = qa_ref[r8, :], ka_ref[r8, :], va_ref[r8, :]
    fh8, kh8, qh8, vh8 = fh_ref[r8, :], kh_ref[r8, :], qh_ref[r8, :], vh_ref[r8, :]
    oa_rows = [[] for _ in range(ATTN_Q_HEADS // 2)]
    oh_rows = [[] for _ in range(HG_HEADS)]
    seqs = range(tile)
    def as_columns(rows8):
        pad = jnp.zeros((LANES - tile, rows8.shape[1]), F32)
        return jnp.concatenate([rows8, pad], axis=0).T
    k_cols, v_cols = as_columns(ka8), as_columns(va8)
    kws, vws, q8s = [], [], []
    for bi in seqs:
        r1 = slice(bi, bi + 1)
        kw = jnp.where(newest, pltpu.roll(k_cols, WINDOW - 1 - bi, axis=1),
                       pltpu.roll(ck_ref[bi], WINDOW - 1, axis=1))
        vw = jnp.where(newest, pltpu.roll(v_cols, WINDOW - 1 - bi, axis=1),
                       pltpu.roll(cv_ref[bi], WINDOW - 1, axis=1))
        nk_ref[bi] = kw
        nv_ref[bi] = vw
        kws.append(kw.astype(BF16))
        vws.append(vw.astype(BF16))
        qrows = []
        for hd in range(ATTN_Q_HEADS):
            slab = qa8[r1, (hd // 2) * LANES:(hd // 2 + 1) * LANES]
            in_place = (hd % 2) == (hd // group)
            src = slab if in_place else pltpu.roll(slab, HEAD_DIM, axis=1)
            on_kv_lanes = low if hd // group == 0 else jnp.logical_not(low)
            qrows.append(jnp.where(on_kv_lanes, src, 0.0))
        q8s.append(_row_select(qrows).astype(BF16))
    scores = [_dot(q8s[bi], kws[bi]) for bi in seqs]
    probs = []
    for s in scores:
        m = jnp.maximum(jnp.max(s, axis=1, keepdims=True), sk)
        p = jnp.exp(s - m)
        probs.append((p / (jnp.sum(p, axis=1, keepdims=True) + jnp.exp(sk - m))).astype(BF16))
    for bi in seqs:
        o8 = lax.dot_general(probs[bi], vws[bi], NT_DIMS, preferred_element_type=F32)
        for slab in range(ATTN_Q_HEADS // 2):
            kv = (2 * slab) // group
            even, odd = o8[2 * slab:2 * slab + 1, :], o8[2 * slab + 1:2 * slab + 2, :]
            if kv == 0:
                out = jnp.where(low, even, pltpu.roll(odd, HEAD_DIM, axis=1))
            else:
                out = jnp.where(low, pltpu.roll(even, HEAD_DIM, axis=1), odd)
            oa_rows[slab].append(out)

    def columns(a8, hd):
        pad = jnp.zeros((HG_D - tile, HG_D), F32)
        return jnp.concatenate([_head(a8, hd), pad], axis=0).T
    seq_row = lax.broadcasted_iota(jnp.int32, (HG_D, HG_D), 0)
    heads = range(HG_HEADS)
    head_rows = [slice(hd * HG_D, (hd + 1) * HG_D) for hd in heads]
    f_cols = [columns(fh8, hd) for hd in heads]
    k_cols = [columns(kh8, hd).astype(BF16) for hd in heads]
    v_rows = [jnp.concatenate([_head(vh8, hd), jnp.zeros((HG_D - tile, HG_D), F32)], axis=0)
              for hd in heads]
    outers = [[_dot(k_cols[hd], jnp.where(seq_row == bi, v_rows[hd], 0.0).astype(BF16))
               for bi in seqs] for hd in heads]
    for hd in heads:
        for bi in seqs:
            f_col = jnp.broadcast_to(f_cols[hd][:, bi:bi + 1], (HG_D, HG_D))
            ns_ref[bi, head_rows[hd], :] = f_col * s0_ref[bi, head_rows[hd], :] + outers[hd][bi]
    outs = [[_dot(_head(qh8, hd).astype(BF16), ns_ref[bi, head_rows[hd], :].astype(BF16))
             for bi in seqs] for hd in heads]
    for hd in heads:
        for bi in seqs:
            oh_rows[hd].append(_rms(outs[hd][bi][bi:bi + 1, :], hgn))
    for slab in range(ATTN_Q_HEADS // 2):
        oa_ref[r8, slab * LANES:(slab + 1) * LANES] = _row_select(oa_rows[slab])
    for hd in range(HG_HEADS):
        oh_ref[r8, hd * HG_D:(hd + 1) * HG_D] = _row_select(oh_rows[hd])

    @pl.when(i == pl.num_programs(0) - 1)
    def _():
        gh = proj(OFF_GH, HG_W)
        oh = oh_ref[...] * (gh * jax.nn.sigmoid(gh))
        m = (_merge_out(oa_ref[...], wao_ref, OFF_GA, proj)
             + _merge_out(oh, who_ref, OFF_GB, proj))
        y_ref[...] = x_ref[...] + _rms(_dot(m.astype(BF16), wo_ref[...]), post_ref[...])


def _mix_sample_call(x, ck, cv, s0, sinks, pre, post, lbl, hgn, win, wao, who, wo):
    n = x.shape[0]
    tile = min(SAMPLE_TILE, n)
    full = pl.BlockSpec((n, D_MODEL), lambda i: (0, 0))
    kv_spec = pl.BlockSpec((tile, ATTN_KV_W, WINDOW), lambda i: (i, 0, 0))
    st_spec = pl.BlockSpec((tile, HG_W, HG_D), lambda i: (i, 0, 0))
    act = pltpu.VMEM((n, HG_W), F32)
    kv_act = pltpu.VMEM((n, ATTN_KV_W), F32)
    return pl.pallas_call(
        _mix_sample_kernel,
        grid=(n // tile,),
        in_specs=[pl.BlockSpec(memory_space=pltpu.SMEM), full, kv_spec, kv_spec, st_spec,
                  _const_spec((1, D_MODEL)), _const_spec((1, D_MODEL)),
                  _const_spec(lbl.shape), _const_spec((1, HG_D)),
                  HBM_SPEC, HBM_SPEC, HBM_SPEC, HBM_SPEC],
        out_specs=[full, kv_spec, kv_spec, st_spec],
        out_shape=[jax.ShapeDtypeStruct(x.shape, F32),
                   jax.ShapeDtypeStruct(ck.shape, F32),
                   jax.ShapeDtypeStruct(cv.shape, F32),
                   jax.ShapeDtypeStruct(s0.shape, F32)],
        scratch_shapes=[pltpu.VMEM(win.shape, BF16), pltpu.VMEM(wao.shape, BF16),
                        pltpu.VMEM(who.shape, BF16), pltpu.VMEM(wo.shape, BF16),
                        pltpu.VMEM((n, D_MODEL), BF16),
                        pltpu.VMEM((n, ATTN_Q_W), F32), kv_act, kv_act,
                        act, act, act, act,
                        pltpu.VMEM((n, ATTN_Q_W), F32), act],
        compiler_params=pltpu.CompilerParams(
            dimension_semantics=("arbitrary",), vmem_limit_bytes=VMEM_LIMIT),
        name="mix_sample",
    )(sinks, x, ck, cv, s0, pre, post, lbl, hgn, win, wao, who, wo)


def kernel(x_prompt, x_sample, cache_k, cache_v, state_hgrn, norm_ffn1_pre, norm_ffn1_post, w_ffn1_gate, w_ffn1_up, w_ffn1_down, norm_mix_pre, norm_mix_post, w_in, attn_sinks, hgrn_lb_logits, hgrn_norm, w_attn_out, w_hgrn_out, w_out, norm_ffn2_pre, norm_ffn2_post, w_ffn2_gate, w_ffn2_up, w_ffn2_down):
    depth = w_in.shape[0]
    assert depth == 1 and hgrn_lb_logits.shape[0] == 2, "single-layer stack only"
    batch, seq, _ = x_prompt.shape
    n_s = x_sample.shape[0]
    assert x_sample.shape[1] == 1 and seq % WINDOW == 0

    xp = x_prompt.reshape(batch * seq, D_MODEL)
    ck = jnp.swapaxes(cache_k[0].reshape(n_s, WINDOW, ATTN_KV_W), 1, 2)
    cv = jnp.swapaxes(cache_v[0].reshape(n_s, WINDOW, ATTN_KV_W), 1, 2)
    s0 = state_hgrn[0].reshape(n_s, HG_W, HG_D)

    xp, xs = _ffn_call(xp, x_sample, norm_ffn1_pre, norm_ffn1_post,
                       w_ffn1_gate[0], w_ffn1_up[0], w_ffn1_down[0], (n_s, D_MODEL))

    mix_w = (attn_sinks, norm_mix_pre, norm_mix_post, hgrn_lb_logits, hgrn_norm,
             w_in[0], w_attn_out[0], w_hgrn_out[0], w_out[0])
    xp, nkp, nvp, nsp = _mix_prompt_call(xp, batch, *mix_w)
    xs, nks, nvs, nss = _mix_sample_call(xs, ck, cv, s0, *mix_w)

    xp, xs = _ffn_call(xp, xs, norm_ffn2_pre, norm_ffn2_post,
                       w_ffn2_gate[0], w_ffn2_up[0], w_ffn2_down[0], x_sample.shape)

    kv_shape = (1, -1, WINDOW, ATTN_KV_HEADS, HEAD_DIM)
    st_shape = (1, -1, HG_HEADS, HG_D, HG_D)
    def window(t):
        return jnp.swapaxes(t, 1, 2).reshape(kv_shape)
    return (xp.reshape(batch, seq, D_MODEL), xs,
            window(nkp), window(nvp), nsp.reshape(st_shape),
            window(nks), window(nvs), nss.reshape(st_shape))
```

```python
import functools

import jax
import jax.numpy as jnp
from jax import lax
from jax.experimental import pallas as pl
from jax.experimental.pallas import tpu as pltpu

F32 = jnp.float32
BF16 = jnp.bfloat16

D_MODEL = 1024
FFN_DIM = 2816
HEAD_DIM = 64
ATTN_Q_HEADS = 8
ATTN_KV_HEADS = 2
WINDOW = 128
HG_HEADS = 4
HG_D = 128
EPS = 1e-6

ATTN_Q_W = ATTN_Q_HEADS * HEAD_DIM
ATTN_KV_W = ATTN_KV_HEADS * HEAD_DIM
HG_W = HG_HEADS * HG_D
OFF_QA = 0
OFF_KA = OFF_QA + ATTN_Q_W
OFF_VA = OFF_KA + ATTN_KV_W
OFF_QH = OFF_VA + ATTN_KV_W
OFF_FH = OFF_QH + HG_W
OFF_IH = OFF_FH + HG_W
OFF_GH = OFF_IH + HG_W
OFF_GA = OFF_GH + HG_W
OFF_GB = OFF_GA + D_MODEL
IN_WIDTH = OFF_GB + D_MODEL

LANES = 128
SUBLANES = 8
FFN_CHUNK = 256
N_FFN_CHUNKS = FFN_DIM // FFN_CHUNK
FFN_ROWS = 1024
MIX_ROWS = 512
NORM_PARTS = 2
HG_CHUNK = 128
SAMPLE_TILE = 16
VMEM_LIMIT = 56 * 1024 * 1024

NT_DIMS = (((1,), (1,)), ((), ()))
TN_DIMS = (((0,), (0,)), ((), ()))


def _rms(x, g):
    return x * lax.rsqrt(jnp.mean(x * x, axis=-1, keepdims=True) + EPS) * g


def _dot(a, b):
    return jnp.dot(a, b, preferred_element_type=F32)


STREAM_SLOTS = 4
STREAM_CHUNK_BYTES = 2 << 20


def _stream_cast(src_hbm, dst_ref):
    n_rows, n_cols = src_hbm.shape
    row_bytes = n_cols * jnp.dtype(F32).itemsize
    rc = max(SUBLANES, min(n_rows, STREAM_CHUNK_BYTES // row_bytes // SUBLANES * SUBLANES))
    while n_rows % rc:
        rc -= SUBLANES
    n = n_rows // rc
    slots = min(STREAM_SLOTS, n)

    def body(stage, sem):
        def copy(c):
            return pltpu.make_async_copy(src_hbm.at[pl.ds(c * rc, rc), :], stage.at[c % slots],
                                         sem.at[c % slots])
        for c in range(slots):
            copy(c).start(priority=c % 2)
        for c in range(n):
            copy(c).wait()
            dst_ref[c * rc:(c + 1) * rc, :] = stage[c % slots].astype(BF16)
            if c + slots < n:
                copy(c + slots).start(priority=(c + slots) % 2)

    pl.run_scoped(body, pltpu.VMEM((slots, rc, n_cols), F32), pltpu.SemaphoreType.DMA((slots,)))


HBM_SPEC = pl.BlockSpec(memory_space=pl.ANY)


def _const_spec(shape):
    zeros = (0,) * len(shape)
    return pl.BlockSpec(shape, lambda *_: zeros, pipeline_mode=pl.Buffered(1))


def _ffn_tile(x, pre, post, wg_ref, wu_ref, wd_ref):
    h = _rms(x, pre).astype(BF16)
    acc = None
    for c in range(N_FFN_CHUNKS):
        cols = slice(c * FFN_CHUNK, (c + 1) * FFN_CHUNK)
        g = _dot(h, wg_ref[:, cols])
        u = _dot(h, wu_ref[:, cols])
        a = (g * jax.nn.sigmoid(g) * u).astype(BF16)
        d = _dot(a, wd_ref[cols, :])
        acc = d if acc is None else acc + d
    return x + 0.5 * _rms(acc, post)


def _ffn_kernel(n_prompt_steps, xp_ref, xs_ref, pre_ref, post_ref, wg_hbm, wu_hbm, wd_hbm,
                yp_ref, ys_ref, wg_ref, wu_ref, wd_ref):
    i = pl.program_id(0)

    @pl.when(i == 0)
    def _():
        _stream_cast(wg_hbm, wg_ref)
        _stream_cast(wu_hbm, wu_ref)
        _stream_cast(wd_hbm, wd_ref)

    @pl.when(i < n_prompt_steps)
    def _():
        yp_ref[...] = _ffn_tile(xp_ref[...], pre_ref[...], post_ref[...], wg_ref, wu_ref, wd_ref)

    @pl.when(i == n_prompt_steps)
    def _():
        xs = xs_ref[...].reshape(xs_ref.shape[0], D_MODEL)
        ys = _ffn_tile(xs, pre_ref[...], post_ref[...], wg_ref, wu_ref, wd_ref)
        ys_ref[...] = ys.reshape(ys_ref.shape)


def _ffn_call(xp, xs, pre, post, wg, wu, wd, ys_shape):
    n_p, n_s = xp.shape[0], xs.shape[0]
    rows = min(FFN_ROWS, n_p)
    steps = n_p // rows
    prompt_spec = pl.BlockSpec((rows, D_MODEL), lambda i: (jnp.minimum(i, steps - 1), 0))

    def sample_spec(shape):
        zeros = (0,) * len(shape)
        return pl.BlockSpec(shape, lambda i: zeros)
    return pl.pallas_call(
        functools.partial(_ffn_kernel, steps),
        grid=(steps + 1,),
        in_specs=[prompt_spec, sample_spec(xs.shape),
                  _const_spec((1, D_MODEL)), _const_spec((1, D_MODEL)),
                  HBM_SPEC, HBM_SPEC, HBM_SPEC],
        out_specs=[prompt_spec, sample_spec(ys_shape)],
        out_shape=[jax.ShapeDtypeStruct(xp.shape, F32), jax.ShapeDtypeStruct(ys_shape, F32)],
        scratch_shapes=[pltpu.VMEM(wg.shape, BF16), pltpu.VMEM(wu.shape, BF16),
                        pltpu.VMEM(wd.shape, BF16)],
        compiler_params=pltpu.CompilerParams(
            dimension_semantics=("arbitrary",), vmem_limit_bytes=VMEM_LIMIT),
        name="ffn_half",
    )(xp, xs, pre, post, wg, wu, wd)


def _forget_lower_bound(lbl):
    l0, l1 = lbl[0:1, :], lbl[1:2, :]
    m = jnp.maximum(l0, l1)
    e0, e1 = jnp.exp(l0 - m), jnp.exp(l1 - m)
    return e0 / (e0 + e1)


def _lane_is_low(shape):
    return lax.broadcasted_iota(jnp.int32, shape, len(shape) - 1) < HEAD_DIM


def _dup_kv(x):
    swapped = pltpu.roll(x, HEAD_DIM, axis=1)
    low = _lane_is_low(x.shape)
    return jnp.where(low, x, swapped), jnp.where(low, swapped, x)


def _merge_out(h, w_ref, off, gates_from):
    return jax.nn.sigmoid(gates_from(off, D_MODEL)) * _dot(h.astype(BF16), w_ref[...])


def _pair_levels(c):
    t = jnp.arange(c, dtype=jnp.int32)[:, None]
    s = jnp.arange(c, dtype=jnp.int32)[None, :]
    x = jnp.maximum(t ^ s, 1)
    lvl = (31 - lax.clz(x)).astype(jnp.int32)
    return jnp.where(t > s, lvl, -1)


LOG2E = 1.4426950408889634


def _head(a, hd):
    return a[:, hd * HG_D:(hd + 1) * HG_D]


HG_LEVELS = HG_CHUNK.bit_length() - 1


def _chunk_rows(ci):
    return slice(ci * HG_CHUNK, (ci + 1) * HG_CHUNK)


def _hgrn_log_decay(g_ref, b_ref, n_chunks):
    c = HG_CHUNK
    width = g_ref.shape[1]
    row = lax.broadcasted_iota(jnp.int32, (c, c), 0)
    col = lax.broadcasted_iota(jnp.int32, (c, c), 1)
    tril = (col <= row).astype(BF16)
    for ci in range(n_chunks):
        r = _chunk_rows(ci)
        g = g_ref[r, :]
        hi = g.astype(BF16)
        rest = g - hi.astype(F32)
        mid = rest.astype(BF16)
        lo = (rest - mid.astype(F32)).astype(BF16)
        sums = _dot(tril, jnp.concatenate([hi, mid, lo], axis=1))
        b_ref[r, :] = (sums[:, 0:width] + sums[:, width:2 * width] + sums[:, 2 * width:]) * LOG2E


def _hgrn_factor_jobs(q_ref, k_ref, b_ref, x_ref, qe_ref, ke_ref, n_chunks):
    c = HG_CHUNK
    t = lax.broadcasted_iota(jnp.int32, (c, 1), 0)
    last = {}

    def level_job(ci, lvl):
        def run():
            n = 1 << lvl
            r = _chunk_rows(ci)
            bc = b_ref[r, :]
            second = (t & n) != 0
            if n < SUBLANES:
                prev = last.get(ci, bc)
                tiles = prev.reshape(c // SUBLANES, SUBLANES, prev.shape[1])
                second3 = second.reshape(c // SUBLANES, SUBLANES, 1)
                edge = jnp.where(second3, pltpu.roll(tiles, n, axis=1), tiles).reshape(prev.shape)
                if 2 * n < SUBLANES:
                    last[ci] = jnp.where(second3, tiles, pltpu.roll(tiles, SUBLANES - n, axis=1)
                                         ).reshape(prev.shape)
            else:
                edge = jnp.concatenate(
                    [jnp.broadcast_to(bc[p + n - 1:p + n, :], (2 * n, bc.shape[1]))
                     for p in range(0, c, 2 * n)], axis=0)
            w = jnp.exp2(jnp.where(second, bc - edge, edge - bc))
            x_ref[lvl, r, :] = (jnp.where(second, q_ref[r, :], k_ref[r, :]) * w).astype(BF16)
        return run

    def state_job(ci):
        def run():
            r = _chunk_rows(ci)
            bc = b_ref[r, :]
            qe_ref[r, :] = (q_ref[r, :] * jnp.exp2(bc)).astype(BF16)
            ke_ref[r, :] = (k_ref[r, :] * jnp.exp2(bc[c - 1:c, :] - bc)).astype(BF16)
        return run

    jobs = [level_job(ci, lvl) for lvl in range(HG_LEVELS) for ci in range(n_chunks)]
    return jobs + [state_job(ci) for ci in range(n_chunks)]


def _hgrn_pair_jobs(x_ref, levels, n_chunks):
    c = HG_CHUNK
    heads, chunks = range(HG_HEADS), range(n_chunks)
    a = [[jnp.zeros((c, c), F32) for _ in heads] for _ in chunks]

    def job(lvl, ci):
        def run():
            x = x_ref[lvl, _chunk_rows(ci), :]
            for hd in heads:
                al = lax.dot_general(_head(x, hd), _head(x, hd), NT_DIMS,
                                     preferred_element_type=F32)
                a[ci][hd] = jnp.where(levels == lvl, al, a[ci][hd])
        return run
    return [job(lvl, ci) for lvl in range(HG_LEVELS) for ci in chunks], a


def _hgrn_finish(q_ref, k_ref, v_ref, b_ref, a, qe_ref, ke_ref, st, n_chunks):
    c = HG_CHUNK
    heads, chunks = range(HG_HEADS), range(n_chunks)
    local = []
    for ci in chunks:
        r = _chunk_rows(ci)
        q, k, v = q_ref[r, :], k_ref[r, :], v_ref[r, :]
        qk, vb = q * k, v.astype(BF16)
        outs = []
        for hd in heads:
            diag = jnp.sum(_head(qk, hd), axis=1, keepdims=True)
            outs.append(_dot(a[ci][hd].astype(BF16), _head(vb, hd)) + diag * _head(v, hd))
        local.append(outs)

    result = []
    for ci in chunks:
        r = _chunk_rows(ci)
        qe, ke, vb = qe_ref[r, :], ke_ref[r, :], v_ref[r, :].astype(BF16)
        decay = jnp.exp2(b_ref[ci * c + c - 1:ci * c + c, :])
        outs, new = [], []
        for hd in heads:
            outs.append(local[ci][hd] + lax.dot_general(_head(qe, hd), st[hd].astype(BF16), NT_DIMS,
                                                        preferred_element_type=F32))
            new.append(st[hd] * _head(decay, hd)
                       + lax.dot_general(_head(vb, hd), _head(ke, hd), TN_DIMS,
                                         preferred_element_type=F32))
        st = new
        result.append(outs)
    return result, st


def _mix_prompt_kernel(sinks_ref, x_ref, pre_ref, post_ref, lbl_ref, hgn_ref, lvl_ref,
                       win_hbm, wao_hbm, who_hbm, wo_hbm,
                       y_ref, nk_ref, nv_ref, ns_ref,
                       win_ref, wao_ref, who_ref, wo_ref,
                       kbuf, vbuf, st_ref, qh_ref, kh_ref, vh_ref, lf_ref, oa_ref, oh_ref, g_ref,
                       b_ref, xl_ref, qe_ref, ke_ref):
    j = pl.program_id(1)
    rows = x_ref.shape[0]
    n_blocks = rows // WINDOW

    @pl.when((pl.program_id(0) == 0) & (j == 0))
    def _():
        for src, dst in ((win_hbm, win_ref), (wao_hbm, wao_ref), (who_hbm, who_ref),
                         (wo_hbm, wo_ref)):
            _stream_cast(src, dst)

    @pl.when(j == 0)
    def _():
        kbuf[0:WINDOW, :] = jnp.zeros((WINDOW, LANES), F32)
        vbuf[0:WINDOW, :] = jnp.zeros((WINDOW, LANES), F32)
        st_ref[...] = jnp.zeros(st_ref.shape, F32)

    part_rows = [slice(r0, r0 + rows // NORM_PARTS) for r0 in range(0, rows, rows // NORM_PARTS)]
    h_parts, qkv_parts = [], []
    for r in part_rows:
        hp = _rms(x_ref[r, :], pre_ref[...]).astype(BF16)
        h_parts.append(hp)
        qkv_parts.append(_dot(hp, win_ref[:, OFF_QA:OFF_QH]))
    h = jnp.concatenate(h_parts, axis=0)
    qkv = jnp.concatenate(qkv_parts, axis=0)

    def proj(off, width):
        return _dot(h, win_ref[:, off:off + width])

    lb = _forget_lower_bound(lbl_ref[...])
    n_chunks = rows // HG_CHUNK

    def proj_slab(off):
        def run():
            z = proj(off, FFN_CHUNK)
            if off < OFF_FH:
                cols = slice(off - OFF_QH, off - OFF_QH + FFN_CHUNK)
                qh_ref[:, cols] = z * jax.nn.sigmoid(z)
            elif off < OFF_IH:
                cols = slice(off - OFF_FH, off - OFF_FH + FFN_CHUNK)
                lbs = lb[:, cols]
                lf_ref[:, cols] = jnp.log(lbs + (1.0 - lbs) * jax.nn.sigmoid(z))
                kh_ref[:, cols] = (1.0 - lbs) * jax.nn.sigmoid(-z)
            elif off < OFF_GH:
                cols = slice(off - OFF_IH, off - OFF_IH + FFN_CHUNK)
                vh_ref[:, cols] = z
            else:
                cols = slice(off - OFF_GH, off - OFF_GH + FFN_CHUNK)
                g_ref[:, cols] = z
        return run
    hgrn_slabs = [proj_slab(off) for off in range(OFF_QH, OFF_GH, FFN_CHUNK)]
    gate_slabs = [proj_slab(off) for off in range(OFF_GH, IN_WIDTH, FFN_CHUNK)]
    jobs = _hgrn_factor_jobs(qh_ref, kh_ref, b_ref, xl_ref, qe_ref, ke_ref, n_chunks)
    pair_jobs, in_chunk = _hgrn_pair_jobs(xl_ref, lvl_ref[...], n_chunks)

    qa = qkv[:, OFF_QA:OFF_KA] * (HEAD_DIM ** -0.5)
    ka = qkv[:, OFF_KA:OFF_VA]
    va = qkv[:, OFF_VA:OFF_QH]
    kbuf[WINDOW:WINDOW + rows, :] = ka
    vbuf[WINDOW:WINDOW + rows, :] = va

    qi = lax.broadcasted_iota(jnp.int32, (WINDOW, 2 * WINDOW), 0)
    kj = lax.broadcasted_iota(jnp.int32, (WINDOW, 2 * WINDOW), 1)
    band = (kj > qi) & (kj <= qi + WINDOW)
    low = _lane_is_low((WINDOW, LANES))
    top = lax.broadcasted_iota(jnp.int32, (2 * WINDOW, 1), 0) < WINDOW
    n_slabs = ATTN_Q_HEADS // 2
    group = ATTN_Q_HEADS // ATTN_KV_HEADS
    kds = [_dup_kv(kbuf[n * WINDOW:(n + 2) * WINDOW, :]) for n in range(n_blocks)]
    vds = [_dup_kv(vbuf[n * WINDOW:(n + 2) * WINDOW, :]) for n in range(n_blocks)]

    def scores(n, slab):
        qs = qa[n * WINDOW:(n + 1) * WINDOW, slab * LANES:(slab + 1) * LANES]
        q2 = jnp.concatenate([jnp.where(low, qs, 0.0), jnp.where(low, 0.0, qs)], axis=0)
        return lax.dot_general(q2.astype(BF16), kds[n][(2 * slab) // group].astype(BF16), NT_DIMS,
                               preferred_element_type=F32)

    def attend(n, slab, s):
        valid = band & ((j > 0) | (kj >= WINDOW)) if n == 0 else band
        s = jnp.where(jnp.concatenate([valid, valid], axis=0), s, -jnp.inf)
        sk = jnp.where(top, sinks_ref[0, 2 * slab], sinks_ref[0, 2 * slab + 1])
        m = jnp.maximum(jnp.max(s, axis=1, keepdims=True), sk)
        p = jnp.exp(s - m)
        den = jnp.sum(p, axis=1, keepdims=True) + jnp.exp(sk - m)
        o2 = _dot(p.astype(BF16), vds[n][(2 * slab) // group].astype(BF16)) / den
        oa_ref[n * WINDOW:(n + 1) * WINDOW, slab * LANES:(slab + 1) * LANES] = (
            jnp.where(low, o2[0:WINDOW], o2[WINDOW:2 * WINDOW]))

    order = [(n, slab) for n in range(n_blocks) for slab in range(n_slabs)]
    early = len(hgrn_slabs)
    assert len(order) >= early + len(gate_slabs)
    pending = scores(*order[0])
    ready_pairs = 0
    for idx, (n, slab) in enumerate(order):
        nxt = scores(*order[idx + 1]) if idx + 1 < len(order) else None
        if idx < early:
            hgrn_slabs[idx]()
        else:
            if idx == early:
                _hgrn_log_decay(lf_ref, b_ref, n_chunks)
            if gate_slabs:
                gate_slabs.pop(0)()
            for _ in range(min(len(pair_jobs), ready_pairs)):
                pair_jobs.pop(0)()
            ready_pairs = 0
            for _ in range(-(-len(jobs) // (len(order) - 1 - idx)) if idx + 1 < len(order) else 0):
                if jobs:
                    jobs.pop(0)()
                    ready_pairs += 1
        attend(n, slab, pending)
        pending = nxt
    kbuf[0:WINDOW, :] = kbuf[rows:rows + WINDOW, :]
    vbuf[0:WINDOW, :] = vbuf[rows:rows + WINDOW, :]
    assert not jobs and not gate_slabs
    for run in pair_jobs:
        run()

    hgn = hgn_ref[...]
    st0 = [st_ref[hd * HG_D:(hd + 1) * HG_D, :] for hd in range(HG_HEADS)]
    outs, states = _hgrn_finish(qh_ref, kh_ref, vh_ref, b_ref, in_chunk, qe_ref, ke_ref, st0,
                                n_chunks)
    for hd in range(HG_HEADS):
        cs = slice(hd * HG_D, (hd + 1) * HG_D)
        st_ref[cs, :] = states[hd]
        for ci in range(n_chunks):
            oh_ref[ci * HG_CHUNK:(ci + 1) * HG_CHUNK, cs] = _rms(outs[ci][hd], hgn)

    gh = g_ref[:, 0:HG_W]
    oh = oh_ref[...] * (gh * jax.nn.sigmoid(gh))

    def gate(off, width):
        return g_ref[:, off - OFF_GH:off - OFF_GH + width]
    m = (_merge_out(oa_ref[...], wao_ref, OFF_GA, gate)
         + _merge_out(oh, who_ref, OFF_GB, gate))
    mb = m.astype(BF16)
    for r in part_rows:
        y_ref[r, :] = x_ref[r, :] + _rms(_dot(mb[r, :], wo_ref[...]), post_ref[...])

    @pl.when(j == pl.num_programs(1) - 1)
    def _():
        nk_ref[0] = kbuf[0:WINDOW, :].T
        nv_ref[0] = vbuf[0:WINDOW, :].T
        for hd in range(HG_HEADS):
            cs = slice(hd * HG_D, (hd + 1) * HG_D)
            ns_ref[0, cs, :] = st_ref[cs, :].T


def _mix_prompt_call(x, batch, sinks, pre, post, lbl, hgn, win, wao, who, wo):
    seq = x.shape[0] // batch
    rows = min(MIX_ROWS, seq)
    steps = seq // rows
    x_spec = pl.BlockSpec((rows, D_MODEL), lambda b, j: (b * steps + j, 0))
    kv_spec = pl.BlockSpec((1, ATTN_KV_W, WINDOW), lambda b, j: (b, 0, 0))
    st_spec = pl.BlockSpec((1, HG_W, HG_D), lambda b, j: (b, 0, 0))
    act = pltpu.VMEM((rows, HG_W), F32)
    return pl.pallas_call(
        _mix_prompt_kernel,
        grid=(batch, steps),
        in_specs=[pl.BlockSpec(memory_space=pltpu.SMEM), x_spec,
                  _const_spec((1, D_MODEL)), _const_spec((1, D_MODEL)),
                  _const_spec(lbl.shape), _const_spec((1, HG_D)),
                  _const_spec((HG_CHUNK, HG_CHUNK)),
                  HBM_SPEC, HBM_SPEC, HBM_SPEC, HBM_SPEC],
        out_specs=[x_spec, kv_spec, kv_spec, st_spec],
        out_shape=[jax.ShapeDtypeStruct(x.shape, F32),
                   jax.ShapeDtypeStruct((batch, ATTN_KV_W, WINDOW), F32),
                   jax.ShapeDtypeStruct((batch, ATTN_KV_W, WINDOW), F32),
                   jax.ShapeDtypeStruct((batch, HG_W, HG_D), F32)],
        scratch_shapes=[pltpu.VMEM(win.shape, BF16), pltpu.VMEM(wao.shape, BF16),
                        pltpu.VMEM(who.shape, BF16), pltpu.VMEM(wo.shape, BF16),
                        pltpu.VMEM((rows + WINDOW, ATTN_KV_W), F32),
                        pltpu.VMEM((rows + WINDOW, ATTN_KV_W), F32),
                        pltpu.VMEM((HG_W, HG_D), F32),
                        act, act, act, act,
                        pltpu.VMEM((rows, ATTN_Q_W), F32), act,
                        pltpu.VMEM((rows, IN_WIDTH - OFF_GH), F32),
                        act, pltpu.VMEM((HG_LEVELS, rows, HG_W), BF16),
                        pltpu.VMEM((rows, HG_W), BF16), pltpu.VMEM((rows, HG_W), BF16)],
        compiler_params=pltpu.CompilerParams(
            dimension_semantics=("arbitrary", "arbitrary"), vmem_limit_bytes=VMEM_LIMIT),
        name="mix_prompt",
    )(sinks, x, pre, post, lbl, hgn, _pair_levels(HG_CHUNK), win, wao, who, wo)


def _row_select(rows_list):
    n = -(-len(rows_list) // 8) * 8
    lanes = rows_list[0].shape[1]
    ridx = lax.broadcasted_iota(jnp.int32, (n, lanes), 0)
    out = jnp.zeros((n, lanes), F32)
    for i, r in enumerate(rows_list):
        out = jnp.where(ridx == i, jnp.broadcast_to(r, (n, lanes)), out)
    return out


def _mix_sample_kernel(sinks_ref, x_ref, ck_ref, cv_ref, s0_ref, pre_ref, post_ref, lbl_ref, hgn_ref,
                       win_hbm, wao_hbm, who_hbm, wo_hbm,
                       y_ref, nk_ref, nv_ref, ns_ref,
                       win_ref, wao_ref, who_ref, wo_ref,
                       h_ref, qa_ref, ka_ref, va_ref, qh_ref, kh_ref, fh_ref, vh_ref, oa_ref, oh_ref):
    i = pl.program_id(0)
    tile = ck_ref.shape[0]

    def proj(off, width):
        return _dot(h_ref[...], win_ref[:, off:off + width])

    @pl.when(i == 0)
    def _():
        for src, dst in ((win_hbm, win_ref), (wao_hbm, wao_ref), (who_hbm, who_ref),
                         (wo_hbm, wo_ref)):
            _stream_cast(src, dst)
        h_ref[...] = _rms(x_ref[...], pre_ref[...]).astype(BF16)
        qa_ref[...] = proj(OFF_QA, ATTN_Q_W) * (HEAD_DIM ** -0.5)
        ka_ref[...] = proj(OFF_KA, ATTN_KV_W)
        va_ref[...] = proj(OFF_VA, ATTN_KV_W)
        lb = _forget_lower_bound(lbl_ref[...])
        qh = proj(OFF_QH, HG_W)
        qh_ref[...] = qh * jax.nn.sigmoid(qh)
        fp = proj(OFF_FH, HG_W)
        fh_ref[...] = jnp.exp(jnp.log(lb + (1.0 - lb) * jax.nn.sigmoid(fp)))
        kh_ref[...] = (1.0 - lb) * jax.nn.sigmoid(-fp)
        vh_ref[...] = proj(OFF_IH, HG_W)

    low = _lane_is_low((1, LANES))
    newest = lax.broadcasted_iota(jnp.int32, (ATTN_KV_W, WINDOW), 1) == WINDOW - 1
    hrow = lax.broadcasted_iota(jnp.int32, (ATTN_Q_HEADS, 1), 0)
    sk = jnp.zeros((ATTN_Q_HEADS, 1), F32)
    for hd in range(ATTN_Q_HEADS):
        sk = jnp.where(hrow == hd, sinks_ref[0, hd], sk)
    group = ATTN_Q_HEADS // ATTN_KV_HEADS
    hgn = hgn_ref[...]

    r8 = pl.ds(pl.multiple_of(i * tile, tile), tile)
    qa8, ka8, va8 = qa_ref[r8, :], ka_ref[r8, :], va_ref[r8, :]
    fh8, kh8, qh8, vh8 = fh_ref[r8, :], kh_ref[r8, :], qh_ref[r8, :], vh_ref[r8, :]
    oa_rows = [[] for _ in range(ATTN_Q_HEADS // 2)]
    oh_rows = [[] for _ in range(HG_HEADS)]
    seqs = range(tile)
    def as_columns(rows8):
        pad = jnp.zeros((LANES - tile, rows8.shape[1]), F32)
        return jnp.concatenate([rows8, pad], axis=0).T
    k_cols, v_cols = as_columns(ka8), as_columns(va8)
    kws, vws, q8s = [], [], []
    for bi in seqs:
        r1 = slice(bi, bi + 1)
        kw = jnp.where(newest, pltpu.roll(k_cols, WINDOW - 1 - bi, axis=1),
                       pltpu.roll(ck_ref[bi], WINDOW - 1, axis=1))
        vw = jnp.where(newest, pltpu.roll(v_cols, WINDOW - 1 - bi, axis=1),
                       pltpu.roll(cv_ref[bi], WINDOW - 1, axis=1))
        nk_ref[bi] = kw
        nv_ref[bi] = vw
        kws.append(kw.astype(BF16))
        vws.append(vw.astype(BF16))
        qrows = []
        for hd in range(ATTN_Q_HEADS):
            slab = qa8[r1, (hd // 2) * LANES:(hd // 2 + 1) * LANES]
            in_place = (hd % 2) == (hd // group)
            src = slab if in_place else pltpu.roll(slab, HEAD_DIM, axis=1)
            on_kv_lanes = low if hd // group == 0 else jnp.logical_not(low)
            qrows.append(jnp.where(on_kv_lanes, src, 0.0))
        q8s.append(_row_select(qrows).astype(BF16))
    scores = [_dot(q8s[bi], kws[bi]) for bi in seqs]
    probs = []
    for s in scores:
        m = jnp.maximum(jnp.max(s, axis=1, keepdims=True), sk)
        p = jnp.exp(s - m)
        probs.append((p / (jnp.sum(p, axis=1, keepdims=True) + jnp.exp(sk - m))).astype(BF16))
    for bi in seqs:
        o8 = lax.dot_general(probs[bi], vws[bi], NT_DIMS, preferred_element_type=F32)
        for slab in range(ATTN_Q_HEADS // 2):
            kv = (2 * slab) // group
            even, odd = o8[2 * slab:2 * slab + 1, :], o8[2 * slab + 1:2 * slab + 2, :]
            if kv == 0:
                out = jnp.where(low, even, pltpu.roll(odd, HEAD_DIM, axis=1))
            else:
                out = jnp.where(low, pltpu.roll(even, HEAD_DIM, axis=1), odd)
            oa_rows[slab].append(out)

    def columns(a8, hd):
        pad = jnp.zeros((HG_D - tile, HG_D), F32)
        return jnp.concatenate([_head(a8, hd), pad], axis=0).T
    seq_row = lax.broadcasted_iota(jnp.int32, (HG_D, HG_D), 0)
    heads = range(HG_HEADS)
    head_rows = [slice(hd * HG_D, (hd + 1) * HG_D) for hd in heads]
    f_cols = [columns(fh8, hd) for hd in heads]
    k_cols = [columns(kh8, hd).astype(BF16) for hd in heads]
    v_rows = [jnp.concatenate([_head(vh8, hd), jnp.zeros((HG_D - tile, HG_D), F32)], axis=0)
              for hd in heads]
    outers = [[_dot(k_cols[hd], jnp.where(seq_row == bi, v_rows[hd], 0.0).astype(BF16))
               for bi in seqs] for hd in heads]
    for hd in heads:
        for bi in seqs:
            f_col = jnp.broadcast_to(f_cols[hd][:, bi:bi + 1], (HG_D, HG_D))
            ns_ref[bi, head_rows[hd], :] = f_col * s0_ref[bi, head_rows[hd], :] + outers[hd][bi]
    outs = [[_dot(_head(qh8, hd).astype(BF16), ns_ref[bi, head_rows[hd], :].astype(BF16))
             for bi in seqs] for hd in heads]
    for hd in heads:
        for bi in seqs:
            oh_rows[hd].append(_rms(outs[hd][bi][bi:bi + 1, :], hgn))
    for slab in range(ATTN_Q_HEADS // 2):
        oa_ref[r8, slab * LANES:(slab + 1) * LANES] = _row_select(oa_rows[slab])
    for hd in range(HG_HEADS):
        oh_ref[r8, hd * HG_D:(hd + 1) * HG_D] = _row_select(oh_rows[hd])

    @pl.when(i == pl.num_programs(0) - 1)
    def _():
        gh = proj(OFF_GH, HG_W)
        oh = oh_ref[...] * (gh * jax.nn.sigmoid(gh))
        m = (_merge_out(oa_ref[...], wao_ref, OFF_GA, proj)
             + _merge_out(oh, who_ref, OFF_GB, proj))
        y_ref[...] = x_ref[...] + _rms(_dot(m.astype(BF16), wo_ref[...]), post_ref[...])


def _mix_sample_call(x, ck, cv, s0, sinks, pre, post, lbl, hgn, win, wao, who, wo):
    n = x.shape[0]
    tile = min(SAMPLE_TILE, n)
    full = pl.BlockSpec((n, D_MODEL), lambda i: (0, 0))
    kv_spec = pl.BlockSpec((tile, ATTN_KV_W, WINDOW), lambda i: (i, 0, 0))
    st_spec = pl.BlockSpec((tile, HG_W, HG_D), lambda i: (i, 0, 0))
    act = pltpu.VMEM((n, HG_W), F32)
    kv_act = pltpu.VMEM((n, ATTN_KV_W), F32)
    return pl.pallas_call(
        _mix_sample_kernel,
        grid=(n // tile,),
        in_specs=[pl.BlockSpec(memory_space=pltpu.SMEM), full, kv_spec, kv_spec, st_spec,
                  _const_spec((1, D_MODEL)), _const_spec((1, D_MODEL)),
                  _const_spec(lbl.shape), _const_spec((1, HG_D)),
                  HBM_SPEC, HBM_SPEC, HBM_SPEC, HBM_SPEC],
        out_specs=[full, kv_spec, kv_spec, st_spec],
        out_shape=[jax.ShapeDtypeStruct(x.shape, F32),
                   jax.ShapeDtypeStruct(ck.shape, F32),
                   jax.ShapeDtypeStruct(cv.shape, F32),
                   jax.ShapeDtypeStruct(s0.shape, F32)],
        scratch_shapes=[pltpu.VMEM(win.shape, BF16), pltpu.VMEM(wao.shape, BF16),
                        pltpu.VMEM(who.shape, BF16), pltpu.VMEM(wo.shape, BF16),
                        pltpu.VMEM((n, D_MODEL), BF16),
                        pltpu.VMEM((n, ATTN_Q_W), F32), kv_act, kv_act,
                        act, act, act, act,
                        pltpu.VMEM((n, ATTN_Q_W), F32), act],
        compiler_params=pltpu.CompilerParams(
            dimension_semantics=("arbitrary",), vmem_limit_bytes=VMEM_LIMIT),
        name="mix_sample",
    )(sinks, x, ck, cv, s0, pre, post, lbl, hgn, win, wao, who, wo)


def kernel(x_prompt, x_sample, cache_k, cache_v, state_hgrn, norm_ffn1_pre, norm_ffn1_post, w_ffn1_gate, w_ffn1_up, w_ffn1_down, norm_mix_pre, norm_mix_post, w_in, attn_sinks, hgrn_lb_logits, hgrn_norm, w_attn_out, w_hgrn_out, w_out, norm_ffn2_pre, norm_ffn2_post, w_ffn2_gate, w_ffn2_up, w_ffn2_down):
    depth = w_in.shape[0]
    assert depth == 1 and hgrn_lb_logits.shape[0] == 2, "single-layer stack only"
    batch, seq, _ = x_prompt.shape
    n_s = x_sample.shape[0]
    assert x_sample.shape[1] == 1 and seq % WINDOW == 0

    xp = x_prompt.reshape(batch * seq, D_MODEL)
    ck = jnp.swapaxes(cache_k[0].reshape(n_s, WINDOW, ATTN_KV_W), 1, 2)
    cv = jnp.swapaxes(cache_v[0].reshape(n_s, WINDOW, ATTN_KV_W), 1, 2)
    s0 = state_hgrn[0].reshape(n_s, HG_W, HG_D)

    xp, xs = _ffn_call(xp, x_sample, norm_ffn1_pre, norm_ffn1_post,
                       w_ffn1_gate[0], w_ffn1_up[0], w_ffn1_down[0], (n_s, D_MODEL))

    mix_w = (attn_sinks, norm_mix_pre, norm_mix_post, hgrn_lb_logits, hgrn_norm,
             w_in[0], w_attn_out[0], w_hgrn_out[0], w_out[0])
    xp, nkp, nvp, nsp = _mix_prompt_call(xp, batch, *mix_w)
    xs, nks, nvs, nss = _mix_sample_call(xs, ck, cv, s0, *mix_w)

    xp, xs = _ffn_call(xp, xs, norm_ffn2_pre, norm_ffn2_post,
                       w_ffn2_gate[0], w_ffn2_up[0], w_ffn2_down[0], x_sample.shape)

    kv_shape = (1, -1, WINDOW, ATTN_KV_HEADS, HEAD_DIM)
    st_shape = (1, -1, HG_HEADS, HG_D, HG_D)
    def window(t):
        return jnp.swapaxes(t, 1, 2).reshape(kv_shape)
    return (xp.reshape(batch, seq, D_MODEL), xs,
            window(nkp), window(nvp), nsp.reshape(st_shape),
            window(nks), window(nvs), nss.reshape(st_shape))
```

```python
import functools

import jax
import jax.numpy as jnp
from jax import lax
from jax.experimental import pallas as pl
from jax.experimental.pallas import tpu as pltpu

F32 = jnp.float32
BF16 = jnp.bfloat16

D_MODEL = 1024
FFN_DIM = 2816
HEAD_DIM = 64
ATTN_Q_HEADS = 8
ATTN_KV_HEADS = 2
WINDOW = 128
HG_HEADS = 4
HG_D = 128
EPS = 1e-6

ATTN_Q_W = ATTN_Q_HEADS * HEAD_DIM
ATTN_KV_W = ATTN_KV_HEADS * HEAD_DIM
HG_W = HG_HEADS * HG_D
OFF_QA = 0
OFF_KA = OFF_QA + ATTN_Q_W
OFF_VA = OFF_KA + ATTN_KV_W
OFF_QH = OFF_VA + ATTN_KV_W
OFF_FH = OFF_QH + HG_W
OFF_IH = OFF_FH + HG_W
OFF_GH = OFF_IH + HG_W
OFF_GA = OFF_GH + HG_W
OFF_GB = OFF_GA + D_MODEL
IN_WIDTH = OFF_GB + D_MODEL

LANES = 128
SUBLANES = 8
FFN_CHUNK = 256
N_FFN_CHUNKS = FFN_DIM // FFN_CHUNK
FFN_ROWS = 1024
MIX_ROWS = 512
NORM_PARTS = 2
HG_CHUNK = 128
SAMPLE_TILE = 16
VMEM_LIMIT = 56 * 1024 * 1024

NT_DIMS = (((1,), (1,)), ((), ()))
TN_DIMS = (((0,), (0,)), ((), ()))


def _rms(x, g):
    return x * lax.rsqrt(jnp.mean(x * x, axis=-1, keepdims=True) + EPS) * g


def _dot(a, b):
    return jnp.dot(a, b, preferred_element_type=F32)


STREAM_SLOTS = 4
STREAM_CHUNK_BYTES = 2 << 20


def _stream_cast(src_hbm, dst_ref):
    n_rows, n_cols = src_hbm.shape
    row_bytes = n_cols * jnp.dtype(F32).itemsize
    rc = max(SUBLANES, min(n_rows, STREAM_CHUNK_BYTES // row_bytes // SUBLANES * SUBLANES))
    while n_rows % rc:
        rc -= SUBLANES
    n = n_rows // rc
    slots = min(STREAM_SLOTS, n)

    def body(stage, sem):
        def copy(c):
            return pltpu.make_async_copy(src_hbm.at[pl.ds(c * rc, rc), :], stage.at[c % slots],
                                         sem.at[c % slots])
        for c in range(slots):
            copy(c).start()
        for c in range(n):
            copy(c).wait()
            dst_ref[c * rc:(c + 1) * rc, :] = stage[c % slots].astype(BF16)
            if c + slots < n:
                copy(c + slots).start()

    pl.run_scoped(body, pltpu.VMEM((slots, rc, n_cols), F32), pltpu.SemaphoreType.DMA((slots,)))


HBM_SPEC = pl.BlockSpec(memory_space=pl.ANY)


def _const_spec(shape):
    zeros = (0,) * len(shape)
    return pl.BlockSpec(shape, lambda *_: zeros, pipeline_mode=pl.Buffered(1))


def _ffn_tile(x, pre, post, wg_ref, wu_ref, wd_ref):
    h = _rms(x, pre).astype(BF16)
    acc = None
    for c in range(N_FFN_CHUNKS):
        cols = slice(c * FFN_CHUNK, (c + 1) * FFN_CHUNK)
        g = _dot(h, wg_ref[:, cols])
        u = _dot(h, wu_ref[:, cols])
        a = (g * jax.nn.sigmoid(g) * u).astype(BF16)
        d = _dot(a, wd_ref[cols, :])
        acc = d if acc is None else acc + d
    return x + 0.5 * _rms(acc, post)


def _ffn_kernel(n_prompt_steps, xp_ref, xs_ref, pre_ref, post_ref, wg_hbm, wu_hbm, wd_hbm,
                yp_ref, ys_ref, wg_ref, wu_ref, wd_ref):
    i = pl.program_id(0)

    @pl.when(i == 0)
    def _():
        _stream_cast(wg_hbm, wg_ref)
        _stream_cast(wu_hbm, wu_ref)
        _stream_cast(wd_hbm, wd_ref)

    @pl.when(i < n_prompt_steps)
    def _():
        yp_ref[...] = _ffn_tile(xp_ref[...], pre_ref[...], post_ref[...], wg_ref, wu_ref, wd_ref)

    @pl.when(i == n_prompt_steps)
    def _():
        xs = xs_ref[...].reshape(xs_ref.shape[0], D_MODEL)
        ys = _ffn_tile(xs, pre_ref[...], post_ref[...], wg_ref, wu_ref, wd_ref)
        ys_ref[...] = ys.reshape(ys_ref.shape)


def _ffn_call(xp, xs, pre, post, wg, wu, wd, ys_shape):
    n_p, n_s = xp.shape[0], xs.shape[0]
    rows = min(FFN_ROWS, n_p)
    steps = n_p // rows
    prompt_spec = pl.BlockSpec((rows, D_MODEL), lambda i: (jnp.minimum(i, steps - 1), 0))

    def sample_spec(shape):
        zeros = (0,) * len(shape)
        return pl.BlockSpec(shape, lambda i: zeros)
    return pl.pallas_call(
        functools.partial(_ffn_kernel, steps),
        grid=(steps + 1,),
        in_specs=[prompt_spec, sample_spec(xs.shape),
                  _const_spec((1, D_MODEL)), _const_spec((1, D_MODEL)),
                  HBM_SPEC, HBM_SPEC, HBM_SPEC],
        out_specs=[prompt_spec, sample_spec(ys_shape)],
        out_shape=[jax.ShapeDtypeStruct(xp.shape, F32), jax.ShapeDtypeStruct(ys_shape, F32)],
        scratch_shapes=[pltpu.VMEM(wg.shape, BF16), pltpu.VMEM(wu.shape, BF16),
                        pltpu.VMEM(wd.shape, BF16)],
        compiler_params=pltpu.CompilerParams(
            dimension_semantics=("arbitrary",), vmem_limit_bytes=VMEM_LIMIT),
        name="ffn_half",
    )(xp, xs, pre, post, wg, wu, wd)


def _forget_lower_bound(lbl):
    l0, l1 = lbl[0:1, :], lbl[1:2, :]
    m = jnp.maximum(l0, l1)
    e0, e1 = jnp.exp(l0 - m), jnp.exp(l1 - m)
    return e0 / (e0 + e1)


def _lane_is_low(shape):
    return lax.broadcasted_iota(jnp.int32, shape, len(shape) - 1) < HEAD_DIM


def _dup_kv(x):
    swapped = pltpu.roll(x, HEAD_DIM, axis=1)
    low = _lane_is_low(x.shape)
    return jnp.where(low, x, swapped), jnp.where(low, swapped, x)


def _merge_out(h, w_ref, off, gates_from):
    return jax.nn.sigmoid(gates_from(off, D_MODEL)) * _dot(h.astype(BF16), w_ref[...])


def _pair_levels(c):
    t = jnp.arange(c, dtype=jnp.int32)[:, None]
    s = jnp.arange(c, dtype=jnp.int32)[None, :]
    x = jnp.maximum(t ^ s, 1)
    lvl = (31 - lax.clz(x)).astype(jnp.int32)
    return jnp.where(t > s, lvl, -1)


LOG2E = 1.4426950408889634


def _head(a, hd):
    return a[:, hd * HG_D:(hd + 1) * HG_D]


HG_LEVELS = HG_CHUNK.bit_length() - 1


def _chunk_rows(ci):
    return slice(ci * HG_CHUNK, (ci + 1) * HG_CHUNK)


def _hgrn_log_decay(g_ref, b_ref, n_chunks):
    c = HG_CHUNK
    width = g_ref.shape[1]
    row = lax.broadcasted_iota(jnp.int32, (c, c), 0)
    col = lax.broadcasted_iota(jnp.int32, (c, c), 1)
    tril = (col <= row).astype(BF16)
    for ci in range(n_chunks):
        r = _chunk_rows(ci)
        g = g_ref[r, :]
        hi = g.astype(BF16)
        rest = g - hi.astype(F32)
        mid = rest.astype(BF16)
        lo = (rest - mid.astype(F32)).astype(BF16)
        sums = _dot(tril, jnp.concatenate([hi, mid, lo], axis=1))
        b_ref[r, :] = (sums[:, 0:width] + sums[:, width:2 * width] + sums[:, 2 * width:]) * LOG2E


def _hgrn_factor_jobs(q_ref, k_ref, b_ref, x_ref, qe_ref, ke_ref, n_chunks):
    c = HG_CHUNK
    t = lax.broadcasted_iota(jnp.int32, (c, 1), 0)
    last = {}

    def level_job(ci, lvl):
        def run():
            n = 1 << lvl
            r = _chunk_rows(ci)
            bc = b_ref[r, :]
            second = (t & n) != 0
            if n < SUBLANES:
                prev = last.get(ci, bc)
                tiles = prev.reshape(c // SUBLANES, SUBLANES, prev.shape[1])
                second3 = second.reshape(c // SUBLANES, SUBLANES, 1)
                edge = jnp.where(second3, pltpu.roll(tiles, n, axis=1), tiles).reshape(prev.shape)
                if 2 * n < SUBLANES:
                    last[ci] = jnp.where(second3, tiles, pltpu.roll(tiles, SUBLANES - n, axis=1)
                                         ).reshape(prev.shape)
            else:
                edge = jnp.concatenate(
                    [jnp.broadcast_to(bc[p + n - 1:p + n, :], (2 * n, bc.shape[1]))
                     for p in range(0, c, 2 * n)], axis=0)
            w = jnp.exp2(jnp.where(second, bc - edge, edge - bc))
            x_ref[lvl, r, :] = (jnp.where(second, q_ref[r, :], k_ref[r, :]) * w).astype(BF16)
        return run

    def state_job(ci):
        def run():
            r = _chunk_rows(ci)
            bc = b_ref[r, :]
            qe_ref[r, :] = (q_ref[r, :] * jnp.exp2(bc)).astype(BF16)
            ke_ref[r, :] = (k_ref[r, :] * jnp.exp2(bc[c - 1:c, :] - bc)).astype(BF16)
        return run

    jobs = [level_job(ci, lvl) for lvl in range(HG_LEVELS) for ci in range(n_chunks)]
    return jobs + [state_job(ci) for ci in range(n_chunks)]


def _hgrn_pair_jobs(x_ref, levels, n_chunks):
    c = HG_CHUNK
    heads, chunks = range(HG_HEADS), range(n_chunks)
    a = [[jnp.zeros((c, c), F32) for _ in heads] for _ in chunks]

    def job(lvl, ci):
        def run():
            x = x_ref[lvl, _chunk_rows(ci), :]
            for hd in heads:
                al = lax.dot_general(_head(x, hd), _head(x, hd), NT_DIMS,
                                     preferred_element_type=F32)
                a[ci][hd] = jnp.where(levels == lvl, al, a[ci][hd])
        return run
    return [job(lvl, ci) for lvl in range(HG_LEVELS) for ci in chunks], a


def _hgrn_finish(q_ref, k_ref, v_ref, b_ref, a, qe_ref, ke_ref, st, n_chunks):
    c = HG_CHUNK
    heads, chunks = range(HG_HEADS), range(n_chunks)
    local = []
    for ci in chunks:
        r = _chunk_rows(ci)
        q, k, v = q_ref[r, :], k_ref[r, :], v_ref[r, :]
        qk, vb = q * k, v.astype(BF16)
        outs = []
        for hd in heads:
            diag = jnp.sum(_head(qk, hd), axis=1, keepdims=True)
            outs.append(_dot(a[ci][hd].astype(BF16), _head(vb, hd)) + diag * _head(v, hd))
        local.append(outs)

    result = []
    for ci in chunks:
        r = _chunk_rows(ci)
        qe, ke, vb = qe_ref[r, :], ke_ref[r, :], v_ref[r, :].astype(BF16)
        decay = jnp.exp2(b_ref[ci * c + c - 1:ci * c + c, :])
        outs, new = [], []
        for hd in heads:
            outs.append(local[ci][hd] + lax.dot_general(_head(qe, hd), st[hd].astype(BF16), NT_DIMS,
                                                        preferred_element_type=F32))
            new.append(st[hd] * _head(decay, hd)
                       + lax.dot_general(_head(vb, hd), _head(ke, hd), TN_DIMS,
                                         preferred_element_type=F32))
        st = new
        result.append(outs)
    return result, st


def _mix_prompt_kernel(sinks_ref, x_ref, pre_ref, post_ref, lbl_ref, hgn_ref, lvl_ref,
                       win_hbm, wao_hbm, who_hbm, wo_hbm,
                       y_ref, nk_ref, nv_ref, ns_ref,
                       win_ref, wao_ref, who_ref, wo_ref,
                       kbuf, vbuf, st_ref, qh_ref, kh_ref, vh_ref, lf_ref, oa_ref, oh_ref, g_ref,
                       b_ref, xl_ref, qe_ref, ke_ref):
    j = pl.program_id(1)
    rows = x_ref.shape[0]
    n_blocks = rows // WINDOW

    @pl.when((pl.program_id(0) == 0) & (j == 0))
    def _():
        for src, dst in ((win_hbm, win_ref), (wao_hbm, wao_ref), (who_hbm, who_ref),
                         (wo_hbm, wo_ref)):
            _stream_cast(src, dst)

    @pl.when(j == 0)
    def _():
        kbuf[0:WINDOW, :] = jnp.zeros((WINDOW, LANES), F32)
        vbuf[0:WINDOW, :] = jnp.zeros((WINDOW, LANES), F32)
        st_ref[...] = jnp.zeros(st_ref.shape, F32)

    part_rows = [slice(r0, r0 + rows // NORM_PARTS) for r0 in range(0, rows, rows // NORM_PARTS)]
    h_parts, qkv_parts = [], []
    for r in part_rows:
        hp = _rms(x_ref[r, :], pre_ref[...]).astype(BF16)
        h_parts.append(hp)
        qkv_parts.append(_dot(hp, win_ref[:, OFF_QA:OFF_QH]))
    h = jnp.concatenate(h_parts, axis=0)
    qkv = jnp.concatenate(qkv_parts, axis=0)

    def proj(off, width):
        return _dot(h, win_ref[:, off:off + width])

    lb = _forget_lower_bound(lbl_ref[...])
    n_chunks = rows // HG_CHUNK

    def proj_slab(off):
        def run():
            z = proj(off, FFN_CHUNK)
            if off < OFF_FH:
                cols = slice(off - OFF_QH, off - OFF_QH + FFN_CHUNK)
                qh_ref[:, cols] = z * jax.nn.sigmoid(z)
            elif off < OFF_IH:
                cols = slice(off - OFF_FH, off - OFF_FH + FFN_CHUNK)
                lbs = lb[:, cols]
                lf_ref[:, cols] = jnp.log(lbs + (1.0 - lbs) * jax.nn.sigmoid(z))
                kh_ref[:, cols] = (1.0 - lbs) * jax.nn.sigmoid(-z)
            elif off < OFF_GH:
                cols = slice(off - OFF_IH, off - OFF_IH + FFN_CHUNK)
                vh_ref[:, cols] = z
            else:
                cols = slice(off - OFF_GH, off - OFF_GH + FFN_CHUNK)
                g_ref[:, cols] = z
        return run
    hgrn_slabs = [proj_slab(off) for off in range(OFF_QH, OFF_GH, FFN_CHUNK)]
    gate_slabs = [proj_slab(off) for off in range(OFF_GH, IN_WIDTH, FFN_CHUNK)]
    jobs = _hgrn_factor_jobs(qh_ref, kh_ref, b_ref, xl_ref, qe_ref, ke_ref, n_chunks)
    pair_jobs, in_chunk = _hgrn_pair_jobs(xl_ref, lvl_ref[...], n_chunks)

    qa = qkv[:, OFF_QA:OFF_KA] * (HEAD_DIM ** -0.5)
    ka = qkv[:, OFF_KA:OFF_VA]
    va = qkv[:, OFF_VA:OFF_QH]
    kbuf[WINDOW:WINDOW + rows, :] = ka
    vbuf[WINDOW:WINDOW + rows, :] = va

    qi = lax.broadcasted_iota(jnp.int32, (WINDOW, 2 * WINDOW), 0)
    kj = lax.broadcasted_iota(jnp.int32, (WINDOW, 2 * WINDOW), 1)
    band = (kj > qi) & (kj <= qi + WINDOW)
    low = _lane_is_low((WINDOW, LANES))
    top = lax.broadcasted_iota(jnp.int32, (2 * WINDOW, 1), 0) < WINDOW
    n_slabs = ATTN_Q_HEADS // 2
    group = ATTN_Q_HEADS // ATTN_KV_HEADS
    class _PerBlock:
        def __init__(self, buf):
            self.buf, self.made = buf, {}

        def __getitem__(self, n):
            if n not in self.made:
                self.made[n] = _dup_kv(self.buf[n * WINDOW:(n + 2) * WINDOW, :])
            return self.made[n]
    kds, vds = _PerBlock(kbuf), _PerBlock(vbuf)

    def scores(n, slab):
        qs = qa[n * WINDOW:(n + 1) * WINDOW, slab * LANES:(slab + 1) * LANES]
        q2 = jnp.concatenate([jnp.where(low, qs, 0.0), jnp.where(low, 0.0, qs)], axis=0)
        return lax.dot_general(q2.astype(BF16), kds[n][(2 * slab) // group].astype(BF16), NT_DIMS,
                               preferred_element_type=F32)

    def attend(n, slab, s):
        valid = band & ((j > 0) | (kj >= WINDOW)) if n == 0 else band
        s = jnp.where(jnp.concatenate([valid, valid], axis=0), s, -jnp.inf)
        sk = jnp.where(top, sinks_ref[0, 2 * slab], sinks_ref[0, 2 * slab + 1])
        m = jnp.maximum(jnp.max(s, axis=1, keepdims=True), sk)
        p = jnp.exp(s - m)
        den = jnp.sum(p, axis=1, keepdims=True) + jnp.exp(sk - m)
        o2 = _dot(p.astype(BF16), vds[n][(2 * slab) // group].astype(BF16)) / den
        oa_ref[n * WINDOW:(n + 1) * WINDOW, slab * LANES:(slab + 1) * LANES] = (
            jnp.where(low, o2[0:WINDOW], o2[WINDOW:2 * WINDOW]))

    order = [(n, slab) for n in range(n_blocks) for slab in range(n_slabs)]
    early = len(hgrn_slabs)
    assert len(order) >= early + len(gate_slabs)
    pending = scores(*order[0])
    ready_pairs = 0
    for idx, (n, slab) in enumerate(order):
        nxt = scores(*order[idx + 1]) if idx + 1 < len(order) else None
        if idx < early:
            hgrn_slabs[idx]()
        else:
            if idx == early:
                _hgrn_log_decay(lf_ref, b_ref, n_chunks)
            if gate_slabs:
                gate_slabs.pop(0)()
            for _ in range(min(len(pair_jobs), ready_pairs)):
                pair_jobs.pop(0)()
            ready_pairs = 0
            for _ in range(-(-len(jobs) // (len(order) - 1 - idx)) if idx + 1 < len(order) else 0):
                if jobs:
                    jobs.pop(0)()
                    ready_pairs += 1
        attend(n, slab, pending)
        pending = nxt
    kbuf[0:WINDOW, :] = kbuf[rows:rows + WINDOW, :]
    vbuf[0:WINDOW, :] = vbuf[rows:rows + WINDOW, :]
    assert not jobs and not gate_slabs
    for run in pair_jobs:
        run()

    hgn = hgn_ref[...]
    st0 = [st_ref[hd * HG_D:(hd + 1) * HG_D, :] for hd in range(HG_HEADS)]
    outs, states = _hgrn_finish(qh_ref, kh_ref, vh_ref, b_ref, in_chunk, qe_ref, ke_ref, st0,
                                n_chunks)
    for hd in range(HG_HEADS):
        cs = slice(hd * HG_D, (hd + 1) * HG_D)
        st_ref[cs, :] = states[hd]
        for ci in range(n_chunks):
            oh_ref[ci * HG_CHUNK:(ci + 1) * HG_CHUNK, cs] = _rms(outs[ci][hd], hgn)

    gh = g_ref[:, 0:HG_W]
    oh = oh_ref[...] * (gh * jax.nn.sigmoid(gh))

    def gate(off, width):
        return g_ref[:, off - OFF_GH:off - OFF_GH + width]
    m = (_merge_out(oa_ref[...], wao_ref, OFF_GA, gate)
         + _merge_out(oh, who_ref, OFF_GB, gate))
    mb = m.astype(BF16)
    for r in part_rows:
        y_ref[r, :] = x_ref[r, :] + _rms(_dot(mb[r, :], wo_ref[...]), post_ref[...])

    @pl.when(j == pl.num_programs(1) - 1)
    def _():
        nk_ref[0] = kbuf[0:WINDOW, :].T
        nv_ref[0] = vbuf[0:WINDOW, :].T
        for hd in range(HG_HEADS):
            cs = slice(hd * HG_D, (hd + 1) * HG_D)
            ns_ref[0, cs, :] = st_ref[cs, :].T


def _mix_prompt_call(x, batch, sinks, pre, post, lbl, hgn, win, wao, who, wo):
    seq = x.shape[0] // batch
    rows = min(MIX_ROWS, seq)
    steps = seq // rows
    x_spec = pl.BlockSpec((rows, D_MODEL), lambda b, j: (b * steps + j, 0))
    kv_spec = pl.BlockSpec((1, ATTN_KV_W, WINDOW), lambda b, j: (b, 0, 0))
    st_spec = pl.BlockSpec((1, HG_W, HG_D), lambda b, j: (b, 0, 0))
    act = pltpu.VMEM((rows, HG_W), F32)
    return pl.pallas_call(
        _mix_prompt_kernel,
        grid=(batch, steps),
        in_specs=[pl.BlockSpec(memory_space=pltpu.SMEM), x_spec,
                  _const_spec((1, D_MODEL)), _const_spec((1, D_MODEL)),
                  _const_spec(lbl.shape), _const_spec((1, HG_D)),
                  _const_spec((HG_CHUNK, HG_CHUNK)),
                  HBM_SPEC, HBM_SPEC, HBM_SPEC, HBM_SPEC],
        out_specs=[x_spec, kv_spec, kv_spec, st_spec],
        out_shape=[jax.ShapeDtypeStruct(x.shape, F32),
                   jax.ShapeDtypeStruct((batch, ATTN_KV_W, WINDOW), F32),
                   jax.ShapeDtypeStruct((batch, ATTN_KV_W, WINDOW), F32),
                   jax.ShapeDtypeStruct((batch, HG_W, HG_D), F32)],
        scratch_shapes=[pltpu.VMEM(win.shape, BF16), pltpu.VMEM(wao.shape, BF16),
                        pltpu.VMEM(who.shape, BF16), pltpu.VMEM(wo.shape, BF16),
                        pltpu.VMEM((rows + WINDOW, ATTN_KV_W), F32),
                        pltpu.VMEM((rows + WINDOW, ATTN_KV_W), F32),
                        pltpu.VMEM((HG_W, HG_D), F32),
                        act, act, act, act,
                        pltpu.VMEM((rows, ATTN_Q_W), F32), act,
                        pltpu.VMEM((rows, IN_WIDTH - OFF_GH), F32),
                        act, pltpu.VMEM((HG_LEVELS, rows, HG_W), BF16),
                        pltpu.VMEM((rows, HG_W), BF16), pltpu.VMEM((rows, HG_W), BF16)],
        compiler_params=pltpu.CompilerParams(
            dimension_semantics=("arbitrary", "arbitrary"), vmem_limit_bytes=VMEM_LIMIT),
        name="mix_prompt",
    )(sinks, x, pre, post, lbl, hgn, _pair_levels(HG_CHUNK), win, wao, who, wo)


def _row_select(rows_list):
    n = -(-len(rows_list) // 8) * 8
    lanes = rows_list[0].shape[1]
    ridx = lax.broadcasted_iota(jnp.int32, (n, lanes), 0)
    out = jnp.zeros((n, lanes), F32)
    for i, r in enumerate(rows_list):
        out = jnp.where(ridx == i, jnp.broadcast_to(r, (n, lanes)), out)
    return out


def _mix_sample_kernel(sinks_ref, x_ref, ck_ref, cv_ref, s0_ref, pre_ref, post_ref, lbl_ref, hgn_ref,
                       win_hbm, wao_hbm, who_hbm, wo_hbm,
                       y_ref, nk_ref, nv_ref, ns_ref,
                       win_ref, wao_ref, who_ref, wo_ref,
                       h_ref, qa_ref, ka_ref, va_ref, qh_ref, kh_ref, fh_ref, vh_ref, oa_ref, oh_ref):
    i = pl.program_id(0)
    tile = ck_ref.shape[0]

    def proj(off, width):
        return _dot(h_ref[...], win_ref[:, off:off + width])

    @pl.when(i == 0)
    def _():
        for src, dst in ((win_hbm, win_ref), (wao_hbm, wao_ref), (who_hbm, who_ref),
                         (wo_hbm, wo_ref)):
            _stream_cast(src, dst)
        h_ref[...] = _rms(x_ref[...], pre_ref[...]).astype(BF16)
        qa_ref[...] = proj(OFF_QA, ATTN_Q_W) * (HEAD_DIM ** -0.5)
        ka_ref[...] = proj(OFF_KA, ATTN_KV_W)
        va_ref[...] = proj(OFF_VA, ATTN_KV_W)
        lb = _forget_lower_bound(lbl_ref[...])
        qh = proj(OFF_QH, HG_W)
        qh_ref[...] = qh * jax.nn.sigmoid(qh)
        fp = proj(OFF_FH, HG_W)
        fh_ref[...] = jnp.exp(jnp.log(lb + (1.0 - lb) * jax.nn.sigmoid(fp)))
        kh_ref[...] = (1.0 - lb) * jax.nn.sigmoid(-fp)
        vh_ref[...] = proj(OFF_IH, HG_W)

    low = _lane_is_low((1, LANES))
    newest = lax.broadcasted_iota(jnp.int32, (ATTN_KV_W, WINDOW), 1) == WINDOW - 1
    hrow = lax.broadcasted_iota(jnp.int32, (ATTN_Q_HEADS, 1), 0)
    sk = jnp.zeros((ATTN_Q_HEADS, 1), F32)
    for hd in range(ATTN_Q_HEADS):
        sk = jnp.where(hrow == hd, sinks_ref[0, hd], sk)
    group = ATTN_Q_HEADS // ATTN_KV_HEADS
    hgn = hgn_ref[...]

    r8 = pl.ds(pl.multiple_of(i * tile, tile), tile)
    qa8, ka8, va8 = qa_ref[r8, :], ka_ref[r8, :], va_ref[r8, :]
    fh8, kh8, qh8, vh8 = fh_ref[r8, :], kh_ref[r8, :], qh_ref[r8, :], vh_ref[r8, :]
    oa_rows = [[] for _ in range(ATTN_Q_HEADS // 2)]
    oh_rows = [[] for _ in range(HG_HEADS)]
    seqs = range(tile)
    def as_columns(rows8):
        pad = jnp.zeros((LANES - tile, rows8.shape[1]), F32)
        return jnp.concatenate([rows8, pad], axis=0).T
    k_cols, v_cols = as_columns(ka8), as_columns(va8)
    kws, vws, q8s = [], [], []
    for bi in seqs:
        r1 = slice(bi, bi + 1)
        kw = jnp.where(newest, pltpu.roll(k_cols, WINDOW - 1 - bi, axis=1),
                       pltpu.roll(ck_ref[bi], WINDOW - 1, axis=1))
        vw = jnp.where(newest, pltpu.roll(v_cols, WINDOW - 1 - bi, axis=1),
                       pltpu.roll(cv_ref[bi], WINDOW - 1, axis=1))
        nk_ref[bi] = kw
        nv_ref[bi] = vw
        kws.append(kw.astype(BF16))
        vws.append(vw.astype(BF16))
        qrows = []
        for hd in range(ATTN_Q_HEADS):
            slab = qa8[r1, (hd // 2) * LANES:(hd // 2 + 1) * LANES]
            in_place = (hd % 2) == (hd // group)
            src = slab if in_place else pltpu.roll(slab, HEAD_DIM, axis=1)
            on_kv_lanes = low if hd // group == 0 else jnp.logical_not(low)
            qrows.append(jnp.where(on_kv_lanes, src, 0.0))
        q8s.append(_row_select(qrows).astype(BF16))
    scores = [_dot(q8s[bi], kws[bi]) for bi in seqs]
    probs = []
    for s in scores:
        m = jnp.maximum(jnp.max(s, axis=1, keepdims=True), sk)
        p = jnp.exp(s - m)
        probs.append((p / (jnp.sum(p, axis=1, keepdims=True) + jnp.exp(sk - m))).astype(BF16))
    for bi in seqs:
        o8 = lax.dot_general(probs[bi], vws[bi], NT_DIMS, preferred_element_type=F32)
        for slab in range(ATTN_Q_HEADS // 2):
            kv = (2 * slab) // group
            even, odd = o8[2 * slab:2 * slab + 1, :], o8[2 * slab + 1:2 * slab + 2, :]
            if kv == 0:
                out = jnp.where(low, even, pltpu.roll(odd, HEAD_DIM, axis=1))
            else:
                out = jnp.where(low, pltpu.roll(even, HEAD_DIM, axis=1), odd)
            oa_rows[slab].append(out)

    def columns(a8, hd):
        pad = jnp.zeros((HG_D - tile, HG_D), F32)
        return jnp.concatenate([_head(a8, hd), pad], axis=0).T
    seq_row = lax.broadcasted_iota(jnp.int32, (HG_D, HG_D), 0)
    heads = range(HG_HEADS)
    head_rows = [slice(hd * HG_D, (hd + 1) * HG_D) for hd in heads]
    f_cols = [columns(fh8, hd) for hd in heads]
    k_cols = [columns(kh8, hd).astype(BF16) for hd in heads]
    v_rows = [jnp.concatenate([_head(vh8, hd), jnp.zeros((HG_D - tile, HG_D), F32)], axis=0)
              for hd in heads]
    outers = [[_dot(k_cols[hd], jnp.where(seq_row == bi, v_rows[hd], 0.0).astype(BF16))
               for bi in seqs] for hd in heads]
    for hd in heads:
        for bi in seqs:
            f_col = jnp.broadcast_to(f_cols[hd][:, bi:bi + 1], (HG_D, HG_D))
            ns_ref[bi, head_rows[hd], :] = f_col * s0_ref[bi, head_rows[hd], :] + outers[hd][bi]
    outs = [[_dot(_head(qh8, hd).astype(BF16), ns_ref[bi, head_rows[hd], :].astype(BF16))
             for bi in seqs] for hd in heads]
    for hd in heads:
        for bi in seqs:
            oh_rows[hd].append(_rms(outs[hd][bi][bi:bi + 1, :], hgn))
    for slab in range(ATTN_Q_HEADS // 2):
        oa_ref[r8, slab * LANES:(slab + 1) * LANES] = _row_select(oa_rows[slab])
    for hd in range(HG_HEADS):
        oh_ref[r8, hd * HG_D:(hd + 1) * HG_D] = _row_select(oh_rows[hd])

    @pl.when(i == pl.num_programs(0) - 1)
    def _():
        gh = proj(OFF_GH, HG_W)
        oh = oh_ref[...] * (gh * jax.nn.sigmoid(gh))
        m = (_merge_out(oa_ref[...], wao_ref, OFF_GA, proj)
             + _merge_out(oh, who_ref, OFF_GB, proj))
        y_ref[...] = x_ref[...] + _rms(_dot(m.astype(BF16), wo_ref[...]), post_ref[...])


def _mix_sample_call(x, ck, cv, s0, sinks, pre, post, lbl, hgn, win, wao, who, wo):
    n = x.shape[0]
    tile = min(SAMPLE_TILE, n)
    full = pl.BlockSpec((n, D_MODEL), lambda i: (0, 0))
    kv_spec = pl.BlockSpec((tile, ATTN_KV_W, WINDOW), lambda i: (i, 0, 0))
    st_spec = pl.BlockSpec((tile, HG_W, HG_D), lambda i: (i, 0, 0))
    act = pltpu.VMEM((n, HG_W), F32)
    kv_act = pltpu.VMEM((n, ATTN_KV_W), F32)
    return pl.pallas_call(
        _mix_sample_kernel,
        grid=(n // tile,),
        in_specs=[pl.BlockSpec(memory_space=pltpu.SMEM), full, kv_spec, kv_spec, st_spec,
                  _const_spec((1, D_MODEL)), _const_spec((1, D_MODEL)),
                  _const_spec(lbl.shape), _const_spec((1, HG_D)),
                  HBM_SPEC, HBM_SPEC, HBM_SPEC, HBM_SPEC],
        out_specs=[full, kv_spec, kv_spec, st_spec],
        out_shape=[jax.ShapeDtypeStruct(x.shape, F32),
                   jax.ShapeDtypeStruct(ck.shape, F32),
                   jax.ShapeDtypeStruct(cv.shape, F32),
                   jax.ShapeDtypeStruct(s0.shape, F32)],
        scratch_shapes=[pltpu.VMEM(win.shape, BF16), pltpu.VMEM(wao.shape, BF16),
                        pltpu.VMEM(who.shape, BF16), pltpu.VMEM(wo.shape, BF16),
                        pltpu.VMEM((n, D_MODEL), BF16),
                        pltpu.VMEM((n, ATTN_Q_W), F32), kv_act, kv_act,
                        act, act, act, act,
                        pltpu.VMEM((n, ATTN_Q_W), F32), act],
        compiler_params=pltpu.CompilerParams(
            dimension_semantics=("arbitrary",), vmem_limit_bytes=VMEM_LIMIT),
        name="mix_sample",
    )(sinks, x, ck, cv, s0, pre, post, lbl, hgn, win, wao, who, wo)


def kernel(x_prompt, x_sample, cache_k, cache_v, state_hgrn, norm_ffn1_pre, norm_ffn1_post, w_ffn1_gate, w_ffn1_up, w_ffn1_down, norm_mix_pre, norm_mix_post, w_in, attn_sinks, hgrn_lb_logits, hgrn_norm, w_attn_out, w_hgrn_out, w_out, norm_ffn2_pre, norm_ffn2_post, w_ffn2_gate, w_ffn2_up, w_ffn2_down):
    depth = w_in.shape[0]
    assert depth == 1 and hgrn_lb_logits.shape[0] == 2, "single-layer stack only"
    batch, seq, _ = x_prompt.shape
    n_s = x_sample.shape[0]
    assert x_sample.shape[1] == 1 and seq % WINDOW == 0

    xp = x_prompt.reshape(batch * seq, D_MODEL)
    ck = jnp.swapaxes(cache_k[0].reshape(n_s, WINDOW, ATTN_KV_W), 1, 2)
    cv = jnp.swapaxes(cache_v[0].reshape(n_s, WINDOW, ATTN_KV_W), 1, 2)
    s0 = state_hgrn[0].reshape(n_s, HG_W, HG_D)

    xp, xs = _ffn_call(xp, x_sample, norm_ffn1_pre, norm_ffn1_post,
                       w_ffn1_gate[0], w_ffn1_up[0], w_ffn1_down[0], (n_s, D_MODEL))

    mix_w = (attn_sinks, norm_mix_pre, norm_mix_post, hgrn_lb_logits, hgrn_norm,
             w_in[0], w_attn_out[0], w_hgrn_out[0], w_out[0])
    xp, nkp, nvp, nsp = _mix_prompt_call(xp, batch, *mix_w)
    xs, nks, nvs, nss = _mix_sample_call(xs, ck, cv, s0, *mix_w)

    xp, xs = _ffn_call(xp, xs, norm_ffn2_pre, norm_ffn2_post,
                       w_ffn2_gate[0], w_ffn2_up[0], w_ffn2_down[0], x_sample.shape)

    kv_shape = (1, -1, WINDOW, ATTN_KV_HEADS, HEAD_DIM)
    st_shape = (1, -1, HG_HEADS, HG_D, HG_D)
    def window(t):
        return jnp.swapaxes(t, 1, 2).reshape(kv_shape)
    return (xp.reshape(batch, seq, D_MODEL), xs,
            window(nkp), window(nvp), nsp.reshape(st_shape),
            window(nks), window(nvs), nss.reshape(st_shape))
```

```python
import functools

import jax
import jax.numpy as jnp
from jax import lax
from jax.experimental import pallas as pl
from jax.experimental.pallas import tpu as pltpu

F32 = jnp.float32
BF16 = jnp.bfloat16

D_MODEL = 1024
FFN_DIM = 2816
HEAD_DIM = 64
ATTN_Q_HEADS = 8
ATTN_KV_HEADS = 2
WINDOW = 128
HG_HEADS = 4
HG_D = 128
EPS = 1e-6

ATTN_Q_W = ATTN_Q_HEADS * HEAD_DIM
ATTN_KV_W = ATTN_KV_HEADS * HEAD_DIM
HG_W = HG_HEADS * HG_D
OFF_QA = 0
OFF_KA = OFF_QA + ATTN_Q_W
OFF_VA = OFF_KA + ATTN_KV_W
OFF_QH = OFF_VA + ATTN_KV_W
OFF_FH = OFF_QH + HG_W
OFF_IH = OFF_FH + HG_W
OFF_GH = OFF_IH + HG_W
OFF_GA = OFF_GH + HG_W
OFF_GB = OFF_GA + D_MODEL
IN_WIDTH = OFF_GB + D_MODEL

LANES = 128
SUBLANES = 8
FFN_CHUNK = 256
N_FFN_CHUNKS = FFN_DIM // FFN_CHUNK
FFN_ROWS = 1024
MIX_ROWS = 512
NORM_PARTS = 2
HG_CHUNK = 128
SAMPLE_TILE = 16
VMEM_LIMIT = 56 * 1024 * 1024

NT_DIMS = (((1,), (1,)), ((), ()))
TN_DIMS = (((0,), (0,)), ((), ()))


def _rms(x, g):
    return x * lax.rsqrt(jnp.mean(x * x, axis=-1, keepdims=True) + EPS) * g


def _dot(a, b):
    return jnp.dot(a, b, preferred_element_type=F32)


STREAM_SLOTS = 4
STREAM_CHUNK_BYTES = 2 << 20


def _stream_cast(src_hbm, dst_ref):
    n_rows, n_cols = src_hbm.shape
    row_bytes = n_cols * jnp.dtype(F32).itemsize
    rc = max(SUBLANES, min(n_rows, STREAM_CHUNK_BYTES // row_bytes // SUBLANES * SUBLANES))
    while n_rows % rc:
        rc -= SUBLANES
    n = n_rows // rc
    slots = min(STREAM_SLOTS, n)

    def body(stage, sem):
        def copy(c):
            return pltpu.make_async_copy(src_hbm.at[pl.ds(c * rc, rc), :], stage.at[c % slots],
                                         sem.at[c % slots])
        for c in range(slots):
            copy(c).start()
        for c in range(n):
            copy(c).wait()
            dst_ref[c * rc:(c + 1) * rc, :] = stage[c % slots].astype(BF16)
            if c + slots < n:
                copy(c + slots).start()

    pl.run_scoped(body, pltpu.VMEM((slots, rc, n_cols), F32), pltpu.SemaphoreType.DMA((slots,)))


HBM_SPEC = pl.BlockSpec(memory_space=pl.ANY)


def _const_spec(shape):
    zeros = (0,) * len(shape)
    return pl.BlockSpec(shape, lambda *_: zeros, pipeline_mode=pl.Buffered(1))


def _ffn_tile(x, pre, post, wg_ref, wu_ref, wd_ref):
    h = _rms(x, pre).astype(BF16)
    acc = None
    for c in range(N_FFN_CHUNKS):
        cols = slice(c * FFN_CHUNK, (c + 1) * FFN_CHUNK)
        g = _dot(h, wg_ref[:, cols])
        u = _dot(h, wu_ref[:, cols])
        a = (g * jax.nn.sigmoid(g) * u).astype(BF16)
        d = _dot(a, wd_ref[cols, :])
        acc = d if acc is None else acc + d
    return x + 0.5 * _rms(acc, post)


def _ffn_kernel(n_prompt_steps, xp_ref, xs_ref, pre_ref, post_ref, wg_hbm, wu_hbm, wd_hbm,
                yp_ref, ys_ref, wg_ref, wu_ref, wd_ref):
    i = pl.program_id(0)

    @pl.when(i == 0)
    def _():
        _stream_cast(wg_hbm, wg_ref)
        _stream_cast(wu_hbm, wu_ref)
        _stream_cast(wd_hbm, wd_ref)

    @pl.when(i < n_prompt_steps)
    def _():
        yp_ref[...] = _ffn_tile(xp_ref[...], pre_ref[...], post_ref[...], wg_ref, wu_ref, wd_ref)

    @pl.when(i == n_prompt_steps)
    def _():
        xs = xs_ref[...].reshape(xs_ref.shape[0], D_MODEL)
        ys = _ffn_tile(xs, pre_ref[...], post_ref[...], wg_ref, wu_ref, wd_ref)
        ys_ref[...] = ys.reshape(ys_ref.shape)


def _ffn_call(xp, xs, pre, post, wg, wu, wd, ys_shape):
    n_p, n_s = xp.shape[0], xs.shape[0]
    rows = min(FFN_ROWS, n_p)
    steps = n_p // rows
    prompt_spec = pl.BlockSpec((rows, D_MODEL), lambda i: (jnp.minimum(i, steps - 1), 0))

    def sample_spec(shape):
        zeros = (0,) * len(shape)
        return pl.BlockSpec(shape, lambda i: zeros)
    return pl.pallas_call(
        functools.partial(_ffn_kernel, steps),
        grid=(steps + 1,),
        in_specs=[prompt_spec, sample_spec(xs.shape),
                  _const_spec((1, D_MODEL)), _const_spec((1, D_MODEL)),
                  HBM_SPEC, HBM_SPEC, HBM_SPEC],
        out_specs=[prompt_spec, sample_spec(ys_shape)],
        out_shape=[jax.ShapeDtypeStruct(xp.shape, F32), jax.ShapeDtypeStruct(ys_shape, F32)],
        scratch_shapes=[pltpu.VMEM(wg.shape, BF16), pltpu.VMEM(wu.shape, BF16),
                        pltpu.VMEM(wd.shape, BF16)],
        compiler_params=pltpu.CompilerParams(
            dimension_semantics=("arbitrary",), vmem_limit_bytes=VMEM_LIMIT),
        name="ffn_half",
    )(xp, xs, pre, post, wg, wu, wd)


def _forget_lower_bound(lbl):
    l0, l1 = lbl[0:1, :], lbl[1:2, :]
    m = jnp.maximum(l0, l1)
    e0, e1 = jnp.exp(l0 - m), jnp.exp(l1 - m)
    return e0 / (e0 + e1)


def _lane_is_low(shape):
    return lax.broadcasted_iota(jnp.int32, shape, len(shape) - 1) < HEAD_DIM


def _dup_kv(x):
    swapped = pltpu.roll(x, HEAD_DIM, axis=1)
    low = _lane_is_low(x.shape)
    return jnp.where(low, x, swapped), jnp.where(low, swapped, x)


def _merge_out(h, w_ref, off, gates_from):
    return jax.nn.sigmoid(gates_from(off, D_MODEL)) * _dot(h.astype(BF16), w_ref[...])


def _pair_levels(c):
    t = jnp.arange(c, dtype=jnp.int32)[:, None]
    s = jnp.arange(c, dtype=jnp.int32)[None, :]
    x = jnp.maximum(t ^ s, 1)
    lvl = (31 - lax.clz(x)).astype(jnp.int32)
    return jnp.where(t > s, lvl, -1)


LOG2E = 1.4426950408889634


def _head(a, hd):
    return a[:, hd * HG_D:(hd + 1) * HG_D]


HG_LEVELS = HG_CHUNK.bit_length() - 1


def _chunk_rows(ci):
    return slice(ci * HG_CHUNK, (ci + 1) * HG_CHUNK)


def _hgrn_log_decay(g_ref, b_ref, n_chunks):
    c = HG_CHUNK
    width = g_ref.shape[1]
    row = lax.broadcasted_iota(jnp.int32, (c, c), 0)
    col = lax.broadcasted_iota(jnp.int32, (c, c), 1)
    tril = (col <= row).astype(BF16)
    for ci in range(n_chunks):
        r = _chunk_rows(ci)
        g = g_ref[r, :]
        hi = g.astype(BF16)
        rest = g - hi.astype(F32)
        mid = rest.astype(BF16)
        lo = (rest - mid.astype(F32)).astype(BF16)
        sums = _dot(tril, jnp.concatenate([hi, mid, lo], axis=1))
        b_ref[r, :] = (sums[:, 0:width] + sums[:, width:2 * width] + sums[:, 2 * width:]) * LOG2E


def _hgrn_factor_jobs(q_ref, k_ref, b_ref, x_ref, qe_ref, ke_ref, n_chunks):
    c = HG_CHUNK
    t = lax.broadcasted_iota(jnp.int32, (c, 1), 0)
    last = {}

    def level_job(ci, lvl):
        def run():
            n = 1 << lvl
            r = _chunk_rows(ci)
            bc = b_ref[r, :]
            second = (t & n) != 0
            if n < SUBLANES:
                prev = last.get(ci, bc)
                tiles = prev.reshape(c // SUBLANES, SUBLANES, prev.shape[1])
                second3 = second.reshape(c // SUBLANES, SUBLANES, 1)
                edge = jnp.where(second3, pltpu.roll(tiles, n, axis=1), tiles).reshape(prev.shape)
                if 2 * n < SUBLANES:
                    last[ci] = jnp.where(second3, tiles, pltpu.roll(tiles, SUBLANES - n, axis=1)
                                         ).reshape(prev.shape)
            else:
                edge = jnp.concatenate(
                    [jnp.broadcast_to(bc[p + n - 1:p + n, :], (2 * n, bc.shape[1]))
                     for p in range(0, c, 2 * n)], axis=0)
            w = jnp.exp2(jnp.where(second, bc - edge, edge - bc))
            x_ref[lvl, r, :] = (jnp.where(second, q_ref[r, :], k_ref[r, :]) * w).astype(BF16)
        return run

    def state_job(ci):
        def run():
            r = _chunk_rows(ci)
            bc = b_ref[r, :]
            qe_ref[r, :] = (q_ref[r, :] * jnp.exp2(bc)).astype(BF16)
            ke_ref[r, :] = (k_ref[r, :] * jnp.exp2(bc[c - 1:c, :] - bc)).astype(BF16)
        return run

    jobs = [level_job(ci, lvl) for lvl in range(HG_LEVELS) for ci in range(n_chunks)]
    return jobs + [state_job(ci) for ci in range(n_chunks)]


def _hgrn_pair_jobs(x_ref, levels, n_chunks):
    c = HG_CHUNK
    heads, chunks = range(HG_HEADS), range(n_chunks)
    a = [[jnp.zeros((c, c), F32) for _ in heads] for _ in chunks]

    def job(lvl, ci):
        def run():
            x = x_ref[lvl, _chunk_rows(ci), :]
            for hd in heads:
                al = lax.dot_general(_head(x, hd), _head(x, hd), NT_DIMS,
                                     preferred_element_type=F32)
                a[ci][hd] = jnp.where(levels == lvl, al, a[ci][hd])
        return run
    return [job(lvl, ci) for lvl in range(HG_LEVELS) for ci in chunks], a


def _hgrn_finish(q_ref, k_ref, v_ref, b_ref, a, qe_ref, ke_ref, st, n_chunks):
    c = HG_CHUNK
    heads, chunks = range(HG_HEADS), range(n_chunks)
    local = []
    for ci in chunks:
        r = _chunk_rows(ci)
        q, k, v = q_ref[r, :], k_ref[r, :], v_ref[r, :]
        qk, vb = q * k, v.astype(BF16)
        outs = []
        for hd in heads:
            diag = jnp.sum(_head(qk, hd), axis=1, keepdims=True)
            outs.append(_dot(a[ci][hd].astype(BF16), _head(vb, hd)) + diag * _head(v, hd))
        local.append(outs)

    result = []
    for ci in chunks:
        r = _chunk_rows(ci)
        qe, ke, vb = qe_ref[r, :], ke_ref[r, :], v_ref[r, :].astype(BF16)
        decay = jnp.exp2(b_ref[ci * c + c - 1:ci * c + c, :])
        outs, new = [], []
        for hd in heads:
            outs.append(local[ci][hd] + lax.dot_general(_head(qe, hd), st[hd].astype(BF16), NT_DIMS,
                                                        preferred_element_type=F32))
            new.append(st[hd] * _head(decay, hd)
                       + lax.dot_general(_head(vb, hd), _head(ke, hd), TN_DIMS,
                                         preferred_element_type=F32))
        st = new
        result.append(outs)
    return result, st


def _mix_prompt_kernel(sinks_ref, x_ref, pre_ref, post_ref, lbl_ref, hgn_ref, lvl_ref,
                       win_hbm, wao_hbm, who_hbm, wo_hbm,
                       y_ref, nk_ref, nv_ref, ns_ref,
                       win_ref, wao_ref, who_ref, wo_ref,
                       kbuf, vbuf, st_ref, qh_ref, kh_ref, vh_ref, lf_ref, oa_ref, oh_ref, g_ref,
                       b_ref, xl_ref, qe_ref, ke_ref):
    j = pl.program_id(1)
    rows = x_ref.shape[0]
    n_blocks = rows // WINDOW

    @pl.when((pl.program_id(0) == 0) & (j == 0))
    def _():
        for src, dst in ((win_hbm, win_ref), (wao_hbm, wao_ref), (who_hbm, who_ref),
                         (wo_hbm, wo_ref)):
            _stream_cast(src, dst)

    @pl.when(j == 0)
    def _():
        kbuf[0:WINDOW, :] = jnp.zeros((WINDOW, LANES), F32)
        vbuf[0:WINDOW, :] = jnp.zeros((WINDOW, LANES), F32)
        st_ref[...] = jnp.zeros(st_ref.shape, F32)

    part_rows = [slice(r0, r0 + rows // NORM_PARTS) for r0 in range(0, rows, rows // NORM_PARTS)]
    h_parts, qkv_parts = [], []
    for r in part_rows:
        hp = _rms(x_ref[r, :], pre_ref[...]).astype(BF16)
        h_parts.append(hp)
        qkv_parts.append(_dot(hp, win_ref[:, OFF_QA:OFF_QH]))
    h = jnp.concatenate(h_parts, axis=0)
    qkv = jnp.concatenate(qkv_parts, axis=0)

    def proj(off, width):
        return _dot(h, win_ref[:, off:off + width])

    lb = _forget_lower_bound(lbl_ref[...])
    n_chunks = rows // HG_CHUNK

    def proj_slab(off):
        def run():
            z = proj(off, FFN_CHUNK)
            if off < OFF_FH:
                cols = slice(off - OFF_QH, off - OFF_QH + FFN_CHUNK)
                qh_ref[:, cols] = z * jax.nn.sigmoid(z)
            elif off < OFF_IH:
                cols = slice(off - OFF_FH, off - OFF_FH + FFN_CHUNK)
                lbs = lb[:, cols]
                lf_ref[:, cols] = jnp.log(lbs + (1.0 - lbs) * jax.nn.sigmoid(z))
                kh_ref[:, cols] = (1.0 - lbs) * jax.nn.sigmoid(-z)
            elif off < OFF_GH:
                cols = slice(off - OFF_IH, off - OFF_IH + FFN_CHUNK)
                vh_ref[:, cols] = z
            else:
                cols = slice(off - OFF_GH, off - OFF_GH + FFN_CHUNK)
                g_ref[:, cols] = z
        return run
    hgrn_slabs = [proj_slab(off) for off in range(OFF_QH, OFF_GH, FFN_CHUNK)]
    gate_slabs = [proj_slab(off) for off in range(OFF_GH, IN_WIDTH, FFN_CHUNK)]
    jobs = _hgrn_factor_jobs(qh_ref, kh_ref, b_ref, xl_ref, qe_ref, ke_ref, n_chunks)
    pair_jobs, in_chunk = _hgrn_pair_jobs(xl_ref, lvl_ref[...], n_chunks)

    qa = qkv[:, OFF_QA:OFF_KA] * (HEAD_DIM ** -0.5)
    ka = qkv[:, OFF_KA:OFF_VA]
    va = qkv[:, OFF_VA:OFF_QH]
    kbuf[WINDOW:WINDOW + rows, :] = ka
    vbuf[WINDOW:WINDOW + rows, :] = va

    qi = lax.broadcasted_iota(jnp.int32, (WINDOW, 2 * WINDOW), 0)
    kj = lax.broadcasted_iota(jnp.int32, (WINDOW, 2 * WINDOW), 1)
    band = (kj > qi) & (kj <= qi + WINDOW)
    low = _lane_is_low((WINDOW, LANES))
    top = lax.broadcasted_iota(jnp.int32, (2 * WINDOW, 1), 0) < WINDOW
    n_slabs = ATTN_Q_HEADS // 2
    group = ATTN_Q_HEADS // ATTN_KV_HEADS
    kds = [_dup_kv(kbuf[n * WINDOW:(n + 2) * WINDOW, :]) for n in range(n_blocks)]
    vds = [_dup_kv(vbuf[n * WINDOW:(n + 2) * WINDOW, :]) for n in range(n_blocks)]

    def scores(n, slab):
        qs = qa[n * WINDOW:(n + 1) * WINDOW, slab * LANES:(slab + 1) * LANES]
        q2 = jnp.concatenate([jnp.where(low, qs, 0.0), jnp.where(low, 0.0, qs)], axis=0)
        return lax.dot_general(q2.astype(BF16), kds[n][(2 * slab) // group].astype(BF16), NT_DIMS,
                               preferred_element_type=F32)

    band2 = jnp.concatenate([band, band], axis=0)
    first2 = band2 & ((j > 0) | (jnp.concatenate([kj, kj], axis=0) >= WINDOW))

    def attend(n, slab, s):
        s = jnp.where(first2 if n == 0 else band2, s, -jnp.inf)
        sk = jnp.where(top, sinks_ref[0, 2 * slab], sinks_ref[0, 2 * slab + 1])
        m = jnp.maximum(jnp.max(s, axis=1, keepdims=True), sk)
        p = jnp.exp(s - m)
        den = jnp.sum(p, axis=1, keepdims=True) + jnp.exp(sk - m)
        o2 = _dot(p.astype(BF16), vds[n][(2 * slab) // group].astype(BF16)) / den
        oa_ref[n * WINDOW:(n + 1) * WINDOW, slab * LANES:(slab + 1) * LANES] = (
            jnp.where(low, o2[0:WINDOW], o2[WINDOW:2 * WINDOW]))

    order = [(n, slab) for n in range(n_blocks) for slab in range(n_slabs)]
    early = len(hgrn_slabs)
    assert len(order) >= early + len(gate_slabs)
    pending = scores(*order[0])
    ready_pairs = 0
    for idx, (n, slab) in enumerate(order):
        nxt = scores(*order[idx + 1]) if idx + 1 < len(order) else None
        if idx < early:
            hgrn_slabs[idx]()
        else:
            if idx == early:
                _hgrn_log_decay(lf_ref, b_ref, n_chunks)
            if gate_slabs:
                gate_slabs.pop(0)()
            for _ in range(min(len(pair_jobs), ready_pairs)):
                pair_jobs.pop(0)()
            ready_pairs = 0
            for _ in range(-(-len(jobs) // (len(order) - 1 - idx)) if idx + 1 < len(order) else 0):
                if jobs:
                    jobs.pop(0)()
                    ready_pairs += 1
        attend(n, slab, pending)
        pending = nxt
    kbuf[0:WINDOW, :] = kbuf[rows:rows + WINDOW, :]
    vbuf[0:WINDOW, :] = vbuf[rows:rows + WINDOW, :]
    assert not jobs and not gate_slabs
    for run in pair_jobs:
        run()

    hgn = hgn_ref[...]
    st0 = [st_ref[hd * HG_D:(hd + 1) * HG_D, :] for hd in range(HG_HEADS)]
    outs, states = _hgrn_finish(qh_ref, kh_ref, vh_ref, b_ref, in_chunk, qe_ref, ke_ref, st0,
                                n_chunks)
    for hd in range(HG_HEADS):
        cs = slice(hd * HG_D, (hd + 1) * HG_D)
        st_ref[cs, :] = states[hd]
        for ci in range(n_chunks):
            oh_ref[ci * HG_CHUNK:(ci + 1) * HG_CHUNK, cs] = _rms(outs[ci][hd], hgn)

    gh = g_ref[:, 0:HG_W]
    oh = oh_ref[...] * (gh * jax.nn.sigmoid(gh))

    def gate(off, width):
        return g_ref[:, off - OFF_GH:off - OFF_GH + width]
    m = (_merge_out(oa_ref[...], wao_ref, OFF_GA, gate)
         + _merge_out(oh, who_ref, OFF_GB, gate))
    mb = m.astype(BF16)
    for r in part_rows:
        y_ref[r, :] = x_ref[r, :] + _rms(_dot(mb[r, :], wo_ref[...]), post_ref[...])

    @pl.when(j == pl.num_programs(1) - 1)
    def _():
        nk_ref[0] = kbuf[0:WINDOW, :].T
        nv_ref[0] = vbuf[0:WINDOW, :].T
        for hd in range(HG_HEADS):
            cs = slice(hd * HG_D, (hd + 1) * HG_D)
            ns_ref[0, cs, :] = st_ref[cs, :].T


def _mix_prompt_call(x, batch, sinks, pre, post, lbl, hgn, win, wao, who, wo):
    seq = x.shape[0] // batch
    rows = min(MIX_ROWS, seq)
    steps = seq // rows
    x_spec = pl.BlockSpec((rows, D_MODEL), lambda b, j: (b * steps + j, 0))
    kv_spec = pl.BlockSpec((1, ATTN_KV_W, WINDOW), lambda b, j: (b, 0, 0))
    st_spec = pl.BlockSpec((1, HG_W, HG_D), lambda b, j: (b, 0, 0))
    act = pltpu.VMEM((rows, HG_W), F32)
    return pl.pallas_call(
        _mix_prompt_kernel,
        grid=(batch, steps),
        in_specs=[pl.BlockSpec(memory_space=pltpu.SMEM), x_spec,
                  _const_spec((1, D_MODEL)), _const_spec((1, D_MODEL)),
                  _const_spec(lbl.shape), _const_spec((1, HG_D)),
                  _const_spec((HG_CHUNK, HG_CHUNK)),
                  HBM_SPEC, HBM_SPEC, HBM_SPEC, HBM_SPEC],
        out_specs=[x_spec, kv_spec, kv_spec, st_spec],
        out_shape=[jax.ShapeDtypeStruct(x.shape, F32),
                   jax.ShapeDtypeStruct((batch, ATTN_KV_W, WINDOW), F32),
                   jax.ShapeDtypeStruct((batch, ATTN_KV_W, WINDOW), F32),
                   jax.ShapeDtypeStruct((batch, HG_W, HG_D), F32)],
        scratch_shapes=[pltpu.VMEM(win.shape, BF16), pltpu.VMEM(wao.shape, BF16),
                        pltpu.VMEM(who.shape, BF16), pltpu.VMEM(wo.shape, BF16),
                        pltpu.VMEM((rows + WINDOW, ATTN_KV_W), F32),
                        pltpu.VMEM((rows + WINDOW, ATTN_KV_W), F32),
                        pltpu.VMEM((HG_W, HG_D), F32),
                        act, act, act, act,
                        pltpu.VMEM((rows, ATTN_Q_W), F32), act,
                        pltpu.VMEM((rows, IN_WIDTH - OFF_GH), F32),
                        act, pltpu.VMEM((HG_LEVELS, rows, HG_W), BF16),
                        pltpu.VMEM((rows, HG_W), BF16), pltpu.VMEM((rows, HG_W), BF16)],
        compiler_params=pltpu.CompilerParams(
            dimension_semantics=("arbitrary", "arbitrary"), vmem_limit_bytes=VMEM_LIMIT),
        name="mix_prompt",
    )(sinks, x, pre, post, lbl, hgn, _pair_levels(HG_CHUNK), win, wao, who, wo)


def _row_select(rows_list):
    n = -(-len(rows_list) // 8) * 8
    lanes = rows_list[0].shape[1]
    ridx = lax.broadcasted_iota(jnp.int32, (n, lanes), 0)
    out = jnp.zeros((n, lanes), F32)
    for i, r in enumerate(rows_list):
        out = jnp.where(ridx == i, jnp.broadcast_to(r, (n, lanes)), out)
    return out


def _mix_sample_kernel(sinks_ref, x_ref, ck_ref, cv_ref, s0_ref, pre_ref, post_ref, lbl_ref, hgn_ref,
                       win_hbm, wao_hbm, who_hbm, wo_hbm,
                       y_ref, nk_ref, nv_ref, ns_ref,
                       win_ref, wao_ref, who_ref, wo_ref,
                       h_ref, qa_ref, ka_ref, va_ref, qh_ref, kh_ref, fh_ref, vh_ref, oa_ref, oh_ref):
    i = pl.program_id(0)
    tile = ck_ref.shape[0]

    def proj(off, width):
        return _dot(h_ref[...], win_ref[:, off:off + width])

    @pl.when(i == 0)
    def _():
        for src, dst in ((win_hbm, win_ref), (wao_hbm, wao_ref), (who_hbm, who_ref),
                         (wo_hbm, wo_ref)):
            _stream_cast(src, dst)
        h_ref[...] = _rms(x_ref[...], pre_ref[...]).astype(BF16)
        qa_ref[...] = proj(OFF_QA, ATTN_Q_W) * (HEAD_DIM ** -0.5)
        ka_ref[...] = proj(OFF_KA, ATTN_KV_W)
        va_ref[...] = proj(OFF_VA, ATTN_KV_W)
        lb = _forget_lower_bound(lbl_ref[...])
        qh = proj(OFF_QH, HG_W)
        qh_ref[...] = qh * jax.nn.sigmoid(qh)
        fp = proj(OFF_FH, HG_W)
        fh_ref[...] = jnp.exp(jnp.log(lb + (1.0 - lb) * jax.nn.sigmoid(fp)))
        kh_ref[...] = (1.0 - lb) * jax.nn.sigmoid(-fp)
        vh_ref[...] = proj(OFF_IH, HG_W)

    low = _lane_is_low((1, LANES))
    newest = lax.broadcasted_iota(jnp.int32, (ATTN_KV_W, WINDOW), 1) == WINDOW - 1
    hrow = lax.broadcasted_iota(jnp.int32, (ATTN_Q_HEADS, 1), 0)
    sk = jnp.zeros((ATTN_Q_HEADS, 1), F32)
    for hd in range(ATTN_Q_HEADS):
        sk = jnp.where(hrow == hd, sinks_ref[0, hd], sk)
    group = ATTN_Q_HEADS // ATTN_KV_HEADS
    hgn = hgn_ref[...]

    r8 = pl.ds(pl.multiple_of(i * tile, tile), tile)
    qa8, ka8, va8 = qa_ref[r8, :], ka_ref[r8, :], va_ref[r8, :]
    fh8, kh8, qh8, vh8 = fh_ref[r8, :], kh_ref[r8, :], qh_ref[r8, :], vh_ref[r8, :]
    oa_rows = [[] for _ in range(ATTN_Q_HEADS // 2)]
    oh_rows = [[] for _ in range(HG_HEADS)]
    seqs = range(tile)
    def as_columns(rows8):
        pad = jnp.zeros((LANES - tile, rows8.shape[1]), F32)
        return jnp.concatenate([rows8, pad], axis=0).T
    k_cols, v_cols = as_columns(ka8), as_columns(va8)
    kws, vws, q8s = [], [], []
    for bi in seqs:
        r1 = slice(bi, bi + 1)
        kw = jnp.where(newest, pltpu.roll(k_cols, WINDOW - 1 - bi, axis=1),
                       pltpu.roll(ck_ref[bi], WINDOW - 1, axis=1))
        vw = jnp.where(newest, pltpu.roll(v_cols, WINDOW - 1 - bi, axis=1),
                       pltpu.roll(cv_ref[bi], WINDOW - 1, axis=1))
        nk_ref[bi] = kw
        nv_ref[bi] = vw
        kws.append(kw.astype(BF16))
        vws.append(vw.astype(BF16))
        qrows = []
        for hd in range(ATTN_Q_HEADS):
            slab = qa8[r1, (hd // 2) * LANES:(hd // 2 + 1) * LANES]
            in_place = (hd % 2) == (hd // group)
            src = slab if in_place else pltpu.roll(slab, HEAD_DIM, axis=1)
            on_kv_lanes = low if hd // group == 0 else jnp.logical_not(low)
            qrows.append(jnp.where(on_kv_lanes, src, 0.0))
        q8s.append(_row_select(qrows).astype(BF16))
    scores = [_dot(q8s[bi], kws[bi]) for bi in seqs]
    probs = []
    for s in scores:
        m = jnp.maximum(jnp.max(s, axis=1, keepdims=True), sk)
        p = jnp.exp(s - m)
        probs.append((p / (jnp.sum(p, axis=1, keepdims=True) + jnp.exp(sk - m))).astype(BF16))
    for bi in seqs:
        o8 = lax.dot_general(probs[bi], vws[bi], NT_DIMS, preferred_element_type=F32)
        for slab in range(ATTN_Q_HEADS // 2):
            kv = (2 * slab) // group
            even, odd = o8[2 * slab:2 * slab + 1, :], o8[2 * slab + 1:2 * slab + 2, :]
            if kv == 0:
                out = jnp.where(low, even, pltpu.roll(odd, HEAD_DIM, axis=1))
            else:
                out = jnp.where(low, pltpu.roll(even, HEAD_DIM, axis=1), odd)
            oa_rows[slab].append(out)

    def columns(a8, hd):
        pad = jnp.zeros((HG_D - tile, HG_D), F32)
        return jnp.concatenate([_head(a8, hd), pad], axis=0).T
    seq_row = lax.broadcasted_iota(jnp.int32, (HG_D, HG_D), 0)
    heads = range(HG_HEADS)
    head_rows = [slice(hd * HG_D, (hd + 1) * HG_D) for hd in heads]
    f_cols = [columns(fh8, hd) for hd in heads]
    k_cols = [columns(kh8, hd).astype(BF16) for hd in heads]
    v_rows = [jnp.concatenate([_head(vh8, hd), jnp.zeros((HG_D - tile, HG_D), F32)], axis=0)
              for hd in heads]
    outers = [[_dot(k_cols[hd], jnp.where(seq_row == bi, v_rows[hd], 0.0).astype(BF16))
               for bi in seqs] for hd in heads]
    for hd in heads:
        for bi in seqs:
            f_col = jnp.broadcast_to(f_cols[hd][:, bi:bi + 1], (HG_D, HG_D))
            ns_ref[bi, head_rows[hd], :] = f_col * s0_ref[bi, head_rows[hd], :] + outers[hd][bi]
    outs = [[_dot(_head(qh8, hd).astype(BF16), ns_ref[bi, head_rows[hd], :].astype(BF16))
             for bi in seqs] for hd in heads]
    for hd in heads:
        for bi in seqs:
            oh_rows[hd].append(_rms(outs[hd][bi][bi:bi + 1, :], hgn))
    for slab in range(ATTN_Q_HEADS // 2):
        oa_ref[r8, slab * LANES:(slab + 1) * LANES] = _row_select(oa_rows[slab])
    for hd in range(HG_HEADS):
        oh_ref[r8, hd * HG_D:(hd + 1) * HG_D] = _row_select(oh_rows[hd])

    @pl.when(i == pl.num_programs(0) - 1)
    def _():
        gh = proj(OFF_GH, HG_W)
        oh = oh_ref[...] * (gh * jax.nn.sigmoid(gh))
        m = (_merge_out(oa_ref[...], wao_ref, OFF_GA, proj)
             + _merge_out(oh, who_ref, OFF_GB, proj))
        y_ref[...] = x_ref[...] + _rms(_dot(m.astype(BF16), wo_ref[...]), post_ref[...])


def _mix_sample_call(x, ck, cv, s0, sinks, pre, post, lbl, hgn, win, wao, who, wo):
    n = x.shape[0]
    tile = min(SAMPLE_TILE, n)
    full = pl.BlockSpec((n, D_MODEL), lambda i: (0, 0))
    kv_spec = pl.BlockSpec((tile, ATTN_KV_W, WINDOW), lambda i: (i, 0, 0))
    st_spec = pl.BlockSpec((tile, HG_W, HG_D), lambda i: (i, 0, 0))
    act = pltpu.VMEM((n, HG_W), F32)
    kv_act = pltpu.VMEM((n, ATTN_KV_W), F32)
    return pl.pallas_call(
        _mix_sample_kernel,
        grid=(n // tile,),
        in_specs=[pl.BlockSpec(memory_space=pltpu.SMEM), full, kv_spec, kv_spec, st_spec,
                  _const_spec((1, D_MODEL)), _const_spec((1, D_MODEL)),
                  _const_spec(lbl.shape), _const_spec((1, HG_D)),
                  HBM_SPEC, HBM_SPEC, HBM_SPEC, HBM_SPEC],
        out_specs=[full, kv_spec, kv_spec, st_spec],
        out_shape=[jax.ShapeDtypeStruct(x.shape, F32),
                   jax.ShapeDtypeStruct(ck.shape, F32),
                   jax.ShapeDtypeStruct(cv.shape, F32),
                   jax.ShapeDtypeStruct(s0.shape, F32)],
        scratch_shapes=[pltpu.VMEM(win.shape, BF16), pltpu.VMEM(wao.shape, BF16),
                        pltpu.VMEM(who.shape, BF16), pltpu.VMEM(wo.shape, BF16),
                        pltpu.VMEM((n, D_MODEL), BF16),
                        pltpu.VMEM((n, ATTN_Q_W), F32), kv_act, kv_act,
                        act, act, act, act,
                        pltpu.VMEM((n, ATTN_Q_W), F32), act],
        compiler_params=pltpu.CompilerParams(
            dimension_semantics=("arbitrary",), vmem_limit_bytes=VMEM_LIMIT),
        name="mix_sample",
    )(sinks, x, ck, cv, s0, pre, post, lbl, hgn, win, wao, who, wo)


def kernel(x_prompt, x_sample, cache_k, cache_v, state_hgrn, norm_ffn1_pre, norm_ffn1_post, w_ffn1_gate, w_ffn1_up, w_ffn1_down, norm_mix_pre, norm_mix_post, w_in, attn_sinks, hgrn_lb_logits, hgrn_norm, w_attn_out, w_hgrn_out, w_out, norm_ffn2_pre, norm_ffn2_post, w_ffn2_gate, w_ffn2_up, w_ffn2_down):
    depth = w_in.shape[0]
    assert depth == 1 and hgrn_lb_logits.shape[0] == 2, "single-layer stack only"
    batch, seq, _ = x_prompt.shape
    n_s = x_sample.shape[0]
    assert x_sample.shape[1] == 1 and seq % WINDOW == 0

    xp = x_prompt.reshape(batch * seq, D_MODEL)
    ck = jnp.swapaxes(cache_k[0].reshape(n_s, WINDOW, ATTN_KV_W), 1, 2)
    cv = jnp.swapaxes(cache_v[0].reshape(n_s, WINDOW, ATTN_KV_W), 1, 2)
    s0 = state_hgrn[0].reshape(n_s, HG_W, HG_D)

    xp, xs = _ffn_call(xp, x_sample, norm_ffn1_pre, norm_ffn1_post,
                       w_ffn1_gate[0], w_ffn1_up[0], w_ffn1_down[0], (n_s, D_MODEL))

    mix_w = (attn_sinks, norm_mix_pre, norm_mix_post, hgrn_lb_logits, hgrn_norm,
             w_in[0], w_attn_out[0], w_hgrn_out[0], w_out[0])
    xp, nkp, nvp, nsp = _mix_prompt_call(xp, batch, *mix_w)
    xs, nks, nvs, nss = _mix_sample_call(xs, ck, cv, s0, *mix_w)

    xp, xs = _ffn_call(xp, xs, norm_ffn2_pre, norm_ffn2_post,
                       w_ffn2_gate[0], w_ffn2_up[0], w_ffn2_down[0], x_sample.shape)

    kv_shape = (1, -1, WINDOW, ATTN_KV_HEADS, HEAD_DIM)
    st_shape = (1, -1, HG_HEADS, HG_D, HG_D)
    def window(t):
        return jnp.swapaxes(t, 1, 2).reshape(kv_shape)
    return (xp.reshape(batch, seq, D_MODEL), xs,
            window(nkp), window(nvp), nsp.reshape(st_shape),
            window(nks), window(nvs), nss.reshape(st_shape))
```
